```python
import math
import jax, jax.numpy as jnp
from jax import lax
import numpy as np

D_MODEL = 2048
BATCH = 2
SEQ = 4096
DEPTH = 1
DEC_BATCH = 4
DEC_SEQ = 8192
PAST_LEN = 128

GRID_W = 64
D_ATTN = D_MODEL // 2
D_HYENA = D_MODEL - D_ATTN
HEAD_DIM = 128
N_Q_HEADS = D_ATTN // HEAD_DIM
N_KV_HEADS = 2
Q_PER_KV = N_Q_HEADS // N_KV_HEADS
ROPE_HALF = HEAD_DIM // 2
ROPE_THETA = 10000.0
Q_BLOCK = 128
HYENA_ORDER = 2
SHORT_CONV = 3
FILTER_EMB = 33
FILTER_HIDDEN = 64
DECAY_MIN = -math.log(1e-2) / 1.5
DECAY_MAX = -math.log(1e-2) / 0.3
D_FF = 5632
EPS = 1e-6
D_KV = N_KV_HEADS * HEAD_DIM
D_IN_PROJ = D_ATTN + 2 * D_KV + (HYENA_ORDER + 1) * D_HYENA

kernel_name = "hybrid_attn_hyena_macaron_encoder"


def rmsnorm(x, g):
    xf = x.astype(jnp.float32)
    y = xf * lax.rsqrt(jnp.mean(xf * xf, axis=-1, keepdims=True) + EPS)
    return (y * g.astype(jnp.float32)).astype(x.dtype)


def swiglu(h, w13, w2):
    gate, up = jnp.split(h @ w13, 2, axis=-1)
    return (jax.nn.silu(gate) * up) @ w2


def axial_rope_tables(L):
    rows = L // GRID_W
    row = jnp.repeat(jnp.arange(rows, dtype=jnp.float32), GRID_W)
    col = jnp.tile(jnp.arange(GRID_W, dtype=jnp.float32), rows)
    inv = ROPE_THETA ** (-jnp.arange(0, ROPE_HALF, 2, dtype=jnp.float32) / ROPE_HALF)
    ang_r = row[:, None] * inv[None]
    ang_c = col[:, None] * inv[None]
    return jnp.cos(ang_r), jnp.sin(ang_r), jnp.cos(ang_c), jnp.sin(ang_c)


def rotate(x, cos, sin):
    x1, x2 = jnp.split(x, 2, axis=-1)
    c = cos[None, :, None, :]
    s = sin[None, :, None, :]
    return jnp.concatenate([x1 * c - x2 * s, x2 * c + x1 * s], axis=-1)


def apply_axial_rope(x, tabs):
    cr, sr, cc, sc = tabs
    xf = x.astype(jnp.float32)
    out = jnp.concatenate([rotate(xf[..., :ROPE_HALF], cr, sr),
                           rotate(xf[..., ROPE_HALF:], cc, sc)], axis=-1)
    return out.astype(x.dtype)


def block_attention(q, k, v):
    B, L, _, _ = q.shape
    nb = L // Q_BLOCK
    qb = q.reshape(B, nb, Q_BLOCK, N_KV_HEADS, Q_PER_KV, HEAD_DIM).transpose(1, 0, 2, 3, 4, 5)

    def one_block(qi):
        s = jnp.einsum('bqkgd,bskd->bkgqs', qi, k).astype(jnp.float32)
        p = jax.nn.softmax(s, axis=-1).astype(v.dtype)
        return jnp.einsum('bkgqs,bskd->bqkgd', p, v)

    o = lax.map(one_block, qb)
    return o.transpose(1, 0, 2, 3, 4, 5).reshape(B, L, N_Q_HEADS * HEAD_DIM)


def short_conv(x, w, b):
    xp = jnp.pad(x, ((0, 0), (1, 1), (0, 0)))
    return xp[:, :-2] * w[0] + xp[:, 1:-1] * w[1] + xp[:, 2:] * w[2] + b


def hyena_filters(L, w1, b1, w2, b2, w3, freq, decay):
    f32 = jnp.float32
    t01 = jnp.linspace(0.0, 1.0, L, dtype=f32)[:, None]
    bands = (FILTER_EMB - 1) // 2
    fr = jnp.linspace(1e-4, bands - 1, bands, dtype=f32)[None]
    w = 2.0 * math.pi * jnp.arange(L, dtype=f32)[:, None] / L
    feats = jnp.concatenate([t01, jnp.cos(fr * w), -jnp.sin(fr * w)], axis=-1)
    fq = freq.astype(f32)
    h = jnp.sin(fq * (feats @ w1.astype(f32) + b1.astype(f32)))
    h = jnp.sin(fq * (h @ w2.astype(f32) + b2.astype(f32)))
    h = (h @ w3.astype(f32)).reshape(L, 2, HYENA_ORDER, D_HYENA)
    h = h * jnp.exp(-t01[:, :, None, None] * jnp.abs(decay.astype(f32))[None])
    fwd, bwd = h[:, 0], h[:, 1]
    kfull = jnp.concatenate([fwd, jnp.zeros((1, HYENA_ORDER, D_HYENA), f32), bwd[1:][::-1]], axis=0)
    return kfull / jnp.sum(jnp.abs(kfull), axis=0, keepdims=True)


def fftconv(z, kf, d):
    L = z.shape[1]
    zf = z.astype(jnp.float32)
    Z = jnp.fft.rfft(zf, n=2 * L, axis=1)
    K = jnp.fft.rfft(kf, axis=0)
    y = jnp.fft.irfft(Z * K[None], n=2 * L, axis=1)[:, :L]
    return (y + zf * d.astype(jnp.float32)).astype(z.dtype)


def encoder_layer(x, ffn1_norm, ffn1_w13, ffn1_w2, mix_norm, w_in, q_norm, k_norm,
                  conv_w, conv_b, filt_w1, filt_b1, filt_w2, filt_b2, filt_w3, filt_freq,
                  hyena_decay, hyena_bias, group_out_norm, w_out,
                  ffn2_norm, ffn2_w13, ffn2_w2):
    B, L, _ = x.shape
    x = x + 0.5 * swiglu(rmsnorm(x, ffn1_norm), ffn1_w13, ffn1_w2)
    h = rmsnorm(x, mix_norm)
    p = h @ w_in
    q = p[..., :D_ATTN].reshape(B, L, N_Q_HEADS, HEAD_DIM)
    k = p[..., D_ATTN:D_ATTN + D_KV].reshape(B, L, N_KV_HEADS, HEAD_DIM)
    v = p[..., D_ATTN + D_KV:D_ATTN + 2 * D_KV].reshape(B, L, N_KV_HEADS, HEAD_DIM)
    hy = p[..., D_ATTN + 2 * D_KV:]
    tabs = axial_rope_tables(L)
    q = apply_axial_rope(rmsnorm(q, q_norm), tabs) * (HEAD_DIM ** -0.5)
    k = apply_axial_rope(rmsnorm(k, k_norm), tabs)
    attn_out = block_attention(q, k, v)
    hy = short_conv(hy, conv_w, conv_b)
    hv, hx1, hx2 = jnp.split(hy, 3, axis=-1)
    kf = hyena_filters(L, filt_w1, filt_b1, filt_w2, filt_b2, filt_w3, filt_freq, hyena_decay)
    z = hx1 * fftconv(hv, kf[:, 0], hyena_bias[0])
    hyena_out = hx2 * fftconv(z, kf[:, 1], hyena_bias[1])
    merged = jnp.concatenate([rmsnorm(attn_out, group_out_norm[:D_ATTN]),
                              rmsnorm(hyena_out, group_out_norm[D_ATTN:])], axis=-1)
    x = x + merged @ w_out
    x = x + 0.5 * swiglu(rmsnorm(x, ffn2_norm), ffn2_w13, ffn2_w2)
    return x


def setup_inputs(seed: int = 0) -> dict:
    key = jax.random.key(seed)
    ks = jax.random.split(key, 32)
    f32 = jnp.float32

    def nrm(k, shape, scale):
        return jax.random.normal(k, shape, f32) * scale

    def gain(k, shape):
        return 1.0 + 0.02 * jax.random.normal(k, shape, f32)

    rates = jnp.linspace(DECAY_MIN, DECAY_MAX, D_HYENA, dtype=f32)
    return {
        "x_prompt": jax.random.normal(ks[0], (BATCH, SEQ, D_MODEL), f32),
        "x_sample": jax.random.normal(ks[1], (DEC_BATCH, DEC_SEQ, D_MODEL), f32),
        "ffn1_norm": gain(ks[2], (DEPTH, D_MODEL)),
        "ffn1_w13": nrm(ks[3], (DEPTH, D_MODEL, 2 * D_FF), D_MODEL ** -0.5),
        "ffn1_w2": nrm(ks[4], (DEPTH, D_FF, D_MODEL), D_FF ** -0.5),
        "mix_norm": gain(ks[5], (DEPTH, D_MODEL)),
        "w_in": nrm(ks[6], (DEPTH, D_MODEL, D_IN_PROJ), D_MODEL ** -0.5),
        "q_norm": gain(ks[7], (DEPTH, HEAD_DIM)),
        "k_norm": gain(ks[8], (DEPTH, HEAD_DIM)),
        "conv_w": nrm(ks[9], (DEPTH, SHORT_CONV, 3 * D_HYENA), SHORT_CONV ** -0.5),
        "conv_b": nrm(ks[10], (DEPTH, 3 * D_HYENA), 0.02),
        "filt_w1": nrm(ks[11], (DEPTH, FILTER_EMB, FILTER_HIDDEN), FILTER_EMB ** -0.5),
        "filt_b1": nrm(ks[12], (DEPTH, FILTER_HIDDEN), 0.1),
        "filt_w2": nrm(ks[13], (DEPTH, FILTER_HIDDEN, FILTER_HIDDEN), FILTER_HIDDEN ** -0.5),
        "filt_b2": nrm(ks[14], (DEPTH, FILTER_HIDDEN), 0.1),
        "filt_w3": nrm(ks[15], (DEPTH, FILTER_HIDDEN, 2 * HYENA_ORDER * D_HYENA), FILTER_HIDDEN ** -0.5),
        "filt_freq": gain(ks[16], (DEPTH, FILTER_HIDDEN)),
        "hyena_decay": rates[None, None, None, :] * (1.0 + 0.05 * jax.random.normal(ks[17], (DEPTH, 2, HYENA_ORDER, D_HYENA), f32)),
        "hyena_bias": nrm(ks[18], (DEPTH, HYENA_ORDER, D_HYENA), 1.0),
        "group_out_norm": gain(ks[19], (DEPTH, D_MODEL)),
        "w_out": nrm(ks[20], (DEPTH, D_MODEL, D_MODEL), D_MODEL ** -0.5),
        "ffn2_norm": gain(ks[21], (DEPTH, D_MODEL)),
        "ffn2_w13": nrm(ks[22], (DEPTH, D_MODEL, 2 * D_FF), D_MODEL ** -0.5),
        "ffn2_w2": nrm(ks[23], (DEPTH, D_FF, D_MODEL), D_FF ** -0.5),
        "final_norm": gain(ks[24], (D_MODEL,)),
    }


def reference(x_prompt, x_sample, ffn1_norm, ffn1_w13, ffn1_w2, mix_norm, w_in, q_norm, k_norm,
              conv_w, conv_b, filt_w1, filt_b1, filt_w2, filt_b2, filt_w3, filt_freq,
              hyena_decay, hyena_bias, group_out_norm, w_out,
              ffn2_norm, ffn2_w13, ffn2_w2, final_norm):
    def trunk(x):
        for i in range(DEPTH):
            x = encoder_layer(x, ffn1_norm[i], ffn1_w13[i], ffn1_w2[i], mix_norm[i], w_in[i],
                              q_norm[i], k_norm[i], conv_w[i], conv_b[i],
                              filt_w1[i], filt_b1[i], filt_w2[i], filt_b2[i], filt_w3[i], filt_freq[i],
                              hyena_decay[i], hyena_bias[i], group_out_norm[i], w_out[i],
                              ffn2_norm[i], ffn2_w13[i], ffn2_w2[i])
        return rmsnorm(x, final_norm)

    y_prompt = trunk(x_prompt)
    y_sample = trunk(x_sample)
    return (y_prompt, y_sample)
```

```python
import functools
import math

import jax
import jax.numpy as jnp
from jax import lax
from jax.experimental import pallas as pl
from jax.experimental.pallas import tpu as pltpu

F32 = jnp.float32
BF16 = jnp.bfloat16

D_MODEL = 2048
GRID_W = 64
D_ATTN = D_MODEL // 2
D_HYENA = D_MODEL - D_ATTN
HEAD_DIM = 128
N_Q_HEADS = D_ATTN // HEAD_DIM
N_KV_HEADS = 2
Q_PER_KV = N_Q_HEADS // N_KV_HEADS
ROPE_HALF = HEAD_DIM // 2
ROPE_THETA = 10000.0
FILTER_EMB = 33
FILTER_HIDDEN = 64
D_FF = 5632
EPS = 1e-6
D_KV = N_KV_HEADS * HEAD_DIM

LANES = 128
FILT_PAD = 128
FFT_N2 = 128
FFT_W = 256
FFT_G = 8
MIB = 1024 * 1024


def _cparams(semantics, vmem_mib):
    return pltpu.CompilerParams(dimension_semantics=semantics,
                                vmem_limit_bytes=vmem_mib * MIB)


def _rms(x, g):
    ms = jnp.mean(x * x, axis=-1, keepdims=True)
    return x * lax.rsqrt(ms + EPS) * g


def _resident(shape):
    nd = len(shape)
    return pl.BlockSpec(shape, lambda *_: (0,) * nd, pipeline_mode=pl.Buffered(1))


def _ffn_body(x_ref, g_ref, wg_ref, wu_ref, w2_ref, gf_ref, o_ref, xn_ref, *, nj, final_norm):
    j = pl.program_id(1)

    @pl.when(j == 0)
    def _init():
        x = x_ref[...]
        xn_ref[...] = _rms(x, g_ref[...]).astype(BF16)
        o_ref[...] = x

    xn = xn_ref[...]
    hg = jnp.dot(xn, wg_ref[...], preferred_element_type=F32)
    hu = jnp.dot(xn, wu_ref[...], preferred_element_type=F32)
    a = (hg * (0.5 / (1.0 + jnp.exp(-hg))) * hu).astype(BF16)
    o_ref[...] += jnp.dot(a, w2_ref[...], preferred_element_type=F32)

    if final_norm:
        @pl.when(j == nj - 1)
        def _fin():
            o_ref[...] = _rms(o_ref[...], gf_ref[...])


def _ffn(x, g, w13, w2, gf, *, final_norm, tm=512, tf=512):
    T, D = x.shape
    ff = w2.shape[0]
    nj = ff // tf
    body = functools.partial(_ffn_body, nj=nj, final_norm=final_norm)
    return pl.pallas_call(
        body,
        grid=(T // tm, nj),
        in_specs=[
            pl.BlockSpec((tm, D), lambda i, j: (i, 0)),
            pl.BlockSpec((1, D), lambda i, j: (0, 0)),
            pl.BlockSpec((D, tf), lambda i, j: (0, j)),
            pl.BlockSpec((D, tf), lambda i, j: (0, j + nj)),
            pl.BlockSpec((tf, D), lambda i, j: (j, 0)),
            pl.BlockSpec((1, D), lambda i, j: (0, 0)),
        ],
        out_specs=pl.BlockSpec((tm, D), lambda i, j: (i, 0)),
        out_shape=jax.ShapeDtypeStruct((T, D), F32),
        scratch_shapes=[pltpu.VMEM((tm, D), BF16)],
        compiler_params=_cparams(("parallel", "arbitrary"), 48),
        name="ffn",
    )(x, g, w13, w13, w2, gf)


def _mix_body(x_ref, g_ref, wq_ref, wkv_ref, wh_ref, qn_ref, kn_ref, cos_ref, sin_ref,
              q_ref, k_ref, v_ref, hy_ref):
    h = _rms(x_ref[...], g_ref[...]).astype(BF16)
    tm = h.shape[0]
    lane = lax.broadcasted_iota(jnp.int32, (tm, HEAD_DIM), 1)
    low_half = (lane % (2 * (ROPE_HALF // 2))) < (ROPE_HALF // 2)
    c = cos_ref[...]
    s = sin_ref[...]

    def head(p, gain, scale):
        y = _rms(p, gain)
        partner = jnp.where(low_half, pltpu.roll(y, HEAD_DIM - ROPE_HALF // 2, 1),
                            pltpu.roll(y, ROPE_HALF // 2, 1))
        y = y * c + partner * s
        return y if scale is None else y * scale

    q = jnp.dot(h, wq_ref[...], preferred_element_type=F32)
    for hd in range(N_Q_HEADS):
        sl = slice(hd * HEAD_DIM, (hd + 1) * HEAD_DIM)
        q_ref[:, sl] = head(q[:, sl], qn_ref[...], HEAD_DIM ** -0.5).astype(BF16)
    kv = jnp.dot(h, wkv_ref[...], preferred_element_type=F32)
    for hd in range(N_KV_HEADS):
        sl = slice(hd * HEAD_DIM, (hd + 1) * HEAD_DIM)
        k_ref[:, sl] = head(kv[:, sl], kn_ref[...], None).astype(BF16)
    v_ref[...] = kv[:, D_KV:].astype(BF16)
    hy_ref[...] = jnp.dot(h, wh_ref[...], preferred_element_type=F32)


def _mix_in(x, g, wq, wkv, wh, qn, kn, cos_t, sin_t, *, L, tm=256):
    T, D = x.shape
    npos = L // tm
    return pl.pallas_call(
        _mix_body,
        grid=(T // tm,),
        in_specs=[
            pl.BlockSpec((tm, D), lambda i: (i, 0)),
            _resident((1, D)),
            _resident(wq.shape),
            _resident(wkv.shape),
            _resident(wh.shape),
            _resident((1, HEAD_DIM)),
            _resident((1, HEAD_DIM)),
            pl.BlockSpec((tm, HEAD_DIM), lambda i: (i % npos, 0)),
            pl.BlockSpec((tm, HEAD_DIM), lambda i: (i % npos, 0)),
        ],
        out_specs=[
            pl.BlockSpec((tm, D_ATTN), lambda i: (i, 0)),
            pl.BlockSpec((tm, D_KV), lambda i: (i, 0)),
            pl.BlockSpec((tm, D_KV), lambda i: (i, 0)),
            pl.BlockSpec((tm, 3 * D_HYENA), lambda i: (i, 0)),
        ],
        out_shape=[
            jax.ShapeDtypeStruct((T, D_ATTN), BF16),
            jax.ShapeDtypeStruct((T, D_KV), BF16),
            jax.ShapeDtypeStruct((T, D_KV), BF16),
            jax.ShapeDtypeStruct((T, 3 * D_HYENA), F32),
        ],
        compiler_params=_cparams(("parallel",), 48),
        name="mix_in",
    )(x, g, wq, wkv, wh, qn, kn, cos_t, sin_t)


def _attn_body(q_ref, k_ref, v_ref, o_ref, *, tq, tk, nk):
    q = jnp.concatenate([q_ref[:, h * HEAD_DIM:(h + 1) * HEAD_DIM] for h in range(Q_PER_KV)], axis=0)
    m_rows = Q_PER_KV * tq

    def step(i, carry):
        m, l, acc = carry
        start = pl.multiple_of(i * tk, tk)
        kc = k_ref[pl.ds(start, tk), :]
        vc = v_ref[pl.ds(start, tk), :]
        s = lax.dot_general(q, kc, (((1,), (1,)), ((), ())), preferred_element_type=F32)
        m_new = jnp.maximum(m, jnp.max(s, axis=-1, keepdims=True))
        alpha = jnp.exp(m - m_new)
        p = jnp.exp(s - m_new)
        l = alpha * l + jnp.sum(p, axis=-1, keepdims=True)
        acc = alpha * acc + jnp.dot(p.astype(BF16), vc, preferred_element_type=F32)
        return m_new, l, acc

    m0 = jnp.full((m_rows, 1), -jnp.inf, F32)
    l0 = jnp.zeros((m_rows, 1), F32)
    a0 = jnp.zeros((m_rows, HEAD_DIM), F32)
    _, l, acc = lax.fori_loop(0, nk, step, (m0, l0, a0))
    o = acc / l
    for h in range(Q_PER_KV):
        o_ref[:, h * HEAD_DIM:(h + 1) * HEAD_DIM] = o[h * tq:(h + 1) * tq]


def _attention(q, k, v, *, tq=128, tk=512):
    B, L, _ = q.shape
    qw = Q_PER_KV * HEAD_DIM
    body = functools.partial(_attn_body, tq=tq, tk=tk, nk=L // tk)
    return pl.pallas_call(
        body,
        grid=(B, N_KV_HEADS, L // tq),
        in_specs=[
            pl.BlockSpec((None, tq, qw), lambda b, g, i: (b, i, g)),
            pl.BlockSpec((None, L, HEAD_DIM), lambda b, g, i: (b, 0, g)),
            pl.BlockSpec((None, L, HEAD_DIM), lambda b, g, i: (b, 0, g)),
        ],
        out_specs=pl.BlockSpec((None, tq, qw), lambda b, g, i: (b, i, g)),
        out_shape=jax.ShapeDtypeStruct((B, L, D_ATTN), F32),
        compiler_params=_cparams(("parallel", "parallel", "arbitrary"), 32),
        name="attn",
    )(q, k, v)


def _sconv_body(x_ref, w_ref, b_ref, o_ref, *, L, R):
    w0 = w_ref[0:1, :]
    w1 = w_ref[1:2, :]
    w2 = w_ref[2:3, :]
    b = b_ref[...]
    nchunk = L // R
    rid = lax.broadcasted_iota(jnp.int32, (R, x_ref.shape[-1]), 0)

    def chunk(r, carry):
        base = pl.multiple_of(r * R, R)
        cur = x_ref[pl.ds(base, R), :]
        before = x_ref[pl.ds(jnp.maximum(base - 8, 0), 8), :][7:8, :]
        after = x_ref[pl.ds(jnp.minimum(base + R, L - 8), 8), :][0:1, :]
        before = jnp.where(r > 0, before, 0.0)
        after = jnp.where(r < nchunk - 1, after, 0.0)
        xm = jnp.where(rid == 0, before, pltpu.roll(cur, 1, 0))
        xp = jnp.where(rid == R - 1, after, pltpu.roll(cur, R - 1, 0))
        o_ref[pl.ds(base, R), :] = xm * w0 + cur * w1 + xp * w2 + b
        return carry

    lax.fori_loop(0, nchunk, chunk, 0)


def _short_conv(hy, w, b, *, R=512, wc=LANES):
    B, L, C = hy.shape
    body = functools.partial(_sconv_body, L=L, R=R)
    return pl.pallas_call(
        body,
        grid=(B, C // wc),
        in_specs=[
            pl.BlockSpec((None, L, wc), lambda bi, c: (bi, 0, c)),
            pl.BlockSpec((3, wc), lambda bi, c: (0, c)),
            pl.BlockSpec((1, wc), lambda bi, c: (0, c)),
        ],
        out_specs=pl.BlockSpec((None, L, wc), lambda bi, c: (bi, 0, c)),
        out_shape=jax.ShapeDtypeStruct((B, L, C), F32),
        compiler_params=_cparams(("parallel", "parallel"), 32),
        name="sconv",
    )(hy, w, b)


def _filt_body(feat_ref, w1_ref, b1_ref, w2_ref, b2_ref, w3_ref, fq_ref, dec_ref, hf_ref, asum_ref, *, tl):
    i = pl.program_id(0)
    hp = lax.Precision.HIGHEST
    fq = fq_ref[...]
    feat = feat_ref[...]
    h = jnp.sin(fq * (jnp.dot(feat, w1_ref[...], precision=hp, preferred_element_type=F32) + b1_ref[...]))
    h = jnp.sin(fq * (jnp.dot(h, w2_ref[...], precision=hp, preferred_element_type=F32) + b2_ref[...]))
    h3 = jnp.dot(h, w3_ref[...], precision=hp, preferred_element_type=F32)
    t01 = feat[:, 0:1]
    hf = h3 * jnp.exp(-t01 * jnp.abs(dec_ref[...]))
    ncol = hf.shape[1]
    row = lax.broadcasted_iota(jnp.int32, hf.shape, 0) + i * tl
    col = lax.broadcasted_iota(jnp.int32, hf.shape, 1)
    hf = jnp.where((row == 0) & (col >= ncol // 2), 0.0, hf)
    hf_ref[...] = hf
    part = jnp.sum(jnp.abs(hf), axis=0, keepdims=True)

    @pl.when(i == 0)
    def _first():
        asum_ref[...] = part

    @pl.when(i > 0)
    def _rest():
        asum_ref[...] += part


def _filters(feats, w1, b1, w2, b2, w3, fq, dec, *, tl=256):
    L = feats.shape[0]
    nc = w3.shape[1]
    body = functools.partial(_filt_body, tl=tl)
    return pl.pallas_call(
        body,
        grid=(L // tl,),
        in_specs=[
            pl.BlockSpec((tl, FILT_PAD), lambda i: (i, 0)),
            _resident(w1.shape), _resident(b1.shape), _resident(w2.shape), _resident(b2.shape),
            _resident(w3.shape), _resident(fq.shape), _resident(dec.shape),
        ],
        out_specs=[
            pl.BlockSpec((tl, nc), lambda i: (i, 0)),
            pl.BlockSpec((1, nc), lambda i: (0, 0)),
        ],
        out_shape=[jax.ShapeDtypeStruct((L, nc), F32), jax.ShapeDtypeStruct((1, nc), F32)],
        compiler_params=_cparams(("arbitrary",), 48),
        name="filt",
    )(feats, w1, b1, w2, b2, w3, fq, dec)


def _fft_a_body(x_ref, fa_ref, o_ref):
    for j in range(FFT_G):
        xs = jnp.concatenate([x_ref[0, :, j, :], x_ref[1, :, j, :]], axis=0).astype(BF16)
        o_ref[j] = jnp.dot(fa_ref[j], xs, preferred_element_type=F32)


def _fft_a(xv, fa, *, ncols, col0=0):
    P, _, n1h, n2, _ = xv.shape
    two_n1 = fa.shape[1]
    c0 = col0 // FFT_W
    return pl.pallas_call(
        _fft_a_body,
        grid=(n2 // FFT_G, P, ncols // FFT_W),
        in_specs=[
            pl.BlockSpec((None, 2, n1h, FFT_G, FFT_W), lambda g, p, c: (p, 0, 0, g, c + c0)),
            pl.BlockSpec((FFT_G, two_n1, 2 * n1h), lambda g, p, c: (g, 0, 0)),
        ],
        out_specs=pl.BlockSpec((None, FFT_G, two_n1, FFT_W), lambda g, p, c: (p, g, 0, c)),
        out_shape=jax.ShapeDtypeStruct((P, n2, two_n1, ncols), F32),
        compiler_params=_cparams(("arbitrary", "arbitrary", "arbitrary"), 32),
        name="fft_a",
    )(xv, fa)


def _fft_a_real_body(x_ref, fa_ref, o_ref):
    for j in range(FFT_G):
        o_ref[j] = jnp.dot(fa_ref[j], x_ref[:, j, :].astype(BF16), preferred_element_type=F32)


def _fft_a_real(xv, fa_re):
    n1h, n2, C = xv.shape
    two_n1 = fa_re.shape[1]
    return pl.pallas_call(
        _fft_a_real_body,
        grid=(n2 // FFT_G, C // FFT_W),
        in_specs=[
            pl.BlockSpec((n1h, FFT_G, FFT_W), lambda g, c: (0, g, c)),
            pl.BlockSpec((FFT_G, two_n1, n1h), lambda g, c: (g, 0, 0)),
        ],
        out_specs=pl.BlockSpec((FFT_G, two_n1, FFT_W), lambda g, c: (g, 0, c)),
        out_shape=jax.ShapeDtypeStruct((n2, two_n1, C), F32),
        compiler_params=_cparams(("arbitrary", "arbitrary"), 32),
        name="fft_a_real",
    )(xv, fa_re)


def _fft_b_filt_body(fr_ref, fi_ref, br_ref, bi_ref, fb_ref, asf_ref, asb_ref, o_ref):
    n2 = FFT_N2
    inv = 1.0 / (asf_ref[...] + asb_ref[...])
    fb = fb_ref[...]
    for j in range(FFT_G):
        yf = jnp.concatenate([fr_ref[:, j, :], fi_ref[:, j, :]], axis=0).astype(BF16)
        yb = jnp.concatenate([br_ref[:, j, :], bi_ref[:, j, :]], axis=0).astype(BF16)
        u = jnp.dot(fb, yf, preferred_element_type=F32)
        w = jnp.dot(fb, yb, preferred_element_type=F32)
        kr = (u[:n2] + w[:n2]) * inv
        ki = (u[n2:] - w[n2:]) * inv
        o_ref[j] = jnp.concatenate([kr, ki], axis=0)


def _fft_b_filt(s1f, fb, asum):
    n2, two_n1, c2 = s1f.shape
    n1 = two_n1 // 2
    cf = c2 // 2
    nb = cf // FFT_W
    ng = n1 // FFT_G
    blk = (n2, FFT_G, FFT_W)
    return pl.pallas_call(
        _fft_b_filt_body,
        grid=(ng, nb),
        in_specs=[
            pl.BlockSpec(blk, lambda g, c: (0, g, c)),
            pl.BlockSpec(blk, lambda g, c: (0, g + ng, c)),
            pl.BlockSpec(blk, lambda g, c: (0, g, c + nb)),
            pl.BlockSpec(blk, lambda g, c: (0, g + ng, c + nb)),
            _resident(fb.shape),
            pl.BlockSpec((1, FFT_W), lambda g, c: (0, c)),
            pl.BlockSpec((1, FFT_W), lambda g, c: (0, c + nb)),
        ],
        out_specs=pl.BlockSpec((FFT_G, 2 * n2, FFT_W), lambda g, c: (g, 0, c)),
        out_shape=jax.ShapeDtypeStruct((n1, 2 * n2, cf), F32),
        compiler_params=_cparams(("parallel", "parallel"), 32),
        name="fft_b_filt",
    )(s1f, s1f, s1f, s1f, fb, asum, asum)


def _fft_b_body(sr_ref, si_ref, kf_ref, fb_ref, fbi_ref, o_ref):
    n2 = FFT_N2
    fb = fb_ref[...]
    fbi = fbi_ref[...]
    for j in range(FFT_G):
        y = jnp.concatenate([sr_ref[:, j, :], si_ref[:, j, :]], axis=0).astype(BF16)
        x = jnp.dot(fb, y, preferred_element_type=F32)
        xr, xi = x[:n2], x[n2:]
        kr, ki = kf_ref[j, :n2, :], kf_ref[j, n2:, :]
        pm = jnp.concatenate([xr * kr - xi * ki, xr * ki + xi * kr], axis=0).astype(BF16)
        o_ref[j] = jnp.dot(fbi, pm, preferred_element_type=F32)


def _fft_b(s1, kf, fb, fbi, *, kcol0):
    P, n2, two_n1, C = s1.shape
    n1 = two_n1 // 2
    ng = n1 // FFT_G
    k0 = kcol0 // FFT_W
    blk = (None, n2, FFT_G, FFT_W)
    return pl.pallas_call(
        _fft_b_body,
        grid=(ng, C // FFT_W, P),
        in_specs=[
            pl.BlockSpec(blk, lambda g, c, p: (p, 0, g, c)),
            pl.BlockSpec(blk, lambda g, c, p: (p, 0, g + ng, c)),
            pl.BlockSpec((FFT_G, 2 * n2, FFT_W), lambda g, c, p: (g, 0, c + k0)),
            _resident(fb.shape),
            _resident(fbi.shape),
        ],
        out_specs=pl.BlockSpec((None, FFT_G, 2 * n2, FFT_W), lambda g, c, p: (p, g, 0, c)),
        out_shape=jax.ShapeDtypeStruct((P, n1, 2 * n2, C), F32),
        compiler_params=_cparams(("parallel", "parallel", "arbitrary"), 32),
        name="fft_b",
    )(s1, s1, kf, fb, fbi)


def _fft_c_body(zr_ref, zi_ref, fc_ref, u_ref, gate_ref, d_ref, o_ref):
    n1h = o_ref.shape[1]
    d = d_ref[...]
    for j in range(FFT_G):
        z = jnp.concatenate([zr_ref[:, j, :], zi_ref[:, j, :]], axis=0).astype(BF16)
        y = jnp.dot(fc_ref[j], z, preferred_element_type=F32)
        for b in range(2):
            yb = y[b * n1h:(b + 1) * n1h]
            o_ref[b, :, j, :] = gate_ref[b, :, j, :] * (yb + u_ref[b, :, j, :] * d)


def _fft_c(s2, fc, uv, gv, d, *, ucol0, gcol0):
    P, n1, two_n2, C = s2.shape
    n2 = two_n2 // 2
    n1h = n1 // 2
    ng = n2 // FFT_G
    u0 = ucol0 // FFT_W
    g0 = gcol0 // FFT_W
    zblk = (None, n1, FFT_G, FFT_W)
    tblk = (None, 2, n1h, FFT_G, FFT_W)
    return pl.pallas_call(
        _fft_c_body,
        grid=(ng, P, C // FFT_W),
        in_specs=[
            pl.BlockSpec(zblk, lambda g, p, c: (p, 0, g, c)),
            pl.BlockSpec(zblk, lambda g, p, c: (p, 0, g + ng, c)),
            pl.BlockSpec((FFT_G, 2 * n1h, 2 * n1), lambda g, p, c: (g, 0, 0)),
            pl.BlockSpec(tblk, lambda g, p, c: (p, 0, 0, g, c + u0)),
            pl.BlockSpec(tblk, lambda g, p, c: (p, 0, 0, g, c + g0)),
            pl.BlockSpec((1, FFT_W), lambda g, p, c: (0, c)),
        ],
        out_specs=pl.BlockSpec(tblk, lambda g, p, c: (p, 0, 0, g, c)),
        out_shape=jax.ShapeDtypeStruct((P, 2, n1h, n2, C), F32),
        compiler_params=_cparams(("arbitrary", "arbitrary", "arbitrary"), 32),
        name="fft_c",
    )(s2, s2, fc, uv, gv, d)


def _oproj_body(x_ref, a_ref, h_ref, ga_ref, gh_ref, wa_ref, wh_ref, o_ref):
    ma = _rms(a_ref[...], ga_ref[...]).astype(BF16)
    mh = _rms(h_ref[...], gh_ref[...]).astype(BF16)
    o_ref[...] = (x_ref[...] + jnp.dot(ma, wa_ref[...], preferred_element_type=F32)
                  + jnp.dot(mh, wh_ref[...], preferred_element_type=F32))


def _out_proj(x, a, h, ga, gh, wa, wh, *, tm=512):
    T, D = x.shape
    return pl.pallas_call(
        _oproj_body,
        grid=(T // tm,),
        in_specs=[
            pl.BlockSpec((tm, D), lambda i: (i, 0)),
            pl.BlockSpec((tm, D_ATTN), lambda i: (i, 0)),
            pl.BlockSpec((tm, D_HYENA), lambda i: (i, 0)),
            _resident((1, D_ATTN)), _resident((1, D_HYENA)),
            _resident(wa.shape), _resident(wh.shape),
        ],
        out_specs=pl.BlockSpec((tm, D), lambda i: (i, 0)),
        out_shape=jax.ShapeDtypeStruct((T, D), F32),
        compiler_params=_cparams(("parallel",), 48),
        name="oproj",
    )(x, a, h, ga, gh, wa, wh)


def _rope_tables(L):
    rows = L // GRID_W
    row = jnp.repeat(jnp.arange(rows, dtype=F32), GRID_W)
    col = jnp.tile(jnp.arange(GRID_W, dtype=F32), rows)
    inv = ROPE_THETA ** (-jnp.arange(0, ROPE_HALF, 2, dtype=F32) / ROPE_HALF)
    ar = row[:, None] * inv[None]
    ac = col[:, None] * inv[None]
    cr, sr, cc, sc = jnp.cos(ar), jnp.sin(ar), jnp.cos(ac), jnp.sin(ac)
    cos_t = jnp.concatenate([cr, cr, cc, cc], axis=-1)
    sin_t = jnp.concatenate([-sr, sr, -sc, sc], axis=-1)
    return cos_t, sin_t


def _filter_feats(L):
    t01 = jnp.linspace(0.0, 1.0, L, dtype=F32)[:, None]
    bands = (FILTER_EMB - 1) // 2
    fr = jnp.linspace(1e-4, bands - 1, bands, dtype=F32)[None]
    w = 2.0 * math.pi * jnp.arange(L, dtype=F32)[:, None] / L
    feats = jnp.concatenate([t01, jnp.cos(fr * w), -jnp.sin(fr * w)], axis=-1)
    return jnp.pad(feats, ((0, 0), (0, FILT_PAD - FILTER_EMB)))


def _dft_tables(L):
    n = 2 * L
    n2 = FFT_N2
    n1 = n // n2
    n1h = n1 // 2
    k1 = jnp.arange(n1, dtype=jnp.int32)[None, :, None]
    t1 = jnp.arange(n1h, dtype=jnp.int32)[None, None, :]
    t2 = jnp.arange(n2, dtype=jnp.int32)[:, None, None]
    ang = (-2.0 * math.pi / n) * ((k1 * (n2 * t1 + t2)) % n).astype(F32)
    cr, ci = jnp.cos(ang), jnp.sin(ang)
    fa = jnp.concatenate([jnp.concatenate([cr, -ci], -1), jnp.concatenate([ci, cr], -1)], -2)
    fa_re = jnp.concatenate([cr, ci], -2)
    crt, cit = jnp.swapaxes(cr, 1, 2), jnp.swapaxes(ci, 1, 2)
    fc = jnp.concatenate([jnp.concatenate([crt, cit], -1), jnp.concatenate([-cit, crt], -1)], -2) / n
    a = jnp.arange(n2, dtype=jnp.int32)
    ang2 = (-2.0 * math.pi / n2) * ((a[:, None] * a[None, :]) % n2).astype(F32)
    fr, fi = jnp.cos(ang2), jnp.sin(ang2)
    fb = jnp.concatenate([jnp.concatenate([fr, -fi], -1), jnp.concatenate([fi, fr], -1)], -2)
    fbi = jnp.concatenate([jnp.concatenate([fr, fi], -1), jnp.concatenate([-fi, fr], -1)], -2)
    return (fa.astype(BF16), fa_re.astype(BF16), fb.astype(BF16), fbi.astype(BF16), fc.astype(BF16))


def _trunk(x, p):
    B, L, D = x.shape
    T = B * L
    P = B // 2
    n2 = FFT_N2
    n1h = L // n2
    C = D_HYENA

    x0 = x.reshape(T, D)
    x1 = _ffn(x0, p["ffn1_norm"], p["ffn1_w13"], p["ffn1_w2"], p["final_norm"], final_norm=False)

    cos_t, sin_t = _rope_tables(L)
    q, k, v, hy = _mix_in(x1, p["mix_norm"], p["wq"], p["wkv"], p["wh"], p["q_norm"], p["k_norm"],
                          cos_t, sin_t, L=L)
    attn = _attention(q.reshape(B, L, D_ATTN), k.reshape(B, L, D_KV), v.reshape(B, L, D_KV))

    hyc = _short_conv(hy.reshape(B, L, 3 * C), p["conv_w"], p["conv_b"])
    hyv = hyc.reshape(P, 2, n1h, n2, 3 * C)

    fa, fa_re, fb, fbi, fc = _dft_tables(L)
    hf, asum = _filters(_filter_feats(L), p["filt_w1"], p["filt_b1"], p["filt_w2"], p["filt_b2"],
                        p["filt_w3"], p["filt_freq"], p["decay"])
    s1f = _fft_a_real(hf.reshape(n1h, n2, 4 * C), fa_re)
    kf = _fft_b_filt(s1f, fb, asum)

    s1 = _fft_a(hyv, fa, ncols=C, col0=0)
    s2 = _fft_b(s1, kf, fb, fbi, kcol0=0)
    z = _fft_c(s2, fc, hyv, hyv, p["bias0"], ucol0=0, gcol0=C)
    s1 = _fft_a(z, fa, ncols=C, col0=0)
    s2 = _fft_b(s1, kf, fb, fbi, kcol0=C)
    ho = _fft_c(s2, fc, z, hyv, p["bias1"], ucol0=0, gcol0=2 * C)

    x2 = _out_proj(x1, attn.reshape(T, D_ATTN), ho.reshape(T, C), p["gon_a"], p["gon_h"],
                   p["wo_a"], p["wo_h"])
    x3 = _ffn(x2, p["ffn2_norm"], p["ffn2_w13"], p["ffn2_w2"], p["final_norm"], final_norm=True)
    return x3.reshape(B, L, D)


def kernel(x_prompt, x_sample, ffn1_norm, ffn1_w13, ffn1_w2, mix_norm, w_in, q_norm, k_norm, conv_w, conv_b, filt_w1, filt_b1, filt_w2, filt_b2, filt_w3, filt_freq, hyena_decay, hyena_bias, group_out_norm, w_out, ffn2_norm, ffn2_w13, ffn2_w2, final_norm):
    hp = FILT_PAD - FILTER_HIDDEN
    w_in0 = w_in[0]
    p = {
        "ffn1_norm": ffn1_norm[0][None], "ffn2_norm": ffn2_norm[0][None],
        "ffn1_w13": ffn1_w13[0].astype(BF16), "ffn1_w2": ffn1_w2[0].astype(BF16),
        "ffn2_w13": ffn2_w13[0].astype(BF16), "ffn2_w2": ffn2_w2[0].astype(BF16),
        "final_norm": final_norm[None],
        "mix_norm": mix_norm[0][None],
        "wq": w_in0[:, :D_ATTN].astype(BF16),
        "wkv": w_in0[:, D_ATTN:D_ATTN + 2 * D_KV].astype(BF16),
        "wh": w_in0[:, D_ATTN + 2 * D_KV:].astype(BF16),
        "q_norm": q_norm[0][None], "k_norm": k_norm[0][None],
        "conv_w": conv_w[0], "conv_b": conv_b[0][None],
        "filt_w1": jnp.pad(filt_w1[0], ((0, FILT_PAD - FILTER_EMB), (0, hp))),
        "filt_b1": jnp.pad(filt_b1[0], (0, hp))[None],
        "filt_w2": jnp.pad(filt_w2[0], ((0, hp), (0, hp))),
        "filt_b2": jnp.pad(filt_b2[0], (0, hp))[None],
        "filt_w3": jnp.pad(filt_w3[0], ((0, hp), (0, 0))),
        "filt_freq": jnp.pad(filt_freq[0], (0, hp))[None],
        "decay": hyena_decay[0].reshape(1, 4 * D_HYENA),
        "bias0": hyena_bias[0, 0][None], "bias1": hyena_bias[0, 1][None],
        "gon_a": group_out_norm[0, :D_ATTN][None], "gon_h": group_out_norm[0, D_ATTN:][None],
        "wo_a": w_out[0, :D_ATTN].astype(BF16), "wo_h": w_out[0, D_ATTN:].astype(BF16),
    }
    return (_trunk(x_prompt, p), _trunk(x_sample, p))
```

```python
import functools
import math

import jax
import jax.numpy as jnp
from jax import lax
from jax.experimental import pallas as pl
from jax.experimental.pallas import tpu as pltpu

F32 = jnp.float32
BF16 = jnp.bfloat16

D_MODEL = 2048
GRID_W = 64
D_ATTN = D_MODEL // 2
D_HYENA = D_MODEL - D_ATTN
HEAD_DIM = 128
N_Q_HEADS = D_ATTN // HEAD_DIM
N_KV_HEADS = 2
Q_PER_KV = N_Q_HEADS // N_KV_HEADS
ROPE_HALF = HEAD_DIM // 2
ROPE_THETA = 10000.0
FILTER_EMB = 33
FILTER_HIDDEN = 64
D_FF = 5632
EPS = 1e-6
D_KV = N_KV_HEADS * HEAD_DIM

LANES = 128
FILT_PAD = 128
FFT_N2 = 128
FFT_W = 256
FFT_G = 8
MIB = 1024 * 1024
LOG2_E = math.log2(math.e)


def _cparams(semantics, vmem_mib):
    return pltpu.CompilerParams(dimension_semantics=semantics,
                                vmem_limit_bytes=vmem_mib * MIB)


def _rms(x, g):
    ms = jnp.mean(x * x, axis=-1, keepdims=True)
    return x * lax.rsqrt(ms + EPS) * g


def _resident(shape):
    nd = len(shape)
    return pl.BlockSpec(shape, lambda *_: (0,) * nd, pipeline_mode=pl.Buffered(1))


def _ffn_body(x_ref, g_ref, wg_ref, wu_ref, w2_ref, gf_ref, o_ref, xn_ref, *, nj, final_norm):
    j = pl.program_id(1)

    @pl.when(j == 0)
    def _init():
        x = x_ref[...]
        xn_ref[...] = _rms(x, g_ref[...]).astype(BF16)
        o_ref[...] = x

    xn = xn_ref[...]
    hg = jnp.dot(xn, wg_ref[...], preferred_element_type=F32)
    hu = jnp.dot(xn, wu_ref[...], preferred_element_type=F32)
    a = (hg * (0.5 / (1.0 + jnp.exp(-hg))) * hu).astype(BF16)
    o_ref[...] += jnp.dot(a, w2_ref[...], preferred_element_type=F32)

    if final_norm:
        @pl.when(j == nj - 1)
        def _fin():
            o_ref[...] = _rms(o_ref[...], gf_ref[...])


def _ffn(x, g, w13, w2, gf, *, final_norm, tm=512, tf=512):
    T, D = x.shape
    ff = w2.shape[0]
    nj = ff // tf
    body = functools.partial(_ffn_body, nj=nj, final_norm=final_norm)
    return pl.pallas_call(
        body,
        grid=(T // tm, nj),
        in_specs=[
            pl.BlockSpec((tm, D), lambda i, j: (i, 0)),
            pl.BlockSpec((1, D), lambda i, j: (0, 0)),
            pl.BlockSpec((D, tf), lambda i, j: (0, j)),
            pl.BlockSpec((D, tf), lambda i, j: (0, j + nj)),
            pl.BlockSpec((tf, D), lambda i, j: (j, 0)),
            pl.BlockSpec((1, D), lambda i, j: (0, 0)),
        ],
        out_specs=pl.BlockSpec((tm, D), lambda i, j: (i, 0)),
        out_shape=jax.ShapeDtypeStruct((T, D), F32),
        scratch_shapes=[pltpu.VMEM((tm, D), BF16)],
        compiler_params=_cparams(("parallel", "arbitrary"), 48),
        name="ffn",
    )(x, g, w13, w13, w2, gf)


def _mix_body(x_ref, g_ref, wq_ref, wkv_ref, wh_ref, qn_ref, kn_ref, cos_ref, sin_ref,
              q_ref, k_ref, v_ref, hy_ref):
    h = _rms(x_ref[...], g_ref[...]).astype(BF16)
    tm = h.shape[0]
    lane = lax.broadcasted_iota(jnp.int32, (tm, HEAD_DIM), 1)
    low_half = (lane % (2 * (ROPE_HALF // 2))) < (ROPE_HALF // 2)
    c = cos_ref[...]
    s = sin_ref[...]

    def head(p, gain, scale):
        y = _rms(p, gain)
        partner = jnp.where(low_half, pltpu.roll(y, HEAD_DIM - ROPE_HALF // 2, 1),
                            pltpu.roll(y, ROPE_HALF // 2, 1))
        y = y * c + partner * s
        return y if scale is None else y * scale

    q = jnp.dot(h, wq_ref[...], preferred_element_type=F32)
    for hd in range(N_Q_HEADS):
        sl = slice(hd * HEAD_DIM, (hd + 1) * HEAD_DIM)
        q_ref[:, sl] = head(q[:, sl], qn_ref[...], HEAD_DIM ** -0.5 * LOG2_E).astype(BF16)
    kv = jnp.dot(h, wkv_ref[...], preferred_element_type=F32)
    for hd in range(N_KV_HEADS):
        sl = slice(hd * HEAD_DIM, (hd + 1) * HEAD_DIM)
        k_ref[:, sl] = head(kv[:, sl], kn_ref[...], None).astype(BF16)
    v_ref[...] = kv[:, D_KV:].astype(BF16)
    hy_ref[...] = jnp.dot(h, wh_ref[...], preferred_element_type=F32)


def _mix_in(x, g, wq, wkv, wh, qn, kn, cos_t, sin_t, *, L, tm=256):
    T, D = x.shape
    npos = L // tm
    return pl.pallas_call(
        _mix_body,
        grid=(T // tm,),
        in_specs=[
            pl.BlockSpec((tm, D), lambda i: (i, 0)),
            _resident((1, D)),
            _resident(wq.shape),
            _resident(wkv.shape),
            _resident(wh.shape),
            _resident((1, HEAD_DIM)),
            _resident((1, HEAD_DIM)),
            pl.BlockSpec((tm, HEAD_DIM), lambda i: (i % npos, 0)),
            pl.BlockSpec((tm, HEAD_DIM), lambda i: (i % npos, 0)),
        ],
        out_specs=[
            pl.BlockSpec((tm, D_ATTN), lambda i: (i, 0)),
            pl.BlockSpec((tm, D_KV), lambda i: (i, 0)),
            pl.BlockSpec((tm, D_KV), lambda i: (i, 0)),
            pl.BlockSpec((tm, 3 * D_HYENA), lambda i: (i, 0)),
        ],
        out_shape=[
            jax.ShapeDtypeStruct((T, D_ATTN), BF16),
            jax.ShapeDtypeStruct((T, D_KV), BF16),
            jax.ShapeDtypeStruct((T, D_KV), BF16),
            jax.ShapeDtypeStruct((T, 3 * D_HYENA), F32),
        ],
        compiler_params=_cparams(("parallel",), 48),
        name="mix_in",
    )(x, g, wq, wkv, wh, qn, kn, cos_t, sin_t)


ATTN_ROWS = 32


ATTN_RING = 4


def _attn_body(q_ref, k_ref, v_ref, o_ref, vt_ref, *bufs, tq, tk, nk):
    s_refs, p_refs = bufs[:ATTN_RING], bufs[ATTN_RING:]

    @pl.when(pl.program_id(2) == 0)
    def _transpose_v():
        for c in range(nk):
            vt_ref[c] = v_ref[pl.ds(c * tk, tk), :].astype(F32).T.astype(BF16)

    q = jnp.concatenate([q_ref[:, h * HEAD_DIM:(h + 1) * HEAD_DIM] for h in range(Q_PER_KV)], axis=0)
    m_cols = Q_PER_KV * tq
    nchunk = tk // ATTN_ROWS

    def stage_a(i, s_ref):
        kc = k_ref[pl.ds(pl.multiple_of(i * tk, tk), tk), :]
        s_ref[...] = lax.dot_general(kc, q, (((1,), (1,)), ((), ())), preferred_element_type=F32)

    def stage_b(s_ref, p_ref, m, l):
        rows = lambda c: slice(c * ATTN_ROWS, (c + 1) * ATTN_ROWS)
        mx = s_ref[rows(0), :]
        for c in range(1, nchunk):
            mx = jnp.maximum(mx, s_ref[rows(c), :])
        m_new = jnp.maximum(m, jnp.max(mx, axis=0, keepdims=True))
        alpha = jnp.exp2(m - m_new)
        psum = None
        for c in range(nchunk):
            p = jnp.exp2(s_ref[rows(c), :] - m_new)
            psum = p if psum is None else psum + p
            p_ref[rows(c), :] = p.astype(BF16)
        return m_new, alpha * l + jnp.sum(psum, axis=0, keepdims=True), alpha

    def stage_d(i, p_ref, alpha, acc):
        return alpha * acc + jnp.dot(vt_ref[i], p_ref[...], preferred_element_type=F32)

    m = jnp.full((1, m_cols), -jnp.inf, F32)
    l = jnp.zeros((1, m_cols), F32)
    acc = jnp.zeros((HEAD_DIM, m_cols), F32)
    def ring(i, m, l, al0, al1, acc, n_a, n_b):
        alphas = {0: al0, 1: al1}
        for u in range(ATTN_RING):
            if u < n_a:
                stage_a(i + u + 4, s_refs[u])
            if u < n_b:
                v = (u + 2) % ATTN_RING
                m, l, alphas[u + 2] = stage_b(s_refs[v], p_refs[v], m, l)
            acc = stage_d(i + u, p_refs[u], alphas[u], acc)
        return m, l, alphas.get(4), alphas.get(5), acc

    for u in range(ATTN_RING):
        stage_a(u, s_refs[u])
    m, l, al0 = stage_b(s_refs[0], p_refs[0], m, l)
    m, l, al1 = stage_b(s_refs[1], p_refs[1], m, l)

    def trip(j, carry):
        return ring(ATTN_RING * j, *carry, ATTN_RING, ATTN_RING)

    m, l, al0, al1, acc = lax.fori_loop(0, nk // ATTN_RING - 1, trip, (m, l, al0, al1, acc))
    _, l, _, _, acc = ring(nk - ATTN_RING, m, l, al0, al1, acc, 0, 2)
    o = acc / l
    for h in range(Q_PER_KV):
        o_ref[:, h * HEAD_DIM:(h + 1) * HEAD_DIM] = o[:, h * tq:(h + 1) * tq].T


def _attention(q, k, v, *, tq=128, tk=512):
    B, L, _ = q.shape
    qw = Q_PER_KV * HEAD_DIM
    nk = L // tk
    assert nk % ATTN_RING == 0 and nk >= 2 * ATTN_RING
    body = functools.partial(_attn_body, tq=tq, tk=tk, nk=nk)
    m_cols = Q_PER_KV * tq
    return pl.pallas_call(
        body,
        grid=(B, N_KV_HEADS, L // tq),
        in_specs=[
            pl.BlockSpec((None, tq, qw), lambda b, g, i: (b, i, g)),
            pl.BlockSpec((None, L, HEAD_DIM), lambda b, g, i: (b, 0, g)),
            pl.BlockSpec((None, L, HEAD_DIM), lambda b, g, i: (b, 0, g)),
        ],
        out_specs=pl.BlockSpec((None, tq, qw), lambda b, g, i: (b, i, g)),
        out_shape=jax.ShapeDtypeStruct((B, L, D_ATTN), F32),
        scratch_shapes=([pltpu.VMEM((nk, HEAD_DIM, tk), BF16)]
                        + [pltpu.VMEM((tk, m_cols), F32)] * ATTN_RING
                        + [pltpu.VMEM((tk, m_cols), BF16)] * ATTN_RING),
        compiler_params=_cparams(("parallel", "parallel", "arbitrary"), 32),
        name="attn",
    )(q, k, v)


def _sconv_body(x_ref, w_ref, b_ref, o_ref, *, L, R):
    w0 = w_ref[0:1, :]
    w1 = w_ref[1:2, :]
    w2 = w_ref[2:3, :]
    b = b_ref[...]
    nchunk = L // R
    rid = lax.broadcasted_iota(jnp.int32, (R, x_ref.shape[-1]), 0)

    def chunk(r, carry):
        base = pl.multiple_of(r * R, R)
        cur = x_ref[pl.ds(base, R), :]
        before = x_ref[pl.ds(jnp.maximum(base - 8, 0), 8), :][7:8, :]
        after = x_ref[pl.ds(jnp.minimum(base + R, L - 8), 8), :][0:1, :]
        before = jnp.where(r > 0, before, 0.0)
        after = jnp.where(r < nchunk - 1, after, 0.0)
        xm = jnp.where(rid == 0, before, pltpu.roll(cur, 1, 0))
        xp = jnp.where(rid == R - 1, after, pltpu.roll(cur, R - 1, 0))
        o_ref[pl.ds(base, R), :] = xm * w0 + cur * w1 + xp * w2 + b
        return carry

    lax.fori_loop(0, nchunk, chunk, 0)


def _short_conv(hy, w, b, *, R=512, wc=LANES):
    B, L, C = hy.shape
    body = functools.partial(_sconv_body, L=L, R=R)
    return pl.pallas_call(
        body,
        grid=(B, C // wc),
        in_specs=[
            pl.BlockSpec((None, L, wc), lambda bi, c: (bi, 0, c)),
            pl.BlockSpec((3, wc), lambda bi, c: (0, c)),
            pl.BlockSpec((1, wc), lambda bi, c: (0, c)),
        ],
        out_specs=pl.BlockSpec((None, L, wc), lambda bi, c: (bi, 0, c)),
        out_shape=jax.ShapeDtypeStruct((B, L, C), F32),
        compiler_params=_cparams(("parallel", "parallel"), 32),
        name="sconv",
    )(hy, w, b)


def _filt_body(feat_ref, w1_ref, b1_ref, w2_ref, b2_ref, w3_ref, fq_ref, dec_ref, hf_ref, asum_ref, *, tl):
    i = pl.program_id(0)
    hp = lax.Precision.HIGHEST
    fq = fq_ref[...]
    feat = feat_ref[...]
    h = jnp.sin(fq * (jnp.dot(feat, w1_ref[...], precision=hp, preferred_element_type=F32) + b1_ref[...]))
    h = jnp.sin(fq * (jnp.dot(h, w2_ref[...], precision=hp, preferred_element_type=F32) + b2_ref[...]))
    h3 = jnp.dot(h, w3_ref[...], precision=hp, preferred_element_type=F32)
    t01 = feat[:, 0:1]
    hf = h3 * jnp.exp(-t01 * jnp.abs(dec_ref[...]))
    ncol = hf.shape[1]
    row = lax.broadcasted_iota(jnp.int32, hf.shape, 0) + i * tl
    col = lax.broadcasted_iota(jnp.int32, hf.shape, 1)
    hf = jnp.where((row == 0) & (col >= ncol // 2), 0.0, hf)
    hf_ref[...] = hf
    part = jnp.sum(jnp.abs(hf), axis=0, keepdims=True)

    @pl.when(i == 0)
    def _first():
        asum_ref[...] = part

    @pl.when(i > 0)
    def _rest():
        asum_ref[...] += part


def _filters(feats, w1, b1, w2, b2, w3, fq, dec, *, tl=256):
    L = feats.shape[0]
    nc = w3.shape[1]
    body = functools.partial(_filt_body, tl=tl)
    return pl.pallas_call(
        body,
        grid=(L // tl,),
        in_specs=[
            pl.BlockSpec((tl, FILT_PAD), lambda i: (i, 0)),
            _resident(w1.shape), _resident(b1.shape), _resident(w2.shape), _resident(b2.shape),
            _resident(w3.shape), _resident(fq.shape), _resident(dec.shape),
        ],
        out_specs=[
            pl.BlockSpec((tl, nc), lambda i: (i, 0)),
            pl.BlockSpec((1, nc), lambda i: (0, 0)),
        ],
        out_shape=[jax.ShapeDtypeStruct((L, nc), F32), jax.ShapeDtypeStruct((1, nc), F32)],
        compiler_params=_cparams(("arbitrary",), 48),
        name="filt",
    )(feats, w1, b1, w2, b2, w3, fq, dec)


def _fft_a_body(x_ref, fa_ref, o_ref):
    for j in range(FFT_G):
        xs = jnp.concatenate([x_ref[0, :, j, :], x_ref[1, :, j, :]], axis=0).astype(BF16)
        o_ref[j] = jnp.dot(fa_ref[j], xs, preferred_element_type=F32)


def _fft_a(xv, fa, *, ncols, col0=0):
    P, _, n1h, n2, _ = xv.shape
    two_n1 = fa.shape[1]
    c0 = col0 // FFT_W
    return pl.pallas_call(
        _fft_a_body,
        grid=(n2 // FFT_G, P, ncols // FFT_W),
        in_specs=[
            pl.BlockSpec((None, 2, n1h, FFT_G, FFT_W), lambda g, p, c: (p, 0, 0, g, c + c0)),
            pl.BlockSpec((FFT_G, two_n1, 2 * n1h), lambda g, p, c: (g, 0, 0)),
        ],
        out_specs=pl.BlockSpec((None, FFT_G, two_n1, FFT_W), lambda g, p, c: (p, g, 0, c)),
        out_shape=jax.ShapeDtypeStruct((P, n2, two_n1, ncols), F32),
        compiler_params=_cparams(("arbitrary", "arbitrary", "arbitrary"), 32),
        name="fft_a",
    )(xv, fa)


def _fft_a_real_body(x_ref, fa_ref, o_ref):
    for j in range(FFT_G):
        o_ref[j] = jnp.dot(fa_ref[j], x_ref[:, j, :].astype(BF16), preferred_element_type=F32)


def _fft_a_real(xv, fa_re):
    n1h, n2, C = xv.shape
    two_n1 = fa_re.shape[1]
    return pl.pallas_call(
        _fft_a_real_body,
        grid=(n2 // FFT_G, C // FFT_W),
        in_specs=[
            pl.BlockSpec((n1h, FFT_G, FFT_W), lambda g, c: (0, g, c)),
            pl.BlockSpec((FFT_G, two_n1, n1h), lambda g, c: (g, 0, 0)),
        ],
        out_specs=pl.BlockSpec((FFT_G, two_n1, FFT_W), lambda g, c: (g, 0, c)),
        out_shape=jax.ShapeDtypeStruct((n2, two_n1, C), F32),
        compiler_params=_cparams(("arbitrary", "arbitrary"), 32),
        name="fft_a_real",
    )(xv, fa_re)


def _fft_b_filt_body(fr_ref, fi_ref, br_ref, bi_ref, fb_ref, asf_ref, asb_ref, o_ref):
    n2 = FFT_N2
    inv = 1.0 / (asf_ref[...] + asb_ref[...])
    fb = fb_ref[...]
    for j in range(FFT_G):
        yf = jnp.concatenate([fr_ref[:, j, :], fi_ref[:, j, :]], axis=0).astype(BF16)
        yb = jnp.concatenate([br_ref[:, j, :], bi_ref[:, j, :]], axis=0).astype(BF16)
        u = jnp.dot(fb, yf, preferred_element_type=F32)
        w = jnp.dot(fb, yb, preferred_element_type=F32)
        kr = (u[:n2] + w[:n2]) * inv
        ki = (u[n2:] - w[n2:]) * inv
        o_ref[j] = jnp.concatenate([kr, ki], axis=0)


def _fft_b_filt(s1f, fb, asum):
    n2, two_n1, c2 = s1f.shape
    n1 = two_n1 // 2
    cf = c2 // 2
    nb = cf // FFT_W
    ng = n1 // FFT_G
    blk = (n2, FFT_G, FFT_W)
    return pl.pallas_call(
        _fft_b_filt_body,
        grid=(ng, nb),
        in_specs=[
            pl.BlockSpec(blk, lambda g, c: (0, g, c)),
            pl.BlockSpec(blk, lambda g, c: (0, g + ng, c)),
            pl.BlockSpec(blk, lambda g, c: (0, g, c + nb)),
            pl.BlockSpec(blk, lambda g, c: (0, g + ng, c + nb)),
            _resident(fb.shape),
            pl.BlockSpec((1, FFT_W), lambda g, c: (0, c)),
            pl.BlockSpec((1, FFT_W), lambda g, c: (0, c + nb)),
        ],
        out_specs=pl.BlockSpec((FFT_G, 2 * n2, FFT_W), lambda g, c: (g, 0, c)),
        out_shape=jax.ShapeDtypeStruct((n1, 2 * n2, cf), F32),
        compiler_params=_cparams(("parallel", "parallel"), 32),
        name="fft_b_filt",
    )(s1f, s1f, s1f, s1f, fb, asum, asum)


def _fft_b_body(sr_ref, si_ref, kf_ref, fb_ref, fbi_ref, o_ref):
    n2 = FFT_N2
    fb = fb_ref[...]
    fbi = fbi_ref[...]
    for j in range(FFT_G):
        y = jnp.concatenate([sr_ref[:, j, :], si_ref[:, j, :]], axis=0).astype(BF16)
        x = jnp.dot(fb, y, preferred_element_type=F32)
        xr, xi = x[:n2], x[n2:]
        kr, ki = kf_ref[j, :n2, :], kf_ref[j, n2:, :]
        pm = jnp.concatenate([xr * kr - xi * ki, xr * ki + xi * kr], axis=0).astype(BF16)
        o_ref[j] = jnp.dot(fbi, pm, preferred_element_type=F32)


def _fft_b(s1, kf, fb, fbi, *, kcol0):
    P, n2, two_n1, C = s1.shape
    n1 = two_n1 // 2
    ng = n1 // FFT_G
    k0 = kcol0 // FFT_W
    blk = (None, n2, FFT_G, FFT_W)
    return pl.pallas_call(
        _fft_b_body,
        grid=(ng, C // FFT_W, P),
        in_specs=[
            pl.BlockSpec(blk, lambda g, c, p: (p, 0, g, c)),
            pl.BlockSpec(blk, lambda g, c, p: (p, 0, g + ng, c)),
            pl.BlockSpec((FFT_G, 2 * n2, FFT_W), lambda g, c, p: (g, 0, c + k0)),
            _resident(fb.shape),
            _resident(fbi.shape),
        ],
        out_specs=pl.BlockSpec((None, FFT_G, 2 * n2, FFT_W), lambda g, c, p: (p, g, 0, c)),
        out_shape=jax.ShapeDtypeStruct((P, n1, 2 * n2, C), F32),
        compiler_params=_cparams(("parallel", "parallel", "arbitrary"), 32),
        name="fft_b",
    )(s1, s1, kf, fb, fbi)


def _fft_c_body(zr_ref, zi_ref, fc_ref, u_ref, gate_ref, d_ref, o_ref):
    n1h = o_ref.shape[1]
    d = d_ref[...]
    for j in range(FFT_G):
        z = jnp.concatenate([zr_ref[:, j, :], zi_ref[:, j, :]], axis=0).astype(BF16)
        y = jnp.dot(fc_ref[j], z, preferred_element_type=F32)
        for b in range(2):
            yb = y[b * n1h:(b + 1) * n1h]
            o_ref[b, :, j, :] = gate_ref[b, :, j, :] * (yb + u_ref[b, :, j, :] * d)


def _fft_c(s2, fc, uv, gv, d, *, ucol0, gcol0):
    P, n1, two_n2, C = s2.shape
    n2 = two_n2 // 2
    n1h = n1 // 2
    ng = n2 // FFT_G
    u0 = ucol0 // FFT_W
    g0 = gcol0 // FFT_W
    zblk = (None, n1, FFT_G, FFT_W)
    tblk = (None, 2, n1h, FFT_G, FFT_W)
    return pl.pallas_call(
        _fft_c_body,
        grid=(ng, P, C // FFT_W),
        in_specs=[
            pl.BlockSpec(zblk, lambda g, p, c: (p, 0, g, c)),
            pl.BlockSpec(zblk, lambda g, p, c: (p, 0, g + ng, c)),
            pl.BlockSpec((FFT_G, 2 * n1h, 2 * n1), lambda g, p, c: (g, 0, 0)),
            pl.BlockSpec(tblk, lambda g, p, c: (p, 0, 0, g, c + u0)),
            pl.BlockSpec(tblk, lambda g, p, c: (p, 0, 0, g, c + g0)),
            pl.BlockSpec((1, FFT_W), lambda g, p, c: (0, c)),
        ],
        out_specs=pl.BlockSpec(tblk, lambda g, p, c: (p, 0, 0, g, c)),
        out_shape=jax.ShapeDtypeStruct((P, 2, n1h, n2, C), F32),
        compiler_params=_cparams(("arbitrary", "arbitrary", "arbitrary"), 32),
        name="fft_c",
    )(s2, s2, fc, uv, gv, d)


def _oproj_body(x_ref, a_ref, h_ref, ga_ref, gh_ref, wa_ref, wh_ref, o_ref):
    ma = _rms(a_ref[...], ga_ref[...]).astype(BF16)
    mh = _rms(h_ref[...], gh_ref[...]).astype(BF16)
    o_ref[...] = (x_ref[...] + jnp.dot(ma, wa_ref[...], preferred_element_type=F32)
                  + jnp.dot(mh, wh_ref[...], preferred_element_type=F32))


def _out_proj(x, a, h, ga, gh, wa, wh, *, tm=512):
    T, D = x.shape
    return pl.pallas_call(
        _oproj_body,
        grid=(T // tm,),
        in_specs=[
            pl.BlockSpec((tm, D), lambda i: (i, 0)),
            pl.BlockSpec((tm, D_ATTN), lambda i: (i, 0)),
            pl.BlockSpec((tm, D_HYENA), lambda i: (i, 0)),
            _resident((1, D_ATTN)), _resident((1, D_HYENA)),
            _resident(wa.shape), _resident(wh.shape),
        ],
        out_specs=pl.BlockSpec((tm, D), lambda i: (i, 0)),
        out_shape=jax.ShapeDtypeStruct((T, D), F32),
        compiler_params=_cparams(("parallel",), 48),
        name="oproj",
    )(x, a, h, ga, gh, wa, wh)


def _rope_tables(L):
    rows = L // GRID_W
    row = jnp.repeat(jnp.arange(rows, dtype=F32), GRID_W)
    col = jnp.tile(jnp.arange(GRID_W, dtype=F32), rows)
    inv = ROPE_THETA ** (-jnp.arange(0, ROPE_HALF, 2, dtype=F32) / ROPE_HALF)
    ar = row[:, None] * inv[None]
    ac = col[:, None] * inv[None]
    cr, sr, cc, sc = jnp.cos(ar), jnp.sin(ar), jnp.cos(ac), jnp.sin(ac)
    cos_t = jnp.concatenate([cr, cr, cc, cc], axis=-1)
    sin_t = jnp.concatenate([-sr, sr, -sc, sc], axis=-1)
    return cos_t, sin_t


def _filter_feats(L):
    t01 = jnp.linspace(0.0, 1.0, L, dtype=F32)[:, None]
    bands = (FILTER_EMB - 1) // 2
    fr = jnp.linspace(1e-4, bands - 1, bands, dtype=F32)[None]
    w = 2.0 * math.pi * jnp.arange(L, dtype=F32)[:, None] / L
    feats = jnp.concatenate([t01, jnp.cos(fr * w), -jnp.sin(fr * w)], axis=-1)
    return jnp.pad(feats, ((0, 0), (0, FILT_PAD - FILTER_EMB)))


def _dft_tables(L):
    n = 2 * L
    n2 = FFT_N2
    n1 = n // n2
    n1h = n1 // 2
    k1 = jnp.arange(n1, dtype=jnp.int32)[None, :, None]
    t1 = jnp.arange(n1h, dtype=jnp.int32)[None, None, :]
    t2 = jnp.arange(n2, dtype=jnp.int32)[:, None, None]
    ang = (-2.0 * math.pi / n) * ((k1 * (n2 * t1 + t2)) % n).astype(F32)
    cr, ci = jnp.cos(ang), jnp.sin(ang)
    fa = jnp.concatenate([jnp.concatenate([cr, -ci], -1), jnp.concatenate([ci, cr], -1)], -2)
    fa_re = jnp.concatenate([cr, ci], -2)
    crt, cit = jnp.swapaxes(cr, 1, 2), jnp.swapaxes(ci, 1, 2)
    fc = jnp.concatenate([jnp.concatenate([crt, cit], -1), jnp.concatenate([-cit, crt], -1)], -2) / n
    a = jnp.arange(n2, dtype=jnp.int32)
    ang2 = (-2.0 * math.pi / n2) * ((a[:, None] * a[None, :]) % n2).astype(F32)
    fr, fi = jnp.cos(ang2), jnp.sin(ang2)
    fb = jnp.concatenate([jnp.concatenate([fr, -fi], -1), jnp.concatenate([fi, fr], -1)], -2)
    fbi = jnp.concatenate([jnp.concatenate([fr, fi], -1), jnp.concatenate([-fi, fr], -1)], -2)
    return (fa.astype(BF16), fa_re.astype(BF16), fb.astype(BF16), fbi.astype(BF16), fc.astype(BF16))


def _trunk(x, p):
    B, L, D = x.shape
    T = B * L
    P = B // 2
    n2 = FFT_N2
    n1h = L // n2
    C = D_HYENA

    x0 = x.reshape(T, D)
    x1 = _ffn(x0, p["ffn1_norm"], p["ffn1_w13"], p["ffn1_w2"], p["final_norm"], final_norm=False)

    cos_t, sin_t = _rope_tables(L)
    q, k, v, hy = _mix_in(x1, p["mix_norm"], p["wq"], p["wkv"], p["wh"], p["q_norm"], p["k_norm"],
                          cos_t, sin_t, L=L)
    attn = _attention(q.reshape(B, L, D_ATTN), k.reshape(B, L, D_KV), v.reshape(B, L, D_KV))

    hyc = _short_conv(hy.reshape(B, L, 3 * C), p["conv_w"], p["conv_b"])
    hyv = hyc.reshape(P, 2, n1h, n2, 3 * C)

    fa, fa_re, fb, fbi, fc = _dft_tables(L)
    hf, asum = _filters(_filter_feats(L), p["filt_w1"], p["filt_b1"], p["filt_w2"], p["filt_b2"],
                        p["filt_w3"], p["filt_freq"], p["decay"])
    s1f = _fft_a_real(hf.reshape(n1h, n2, 4 * C), fa_re)
    kf = _fft_b_filt(s1f, fb, asum)

    s1 = _fft_a(hyv, fa, ncols=C, col0=0)
    s2 = _fft_b(s1, kf, fb, fbi, kcol0=0)
    z = _fft_c(s2, fc, hyv, hyv, p["bias0"], ucol0=0, gcol0=C)
    s1 = _fft_a(z, fa, ncols=C, col0=0)
    s2 = _fft_b(s1, kf, fb, fbi, kcol0=C)
    ho = _fft_c(s2, fc, z, hyv, p["bias1"], ucol0=0, gcol0=2 * C)

    x2 = _out_proj(x1, attn.reshape(T, D_ATTN), ho.reshape(T, C), p["gon_a"], p["gon_h"],
                   p["wo_a"], p["wo_h"])
    x3 = _ffn(x2, p["ffn2_norm"], p["ffn2_w13"], p["ffn2_w2"], p["final_norm"], final_norm=True)
    return x3.reshape(B, L, D)


def kernel(x_prompt, x_sample, ffn1_norm, ffn1_w13, ffn1_w2, mix_norm, w_in, q_norm, k_norm, conv_w, conv_b, filt_w1, filt_b1, filt_w2, filt_b2, filt_w3, filt_freq, hyena_decay, hyena_bias, group_out_norm, w_out, ffn2_norm, ffn2_w13, ffn2_w2, final_norm):
    hp = FILT_PAD - FILTER_HIDDEN
    w_in0 = w_in[0]
    p = {
        "ffn1_norm": ffn1_norm[0][None], "ffn2_norm": ffn2_norm[0][None],
        "ffn1_w13": ffn1_w13[0].astype(BF16), "ffn1_w2": ffn1_w2[0].astype(BF16),
        "ffn2_w13": ffn2_w13[0].astype(BF16), "ffn2_w2": ffn2_w2[0].astype(BF16),
        "final_norm": final_norm[None],
        "mix_norm": mix_norm[0][None],
        "wq": w_in0[:, :D_ATTN].astype(BF16),
        "wkv": w_in0[:, D_ATTN:D_ATTN + 2 * D_KV].astype(BF16),
        "wh": w_in0[:, D_ATTN + 2 * D_KV:].astype(BF16),
        "q_norm": q_norm[0][None], "k_norm": k_norm[0][None],
        "conv_w": conv_w[0], "conv_b": conv_b[0][None],
        "filt_w1": jnp.pad(filt_w1[0], ((0, FILT_PAD - FILTER_EMB), (0, hp))),
        "filt_b1": jnp.pad(filt_b1[0], (0, hp))[None],
        "filt_w2": jnp.pad(filt_w2[0], ((0, hp), (0, hp))),
        "filt_b2": jnp.pad(filt_b2[0], (0, hp))[None],
        "filt_w3": jnp.pad(filt_w3[0], ((0, hp), (0, 0))),
        "filt_freq": jnp.pad(filt_freq[0], (0, hp))[None],
        "decay": hyena_decay[0].reshape(1, 4 * D_HYENA),
        "bias0": hyena_bias[0, 0][None], "bias1": hyena_bias[0, 1][None],
        "gon_a": group_out_norm[0, :D_ATTN][None], "gon_h": group_out_norm[0, D_ATTN:][None],
        "wo_a": w_out[0, :D_ATTN].astype(BF16), "wo_h": w_out[0, D_ATTN:].astype(BF16),
    }
    return (_trunk(x_prompt, p), _trunk(x_sample, p))
```

```python
import functools
import math

import jax
import jax.numpy as jnp
from jax import lax
from jax.experimental import pallas as pl
from jax.experimental.pallas import tpu as pltpu

F32 = jnp.float32
BF16 = jnp.bfloat16
U32 = jnp.uint32

D_MODEL = 2048
GRID_W = 64
D_ATTN = D_MODEL // 2
D_HYENA = D_MODEL - D_ATTN
HEAD_DIM = 128
N_Q_HEADS = D_ATTN // HEAD_DIM
N_KV_HEADS = 2
Q_PER_KV = N_Q_HEADS // N_KV_HEADS
ROPE_HALF = HEAD_DIM // 2
ROPE_THETA = 10000.0
FILTER_EMB = 33
FILTER_HIDDEN = 64
D_FF = 5632
EPS = 1e-6
D_KV = N_KV_HEADS * HEAD_DIM

LANES = 128
FILT_PAD = 128
FFT_N2 = 128
FFT_W = 256
FFT_G = 8
MIB = 1024 * 1024
LOG2_E = math.log2(math.e)


def _cparams(semantics, vmem_mib):
    return pltpu.CompilerParams(dimension_semantics=semantics,
                                vmem_limit_bytes=vmem_mib * MIB)


def _rms(x, g):
    ms = jnp.mean(x * x, axis=-1, keepdims=True)
    return x * lax.rsqrt(ms + EPS) * g


def _resident(shape):
    nd = len(shape)
    return pl.BlockSpec(shape, lambda *_: (0,) * nd, pipeline_mode=pl.Buffered(1))


def _ffn_body(x_ref, g_ref, wg_ref, wu_ref, w2_ref, gf_ref, o_ref, xn_ref, *, nj, final_norm):
    j = pl.program_id(1)

    @pl.when(j == 0)
    def _init():
        x = x_ref[...]
        xn_ref[...] = _rms(x, g_ref[...]).astype(BF16)
        o_ref[...] = x

    xn = xn_ref[...]
    hg = jnp.dot(xn, wg_ref[...], preferred_element_type=F32)
    hu = jnp.dot(xn, wu_ref[...], preferred_element_type=F32)
    a = (hg * (0.5 / (1.0 + jnp.exp(-hg))) * hu).astype(BF16)
    o_ref[...] += jnp.dot(a, w2_ref[...], preferred_element_type=F32)

    if final_norm:
        @pl.when(j == nj - 1)
        def _fin():
            o_ref[...] = _rms(o_ref[...], gf_ref[...])


def _ffn(x, g, w13, w2, gf, *, final_norm, tm=512, tf=512):
    T, D = x.shape
    ff = w2.shape[0]
    nj = ff // tf
    body = functools.partial(_ffn_body, nj=nj, final_norm=final_norm)
    return pl.pallas_call(
        body,
        grid=(T // tm, nj),
        in_specs=[
            pl.BlockSpec((tm, D), lambda i, j: (i, 0)),
            pl.BlockSpec((1, D), lambda i, j: (0, 0)),
            pl.BlockSpec((D, tf), lambda i, j: (0, j)),
            pl.BlockSpec((D, tf), lambda i, j: (0, j + nj)),
            pl.BlockSpec((tf, D), lambda i, j: (j, 0)),
            pl.BlockSpec((1, D), lambda i, j: (0, 0)),
        ],
        out_specs=pl.BlockSpec((tm, D), lambda i, j: (i, 0)),
        out_shape=jax.ShapeDtypeStruct((T, D), F32),
        scratch_shapes=[pltpu.VMEM((tm, D), BF16)],
        compiler_params=_cparams(("parallel", "arbitrary"), 48),
        name="ffn",
    )(x, g, w13, w13, w2, gf)


MIX_HALO = 16


def _mix_body(xb_ref, x_ref, xa_ref, g_ref, wq_ref, wkv_ref, wh_ref, qn_ref, kn_ref, cos_ref, sin_ref,
              cw_ref, cb_ref, q_ref, k_ref, v_ref, hy_ref, *, npos):
    tm = x_ref.shape[0]
    x_ext = jnp.concatenate([xb_ref[...], x_ref[...], xa_ref[...]], axis=0)
    h_ext = _rms(x_ext, g_ref[...]).astype(BF16)
    h = h_ext[MIX_HALO:MIX_HALO + tm]
    lane = lax.broadcasted_iota(jnp.int32, (tm, HEAD_DIM), 1)
    low_half = (lane % (2 * (ROPE_HALF // 2))) < (ROPE_HALF // 2)
    c = cos_ref[...]
    s = sin_ref[...]

    def head(p, gain, scale):
        y = _rms(p, gain)
        partner = jnp.where(low_half, pltpu.roll(y, HEAD_DIM - ROPE_HALF // 2, 1),
                            pltpu.roll(y, ROPE_HALF // 2, 1))
        y = y * c + partner * s
        return y if scale is None else y * scale

    q = jnp.dot(h, wq_ref[...], preferred_element_type=F32)
    for hd in range(N_Q_HEADS):
        sl = slice(hd * HEAD_DIM, (hd + 1) * HEAD_DIM)
        q_ref[:, sl] = head(q[:, sl], qn_ref[...], HEAD_DIM ** -0.5 * LOG2_E).astype(BF16)
    kv = jnp.dot(h, wkv_ref[...], preferred_element_type=F32)
    for hd in range(N_KV_HEADS):
        sl = slice(hd * HEAD_DIM, (hd + 1) * HEAD_DIM)
        k_ref[:, sl] = head(kv[:, sl], kn_ref[...], None).astype(BF16)
    v_ref[...] = kv[:, D_KV:].astype(BF16)

    hy = jnp.dot(h_ext, wh_ref[...], preferred_element_type=F32)
    pos = pl.program_id(0) % npos
    rid = lax.broadcasted_iota(jnp.int32, (tm, 1), 0)
    before = jnp.where((rid == 0) & (pos == 0), 0.0, hy[MIX_HALO - 1:MIX_HALO - 1 + tm])
    after = jnp.where((rid == tm - 1) & (pos == npos - 1), 0.0, hy[MIX_HALO + 1:MIX_HALO + 1 + tm])
    hy_ref[...] = (before * cw_ref[0:1, :] + hy[MIX_HALO:MIX_HALO + tm] * cw_ref[1:2, :]
                   + after * cw_ref[2:3, :] + cb_ref[...])


def _mix_in(x, g, wq, wkv, wh, qn, kn, cos_t, sin_t, cw, cb, *, L, tm=256):
    T, D = x.shape
    npos = L // tm
    hb = tm // MIX_HALO
    nhb = T // MIX_HALO
    return pl.pallas_call(
        functools.partial(_mix_body, npos=npos),
        grid=(T // tm,),
        in_specs=[
            pl.BlockSpec((MIX_HALO, D), lambda i: (jnp.maximum(i * hb - 1, 0), 0)),
            pl.BlockSpec((tm, D), lambda i: (i, 0)),
            pl.BlockSpec((MIX_HALO, D), lambda i: (jnp.minimum((i + 1) * hb, nhb - 1), 0)),
            _resident((1, D)),
            _resident(wq.shape),
            _resident(wkv.shape),
            _resident(wh.shape),
            _resident((1, HEAD_DIM)),
            _resident((1, HEAD_DIM)),
            pl.BlockSpec((tm, HEAD_DIM), lambda i: (i % npos, 0)),
            pl.BlockSpec((tm, HEAD_DIM), lambda i: (i % npos, 0)),
            _resident(cw.shape),
            _resident(cb.shape),
        ],
        out_specs=[
            pl.BlockSpec((tm, D_ATTN), lambda i: (i, 0)),
            pl.BlockSpec((tm, D_KV), lambda i: (i, 0)),
            pl.BlockSpec((tm, D_KV), lambda i: (i, 0)),
            pl.BlockSpec((tm, 3 * D_HYENA), lambda i: (i, 0)),
        ],
        out_shape=[
            jax.ShapeDtypeStruct((T, D_ATTN), BF16),
            jax.ShapeDtypeStruct((T, D_KV), BF16),
            jax.ShapeDtypeStruct((T, D_KV), BF16),
            jax.ShapeDtypeStruct((T, 3 * D_HYENA), F32),
        ],
        compiler_params=_cparams(("parallel",), 48),
        name="mix_in",
    )(x, x, x, g, wq, wkv, wh, qn, kn, cos_t, sin_t, cw, cb)


ATTN_ROWS = 32


ATTN_RING = 4


def _attn_body(q_ref, k_ref, v_ref, o_ref, vt_ref, *bufs, tq, tk, nk):
    s_refs, p_refs = bufs[:ATTN_RING], bufs[ATTN_RING:]

    @pl.when(pl.program_id(2) == 0)
    def _transpose_v():
        for c in range(nk):
            vt_ref[c] = v_ref[pl.ds(c * tk, tk), :].astype(F32).T.astype(BF16)

    q = jnp.concatenate([q_ref[:, h * HEAD_DIM:(h + 1) * HEAD_DIM] for h in range(Q_PER_KV)], axis=0)
    m_cols = Q_PER_KV * tq
    nchunk = tk // ATTN_ROWS

    def stage_a(i, s_ref):
        kc = k_ref[pl.ds(pl.multiple_of(i * tk, tk), tk), :]
        s_ref[...] = lax.dot_general(kc, q, (((1,), (1,)), ((), ())), preferred_element_type=F32)

    def stage_b(s_ref, p_ref, m, l):
        rows = lambda c: slice(c * ATTN_ROWS, (c + 1) * ATTN_ROWS)
        mx = s_ref[rows(0), :]
        for c in range(1, nchunk):
            mx = jnp.maximum(mx, s_ref[rows(c), :])
        m_new = jnp.maximum(m, jnp.max(mx, axis=0, keepdims=True))
        alpha = jnp.exp2(m - m_new)
        psum = None
        for c in range(nchunk):
            p = jnp.exp2(s_ref[rows(c), :] - m_new)
            psum = p if psum is None else psum + p
            p_ref[rows(c), :] = p.astype(BF16)
        return m_new, alpha * l + jnp.sum(psum, axis=0, keepdims=True), alpha

    def stage_d(i, p_ref, alpha, acc):
        return alpha * acc + jnp.dot(vt_ref[i], p_ref[...], preferred_element_type=F32)

    m = jnp.full((1, m_cols), -jnp.inf, F32)
    l = jnp.zeros((1, m_cols), F32)
    acc = jnp.zeros((HEAD_DIM, m_cols), F32)
    def ring(i, m, l, al0, al1, acc, n_a, n_b):
        alphas = {0: al0, 1: al1}
        for u in range(ATTN_RING):
            if u < n_a:
                stage_a(i + u + 4, s_refs[u])
            if u < n_b:
                v = (u + 2) % ATTN_RING
                m, l, alphas[u + 2] = stage_b(s_refs[v], p_refs[v], m, l)
            acc = stage_d(i + u, p_refs[u], alphas[u], acc)
        return m, l, alphas.get(4), alphas.get(5), acc

    for u in range(ATTN_RING):
        stage_a(u, s_refs[u])
    m, l, al0 = stage_b(s_refs[0], p_refs[0], m, l)
    m, l, al1 = stage_b(s_refs[1], p_refs[1], m, l)

    def trip(j, carry):
        return ring(ATTN_RING * j, *carry, ATTN_RING, ATTN_RING)

    m, l, al0, al1, acc = lax.fori_loop(0, nk // ATTN_RING - 1, trip, (m, l, al0, al1, acc))
    _, l, _, _, acc = ring(nk - ATTN_RING, m, l, al0, al1, acc, 0, 2)
    o = acc / l
    for h in range(Q_PER_KV):
        o_ref[:, h * HEAD_DIM:(h + 1) * HEAD_DIM] = o[:, h * tq:(h + 1) * tq].T


def _attention(q, k, v, *, tq=128, tk=512):
    B, L, _ = q.shape
    qw = Q_PER_KV * HEAD_DIM
    nk = L // tk
    assert nk % ATTN_RING == 0 and nk >= 2 * ATTN_RING
    body = functools.partial(_attn_body, tq=tq, tk=tk, nk=nk)
    m_cols = Q_PER_KV * tq
    return pl.pallas_call(
        body,
        grid=(B, N_KV_HEADS, L // tq),
        in_specs=[
            pl.BlockSpec((None, tq, qw), lambda b, g, i: (b, i, g)),
            pl.BlockSpec((None, L, HEAD_DIM), lambda b, g, i: (b, 0, g)),
            pl.BlockSpec((None, L, HEAD_DIM), lambda b, g, i: (b, 0, g)),
        ],
        out_specs=pl.BlockSpec((None, tq, qw), lambda b, g, i: (b, i, g)),
        out_shape=jax.ShapeDtypeStruct((B, L, D_ATTN), F32),
        scratch_shapes=([pltpu.VMEM((nk, HEAD_DIM, tk), BF16)]
                        + [pltpu.VMEM((tk, m_cols), F32)] * ATTN_RING
                        + [pltpu.VMEM((tk, m_cols), BF16)] * ATTN_RING),
        compiler_params=_cparams(("parallel", "parallel", "arbitrary"), 32),
        name="attn",
    )(q, k, v)


def _filt_body(feat_ref, w1_ref, b1_ref, w2_ref, b2_ref, w3_ref, fq_ref, dec_ref, hf_ref, asum_ref, *, tl):
    i = pl.program_id(0)
    hp = lax.Precision.HIGHEST
    fq = fq_ref[...]
    feat = feat_ref[...]
    h = jnp.sin(fq * (jnp.dot(feat, w1_ref[...], precision=hp, preferred_element_type=F32) + b1_ref[...]))
    h = jnp.sin(fq * (jnp.dot(h, w2_ref[...], precision=hp, preferred_element_type=F32) + b2_ref[...]))
    h3 = jnp.dot(h, w3_ref[...], precision=hp, preferred_element_type=F32)
    t01 = feat[:, 0:1]
    hf = h3 * jnp.exp(-t01 * jnp.abs(dec_ref[...]))
    ncol = hf.shape[1]
    row = lax.broadcasted_iota(jnp.int32, hf.shape, 0) + i * tl
    col = lax.broadcasted_iota(jnp.int32, hf.shape, 1)
    hf = jnp.where((row == 0) & (col >= ncol // 2), 0.0, hf)
    hf_ref[...] = hf
    part = jnp.sum(jnp.abs(hf), axis=0, keepdims=True)

    @pl.when(i == 0)
    def _first():
        asum_ref[...] = part

    @pl.when(i > 0)
    def _rest():
        asum_ref[...] += part


def _filters(feats, w1, b1, w2, b2, w3, fq, dec, *, tl=256):
    L = feats.shape[0]
    nc = w3.shape[1]
    body = functools.partial(_filt_body, tl=tl)
    return pl.pallas_call(
        body,
        grid=(L // tl,),
        in_specs=[
            pl.BlockSpec((tl, FILT_PAD), lambda i: (i, 0)),
            _resident(w1.shape), _resident(b1.shape), _resident(w2.shape), _resident(b2.shape),
            _resident(w3.shape), _resident(fq.shape), _resident(dec.shape),
        ],
        out_specs=[
            pl.BlockSpec((tl, nc), lambda i: (i, 0)),
            pl.BlockSpec((1, nc), lambda i: (0, 0)),
        ],
        out_shape=[jax.ShapeDtypeStruct((L, nc), F32), jax.ShapeDtypeStruct((1, nc), F32)],
        compiler_params=_cparams(("arbitrary",), 48),
        name="filt",
    )(feats, w1, b1, w2, b2, w3, fq, dec)


def _pack_c(y):
    n = y.shape[0] // 2
    r = lax.bitcast_convert_type(y[:n], jnp.uint32)
    i = lax.bitcast_convert_type(y[n:], jnp.uint32)
    half = jnp.uint32(0x8000)
    return ((r + half) & jnp.uint32(0xFFFF0000)) | ((i + half) >> 16)


def _unpack_c(w):
    re = lax.bitcast_convert_type(w & jnp.uint32(0xFFFF0000), F32)
    im = lax.bitcast_convert_type(w << 16, F32)
    return re, im


def _unpack_rows(w):
    re, im = _unpack_c(w)
    return jnp.concatenate([re, im], axis=0).astype(BF16)


def _fft_a_body(x_ref, fa_ref, o_ref):
    for j in range(FFT_G):
        xs = jnp.concatenate([x_ref[0, :, j, :], x_ref[1, :, j, :]], axis=0).astype(BF16)
        o_ref[j] = _pack_c(jnp.dot(fa_ref[j], xs, preferred_element_type=F32))


def _fft_a(xv, fa, *, ncols, col0=0):
    P, _, n1h, n2, _ = xv.shape
    two_n1 = fa.shape[1]
    n1 = two_n1 // 2
    c0 = col0 // FFT_W
    return pl.pallas_call(
        _fft_a_body,
        grid=(n2 // FFT_G, P, ncols // FFT_W),
        in_specs=[
            pl.BlockSpec((None, 2, n1h, FFT_G, FFT_W), lambda g, p, c: (p, 0, 0, g, c + c0)),
            pl.BlockSpec((FFT_G, two_n1, 2 * n1h), lambda g, p, c: (g, 0, 0)),
        ],
        out_specs=pl.BlockSpec((None, FFT_G, n1, FFT_W), lambda g, p, c: (p, g, 0, c)),
        out_shape=jax.ShapeDtypeStruct((P, n2, n1, ncols), U32),
        compiler_params=_cparams(("arbitrary", "arbitrary", "arbitrary"), 32),
        name="fft_a",
    )(xv, fa)


def _fft_a_real_body(x_ref, fa_ref, o_ref):
    for j in range(FFT_G):
        o_ref[j] = _pack_c(jnp.dot(fa_ref[j], x_ref[:, j, :].astype(BF16), preferred_element_type=F32))


def _fft_a_real(xv, fa_re):
    n1h, n2, C = xv.shape
    two_n1 = fa_re.shape[1]
    n1 = two_n1 // 2
    return pl.pallas_call(
        _fft_a_real_body,
        grid=(n2 // FFT_G, C // FFT_W),
        in_specs=[
            pl.BlockSpec((n1h, FFT_G, FFT_W), lambda g, c: (0, g, c)),
            pl.BlockSpec((FFT_G, two_n1, n1h), lambda g, c: (g, 0, 0)),
        ],
        out_specs=pl.BlockSpec((FFT_G, n1, FFT_W), lambda g, c: (g, 0, c)),
        out_shape=jax.ShapeDtypeStruct((n2, n1, C), U32),
        compiler_params=_cparams(("arbitrary", "arbitrary"), 32),
        name="fft_a_real",
    )(xv, fa_re)


def _fft_b_filt_body(f_ref, b_ref, fb_ref, asf_ref, asb_ref, o_ref):
    n2 = FFT_N2
    inv = 1.0 / (asf_ref[...] + asb_ref[...])
    fb = fb_ref[...]
    for j in range(FFT_G):
        u = jnp.dot(fb, _unpack_rows(f_ref[:, j, :]), preferred_element_type=F32)
        w = jnp.dot(fb, _unpack_rows(b_ref[:, j, :]), preferred_element_type=F32)
        kr = (u[:n2] + w[:n2]) * inv
        ki = (u[n2:] - w[n2:]) * inv
        o_ref[j] = _pack_c(jnp.concatenate([kr, ki], axis=0))


def _fft_b_filt(s1f, fb, asum):
    n2, n1, c2 = s1f.shape
    cf = c2 // 2
    nb = cf // FFT_W
    blk = (n2, FFT_G, FFT_W)
    return pl.pallas_call(
        _fft_b_filt_body,
        grid=(n1 // FFT_G, nb),
        in_specs=[
            pl.BlockSpec(blk, lambda g, c: (0, g, c)),
            pl.BlockSpec(blk, lambda g, c: (0, g, c + nb)),
            _resident(fb.shape),
            pl.BlockSpec((1, FFT_W), lambda g, c: (0, c)),
            pl.BlockSpec((1, FFT_W), lambda g, c: (0, c + nb)),
        ],
        out_specs=pl.BlockSpec((FFT_G, n2, FFT_W), lambda g, c: (g, 0, c)),
        out_shape=jax.ShapeDtypeStruct((n1, n2, cf), U32),
        compiler_params=_cparams(("parallel", "parallel"), 32),
        name="fft_b_filt",
    )(s1f, s1f, fb, asum, asum)


def _fft_b_body(s_ref, kf_ref, fb_ref, fbi_ref, o_ref):
    n2 = FFT_N2
    fb = fb_ref[...]
    fbi = fbi_ref[...]
    for j in range(FFT_G):
        x = jnp.dot(fb, _unpack_rows(s_ref[:, j, :]), preferred_element_type=F32)
        xr, xi = x[:n2], x[n2:]
        kr, ki = _unpack_c(kf_ref[j])
        pm = jnp.concatenate([xr * kr - xi * ki, xr * ki + xi * kr], axis=0).astype(BF16)
        o_ref[j] = _pack_c(jnp.dot(fbi, pm, preferred_element_type=F32))


def _fft_b(s1, kf, fb, fbi, *, kcol0):
    P, n2, n1, C = s1.shape
    k0 = kcol0 // FFT_W
    return pl.pallas_call(
        _fft_b_body,
        grid=(n1 // FFT_G, C // FFT_W, P),
        in_specs=[
            pl.BlockSpec((None, n2, FFT_G, FFT_W), lambda g, c, p: (p, 0, g, c)),
            pl.BlockSpec((FFT_G, n2, FFT_W), lambda g, c, p: (g, 0, c + k0)),
            _resident(fb.shape),
            _resident(fbi.shape),
        ],
        out_specs=pl.BlockSpec((None, FFT_G, n2, FFT_W), lambda g, c, p: (p, g, 0, c)),
        out_shape=jax.ShapeDtypeStruct((P, n1, n2, C), U32),
        compiler_params=_cparams(("parallel", "parallel", "arbitrary"), 32),
        name="fft_b",
    )(s1, kf, fb, fbi)


def _fft_c_body(z_ref, fc_ref, u_ref, gate_ref, d_ref, o_ref):
    n1h = o_ref.shape[1]
    d = d_ref[...]
    for j in range(FFT_G):
        y = jnp.dot(fc_ref[j], _unpack_rows(z_ref[:, j, :]), preferred_element_type=F32)
        for b in range(2):
            yb = y[b * n1h:(b + 1) * n1h]
            o_ref[b, :, j, :] = gate_ref[b, :, j, :] * (yb + u_ref[b, :, j, :] * d)


def _fft_c(s2, fc, uv, gv, d, *, ucol0, gcol0):
    P, n1, n2, C = s2.shape
    n1h = n1 // 2
    u0 = ucol0 // FFT_W
    g0 = gcol0 // FFT_W
    tblk = (None, 2, n1h, FFT_G, FFT_W)
    return pl.pallas_call(
        _fft_c_body,
        grid=(n2 // FFT_G, P, C // FFT_W),
        in_specs=[
            pl.BlockSpec((None, n1, FFT_G, FFT_W), lambda g, p, c: (p, 0, g, c)),
            pl.BlockSpec((FFT_G, 2 * n1h, 2 * n1), lambda g, p, c: (g, 0, 0)),
            pl.BlockSpec(tblk, lambda g, p, c: (p, 0, 0, g, c + u0)),
            pl.BlockSpec(tblk, lambda g, p, c: (p, 0, 0, g, c + g0)),
            pl.BlockSpec((1, FFT_W), lambda g, p, c: (0, c)),
        ],
        out_specs=pl.BlockSpec(tblk, lambda g, p, c: (p, 0, 0, g, c)),
        out_shape=jax.ShapeDtypeStruct((P, 2, n1h, n2, C), F32),
        compiler_params=_cparams(("arbitrary", "arbitrary", "arbitrary"), 32),
        name="fft_c",
    )(s2, fc, uv, gv, d)


def _oproj_body(x_ref, a_ref, h_ref, ga_ref, gh_ref, wa_ref, wh_ref, o_ref):
    ma = _rms(a_ref[...], ga_ref[...]).astype(BF16)
    mh = _rms(h_ref[...], gh_ref[...]).astype(BF16)
    o_ref[...] = (x_ref[...] + jnp.dot(ma, wa_ref[...], preferred_element_type=F32)
                  + jnp.dot(mh, wh_ref[...], preferred_element_type=F32))


def _out_proj(x, a, h, ga, gh, wa, wh, *, tm=512):
    T, D = x.shape
    return pl.pallas_call(
        _oproj_body,
        grid=(T // tm,),
        in_specs=[
            pl.BlockSpec((tm, D), lambda i: (i, 0)),
            pl.BlockSpec((tm, D_ATTN), lambda i: (i, 0)),
            pl.BlockSpec((tm, D_HYENA), lambda i: (i, 0)),
            _resident((1, D_ATTN)), _resident((1, D_HYENA)),
            _resident(wa.shape), _resident(wh.shape),
        ],
        out_specs=pl.BlockSpec((tm, D), lambda i: (i, 0)),
        out_shape=jax.ShapeDtypeStruct((T, D), F32),
        compiler_params=_cparams(("parallel",), 48),
        name="oproj",
    )(x, a, h, ga, gh, wa, wh)


def _rope_tables(L):
    rows = L // GRID_W
    row = jnp.repeat(jnp.arange(rows, dtype=F32), GRID_W)
    col = jnp.tile(jnp.arange(GRID_W, dtype=F32), rows)
    inv = ROPE_THETA ** (-jnp.arange(0, ROPE_HALF, 2, dtype=F32) / ROPE_HALF)
    ar = row[:, None] * inv[None]
    ac = col[:, None] * inv[None]
    cr, sr, cc, sc = jnp.cos(ar), jnp.sin(ar), jnp.cos(ac), jnp.sin(ac)
    cos_t = jnp.concatenate([cr, cr, cc, cc], axis=-1)
    sin_t = jnp.concatenate([-sr, sr, -sc, sc], axis=-1)
    return cos_t, sin_t


def _filter_feats(L):
    t01 = jnp.linspace(0.0, 1.0, L, dtype=F32)[:, None]
    bands = (FILTER_EMB - 1) // 2
    fr = jnp.linspace(1e-4, bands - 1, bands, dtype=F32)[None]
    w = 2.0 * math.pi * jnp.arange(L, dtype=F32)[:, None] / L
    feats = jnp.concatenate([t01, jnp.cos(fr * w), -jnp.sin(fr * w)], axis=-1)
    return jnp.pad(feats, ((0, 0), (0, FILT_PAD - FILTER_EMB)))


def _dft_tables(L):
    n = 2 * L
    n2 = FFT_N2
    n1 = n // n2
    n1h = n1 // 2
    k1 = jnp.arange(n1, dtype=jnp.int32)[None, :, None]
    t1 = jnp.arange(n1h, dtype=jnp.int32)[None, None, :]
    t2 = jnp.arange(n2, dtype=jnp.int32)[:, None, None]
    ang = (-2.0 * math.pi / n) * ((k1 * (n2 * t1 + t2)) % n).astype(F32)
    cr, ci = jnp.cos(ang), jnp.sin(ang)
    fa = jnp.concatenate([jnp.concatenate([cr, -ci], -1), jnp.concatenate([ci, cr], -1)], -2)
    fa_re = jnp.concatenate([cr, ci], -2)
    crt, cit = jnp.swapaxes(cr, 1, 2), jnp.swapaxes(ci, 1, 2)
    fc = jnp.concatenate([jnp.concatenate([crt, cit], -1), jnp.concatenate([-cit, crt], -1)], -2) / n
    a = jnp.arange(n2, dtype=jnp.int32)
    ang2 = (-2.0 * math.pi / n2) * ((a[:, None] * a[None, :]) % n2).astype(F32)
    fr, fi = jnp.cos(ang2), jnp.sin(ang2)
    fb = jnp.concatenate([jnp.concatenate([fr, -fi], -1), jnp.concatenate([fi, fr], -1)], -2)
    fbi = jnp.concatenate([jnp.concatenate([fr, fi], -1), jnp.concatenate([-fi, fr], -1)], -2)
    return (fa.astype(BF16), fa_re.astype(BF16), fb.astype(BF16), fbi.astype(BF16), fc.astype(BF16))


def _trunk(x, p):
    B, L, D = x.shape
    T = B * L
    P = B // 2
    n2 = FFT_N2
    n1h = L // n2
    C = D_HYENA

    x0 = x.reshape(T, D)
    x1 = _ffn(x0, p["ffn1_norm"], p["ffn1_w13"], p["ffn1_w2"], p["final_norm"], final_norm=False)

    cos_t, sin_t = _rope_tables(L)
    q, k, v, hyc = _mix_in(x1, p["mix_norm"], p["wq"], p["wkv"], p["wh"], p["q_norm"], p["k_norm"],
                           cos_t, sin_t, p["conv_w"], p["conv_b"], L=L)
    attn = _attention(q.reshape(B, L, D_ATTN), k.reshape(B, L, D_KV), v.reshape(B, L, D_KV))
    hyv = hyc.reshape(P, 2, n1h, n2, 3 * C)

    fa, fa_re, fb, fbi, fc = _dft_tables(L)
    hf, asum = _filters(_filter_feats(L), p["filt_w1"], p["filt_b1"], p["filt_w2"], p["filt_b2"],
                        p["filt_w3"], p["filt_freq"], p["decay"])
    s1f = _fft_a_real(hf.reshape(n1h, n2, 4 * C), fa_re)
    kf = _fft_b_filt(s1f, fb, asum)

    s1 = _fft_a(hyv, fa, ncols=C, col0=0)
    s2 = _fft_b(s1, kf, fb, fbi, kcol0=0)
    z = _fft_c(s2, fc, hyv, hyv, p["bias0"], ucol0=0, gcol0=C)
    s1 = _fft_a(z, fa, ncols=C, col0=0)
    s2 = _fft_b(s1, kf, fb, fbi, kcol0=C)
    ho = _fft_c(s2, fc, z, hyv, p["bias1"], ucol0=0, gcol0=2 * C)

    x2 = _out_proj(x1, attn.reshape(T, D_ATTN), ho.reshape(T, C), p["gon_a"], p["gon_h"],
                   p["wo_a"], p["wo_h"])
    x3 = _ffn(x2, p["ffn2_norm"], p["ffn2_w13"], p["ffn2_w2"], p["final_norm"], final_norm=True)
    return x3.reshape(B, L, D)


def kernel(x_prompt, x_sample, ffn1_norm, ffn1_w13, ffn1_w2, mix_norm, w_in, q_norm, k_norm, conv_w, conv_b, filt_w1, filt_b1, filt_w2, filt_b2, filt_w3, filt_freq, hyena_decay, hyena_bias, group_out_norm, w_out, ffn2_norm, ffn2_w13, ffn2_w2, final_norm):
    hp = FILT_PAD - FILTER_HIDDEN
    w_in0 = w_in[0]
    p = {
        "ffn1_norm": ffn1_norm[0][None], "ffn2_norm": ffn2_norm[0][None],
        "ffn1_w13": ffn1_w13[0].astype(BF16), "ffn1_w2": ffn1_w2[0].astype(BF16),
        "ffn2_w13": ffn2_w13[0].astype(BF16), "ffn2_w2": ffn2_w2[0].astype(BF16),
        "final_norm": final_norm[None],
        "mix_norm": mix_norm[0][None],
        "wq": w_in0[:, :D_ATTN].astype(BF16),
        "wkv": w_in0[:, D_ATTN:D_ATTN + 2 * D_KV].astype(BF16),
        "wh": w_in0[:, D_ATTN + 2 * D_KV:].astype(BF16),
        "q_norm": q_norm[0][None], "k_norm": k_norm[0][None],
        "conv_w": conv_w[0], "conv_b": conv_b[0][None],
        "filt_w1": jnp.pad(filt_w1[0], ((0, FILT_PAD - FILTER_EMB), (0, hp))),
        "filt_b1": jnp.pad(filt_b1[0], (0, hp))[None],
        "filt_w2": jnp.pad(filt_w2[0], ((0, hp), (0, hp))),
        "filt_b2": jnp.pad(filt_b2[0], (0, hp))[None],
        "filt_w3": jnp.pad(filt_w3[0], ((0, hp), (0, 0))),
        "filt_freq": jnp.pad(filt_freq[0], (0, hp))[None],
        "decay": hyena_decay[0].reshape(1, 4 * D_HYENA),
        "bias0": hyena_bias[0, 0][None], "bias1": hyena_bias[0, 1][None],
        "gon_a": group_out_norm[0, :D_ATTN][None], "gon_h": group_out_norm[0, D_ATTN:][None],
        "wo_a": w_out[0, :D_ATTN].astype(BF16), "wo_h": w_out[0, D_ATTN:].astype(BF16),
    }
    return (_trunk(x_prompt, p), _trunk(x_sample, p))
```

```python
import functools
import math

import jax
import jax.numpy as jnp
from jax import lax
from jax.experimental import pallas as pl
from jax.experimental.pallas import tpu as pltpu

F32 = jnp.float32
BF16 = jnp.bfloat16
U32 = jnp.uint32

D_MODEL = 2048
GRID_W = 64
D_ATTN = D_MODEL // 2
D_HYENA = D_MODEL - D_ATTN
HEAD_DIM = 128
N_Q_HEADS = D_ATTN // HEAD_DIM
N_KV_HEADS = 2
Q_PER_KV = N_Q_HEADS // N_KV_HEADS
ROPE_HALF = HEAD_DIM // 2
ROPE_THETA = 10000.0
FILTER_EMB = 33
FILTER_HIDDEN = 64
D_FF = 5632
EPS = 1e-6
D_KV = N_KV_HEADS * HEAD_DIM

LANES = 128
FILT_PAD = 128
FFT_N2 = 128
FFT_W = 256
FFT_G = 8
MIB = 1024 * 1024
LOG2_E = math.log2(math.e)


def _cparams(semantics, vmem_mib):
    return pltpu.CompilerParams(dimension_semantics=semantics,
                                vmem_limit_bytes=vmem_mib * MIB)


def _rms(x, g):
    ms = jnp.mean(x * x, axis=-1, keepdims=True)
    return x * lax.rsqrt(ms + EPS) * g


def _resident(shape):
    nd = len(shape)
    return pl.BlockSpec(shape, lambda *_: (0,) * nd, pipeline_mode=pl.Buffered(1))


def _ffn_body(x_ref, g_ref, wg_ref, wu_ref, w2_ref, gf_ref, o_ref, xn_ref, *, nj, final_norm):
    j = pl.program_id(1)

    @pl.when(j == 0)
    def _init():
        x = x_ref[...]
        xn_ref[...] = _rms(x, g_ref[...]).astype(BF16)
        o_ref[...] = x

    xn = xn_ref[...]
    hg = jnp.dot(xn, wg_ref[...], preferred_element_type=F32)
    hu = jnp.dot(xn, wu_ref[...], preferred_element_type=F32)
    a = (hg * (0.5 / (1.0 + jnp.exp(-hg))) * hu).astype(BF16)
    o_ref[...] += jnp.dot(a, w2_ref[...], preferred_element_type=F32)

    if final_norm:
        @pl.when(j == nj - 1)
        def _fin():
            o_ref[...] = _rms(o_ref[...], gf_ref[...])


def _ffn(x, g, w13, w2, gf, *, final_norm, tm=512, tf=512):
    T, D = x.shape
    ff = w2.shape[0]
    nj = ff // tf
    body = functools.partial(_ffn_body, nj=nj, final_norm=final_norm)
    return pl.pallas_call(
        body,
        grid=(T // tm, nj),
        in_specs=[
            pl.BlockSpec((tm, D), lambda i, j: (i, 0)),
            pl.BlockSpec((1, D), lambda i, j: (0, 0)),
            pl.BlockSpec((D, tf), lambda i, j: (0, j)),
            pl.BlockSpec((D, tf), lambda i, j: (0, j + nj)),
            pl.BlockSpec((tf, D), lambda i, j: (j, 0)),
            pl.BlockSpec((1, D), lambda i, j: (0, 0)),
        ],
        out_specs=pl.BlockSpec((tm, D), lambda i, j: (i, 0)),
        out_shape=jax.ShapeDtypeStruct((T, D), F32),
        scratch_shapes=[pltpu.VMEM((tm, D), BF16)],
        compiler_params=_cparams(("parallel", "arbitrary"), 48),
        name="ffn",
    )(x, g, w13, w13, w2, gf)


MIX_HALO = 16


def _mix_body(xb_ref, x_ref, xa_ref, g_ref, wq_ref, wkv_ref, wh_ref, qn_ref, kn_ref, cos_ref, sin_ref,
              cw_ref, cb_ref, q_ref, k_ref, v_ref, hy_ref, *, npos):
    tm = x_ref.shape[0]
    x_ext = jnp.concatenate([xb_ref[...], x_ref[...], xa_ref[...]], axis=0)
    h_ext = _rms(x_ext, g_ref[...]).astype(BF16)
    h = h_ext[MIX_HALO:MIX_HALO + tm]
    lane = lax.broadcasted_iota(jnp.int32, (tm, HEAD_DIM), 1)
    low_half = (lane % (2 * (ROPE_HALF // 2))) < (ROPE_HALF // 2)
    c = cos_ref[...]
    s = sin_ref[...]

    def head(p, gain, scale):
        y = _rms(p, gain)
        partner = jnp.where(low_half, pltpu.roll(y, HEAD_DIM - ROPE_HALF // 2, 1),
                            pltpu.roll(y, ROPE_HALF // 2, 1))
        y = y * c + partner * s
        return y if scale is None else y * scale

    q = jnp.dot(h, wq_ref[...], preferred_element_type=F32)
    for hd in range(N_Q_HEADS):
        sl = slice(hd * HEAD_DIM, (hd + 1) * HEAD_DIM)
        q_ref[:, sl] = head(q[:, sl], qn_ref[...], HEAD_DIM ** -0.5 * LOG2_E).astype(BF16)
    kv = jnp.dot(h, wkv_ref[...], preferred_element_type=F32)
    for hd in range(N_KV_HEADS):
        sl = slice(hd * HEAD_DIM, (hd + 1) * HEAD_DIM)
        k_ref[:, sl] = head(kv[:, sl], kn_ref[...], None).astype(BF16)
    v_ref[...] = kv[:, D_KV:].astype(BF16)

    hy = jnp.dot(h_ext, wh_ref[...], preferred_element_type=F32)
    pos = pl.program_id(0) % npos
    rid = lax.broadcasted_iota(jnp.int32, (tm, 1), 0)
    before = jnp.where((rid == 0) & (pos == 0), 0.0, hy[MIX_HALO - 1:MIX_HALO - 1 + tm])
    after = jnp.where((rid == tm - 1) & (pos == npos - 1), 0.0, hy[MIX_HALO + 1:MIX_HALO + 1 + tm])
    hyc = (before * cw_ref[0:1, :] + hy[MIX_HALO:MIX_HALO + tm] * cw_ref[1:2, :]
           + after * cw_ref[2:3, :] + cb_ref[...])
    for cb in range(hy_ref.shape[0]):
        hy_ref[cb] = hyc[:, cb * LANES:(cb + 1) * LANES]


def _mix_in(x, g, wq, wkv, wh, qn, kn, cos_t, sin_t, cw, cb, *, L, tm=256):
    T, D = x.shape
    npos = L // tm
    hb = tm // MIX_HALO
    nhb = T // MIX_HALO
    return pl.pallas_call(
        functools.partial(_mix_body, npos=npos),
        grid=(T // tm,),
        in_specs=[
            pl.BlockSpec((MIX_HALO, D), lambda i: (jnp.maximum(i * hb - 1, 0), 0)),
            pl.BlockSpec((tm, D), lambda i: (i, 0)),
            pl.BlockSpec((MIX_HALO, D), lambda i: (jnp.minimum((i + 1) * hb, nhb - 1), 0)),
            _resident((1, D)),
            _resident(wq.shape),
            _resident(wkv.shape),
            _resident(wh.shape),
            _resident((1, HEAD_DIM)),
            _resident((1, HEAD_DIM)),
            pl.BlockSpec((tm, HEAD_DIM), lambda i: (i % npos, 0)),
            pl.BlockSpec((tm, HEAD_DIM), lambda i: (i % npos, 0)),
            _resident(cw.shape),
            _resident(cb.shape),
        ],
        out_specs=[
            pl.BlockSpec((tm, D_ATTN), lambda i: (i, 0)),
            pl.BlockSpec((tm, D_KV), lambda i: (i, 0)),
            pl.BlockSpec((tm, D_KV), lambda i: (i, 0)),
            pl.BlockSpec((3 * D_HYENA // LANES, tm, LANES), lambda i: (0, i, 0)),
        ],
        out_shape=[
            jax.ShapeDtypeStruct((T, D_ATTN), BF16),
            jax.ShapeDtypeStruct((T, D_KV), BF16),
            jax.ShapeDtypeStruct((T, D_KV), BF16),
            jax.ShapeDtypeStruct((3 * D_HYENA // LANES, T, LANES), F32),
        ],
        compiler_params=_cparams(("parallel",), 48),
        name="mix_in",
    )(x, x, x, g, wq, wkv, wh, qn, kn, cos_t, sin_t, cw, cb)


ATTN_ROWS = 32
ATTN_RING = 4


def _attn_body(q_ref, k_ref, v_ref, o_ref, vt_ref, *bufs, tq, tk, nk):
    s_refs, p_refs = bufs[:ATTN_RING], bufs[ATTN_RING:]

    @pl.when(pl.program_id(2) == 0)
    def _transpose_v():
        for c in range(nk):
            vt_ref[c] = v_ref[pl.ds(c * tk, tk), :].astype(F32).T.astype(BF16)

    q = jnp.concatenate([q_ref[:, h * HEAD_DIM:(h + 1) * HEAD_DIM] for h in range(Q_PER_KV)], axis=0)
    m_cols = Q_PER_KV * tq
    nchunk = tk // ATTN_ROWS

    def stage_a(i, s_ref):
        kc = k_ref[pl.ds(pl.multiple_of(i * tk, tk), tk), :]
        s_ref[...] = lax.dot_general(kc, q, (((1,), (1,)), ((), ())), preferred_element_type=F32)

    def stage_b(s_ref, p_ref, m, l):
        rows = lambda c: slice(c * ATTN_ROWS, (c + 1) * ATTN_ROWS)
        mx = s_ref[rows(0), :]
        for c in range(1, nchunk):
            mx = jnp.maximum(mx, s_ref[rows(c), :])
        m_new = jnp.maximum(m, jnp.max(mx, axis=0, keepdims=True))
        alpha = jnp.exp2(m - m_new)
        psum = None
        for c in range(nchunk):
            p = jnp.exp2(s_ref[rows(c), :] - m_new)
            psum = p if psum is None else psum + p
            p_ref[rows(c), :] = p.astype(BF16)
        return m_new, alpha * l + jnp.sum(psum, axis=0, keepdims=True), alpha

    def stage_d(i, p_ref, alpha, acc):
        return alpha * acc + jnp.dot(vt_ref[i], p_ref[...], preferred_element_type=F32)

    def ring(i, m, l, al0, al1, acc, n_a, n_b):
        alphas = {0: al0, 1: al1}
        for u in range(ATTN_RING):
            if u < n_a:
                stage_a(i + u + 4, s_refs[u])
            if u < n_b:
                v = (u + 2) % ATTN_RING
                m, l, alphas[u + 2] = stage_b(s_refs[v], p_refs[v], m, l)
            acc = stage_d(i + u, p_refs[u], alphas[u], acc)
        return m, l, alphas.get(4), alphas.get(5), acc

    m = jnp.full((1, m_cols), -jnp.inf, F32)
    l = jnp.zeros((1, m_cols), F32)
    acc = jnp.zeros((HEAD_DIM, m_cols), F32)
    for u in range(ATTN_RING):
        stage_a(u, s_refs[u])
    m, l, al0 = stage_b(s_refs[0], p_refs[0], m, l)
    m, l, al1 = stage_b(s_refs[1], p_refs[1], m, l)

    def trip(j, carry):
        return ring(ATTN_RING * j, *carry, ATTN_RING, ATTN_RING)

    m, l, al0, al1, acc = lax.fori_loop(0, nk // ATTN_RING - 1, trip, (m, l, al0, al1, acc))
    _, l, _, _, acc = ring(nk - ATTN_RING, m, l, al0, al1, acc, 0, 2)
    o = acc / l
    for h in range(Q_PER_KV):
        o_ref[:, h * HEAD_DIM:(h + 1) * HEAD_DIM] = o[:, h * tq:(h + 1) * tq].T


def _attention(q, k, v, *, tq=128, tk=512):
    B, L, _ = q.shape
    qw = Q_PER_KV * HEAD_DIM
    nk = L // tk
    assert nk % ATTN_RING == 0 and nk >= 2 * ATTN_RING
    body = functools.partial(_attn_body, tq=tq, tk=tk, nk=nk)
    m_cols = Q_PER_KV * tq
    return pl.pallas_call(
        body,
        grid=(B, N_KV_HEADS, L // tq),
        in_specs=[
            pl.BlockSpec((None, tq, qw), lambda b, g, i: (b, i, g)),
            pl.BlockSpec((None, L, HEAD_DIM), lambda b, g, i: (b, 0, g)),
            pl.BlockSpec((None, L, HEAD_DIM), lambda b, g, i: (b, 0, g)),
        ],
        out_specs=pl.BlockSpec((None, tq, qw), lambda b, g, i: (b, i, g)),
        out_shape=jax.ShapeDtypeStruct((B, L, D_ATTN), F32),
        scratch_shapes=([pltpu.VMEM((nk, HEAD_DIM, tk), BF16)]
                        + [pltpu.VMEM((tk, m_cols), F32)] * ATTN_RING
                        + [pltpu.VMEM((tk, m_cols), BF16)] * ATTN_RING),
        compiler_params=_cparams(("parallel", "parallel", "arbitrary"), 32),
        name="attn",
    )(q, k, v)


def _filt_body(feat_ref, w1_ref, b1_ref, w2_ref, b2_ref, w3_ref, fq_ref, dec_ref, hf_ref, asum_ref, *, tl):
    i = pl.program_id(0)
    hp = lax.Precision.HIGHEST
    fq = fq_ref[...]
    feat = feat_ref[...]
    h = jnp.sin(fq * (jnp.dot(feat, w1_ref[...], precision=hp, preferred_element_type=F32) + b1_ref[...]))
    h = jnp.sin(fq * (jnp.dot(h, w2_ref[...], precision=hp, preferred_element_type=F32) + b2_ref[...]))
    h3 = jnp.dot(h, w3_ref[...], precision=hp, preferred_element_type=F32)
    t01 = feat[:, 0:1]
    hf = h3 * jnp.exp(-t01 * jnp.abs(dec_ref[...]))
    ncol = hf.shape[1]
    row = lax.broadcasted_iota(jnp.int32, hf.shape, 0) + i * tl
    col = lax.broadcasted_iota(jnp.int32, hf.shape, 1)
    hf = jnp.where((row == 0) & (col >= ncol // 2), 0.0, hf)
    for cb in range(ncol // LANES):
        hf_ref[cb] = hf[:, cb * LANES:(cb + 1) * LANES]
    part = jnp.sum(jnp.abs(hf), axis=0, keepdims=True)

    @pl.when(i == 0)
    def _first():
        asum_ref[...] = part

    @pl.when(i > 0)
    def _rest():
        asum_ref[...] += part


def _filters(feats, w1, b1, w2, b2, w3, fq, dec, *, tl=256):
    L = feats.shape[0]
    nc = w3.shape[1]
    body = functools.partial(_filt_body, tl=tl)
    return pl.pallas_call(
        body,
        grid=(L // tl,),
        in_specs=[
            pl.BlockSpec((tl, FILT_PAD), lambda i: (i, 0)),
            _resident(w1.shape), _resident(b1.shape), _resident(w2.shape), _resident(b2.shape),
            _resident(w3.shape), _resident(fq.shape), _resident(dec.shape),
        ],
        out_specs=[
            pl.BlockSpec((nc // LANES, tl, LANES), lambda i: (0, i, 0)),
            pl.BlockSpec((1, nc), lambda i: (0, 0)),
        ],
        out_shape=[jax.ShapeDtypeStruct((nc // LANES, L, LANES), F32), jax.ShapeDtypeStruct((1, nc), F32)],
        compiler_params=_cparams(("arbitrary",), 48),
        name="filt",
    )(feats, w1, b1, w2, b2, w3, fq, dec)


FFT_CB = FFT_W // LANES


def _pack_c(y):
    n = y.shape[0] // 2
    r = lax.bitcast_convert_type(y[:n], U32)
    i = lax.bitcast_convert_type(y[n:], U32)
    half = jnp.uint32(0x8000)
    return ((r + half) & jnp.uint32(0xFFFF0000)) | ((i + half) >> 16)


def _unpack_c(w):
    re = lax.bitcast_convert_type(w & jnp.uint32(0xFFFF0000), F32)
    im = lax.bitcast_convert_type(w << 16, F32)
    return re, im


def _unpack_rows(w):
    re, im = _unpack_c(w)
    return jnp.concatenate([re, im], axis=0).astype(BF16)


def _flat_scratch(n, dtype, lead=()):
    return pltpu.VMEM(lead + (FFT_CB, n * FFT_G, LANES), dtype)


def _flatten(flat_ref, block_ref, pre=()):
    rows = flat_ref.shape[-2]
    for h in range(FFT_CB):
        flat_ref[pre + (h,)] = block_ref[(h,) + pre].reshape(rows, LANES)


def _unflatten(block_ref, flat_ref, pre=()):
    n = flat_ref.shape[-2] // FFT_G
    for h in range(FFT_CB):
        block_ref[(h,) + pre] = flat_ref[pre + (h,)].reshape(n, FFT_G, LANES)


def _gather_rows(flat_ref, j, pre=()):
    n = flat_ref.shape[-2] // FFT_G
    return jnp.concatenate(
        [flat_ref[pre + (h, pl.ds(j, n, stride=FFT_G), slice(None))] for h in range(FFT_CB)], axis=1)


def _scatter_rows(flat_ref, j, val, pre=()):
    n = flat_ref.shape[-2] // FFT_G
    for h in range(FFT_CB):
        flat_ref[pre + (h, pl.ds(j, n, stride=FFT_G), slice(None))] = val[:, h * LANES:(h + 1) * LANES]


def _store_blocks(ref, j, val):
    for h in range(FFT_CB):
        ref[h, j] = val[:, h * LANES:(h + 1) * LANES]


def _fft_a_body(x_ref, fa_ref, o_ref, xf_ref):
    for b in range(2):
        _flatten(xf_ref, x_ref, (b,))
    for j in range(FFT_G):
        xs = jnp.concatenate([_gather_rows(xf_ref, j, (b,)) for b in range(2)], axis=0).astype(BF16)
        _store_blocks(o_ref, j, _pack_c(jnp.dot(fa_ref[j], xs, preferred_element_type=F32)))


def _fft_a(xv, fa, *, ncb, cb0=0):
    _, P, _, n1h, n2, _ = xv.shape
    two_n1 = fa.shape[1]
    n1 = two_n1 // 2
    c0 = cb0 // FFT_CB
    return pl.pallas_call(
        _fft_a_body,
        grid=(n2 // FFT_G, P, ncb // FFT_CB),
        in_specs=[
            pl.BlockSpec((FFT_CB, None, 2, n1h, FFT_G, LANES), lambda g, p, c: (c + c0, p, 0, 0, g, 0)),
            pl.BlockSpec((FFT_G, two_n1, 2 * n1h), lambda g, p, c: (g, 0, 0)),
        ],
        out_specs=pl.BlockSpec((None, FFT_CB, FFT_G, n1, LANES), lambda g, p, c: (p, c, g, 0, 0)),
        out_shape=jax.ShapeDtypeStruct((P, ncb, n2, n1, LANES), U32),
        scratch_shapes=[_flat_scratch(n1h, F32, (2,))],
        compiler_params=_cparams(("arbitrary", "arbitrary", "arbitrary"), 32),
        name="fft_a",
    )(xv, fa)


def _fft_a_real_body(x_ref, fa_ref, o_ref, xf_ref):
    _flatten(xf_ref, x_ref)
    for j in range(FFT_G):
        xs = _gather_rows(xf_ref, j).astype(BF16)
        _store_blocks(o_ref, j, _pack_c(jnp.dot(fa_ref[j], xs, preferred_element_type=F32)))


def _fft_a_real(xv, fa_re):
    ncb, n1h, n2, _ = xv.shape
    two_n1 = fa_re.shape[1]
    n1 = two_n1 // 2
    return pl.pallas_call(
        _fft_a_real_body,
        grid=(n2 // FFT_G, ncb // FFT_CB),
        in_specs=[
            pl.BlockSpec((FFT_CB, n1h, FFT_G, LANES), lambda g, c: (c, 0, g, 0)),
            pl.BlockSpec((FFT_G, two_n1, n1h), lambda g, c: (g, 0, 0)),
        ],
        out_specs=pl.BlockSpec((FFT_CB, FFT_G, n1, LANES), lambda g, c: (c, g, 0, 0)),
        out_shape=jax.ShapeDtypeStruct((ncb, n2, n1, LANES), U32),
        scratch_shapes=[_flat_scratch(n1h, F32)],
        compiler_params=_cparams(("arbitrary", "arbitrary"), 32),
        name="fft_a_real",
    )(xv, fa_re)


def _fft_b_filt_body(f_ref, b_ref, fb_ref, asf_ref, asb_ref, o_ref, ff_ref, bf_ref):
    n2 = FFT_N2
    _flatten(ff_ref, f_ref)
    _flatten(bf_ref, b_ref)
    inv = 1.0 / (asf_ref[...] + asb_ref[...])
    fb = fb_ref[...]
    for j in range(FFT_G):
        u = jnp.dot(fb, _unpack_rows(_gather_rows(ff_ref, j)), preferred_element_type=F32)
        w = jnp.dot(fb, _unpack_rows(_gather_rows(bf_ref, j)), preferred_element_type=F32)
        kr = (u[:n2] + w[:n2]) * inv
        ki = (u[n2:] - w[n2:]) * inv
        _store_blocks(o_ref, j, _pack_c(jnp.concatenate([kr, ki], axis=0)))


def _fft_b_filt(s1f, fb, asum):
    cb2, n2, n1, _ = s1f.shape
    ncb = cb2 // 2
    nb = ncb // FFT_CB
    blk = (FFT_CB, n2, FFT_G, LANES)
    return pl.pallas_call(
        _fft_b_filt_body,
        grid=(n1 // FFT_G, nb),
        in_specs=[
            pl.BlockSpec(blk, lambda g, c: (c, 0, g, 0)),
            pl.BlockSpec(blk, lambda g, c: (c + nb, 0, g, 0)),
            _resident(fb.shape),
            pl.BlockSpec((1, FFT_W), lambda g, c: (0, c)),
            pl.BlockSpec((1, FFT_W), lambda g, c: (0, c + nb)),
        ],
        out_specs=pl.BlockSpec((FFT_CB, FFT_G, n2, LANES), lambda g, c: (c, g, 0, 0)),
        out_shape=jax.ShapeDtypeStruct((ncb, n1, n2, LANES), U32),
        scratch_shapes=[_flat_scratch(n2, U32), _flat_scratch(n2, U32)],
        compiler_params=_cparams(("parallel", "parallel"), 32),
        name="fft_b_filt",
    )(s1f, s1f, fb, asum, asum)


def _fft_b_body(s_ref, kf_ref, fb_ref, fbi_ref, o_ref, sf_ref):
    n2 = FFT_N2
    _flatten(sf_ref, s_ref)
    fb = fb_ref[...]
    fbi = fbi_ref[...]
    for j in range(FFT_G):
        x = jnp.dot(fb, _unpack_rows(_gather_rows(sf_ref, j)), preferred_element_type=F32)
        xr, xi = x[:n2], x[n2:]
        kr, ki = _unpack_c(jnp.concatenate([kf_ref[h, j] for h in range(FFT_CB)], axis=1))
        pm = jnp.concatenate([xr * kr - xi * ki, xr * ki + xi * kr], axis=0).astype(BF16)
        _store_blocks(o_ref, j, _pack_c(jnp.dot(fbi, pm, preferred_element_type=F32)))


def _fft_b(s1, kf, fb, fbi, *, kcb0):
    P, ncb, n2, n1, _ = s1.shape
    k0 = kcb0 // FFT_CB
    return pl.pallas_call(
        _fft_b_body,
        grid=(n1 // FFT_G, ncb // FFT_CB, P),
        in_specs=[
            pl.BlockSpec((None, FFT_CB, n2, FFT_G, LANES), lambda g, c, p: (p, c, 0, g, 0)),
            pl.BlockSpec((FFT_CB, FFT_G, n2, LANES), lambda g, c, p: (c + k0, g, 0, 0)),
            _resident(fb.shape),
            _resident(fbi.shape),
        ],
        out_specs=pl.BlockSpec((None, FFT_CB, FFT_G, n2, LANES), lambda g, c, p: (p, c, g, 0, 0)),
        out_shape=jax.ShapeDtypeStruct((P, ncb, n1, n2, LANES), U32),
        scratch_shapes=[_flat_scratch(n2, U32)],
        compiler_params=_cparams(("parallel", "parallel", "arbitrary"), 32),
        name="fft_b",
    )(s1, kf, fb, fbi)


def _fft_c_body(z_ref, fc_ref, u_ref, gate_ref, d_ref, o_ref, zf_ref, uf_ref, gf_ref, of_ref):
    n1h = o_ref.shape[2]
    d = d_ref[...]
    _flatten(zf_ref, z_ref)
    for b in range(2):
        _flatten(uf_ref, u_ref, (b,))
        _flatten(gf_ref, gate_ref, (b,))
    for j in range(FFT_G):
        y = jnp.dot(fc_ref[j], _unpack_rows(_gather_rows(zf_ref, j)), preferred_element_type=F32)
        for b in range(2):
            yb = y[b * n1h:(b + 1) * n1h]
            res = _gather_rows(gf_ref, j, (b,)) * (yb + _gather_rows(uf_ref, j, (b,)) * d)
            _scatter_rows(of_ref, j, res, (b,))
    for b in range(2):
        _unflatten(o_ref, of_ref, (b,))


def _fft_c(s2, fc, uv, gv, d, *, ucb0, gcb0):
    P, ncb, n1, n2, _ = s2.shape
    n1h = n1 // 2
    u0 = ucb0 // FFT_CB
    g0 = gcb0 // FFT_CB
    tblk = (FFT_CB, None, 2, n1h, FFT_G, LANES)
    return pl.pallas_call(
        _fft_c_body,
        grid=(n2 // FFT_G, P, ncb // FFT_CB),
        in_specs=[
            pl.BlockSpec((None, FFT_CB, n1, FFT_G, LANES), lambda g, p, c: (p, c, 0, g, 0)),
            pl.BlockSpec((FFT_G, 2 * n1h, 2 * n1), lambda g, p, c: (g, 0, 0)),
            pl.BlockSpec(tblk, lambda g, p, c: (c + u0, p, 0, 0, g, 0)),
            pl.BlockSpec(tblk, lambda g, p, c: (c + g0, p, 0, 0, g, 0)),
            pl.BlockSpec((1, FFT_W), lambda g, p, c: (0, c)),
        ],
        out_specs=pl.BlockSpec(tblk, lambda g, p, c: (c, p, 0, 0, g, 0)),
        out_shape=jax.ShapeDtypeStruct((ncb, P, 2, n1h, n2, LANES), F32),
        scratch_shapes=[_flat_scratch(n1, U32), _flat_scratch(n1h, F32, (2,)), _flat_scratch(n1h, F32, (2,)),
                        _flat_scratch(n1h, F32, (2,))],
        compiler_params=_cparams(("arbitrary", "arbitrary", "arbitrary"), 32),
        name="fft_c",
    )(s2, fc, uv, gv, d)


def _oproj_body(x_ref, a_ref, h_ref, ga_ref, gh_ref, wa_ref, wh_ref, o_ref):
    ma = _rms(a_ref[...], ga_ref[...]).astype(BF16)
    hy = jnp.concatenate([h_ref[cb] for cb in range(h_ref.shape[0])], axis=1)
    mh = _rms(hy, gh_ref[...]).astype(BF16)
    o_ref[...] = (x_ref[...] + jnp.dot(ma, wa_ref[...], preferred_element_type=F32)
                  + jnp.dot(mh, wh_ref[...], preferred_element_type=F32))


def _out_proj(x, a, h, ga, gh, wa, wh, *, tm=512):
    T, D = x.shape
    return pl.pallas_call(
        _oproj_body,
        grid=(T // tm,),
        in_specs=[
            pl.BlockSpec((tm, D), lambda i: (i, 0)),
            pl.BlockSpec((tm, D_ATTN), lambda i: (i, 0)),
            pl.BlockSpec((D_HYENA // LANES, tm, LANES), lambda i: (0, i, 0)),
            _resident((1, D_ATTN)), _resident((1, D_HYENA)),
            _resident(wa.shape), _resident(wh.shape),
        ],
        out_specs=pl.BlockSpec((tm, D), lambda i: (i, 0)),
        out_shape=jax.ShapeDtypeStruct((T, D), F32),
        compiler_params=_cparams(("parallel",), 48),
        name="oproj",
    )(x, a, h, ga, gh, wa, wh)


def _rope_tables(L):
    rows = L // GRID_W
    row = jnp.repeat(jnp.arange(rows, dtype=F32), GRID_W)
    col = jnp.tile(jnp.arange(GRID_W, dtype=F32), rows)
    inv = ROPE_THETA ** (-jnp.arange(0, ROPE_HALF, 2, dtype=F32) / ROPE_HALF)
    ar = row[:, None] * inv[None]
    ac = col[:, None] * inv[None]
    cr, sr, cc, sc = jnp.cos(ar), jnp.sin(ar), jnp.cos(ac), jnp.sin(ac)
    cos_t = jnp.concatenate([cr, cr, cc, cc], axis=-1)
    sin_t = jnp.concatenate([-sr, sr, -sc, sc], axis=-1)
    return cos_t, sin_t


def _filter_feats(L):
    t01 = jnp.linspace(0.0, 1.0, L, dtype=F32)[:, None]
    bands = (FILTER_EMB - 1) // 2
    fr = jnp.linspace(1e-4, bands - 1, bands, dtype=F32)[None]
    w = 2.0 * math.pi * jnp.arange(L, dtype=F32)[:, None] / L
    feats = jnp.concatenate([t01, jnp.cos(fr * w), -jnp.sin(fr * w)], axis=-1)
    return jnp.pad(feats, ((0, 0), (0, FILT_PAD - FILTER_EMB)))


def _dft_tables(L):
    n = 2 * L
    n2 = FFT_N2
    n1 = n // n2
    n1h = n1 // 2
    k1 = jnp.arange(n1, dtype=jnp.int32)[None, :, None]
    t1 = jnp.arange(n1h, dtype=jnp.int32)[None, None, :]
    t2 = jnp.arange(n2, dtype=jnp.int32)[:, None, None]
    ang = (-2.0 * math.pi / n) * ((k1 * (n2 * t1 + t2)) % n).astype(F32)
    cr, ci = jnp.cos(ang), jnp.sin(ang)
    fa = jnp.concatenate([jnp.concatenate([cr, -ci], -1), jnp.concatenate([ci, cr], -1)], -2)
    fa_re = jnp.concatenate([cr, ci], -2)
    crt, cit = jnp.swapaxes(cr, 1, 2), jnp.swapaxes(ci, 1, 2)
    fc = jnp.concatenate([jnp.concatenate([crt, cit], -1), jnp.concatenate([-cit, crt], -1)], -2) / n
    a = jnp.arange(n2, dtype=jnp.int32)
    ang2 = (-2.0 * math.pi / n2) * ((a[:, None] * a[None, :]) % n2).astype(F32)
    fr, fi = jnp.cos(ang2), jnp.sin(ang2)
    fb = jnp.concatenate([jnp.concatenate([fr, -fi], -1), jnp.concatenate([fi, fr], -1)], -2)
    fbi = jnp.concatenate([jnp.concatenate([fr, fi], -1), jnp.concatenate([-fi, fr], -1)], -2)
    return (fa.astype(BF16), fa_re.astype(BF16), fb.astype(BF16), fbi.astype(BF16), fc.astype(BF16))


def _trunk(x, p):
    B, L, D = x.shape
    T = B * L
    P = B // 2
    n2 = FFT_N2
    n1h = L // n2
    C = D_HYENA
    ncb = C // LANES

    x0 = x.reshape(T, D)
    x1 = _ffn(x0, p["ffn1_norm"], p["ffn1_w13"], p["ffn1_w2"], p["final_norm"], final_norm=False)

    cos_t, sin_t = _rope_tables(L)
    q, k, v, hyc = _mix_in(x1, p["mix_norm"], p["wq"], p["wkv"], p["wh"], p["q_norm"], p["k_norm"],
                           cos_t, sin_t, p["conv_w"], p["conv_b"], L=L)
    attn = _attention(q.reshape(B, L, D_ATTN), k.reshape(B, L, D_KV), v.reshape(B, L, D_KV))
    hyv = hyc.reshape(3 * ncb, P, 2, n1h, n2, LANES)

    fa, fa_re, fb, fbi, fc = _dft_tables(L)
    hf, asum = _filters(_filter_feats(L), p["filt_w1"], p["filt_b1"], p["filt_w2"], p["filt_b2"],
                        p["filt_w3"], p["filt_freq"], p["decay"])
    s1f = _fft_a_real(hf.reshape(4 * ncb, n1h, n2, LANES), fa_re)
    kf = _fft_b_filt(s1f, fb, asum)

    s1 = _fft_a(hyv, fa, ncb=ncb, cb0=0)
    s2 = _fft_b(s1, kf, fb, fbi, kcb0=0)
    z = _fft_c(s2, fc, hyv, hyv, p["bias0"], ucb0=0, gcb0=ncb)
    s1 = _fft_a(z, fa, ncb=ncb, cb0=0)
    s2 = _fft_b(s1, kf, fb, fbi, kcb0=ncb)
    ho = _fft_c(s2, fc, z, hyv, p["bias1"], ucb0=0, gcb0=2 * ncb)

    x2 = _out_proj(x1, attn.reshape(T, D_ATTN), ho.reshape(ncb, T, LANES), p["gon_a"], p["gon_h"],
                   p["wo_a"], p["wo_h"])
    x3 = _ffn(x2, p["ffn2_norm"], p["ffn2_w13"], p["ffn2_w2"], p["final_norm"], final_norm=True)
    return x3.reshape(B, L, D)


def kernel(x_prompt, x_sample, ffn1_norm, ffn1_w13, ffn1_w2, mix_norm, w_in, q_norm, k_norm, conv_w, conv_b, filt_w1, filt_b1, filt_w2, filt_b2, filt_w3, filt_freq, hyena_decay, hyena_bias, group_out_norm, w_out, ffn2_norm, ffn2_w13, ffn2_w2, final_norm):
    hp = FILT_PAD - FILTER_HIDDEN
    w_in0 = w_in[0]
    p = {
        "ffn1_norm": ffn1_norm[0][None], "ffn2_norm": ffn2_norm[0][None],
        "ffn1_w13": ffn1_w13[0].astype(BF16), "ffn1_w2": ffn1_w2[0].astype(BF16),
        "ffn2_w13": ffn2_w13[0].astype(BF16), "ffn2_w2": ffn2_w2[0].astype(BF16),
        "final_norm": final_norm[None],
        "mix_norm": mix_norm[0][None],
        "wq": w_in0[:, :D_ATTN].astype(BF16),
        "wkv": w_in0[:, D_ATTN:D_ATTN + 2 * D_KV].astype(BF16),
        "wh": w_in0[:, D_ATTN + 2 * D_KV:].astype(BF16),
        "q_norm": q_norm[0][None], "k_norm": k_norm[0][None],
        "conv_w": conv_w[0], "conv_b": conv_b[0][None],
        "filt_w1": jnp.pad(filt_w1[0], ((0, FILT_PAD - FILTER_EMB), (0, hp))),
        "filt_b1": jnp.pad(filt_b1[0], (0, hp))[None],
        "filt_w2": jnp.pad(filt_w2[0], ((0, hp), (0, hp))),
        "filt_b2": jnp.pad(filt_b2[0], (0, hp))[None],
        "filt_w3": jnp.pad(filt_w3[0], ((0, hp), (0, 0))),
        "filt_freq": jnp.pad(filt_freq[0], (0, hp))[None],
        "decay": hyena_decay[0].reshape(1, 4 * D_HYENA),
        "bias0": hyena_bias[0, 0][None], "bias1": hyena_bias[0, 1][None],
        "gon_a": group_out_norm[0, :D_ATTN][None], "gon_h": group_out_norm[0, D_ATTN:][None],
        "wo_a": w_out[0, :D_ATTN].astype(BF16), "wo_h": w_out[0, D_ATTN:].astype(BF16),
    }
    return (_trunk(x_prompt, p), _trunk(x_sample, p))
```

```python
import functools
import math

import jax
import jax.numpy as jnp
from jax import lax
from jax.experimental import pallas as pl
from jax.experimental.pallas import tpu as pltpu

F32 = jnp.float32
BF16 = jnp.bfloat16
U32 = jnp.uint32

D_MODEL = 2048
GRID_W = 64
D_ATTN = D_MODEL // 2
D_HYENA = D_MODEL - D_ATTN
HEAD_DIM = 128
N_Q_HEADS = D_ATTN // HEAD_DIM
N_KV_HEADS = 2
Q_PER_KV = N_Q_HEADS // N_KV_HEADS
ROPE_HALF = HEAD_DIM // 2
ROPE_THETA = 10000.0
FILTER_EMB = 33
FILTER_HIDDEN = 64
D_FF = 5632
EPS = 1e-6
D_KV = N_KV_HEADS * HEAD_DIM

LANES = 128
FILT_PAD = 128
FFT_N2 = 128
FFT_W = 256
FFT_G = 8
MIB = 1024 * 1024
LOG2_E = math.log2(math.e)


def _cparams(semantics, vmem_mib):
    return pltpu.CompilerParams(dimension_semantics=semantics,
                                vmem_limit_bytes=vmem_mib * MIB)


def _rms(x, g):
    ms = jnp.mean(x * x, axis=-1, keepdims=True)
    return x * lax.rsqrt(ms + EPS) * g


def _resident(shape):
    nd = len(shape)
    return pl.BlockSpec(shape, lambda *_: (0,) * nd, pipeline_mode=pl.Buffered(1))


def _ffn_body(x_ref, g_ref, wg_ref, wu_ref, w2_ref, gf_ref, o_ref, xn_ref, *, nj, final_norm):
    j = pl.program_id(1)

    @pl.when(j == 0)
    def _init():
        x = x_ref[...]
        xn_ref[...] = _rms(x, g_ref[...]).astype(BF16)
        o_ref[...] = x

    xn = xn_ref[...]
    hg = jnp.dot(xn, wg_ref[...], preferred_element_type=F32)
    hu = jnp.dot(xn, wu_ref[...], preferred_element_type=F32)
    a = (hg * (0.5 / (1.0 + jnp.exp(-hg))) * hu).astype(BF16)
    o_ref[...] += jnp.dot(a, w2_ref[...], preferred_element_type=F32)

    if final_norm:
        @pl.when(j == nj - 1)
        def _fin():
            o_ref[...] = _rms(o_ref[...], gf_ref[...])


def _ffn(x, g, w13, w2, gf, *, final_norm, tm=512, tf=512):
    T, D = x.shape
    ff = w2.shape[0]
    nj = ff // tf
    body = functools.partial(_ffn_body, nj=nj, final_norm=final_norm)
    return pl.pallas_call(
        body,
        grid=(T // tm, nj),
        in_specs=[
            pl.BlockSpec((tm, D), lambda i, j: (i, 0)),
            pl.BlockSpec((1, D), lambda i, j: (0, 0)),
            pl.BlockSpec((D, tf), lambda i, j: (0, j)),
            pl.BlockSpec((D, tf), lambda i, j: (0, j + nj)),
            pl.BlockSpec((tf, D), lambda i, j: (j, 0)),
            pl.BlockSpec((1, D), lambda i, j: (0, 0)),
        ],
        out_specs=pl.BlockSpec((tm, D), lambda i, j: (i, 0)),
        out_shape=jax.ShapeDtypeStruct((T, D), F32),
        scratch_shapes=[pltpu.VMEM((tm, D), BF16)],
        compiler_params=_cparams(("parallel", "arbitrary"), 48),
        name="ffn",
    )(x, g, w13, w13, w2, gf)


def _store_time_grouped(ref, cb, val):
    n1t = ref.shape[2]
    v = val.reshape(n1t * (FFT_N2 // FFT_G), FFT_G, LANES)
    for a in range(n1t):
        ref[cb, :, a] = v[a * (FFT_N2 // FFT_G):(a + 1) * (FFT_N2 // FFT_G)]


def _load_time_grouped(ref, cb):
    n1t = ref.shape[2]
    return jnp.concatenate([ref[cb, :, a].reshape(FFT_N2, LANES) for a in range(n1t)], axis=0)


MIX_HALO = 16


def _mix_body(xb_ref, x_ref, xa_ref, g_ref, wq_ref, wkv_ref, wh_ref, qn_ref, kn_ref, cos_ref, sin_ref,
              cw_ref, cb_ref, q_ref, k_ref, v_ref, hy_ref, *, npos):
    tm = x_ref.shape[0]
    x_ext = jnp.concatenate([xb_ref[...], x_ref[...], xa_ref[...]], axis=0)
    h_ext = _rms(x_ext, g_ref[...]).astype(BF16)
    h = h_ext[MIX_HALO:MIX_HALO + tm]
    lane = lax.broadcasted_iota(jnp.int32, (tm, HEAD_DIM), 1)
    low_half = (lane % (2 * (ROPE_HALF // 2))) < (ROPE_HALF // 2)
    c = cos_ref[...]
    s = sin_ref[...]

    def head(p, gain, scale):
        y = _rms(p, gain)
        partner = jnp.where(low_half, pltpu.roll(y, HEAD_DIM - ROPE_HALF // 2, 1),
                            pltpu.roll(y, ROPE_HALF // 2, 1))
        y = y * c + partner * s
        return y if scale is None else y * scale

    q = jnp.dot(h, wq_ref[...], preferred_element_type=F32)
    for hd in range(N_Q_HEADS):
        sl = slice(hd * HEAD_DIM, (hd + 1) * HEAD_DIM)
        q_ref[:, sl] = head(q[:, sl], qn_ref[...], HEAD_DIM ** -0.5 * LOG2_E).astype(BF16)
    kv = jnp.dot(h, wkv_ref[...], preferred_element_type=F32)
    for hd in range(N_KV_HEADS):
        sl = slice(hd * HEAD_DIM, (hd + 1) * HEAD_DIM)
        k_ref[:, sl] = head(kv[:, sl], kn_ref[...], None).astype(BF16)
    v_ref[...] = kv[:, D_KV:].astype(BF16)

    hy = jnp.dot(h_ext, wh_ref[...], preferred_element_type=F32)
    pos = pl.program_id(0) % npos
    rid = lax.broadcasted_iota(jnp.int32, (tm, 1), 0)
    before = jnp.where((rid == 0) & (pos == 0), 0.0, hy[MIX_HALO - 1:MIX_HALO - 1 + tm])
    after = jnp.where((rid == tm - 1) & (pos == npos - 1), 0.0, hy[MIX_HALO + 1:MIX_HALO + 1 + tm])
    hyc = (before * cw_ref[0:1, :] + hy[MIX_HALO:MIX_HALO + tm] * cw_ref[1:2, :]
           + after * cw_ref[2:3, :] + cb_ref[...])
    for cb in range(hy_ref.shape[0]):
        _store_time_grouped(hy_ref, cb, hyc[:, cb * LANES:(cb + 1) * LANES])


def _mix_in(x, g, wq, wkv, wh, qn, kn, cos_t, sin_t, cw, cb, *, L, tm=256):
    T, D = x.shape
    npos = L // tm
    ng2 = FFT_N2 // FFT_G
    hb = tm // MIX_HALO
    nhb = T // MIX_HALO
    return pl.pallas_call(
        functools.partial(_mix_body, npos=npos),
        grid=(T // tm,),
        in_specs=[
            pl.BlockSpec((MIX_HALO, D), lambda i: (jnp.maximum(i * hb - 1, 0), 0)),
            pl.BlockSpec((tm, D), lambda i: (i, 0)),
            pl.BlockSpec((MIX_HALO, D), lambda i: (jnp.minimum((i + 1) * hb, nhb - 1), 0)),
            _resident((1, D)),
            _resident(wq.shape),
            _resident(wkv.shape),
            _resident(wh.shape),
            _resident((1, HEAD_DIM)),
            _resident((1, HEAD_DIM)),
            pl.BlockSpec((tm, HEAD_DIM), lambda i: (i % npos, 0)),
            pl.BlockSpec((tm, HEAD_DIM), lambda i: (i % npos, 0)),
            _resident(cw.shape),
            _resident(cb.shape),
        ],
        out_specs=[
            pl.BlockSpec((tm, D_ATTN), lambda i: (i, 0)),
            pl.BlockSpec((tm, D_KV), lambda i: (i, 0)),
            pl.BlockSpec((tm, D_KV), lambda i: (i, 0)),
            pl.BlockSpec((3 * D_HYENA // LANES, None, ng2, tm // FFT_N2, FFT_G, LANES),
                         lambda i: (0, i // npos, 0, i % npos, 0, 0)),
        ],
        out_shape=[
            jax.ShapeDtypeStruct((T, D_ATTN), BF16),
            jax.ShapeDtypeStruct((T, D_KV), BF16),
            jax.ShapeDtypeStruct((T, D_KV), BF16),
            jax.ShapeDtypeStruct((3 * D_HYENA // LANES, T // L, ng2, L // FFT_N2, FFT_G, LANES), F32),
        ],
        compiler_params=_cparams(("parallel",), 48),
        name="mix_in",
    )(x, x, x, g, wq, wkv, wh, qn, kn, cos_t, sin_t, cw, cb)


ATTN_ROWS = 32
ATTN_RING = 4


def _attn_body(q_ref, k_ref, v_ref, o_ref, vt_ref, *bufs, tq, tk, nk):
    s_refs, p_refs = bufs[:ATTN_RING], bufs[ATTN_RING:]

    @pl.when(pl.program_id(2) == 0)
    def _transpose_v():
        for c in range(nk):
            vt_ref[c] = v_ref[pl.ds(c * tk, tk), :].astype(F32).T.astype(BF16)

    q = jnp.concatenate([q_ref[:, h * HEAD_DIM:(h + 1) * HEAD_DIM] for h in range(Q_PER_KV)], axis=0)
    m_cols = Q_PER_KV * tq
    nchunk = tk // ATTN_ROWS

    def stage_a(i, s_ref):
        kc = k_ref[pl.ds(pl.multiple_of(i * tk, tk), tk), :]
        s_ref[...] = lax.dot_general(kc, q, (((1,), (1,)), ((), ())), preferred_element_type=F32)

    def stage_b(s_ref, p_ref, m, l):
        rows = lambda c: slice(c * ATTN_ROWS, (c + 1) * ATTN_ROWS)
        mx = s_ref[rows(0), :]
        for c in range(1, nchunk):
            mx = jnp.maximum(mx, s_ref[rows(c), :])
        m_new = jnp.maximum(m, jnp.max(mx, axis=0, keepdims=True))
        alpha = jnp.exp2(m - m_new)
        psum = None
        for c in range(nchunk):
            p = jnp.exp2(s_ref[rows(c), :] - m_new)
            psum = p if psum is None else psum + p
            p_ref[rows(c), :] = p.astype(BF16)
        return m_new, alpha * l + jnp.sum(psum, axis=0, keepdims=True), alpha

    def stage_d(i, p_ref, alpha, acc):
        return alpha * acc + jnp.dot(vt_ref[i], p_ref[...], preferred_element_type=F32)

    def ring(i, m, l, al0, al1, acc, n_a, n_b):
        alphas = {0: al0, 1: al1}
        for u in range(ATTN_RING):
            if u < n_a:
                stage_a(i + u + 4, s_refs[u])
            if u < n_b:
                v = (u + 2) % ATTN_RING
                m, l, alphas[u + 2] = stage_b(s_refs[v], p_refs[v], m, l)
            acc = stage_d(i + u, p_refs[u], alphas[u], acc)
        return m, l, alphas.get(4), alphas.get(5), acc

    m = jnp.full((1, m_cols), -jnp.inf, F32)
    l = jnp.zeros((1, m_cols), F32)
    acc = jnp.zeros((HEAD_DIM, m_cols), F32)
    for u in range(ATTN_RING):
        stage_a(u, s_refs[u])
    m, l, al0 = stage_b(s_refs[0], p_refs[0], m, l)
    m, l, al1 = stage_b(s_refs[1], p_refs[1], m, l)

    def trip(j, carry):
        return ring(ATTN_RING * j, *carry, ATTN_RING, ATTN_RING)

    m, l, al0, al1, acc = lax.fori_loop(0, nk // ATTN_RING - 1, trip, (m, l, al0, al1, acc))
    _, l, _, _, acc = ring(nk - ATTN_RING, m, l, al0, al1, acc, 0, 2)
    o = acc / l
    for h in range(Q_PER_KV):
        o_ref[:, h * HEAD_DIM:(h + 1) * HEAD_DIM] = o[:, h * tq:(h + 1) * tq].T


def _attention(q, k, v, *, tq=128, tk=512):
    B, L, _ = q.shape
    qw = Q_PER_KV * HEAD_DIM
    nk = L // tk
    assert nk % ATTN_RING == 0 and nk >= 2 * ATTN_RING
    body = functools.partial(_attn_body, tq=tq, tk=tk, nk=nk)
    m_cols = Q_PER_KV * tq
    return pl.pallas_call(
        body,
        grid=(B, N_KV_HEADS, L // tq),
        in_specs=[
            pl.BlockSpec((None, tq, qw), lambda b, g, i: (b, i, g)),
            pl.BlockSpec((None, L, HEAD_DIM), lambda b, g, i: (b, 0, g)),
            pl.BlockSpec((None, L, HEAD_DIM), lambda b, g, i: (b, 0, g)),
        ],
        out_specs=pl.BlockSpec((None, tq, qw), lambda b, g, i: (b, i, g)),
        out_shape=jax.ShapeDtypeStruct((B, L, D_ATTN), F32),
        scratch_shapes=([pltpu.VMEM((nk, HEAD_DIM, tk), BF16)]
                        + [pltpu.VMEM((tk, m_cols), F32)] * ATTN_RING
                        + [pltpu.VMEM((tk, m_cols), BF16)] * ATTN_RING),
        compiler_params=_cparams(("parallel", "parallel", "arbitrary"), 32),
        name="attn",
    )(q, k, v)


def _filt_body(feat_ref, w1_ref, b1_ref, w2_ref, b2_ref, w3_ref, fq_ref, dec_ref, hf_ref, asum_ref, *, tl):
    i = pl.program_id(0)
    hp = lax.Precision.HIGHEST
    fq = fq_ref[...]
    feat = feat_ref[...]
    h = jnp.sin(fq * (jnp.dot(feat, w1_ref[...], precision=hp, preferred_element_type=F32) + b1_ref[...]))
    h = jnp.sin(fq * (jnp.dot(h, w2_ref[...], precision=hp, preferred_element_type=F32) + b2_ref[...]))
    h3 = jnp.dot(h, w3_ref[...], precision=hp, preferred_element_type=F32)
    t01 = feat[:, 0:1]
    hf = h3 * jnp.exp(-t01 * jnp.abs(dec_ref[...]))
    ncol = hf.shape[1]
    row = lax.broadcasted_iota(jnp.int32, hf.shape, 0) + i * tl
    col = lax.broadcasted_iota(jnp.int32, hf.shape, 1)
    hf = jnp.where((row == 0) & (col >= ncol // 2), 0.0, hf)
    for cb in range(ncol // LANES):
        _store_time_grouped(hf_ref, cb, hf[:, cb * LANES:(cb + 1) * LANES])
    part = jnp.sum(jnp.abs(hf), axis=0, keepdims=True)

    @pl.when(i == 0)
    def _first():
        asum_ref[...] = part

    @pl.when(i > 0)
    def _rest():
        asum_ref[...] += part


def _filters(feats, w1, b1, w2, b2, w3, fq, dec, *, tl=256):
    L = feats.shape[0]
    nc = w3.shape[1]
    ng2 = FFT_N2 // FFT_G
    body = functools.partial(_filt_body, tl=tl)
    return pl.pallas_call(
        body,
        grid=(L // tl,),
        in_specs=[
            pl.BlockSpec((tl, FILT_PAD), lambda i: (i, 0)),
            _resident(w1.shape), _resident(b1.shape), _resident(w2.shape), _resident(b2.shape),
            _resident(w3.shape), _resident(fq.shape), _resident(dec.shape),
        ],
        out_specs=[
            pl.BlockSpec((nc // LANES, ng2, tl // FFT_N2, FFT_G, LANES), lambda i: (0, 0, i, 0, 0)),
            pl.BlockSpec((1, nc), lambda i: (0, 0)),
        ],
        out_shape=[jax.ShapeDtypeStruct((nc // LANES, ng2, L // FFT_N2, FFT_G, LANES), F32),
                   jax.ShapeDtypeStruct((1, nc), F32)],
        compiler_params=_cparams(("arbitrary",), 48),
        name="filt",
    )(feats, w1, b1, w2, b2, w3, fq, dec)


FFT_CB = FFT_W // LANES


def _pack_c(y):
    n = y.shape[0] // 2
    r = lax.bitcast_convert_type(y[:n], U32)
    i = lax.bitcast_convert_type(y[n:], U32)
    half = jnp.uint32(0x8000)
    return ((r + half) & jnp.uint32(0xFFFF0000)) | ((i + half) >> 16)


def _unpack_c(w):
    re = lax.bitcast_convert_type(w & jnp.uint32(0xFFFF0000), F32)
    im = lax.bitcast_convert_type(w << 16, F32)
    return re, im


def _unpack_rows(w):
    re, im = _unpack_c(w)
    return jnp.concatenate([re, im], axis=0).astype(BF16)


def _flat_scratch(n, dtype, lead=()):
    return pltpu.VMEM(lead + (FFT_CB, n * FFT_G, LANES), dtype)


def _flatten(flat_ref, block_ref, pre=()):
    rows = flat_ref.shape[-2]
    for h in range(FFT_CB):
        flat_ref[pre + (h,)] = block_ref[(h,) + pre].reshape(rows, LANES)


def _unflatten(block_ref, flat_ref, pre=()):
    n = flat_ref.shape[-2] // FFT_G
    for h in range(FFT_CB):
        block_ref[(h,) + pre] = flat_ref[pre + (h,)].reshape(n, FFT_G, LANES)


def _gather_rows(flat_ref, j, pre=()):
    n = flat_ref.shape[-2] // FFT_G
    return jnp.concatenate(
        [flat_ref[pre + (h, pl.ds(j, n, stride=FFT_G), slice(None))] for h in range(FFT_CB)], axis=1)


def _scatter_rows(flat_ref, j, val, pre=()):
    n = flat_ref.shape[-2] // FFT_G
    for h in range(FFT_CB):
        flat_ref[pre + (h, pl.ds(j, n, stride=FFT_G), slice(None))] = val[:, h * LANES:(h + 1) * LANES]


def _store_blocks(ref, j, val):
    for h in range(FFT_CB):
        ref[h, j] = val[:, h * LANES:(h + 1) * LANES]


def _store_grouped(ref, j, val):
    groups = ref.shape[1]
    for h in range(FFT_CB):
        ref[h, :, j] = val[:, h * LANES:(h + 1) * LANES].reshape(groups, FFT_G, LANES)


def _fft_a_body(x_ref, fa_ref, o_ref, xf_ref):
    for b in range(2):
        _flatten(xf_ref, x_ref, (b,))
    for j in range(FFT_G):
        xs = jnp.concatenate([_gather_rows(xf_ref, j, (b,)) for b in range(2)], axis=0).astype(BF16)
        _store_grouped(o_ref, j, _pack_c(jnp.dot(fa_ref[j], xs, preferred_element_type=F32)))


def _fft_a(xv, fa, *, ncb, cb0=0):
    _, P, _, ng2, n1h, _, _ = xv.shape
    n2 = ng2 * FFT_G
    two_n1 = fa.shape[1]
    n1 = two_n1 // 2
    c0 = cb0 // FFT_CB
    return pl.pallas_call(
        _fft_a_body,
        grid=(ng2, P, ncb // FFT_CB),
        in_specs=[
            pl.BlockSpec((FFT_CB, None, 2, None, n1h, FFT_G, LANES), lambda g, p, c: (c + c0, p, 0, g, 0, 0, 0)),
            pl.BlockSpec((FFT_G, two_n1, 2 * n1h), lambda g, p, c: (g, 0, 0)),
        ],
        out_specs=pl.BlockSpec((None, FFT_CB, n1 // FFT_G, FFT_G, FFT_G, LANES),
                               lambda g, p, c: (p, c, 0, g, 0, 0)),
        out_shape=jax.ShapeDtypeStruct((P, ncb, n1 // FFT_G, n2, FFT_G, LANES), U32),
        scratch_shapes=[_flat_scratch(n1h, F32, (2,))],
        compiler_params=_cparams(("arbitrary", "arbitrary", "arbitrary"), 32),
        name="fft_a",
    )(xv, fa)


def _fft_a_real_body(x_ref, fa_ref, o_ref, xf_ref):
    _flatten(xf_ref, x_ref)
    for j in range(FFT_G):
        xs = _gather_rows(xf_ref, j).astype(BF16)
        _store_grouped(o_ref, j, _pack_c(jnp.dot(fa_ref[j], xs, preferred_element_type=F32)))


def _fft_a_real(xv, fa_re):
    ncb, ng2, n1h, _, _ = xv.shape
    n2 = ng2 * FFT_G
    two_n1 = fa_re.shape[1]
    n1 = two_n1 // 2
    return pl.pallas_call(
        _fft_a_real_body,
        grid=(ng2, ncb // FFT_CB),
        in_specs=[
            pl.BlockSpec((FFT_CB, None, n1h, FFT_G, LANES), lambda g, c: (c, g, 0, 0, 0)),
            pl.BlockSpec((FFT_G, two_n1, n1h), lambda g, c: (g, 0, 0)),
        ],
        out_specs=pl.BlockSpec((FFT_CB, n1 // FFT_G, FFT_G, FFT_G, LANES), lambda g, c: (c, 0, g, 0, 0)),
        out_shape=jax.ShapeDtypeStruct((ncb, n1 // FFT_G, n2, FFT_G, LANES), U32),
        scratch_shapes=[_flat_scratch(n1h, F32)],
        compiler_params=_cparams(("arbitrary", "arbitrary"), 32),
        name="fft_a_real",
    )(xv, fa_re)


def _fft_b_filt_body(f_ref, b_ref, fb_ref, asf_ref, asb_ref, o_ref, ff_ref, bf_ref):
    n2 = FFT_N2
    _flatten(ff_ref, f_ref)
    _flatten(bf_ref, b_ref)
    inv = 1.0 / (asf_ref[...] + asb_ref[...])
    fb = fb_ref[...]
    for j in range(FFT_G):
        u = jnp.dot(fb, _unpack_rows(_gather_rows(ff_ref, j)), preferred_element_type=F32)
        w = jnp.dot(fb, _unpack_rows(_gather_rows(bf_ref, j)), preferred_element_type=F32)
        kr = (u[:n2] + w[:n2]) * inv
        ki = (u[n2:] - w[n2:]) * inv
        _store_blocks(o_ref, j, _pack_c(jnp.concatenate([kr, ki], axis=0)))


def _fft_b_filt(s1f, fb, asum):
    cb2, ng1, n2, _, _ = s1f.shape
    n1 = ng1 * FFT_G
    ncb = cb2 // 2
    nb = ncb // FFT_CB
    blk = (FFT_CB, None, n2, FFT_G, LANES)
    return pl.pallas_call(
        _fft_b_filt_body,
        grid=(ng1, nb),
        in_specs=[
            pl.BlockSpec(blk, lambda g, c: (c, g, 0, 0, 0)),
            pl.BlockSpec(blk, lambda g, c: (c + nb, g, 0, 0, 0)),
            _resident(fb.shape),
            pl.BlockSpec((1, FFT_W), lambda g, c: (0, c)),
            pl.BlockSpec((1, FFT_W), lambda g, c: (0, c + nb)),
        ],
        out_specs=pl.BlockSpec((FFT_CB, FFT_G, n2, LANES), lambda g, c: (c, g, 0, 0)),
        out_shape=jax.ShapeDtypeStruct((ncb, n1, n2, LANES), U32),
        scratch_shapes=[_flat_scratch(n2, U32), _flat_scratch(n2, U32)],
        compiler_params=_cparams(("parallel", "parallel"), 32),
        name="fft_b_filt",
    )(s1f, s1f, fb, asum, asum)


def _fft_b_body(s_ref, kf_ref, fb_ref, fbi_ref, o_ref, sf_ref):
    n2 = FFT_N2
    _flatten(sf_ref, s_ref)
    fb = fb_ref[...]
    fbi = fbi_ref[...]
    for j in range(FFT_G):
        x = jnp.dot(fb, _unpack_rows(_gather_rows(sf_ref, j)), preferred_element_type=F32)
        xr, xi = x[:n2], x[n2:]
        kr, ki = _unpack_c(jnp.concatenate([kf_ref[h, j] for h in range(FFT_CB)], axis=1))
        pm = jnp.concatenate([xr * kr - xi * ki, xr * ki + xi * kr], axis=0).astype(BF16)
        _store_grouped(o_ref, j, _pack_c(jnp.dot(fbi, pm, preferred_element_type=F32)))


def _fft_b(s1, kf, fb, fbi, *, kcb0):
    P, ncb, ng1, n2, _, _ = s1.shape
    n1 = ng1 * FFT_G
    k0 = kcb0 // FFT_CB
    return pl.pallas_call(
        _fft_b_body,
        grid=(ng1, ncb // FFT_CB, P),
        in_specs=[
            pl.BlockSpec((None, FFT_CB, None, n2, FFT_G, LANES), lambda g, c, p: (p, c, g, 0, 0, 0)),
            pl.BlockSpec((FFT_CB, FFT_G, n2, LANES), lambda g, c, p: (c + k0, g, 0, 0)),
            _resident(fb.shape),
            _resident(fbi.shape),
        ],
        out_specs=pl.BlockSpec((None, FFT_CB, n2 // FFT_G, FFT_G, FFT_G, LANES),
                               lambda g, c, p: (p, c, 0, g, 0, 0)),
        out_shape=jax.ShapeDtypeStruct((P, ncb, n2 // FFT_G, n1, FFT_G, LANES), U32),
        scratch_shapes=[_flat_scratch(n2, U32)],
        compiler_params=_cparams(("parallel", "parallel", "arbitrary"), 32),
        name="fft_b",
    )(s1, kf, fb, fbi)


def _fft_c_body(z_ref, fc_ref, u_ref, gate_ref, d_ref, o_ref, zf_ref, uf_ref, gf_ref, of_ref):
    n1h = o_ref.shape[2]
    d = d_ref[...]
    _flatten(zf_ref, z_ref)
    for b in range(2):
        _flatten(uf_ref, u_ref, (b,))
        _flatten(gf_ref, gate_ref, (b,))
    for j in range(FFT_G):
        y = jnp.dot(fc_ref[j], _unpack_rows(_gather_rows(zf_ref, j)), preferred_element_type=F32)
        for b in range(2):
            yb = y[b * n1h:(b + 1) * n1h]
            res = _gather_rows(gf_ref, j, (b,)) * (yb + _gather_rows(uf_ref, j, (b,)) * d)
            _scatter_rows(of_ref, j, res, (b,))
    for b in range(2):
        _unflatten(o_ref, of_ref, (b,))


def _fft_c(s2, fc, uv, gv, d, *, ucb0, gcb0):
    P, ncb, ng2, n1, _, _ = s2.shape
    n1h = n1 // 2
    u0 = ucb0 // FFT_CB
    g0 = gcb0 // FFT_CB
    tblk = (FFT_CB, None, 2, None, n1h, FFT_G, LANES)
    return pl.pallas_call(
        _fft_c_body,
        grid=(ng2, P, ncb // FFT_CB),
        in_specs=[
            pl.BlockSpec((None, FFT_CB, None, n1, FFT_G, LANES), lambda g, p, c: (p, c, g, 0, 0, 0)),
            pl.BlockSpec((FFT_G, 2 * n1h, 2 * n1), lambda g, p, c: (g, 0, 0)),
            pl.BlockSpec(tblk, lambda g, p, c: (c + u0, p, 0, g, 0, 0, 0)),
            pl.BlockSpec(tblk, lambda g, p, c: (c + g0, p, 0, g, 0, 0, 0)),
            pl.BlockSpec((1, FFT_W), lambda g, p, c: (0, c)),
        ],
        out_specs=pl.BlockSpec(tblk, lambda g, p, c: (c, p, 0, g, 0, 0, 0)),
        out_shape=jax.ShapeDtypeStruct((ncb, P, 2, ng2, n1h, FFT_G, LANES), F32),
        scratch_shapes=[_flat_scratch(n1, U32), _flat_scratch(n1h, F32, (2,)), _flat_scratch(n1h, F32, (2,)),
                        _flat_scratch(n1h, F32, (2,))],
        compiler_params=_cparams(("arbitrary", "arbitrary", "arbitrary"), 32),
        name="fft_c",
    )(s2, fc, uv, gv, d)


def _oproj_body(x_ref, a_ref, h_ref, ga_ref, gh_ref, wa_ref, wh_ref, o_ref):
    ma = _rms(a_ref[...], ga_ref[...]).astype(BF16)
    hy = jnp.concatenate([_load_time_grouped(h_ref, cb) for cb in range(h_ref.shape[0])], axis=1)
    mh = _rms(hy, gh_ref[...]).astype(BF16)
    o_ref[...] = (x_ref[...] + jnp.dot(ma, wa_ref[...], preferred_element_type=F32)
                  + jnp.dot(mh, wh_ref[...], preferred_element_type=F32))


def _out_proj(x, a, h, ga, gh, wa, wh, *, tm=512):
    T, D = x.shape
    ncb, _, ng2, n1h, _, _ = h.shape
    npos = n1h * FFT_N2 // tm
    return pl.pallas_call(
        _oproj_body,
        grid=(T // tm,),
        in_specs=[
            pl.BlockSpec((tm, D), lambda i: (i, 0)),
            pl.BlockSpec((tm, D_ATTN), lambda i: (i, 0)),
            pl.BlockSpec((ncb, None, ng2, tm // FFT_N2, FFT_G, LANES), lambda i: (0, i // npos, 0, i % npos, 0, 0)),
            _resident((1, D_ATTN)), _resident((1, D_HYENA)),
            _resident(wa.shape), _resident(wh.shape),
        ],
        out_specs=pl.BlockSpec((tm, D), lambda i: (i, 0)),
        out_shape=jax.ShapeDtypeStruct((T, D), F32),
        compiler_params=_cparams(("parallel",), 48),
        name="oproj",
    )(x, a, h, ga, gh, wa, wh)


def _rope_tables(L):
    rows = L // GRID_W
    row = jnp.repeat(jnp.arange(rows, dtype=F32), GRID_W)
    col = jnp.tile(jnp.arange(GRID_W, dtype=F32), rows)
    inv = ROPE_THETA ** (-jnp.arange(0, ROPE_HALF, 2, dtype=F32) / ROPE_HALF)
    ar = row[:, None] * inv[None]
    ac = col[:, None] * inv[None]
    cr, sr, cc, sc = jnp.cos(ar), jnp.sin(ar), jnp.cos(ac), jnp.sin(ac)
    cos_t = jnp.concatenate([cr, cr, cc, cc], axis=-1)
    sin_t = jnp.concatenate([-sr, sr, -sc, sc], axis=-1)
    return cos_t, sin_t


def _filter_feats(L):
    t01 = jnp.linspace(0.0, 1.0, L, dtype=F32)[:, None]
    bands = (FILTER_EMB - 1) // 2
    fr = jnp.linspace(1e-4, bands - 1, bands, dtype=F32)[None]
    w = 2.0 * math.pi * jnp.arange(L, dtype=F32)[:, None] / L
    feats = jnp.concatenate([t01, jnp.cos(fr * w), -jnp.sin(fr * w)], axis=-1)
    return jnp.pad(feats, ((0, 0), (0, FILT_PAD - FILTER_EMB)))


def _dft_tables(L):
    n = 2 * L
    n2 = FFT_N2
    n1 = n // n2
    n1h = n1 // 2
    k1 = jnp.arange(n1, dtype=jnp.int32)[None, :, None]
    t1 = jnp.arange(n1h, dtype=jnp.int32)[None, None, :]
    t2 = jnp.arange(n2, dtype=jnp.int32)[:, None, None]
    ang = (-2.0 * math.pi / n) * ((k1 * (n2 * t1 + t2)) % n).astype(F32)
    cr, ci = jnp.cos(ang), jnp.sin(ang)
    fa = jnp.concatenate([jnp.concatenate([cr, -ci], -1), jnp.concatenate([ci, cr], -1)], -2)
    fa_re = jnp.concatenate([cr, ci], -2)
    crt, cit = jnp.swapaxes(cr, 1, 2), jnp.swapaxes(ci, 1, 2)
    fc = jnp.concatenate([jnp.concatenate([crt, cit], -1), jnp.concatenate([-cit, crt], -1)], -2) / n
    a = jnp.arange(n2, dtype=jnp.int32)
    ang2 = (-2.0 * math.pi / n2) * ((a[:, None] * a[None, :]) % n2).astype(F32)
    fr, fi = jnp.cos(ang2), jnp.sin(ang2)
    fb = jnp.concatenate([jnp.concatenate([fr, -fi], -1), jnp.concatenate([fi, fr], -1)], -2)
    fbi = jnp.concatenate([jnp.concatenate([fr, fi], -1), jnp.concatenate([-fi, fr], -1)], -2)
    return (fa.astype(BF16), fa_re.astype(BF16), fb.astype(BF16), fbi.astype(BF16), fc.astype(BF16))


def _trunk(x, p):
    B, L, D = x.shape
    T = B * L
    P = B // 2
    n2 = FFT_N2
    n1h = L // n2
    C = D_HYENA
    ncb = C // LANES

    x0 = x.reshape(T, D)
    x1 = _ffn(x0, p["ffn1_norm"], p["ffn1_w13"], p["ffn1_w2"], p["final_norm"], final_norm=False)

    cos_t, sin_t = _rope_tables(L)
    q, k, v, hyc = _mix_in(x1, p["mix_norm"], p["wq"], p["wkv"], p["wh"], p["q_norm"], p["k_norm"],
                           cos_t, sin_t, p["conv_w"], p["conv_b"], L=L)
    attn = _attention(q.reshape(B, L, D_ATTN), k.reshape(B, L, D_KV), v.reshape(B, L, D_KV))
    ng2 = n2 // FFT_G
    hyv = hyc.reshape(3 * ncb, P, 2, ng2, n1h, FFT_G, LANES)

    fa, fa_re, fb, fbi, fc = _dft_tables(L)
    hf, asum = _filters(_filter_feats(L), p["filt_w1"], p["filt_b1"], p["filt_w2"], p["filt_b2"],
                        p["filt_w3"], p["filt_freq"], p["decay"])
    s1f = _fft_a_real(hf, fa_re)
    kf = _fft_b_filt(s1f, fb, asum)

    s1 = _fft_a(hyv, fa, ncb=ncb, cb0=0)
    s2 = _fft_b(s1, kf, fb, fbi, kcb0=0)
    z = _fft_c(s2, fc, hyv, hyv, p["bias0"], ucb0=0, gcb0=ncb)
    s1 = _fft_a(z, fa, ncb=ncb, cb0=0)
    s2 = _fft_b(s1, kf, fb, fbi, kcb0=ncb)
    ho = _fft_c(s2, fc, z, hyv, p["bias1"], ucb0=0, gcb0=2 * ncb)

    x2 = _out_proj(x1, attn.reshape(T, D_ATTN), ho.reshape(ncb, B, ng2, n1h, FFT_G, LANES),
                   p["gon_a"], p["gon_h"], p["wo_a"], p["wo_h"])
    x3 = _ffn(x2, p["ffn2_norm"], p["ffn2_w13"], p["ffn2_w2"], p["final_norm"], final_norm=True)
    return x3.reshape(B, L, D)


def kernel(x_prompt, x_sample, ffn1_norm, ffn1_w13, ffn1_w2, mix_norm, w_in, q_norm, k_norm, conv_w, conv_b, filt_w1, filt_b1, filt_w2, filt_b2, filt_w3, filt_freq, hyena_decay, hyena_bias, group_out_norm, w_out, ffn2_norm, ffn2_w13, ffn2_w2, final_norm):
    hp = FILT_PAD - FILTER_HIDDEN
    w_in0 = w_in[0]
    p = {
        "ffn1_norm": ffn1_norm[0][None], "ffn2_norm": ffn2_norm[0][None],
        "ffn1_w13": ffn1_w13[0].astype(BF16), "ffn1_w2": ffn1_w2[0].astype(BF16),
        "ffn2_w13": ffn2_w13[0].astype(BF16), "ffn2_w2": ffn2_w2[0].astype(BF16),
        "final_norm": final_norm[None],
        "mix_norm": mix_norm[0][None],
        "wq": w_in0[:, :D_ATTN].astype(BF16),
        "wkv": w_in0[:, D_ATTN:D_ATTN + 2 * D_KV].astype(BF16),
        "wh": w_in0[:, D_ATTN + 2 * D_KV:].astype(BF16),
        "q_norm": q_norm[0][None], "k_norm": k_norm[0][None],
        "conv_w": conv_w[0], "conv_b": conv_b[0][None],
        "filt_w1": jnp.pad(filt_w1[0], ((0, FILT_PAD - FILTER_EMB), (0, hp))),
        "filt_b1": jnp.pad(filt_b1[0], (0, hp))[None],
        "filt_w2": jnp.pad(filt_w2[0], ((0, hp), (0, hp))),
        "filt_b2": jnp.pad(filt_b2[0], (0, hp))[None],
        "filt_w3": jnp.pad(filt_w3[0], ((0, hp), (0, 0))),
        "filt_freq": jnp.pad(filt_freq[0], (0, hp))[None],
        "decay": hyena_decay[0].reshape(1, 4 * D_HYENA),
        "bias0": hyena_bias[0, 0][None], "bias1": hyena_bias[0, 1][None],
        "gon_a": group_out_norm[0, :D_ATTN][None], "gon_h": group_out_norm[0, D_ATTN:][None],
        "wo_a": w_out[0, :D_ATTN].astype(BF16), "wo_h": w_out[0, D_ATTN:].astype(BF16),
    }
    return (_trunk(x_prompt, p), _trunk(x_sample, p))
```

```python
import functools
import math

import jax
import jax.numpy as jnp
from jax import lax
from jax.experimental import pallas as pl
from jax.experimental.pallas import tpu as pltpu

F32 = jnp.float32
BF16 = jnp.bfloat16
U32 = jnp.uint32

D_MODEL = 2048
GRID_W = 64
D_ATTN = D_MODEL // 2
D_HYENA = D_MODEL - D_ATTN
HEAD_DIM = 128
N_Q_HEADS = D_ATTN // HEAD_DIM
N_KV_HEADS = 2
Q_PER_KV = N_Q_HEADS // N_KV_HEADS
ROPE_HALF = HEAD_DIM // 2
ROPE_THETA = 10000.0
FILTER_EMB = 33
FILTER_HIDDEN = 64
D_FF = 5632
EPS = 1e-6
D_KV = N_KV_HEADS * HEAD_DIM

LANES = 128
FILT_PAD = 128
FFT_N2 = 128
FFT_W = 512
FFT_G = 8
MIB = 1024 * 1024
LOG2_E = math.log2(math.e)


def _cparams(semantics, vmem_mib):
    return pltpu.CompilerParams(dimension_semantics=semantics,
                                vmem_limit_bytes=vmem_mib * MIB)


def _rms(x, g):
    ms = jnp.mean(x * x, axis=-1, keepdims=True)
    return x * lax.rsqrt(ms + EPS) * g


def _resident(shape):
    nd = len(shape)
    return pl.BlockSpec(shape, lambda *_: (0,) * nd, pipeline_mode=pl.Buffered(1))


def _ffn_body(x_ref, g_ref, wg_ref, wu_ref, w2_ref, gf_ref, o_ref, xn_ref, *, nj, final_norm):
    j = pl.program_id(1)

    @pl.when(j == 0)
    def _init():
        x = x_ref[...]
        xn_ref[...] = _rms(x, g_ref[...]).astype(BF16)
        o_ref[...] = x

    xn = xn_ref[...]
    hg = jnp.dot(xn, wg_ref[...], preferred_element_type=F32)
    hu = jnp.dot(xn, wu_ref[...], preferred_element_type=F32)
    a = (hg * (0.5 / (1.0 + jnp.exp(-hg))) * hu).astype(BF16)
    o_ref[...] += jnp.dot(a, w2_ref[...], preferred_element_type=F32)

    if final_norm:
        @pl.when(j == nj - 1)
        def _fin():
            o_ref[...] = _rms(o_ref[...], gf_ref[...])


def _ffn(x, g, w13, w2, gf, *, final_norm, tm=512, tf=512):
    T, D = x.shape
    ff = w2.shape[0]
    nj = ff // tf
    body = functools.partial(_ffn_body, nj=nj, final_norm=final_norm)
    return pl.pallas_call(
        body,
        grid=(T // tm, nj),
        in_specs=[
            pl.BlockSpec((tm, D), lambda i, j: (i, 0)),
            pl.BlockSpec((1, D), lambda i, j: (0, 0)),
            pl.BlockSpec((D, tf), lambda i, j: (0, j)),
            pl.BlockSpec((D, tf), lambda i, j: (0, j + nj)),
            pl.BlockSpec((tf, D), lambda i, j: (j, 0)),
            pl.BlockSpec((1, D), lambda i, j: (0, 0)),
        ],
        out_specs=pl.BlockSpec((tm, D), lambda i, j: (i, 0)),
        out_shape=jax.ShapeDtypeStruct((T, D), F32),
        scratch_shapes=[pltpu.VMEM((tm, D), BF16)],
        compiler_params=_cparams(("parallel", "arbitrary"), 48),
        name="ffn",
    )(x, g, w13, w13, w2, gf)


def _store_time_grouped(ref, cb, val):
    n1t = ref.shape[2]
    v = val.reshape(n1t * (FFT_N2 // FFT_G), FFT_G, LANES)
    for a in range(n1t):
        ref[cb, :, a] = v[a * (FFT_N2 // FFT_G):(a + 1) * (FFT_N2 // FFT_G)]


def _load_time_grouped(ref, cb):
    n1t = ref.shape[2]
    return jnp.concatenate([ref[cb, :, a].reshape(FFT_N2, LANES) for a in range(n1t)], axis=0)


MIX_HALO = 16


def _mix_body(xb_ref, x_ref, xa_ref, g_ref, wq_ref, wkv_ref, wh_ref, qn_ref, kn_ref, cos_ref, sin_ref,
              cw_ref, cb_ref, q_ref, k_ref, v_ref, hy_ref, *, npos):
    tm = x_ref.shape[0]
    x_ext = jnp.concatenate([xb_ref[...], x_ref[...], xa_ref[...]], axis=0)
    h_ext = _rms(x_ext, g_ref[...]).astype(BF16)
    h = h_ext[MIX_HALO:MIX_HALO + tm]
    lane = lax.broadcasted_iota(jnp.int32, (tm, HEAD_DIM), 1)
    low_half = (lane % (2 * (ROPE_HALF // 2))) < (ROPE_HALF // 2)
    c = cos_ref[...]
    s = sin_ref[...]

    def head(p, gain, scale):
        y = _rms(p, gain)
        partner = jnp.where(low_half, pltpu.roll(y, HEAD_DIM - ROPE_HALF // 2, 1),
                            pltpu.roll(y, ROPE_HALF // 2, 1))
        y = y * c + partner * s
        return y if scale is None else y * scale

    q = jnp.dot(h, wq_ref[...], preferred_element_type=F32)
    for hd in range(N_Q_HEADS):
        sl = slice(hd * HEAD_DIM, (hd + 1) * HEAD_DIM)
        q_ref[:, sl] = head(q[:, sl], qn_ref[...], HEAD_DIM ** -0.5 * LOG2_E).astype(BF16)
    kv = jnp.dot(h, wkv_ref[...], preferred_element_type=F32)
    for hd in range(N_KV_HEADS):
        sl = slice(hd * HEAD_DIM, (hd + 1) * HEAD_DIM)
        k_ref[:, sl] = head(kv[:, sl], kn_ref[...], None).astype(BF16)
    v_ref[...] = kv[:, D_KV:].astype(BF16)

    hy = jnp.dot(h_ext, wh_ref[...], preferred_element_type=F32)
    pos = pl.program_id(0) % npos
    rid = lax.broadcasted_iota(jnp.int32, (tm, 1), 0)
    before = jnp.where((rid == 0) & (pos == 0), 0.0, hy[MIX_HALO - 1:MIX_HALO - 1 + tm])
    after = jnp.where((rid == tm - 1) & (pos == npos - 1), 0.0, hy[MIX_HALO + 1:MIX_HALO + 1 + tm])
    hyc = (before * cw_ref[0:1, :] + hy[MIX_HALO:MIX_HALO + tm] * cw_ref[1:2, :]
           + after * cw_ref[2:3, :] + cb_ref[...])
    for cb in range(hy_ref.shape[0]):
        _store_time_grouped(hy_ref, cb, hyc[:, cb * LANES:(cb + 1) * LANES])


def _mix_in(x, g, wq, wkv, wh, qn, kn, cos_t, sin_t, cw, cb, *, L, tm=256):
    T, D = x.shape
    npos = L // tm
    ng2 = FFT_N2 // FFT_G
    hb = tm // MIX_HALO
    nhb = T // MIX_HALO
    return pl.pallas_call(
        functools.partial(_mix_body, npos=npos),
        grid=(T // tm,),
        in_specs=[
            pl.BlockSpec((MIX_HALO, D), lambda i: (jnp.maximum(i * hb - 1, 0), 0)),
            pl.BlockSpec((tm, D), lambda i: (i, 0)),
            pl.BlockSpec((MIX_HALO, D), lambda i: (jnp.minimum((i + 1) * hb, nhb - 1), 0)),
            _resident((1, D)),
            _resident(wq.shape),
            _resident(wkv.shape),
            _resident(wh.shape),
            _resident((1, HEAD_DIM)),
            _resident((1, HEAD_DIM)),
            pl.BlockSpec((tm, HEAD_DIM), lambda i: (i % npos, 0)),
            pl.BlockSpec((tm, HEAD_DIM), lambda i: (i % npos, 0)),
            _resident(cw.shape),
            _resident(cb.shape),
        ],
        out_specs=[
            pl.BlockSpec((tm, D_ATTN), lambda i: (i, 0)),
            pl.BlockSpec((tm, D_KV), lambda i: (i, 0)),
            pl.BlockSpec((tm, D_KV), lambda i: (i, 0)),
            pl.BlockSpec((3 * D_HYENA // LANES, None, ng2, tm // FFT_N2, FFT_G, LANES),
                         lambda i: (0, i // npos, 0, i % npos, 0, 0)),
        ],
        out_shape=[
            jax.ShapeDtypeStruct((T, D_ATTN), BF16),
            jax.ShapeDtypeStruct((T, D_KV), BF16),
            jax.ShapeDtypeStruct((T, D_KV), BF16),
            jax.ShapeDtypeStruct((3 * D_HYENA // LANES, T // L, ng2, L // FFT_N2, FFT_G, LANES), F32),
        ],
        compiler_params=_cparams(("parallel",), 48),
        name="mix_in",
    )(x, x, x, g, wq, wkv, wh, qn, kn, cos_t, sin_t, cw, cb)


ATTN_ROWS = 32
ATTN_RING = 4
ATTN_ONES = 16


def _attn_body(q_ref, k_ref, v_ref, o_ref, vt_ref, *bufs, tq, tk, nk):
    s_refs, mx_refs, p_refs = (bufs[k * ATTN_RING:(k + 1) * ATTN_RING] for k in range(3))

    @pl.when(pl.program_id(2) == 0)
    def _transpose_v():
        for c in range(nk):
            vt_ref[c, :HEAD_DIM] = v_ref[pl.ds(c * tk, tk), :].astype(F32).T.astype(BF16)
            vt_ref[c, HEAD_DIM:] = jnp.ones((ATTN_ONES, tk), BF16)

    q = jnp.concatenate([q_ref[:, h * HEAD_DIM:(h + 1) * HEAD_DIM] for h in range(Q_PER_KV)], axis=0)
    m_cols = Q_PER_KV * tq
    nchunk = tk // ATTN_ROWS

    def stage_a(i, s_ref, mx_ref):
        kc = k_ref[pl.ds(pl.multiple_of(i * tk, tk), tk), :]
        s = lax.dot_general(kc, q, (((1,), (1,)), ((), ())), preferred_element_type=F32)
        s_ref[...] = s
        mx_ref[...] = jnp.max(s, axis=0, keepdims=True)

    def stage_b(s_ref, mx_ref, p_ref, m):
        m_new = jnp.maximum(m, mx_ref[...])
        alpha = jnp.exp2(m - m_new)
        for c in range(nchunk):
            rows = slice(c * ATTN_ROWS, (c + 1) * ATTN_ROWS)
            p_ref[rows, :] = jnp.exp2((s_ref[rows, :] - m_new).astype(BF16))
        return m_new, alpha

    def stage_d(i, p_ref, alpha, acc):
        return alpha * acc + jnp.dot(vt_ref[i], p_ref[...], preferred_element_type=F32)

    def ring(i, m, al0, al1, acc, n_a, n_b):
        alphas = {0: al0, 1: al1}
        for u in range(ATTN_RING):
            if u < n_a:
                stage_a(i + u + 4, s_refs[u], mx_refs[u])
            if u < n_b:
                v = (u + 2) % ATTN_RING
                m, alphas[u + 2] = stage_b(s_refs[v], mx_refs[v], p_refs[v], m)
            acc = stage_d(i + u, p_refs[u], alphas[u], acc)
        return m, alphas.get(4), alphas.get(5), acc

    m = jnp.full((1, m_cols), -jnp.inf, F32)
    acc = jnp.zeros((HEAD_DIM + ATTN_ONES, m_cols), F32)
    for u in range(ATTN_RING):
        stage_a(u, s_refs[u], mx_refs[u])
    m, al0 = stage_b(s_refs[0], mx_refs[0], p_refs[0], m)
    m, al1 = stage_b(s_refs[1], mx_refs[1], p_refs[1], m)

    def trip(j, carry):
        return ring(ATTN_RING * j, *carry, ATTN_RING, ATTN_RING)

    m, al0, al1, acc = lax.fori_loop(0, nk // ATTN_RING - 1, trip, (m, al0, al1, acc))
    _, _, _, acc = ring(nk - ATTN_RING, m, al0, al1, acc, 0, 2)
    o = acc[:HEAD_DIM] / acc[HEAD_DIM:HEAD_DIM + 1]
    for h in range(Q_PER_KV):
        o_ref[:, h * HEAD_DIM:(h + 1) * HEAD_DIM] = o[:, h * tq:(h + 1) * tq].T


def _attention(q, k, v, *, tq=128, tk=512):
    B, L, _ = q.shape
    qw = Q_PER_KV * HEAD_DIM
    nk = L // tk
    assert nk % ATTN_RING == 0 and nk >= 2 * ATTN_RING
    body = functools.partial(_attn_body, tq=tq, tk=tk, nk=nk)
    m_cols = Q_PER_KV * tq
    return pl.pallas_call(
        body,
        grid=(B, N_KV_HEADS, L // tq),
        in_specs=[
            pl.BlockSpec((None, tq, qw), lambda b, g, i: (b, i, g)),
            pl.BlockSpec((None, L, HEAD_DIM), lambda b, g, i: (b, 0, g)),
            pl.BlockSpec((None, L, HEAD_DIM), lambda b, g, i: (b, 0, g)),
        ],
        out_specs=pl.BlockSpec((None, tq, qw), lambda b, g, i: (b, i, g)),
        out_shape=jax.ShapeDtypeStruct((B, L, D_ATTN), F32),
        scratch_shapes=([pltpu.VMEM((nk, HEAD_DIM + ATTN_ONES, tk), BF16)]
                        + [pltpu.VMEM((tk, m_cols), F32)] * ATTN_RING
                        + [pltpu.VMEM((1, m_cols), F32)] * ATTN_RING
                        + [pltpu.VMEM((tk, m_cols), BF16)] * ATTN_RING),
        compiler_params=_cparams(("parallel", "parallel", "arbitrary"), 32),
        name="attn",
    )(q, k, v)


def _filt_body(feat_ref, w1_ref, b1_ref, w2_ref, b2_ref, w3_ref, fq_ref, dec_ref, hf_ref, asum_ref, *, tl):
    i = pl.program_id(0)
    hp = lax.Precision.HIGHEST
    fq = fq_ref[...]
    feat = feat_ref[...]
    h = jnp.sin(fq * (jnp.dot(feat, w1_ref[...], precision=hp, preferred_element_type=F32) + b1_ref[...]))
    h = jnp.sin(fq * (jnp.dot(h, w2_ref[...], precision=hp, preferred_element_type=F32) + b2_ref[...]))
    h3 = jnp.dot(h, w3_ref[...], precision=hp, preferred_element_type=F32)
    t01 = feat[:, 0:1]
    hf = h3 * jnp.exp(-t01 * jnp.abs(dec_ref[...]))
    ncol = hf.shape[1]
    row = lax.broadcasted_iota(jnp.int32, hf.shape, 0) + i * tl
    col = lax.broadcasted_iota(jnp.int32, hf.shape, 1)
    hf = jnp.where((row == 0) & (col >= ncol // 2), 0.0, hf)
    for cb in range(ncol // LANES):
        _store_time_grouped(hf_ref, cb, hf[:, cb * LANES:(cb + 1) * LANES])
    part = jnp.sum(jnp.abs(hf), axis=0, keepdims=True)

    @pl.when(i == 0)
    def _first():
        asum_ref[...] = part

    @pl.when(i > 0)
    def _rest():
        asum_ref[...] += part


def _filters(feats, w1, b1, w2, b2, w3, fq, dec, *, tl=256):
    L = feats.shape[0]
    nc = w3.shape[1]
    ng2 = FFT_N2 // FFT_G
    body = functools.partial(_filt_body, tl=tl)
    return pl.pallas_call(
        body,
        grid=(L // tl,),
        in_specs=[
            pl.BlockSpec((tl, FILT_PAD), lambda i: (i, 0)),
            _resident(w1.shape), _resident(b1.shape), _resident(w2.shape), _resident(b2.shape),
            _resident(w3.shape), _resident(fq.shape), _resident(dec.shape),
        ],
        out_specs=[
            pl.BlockSpec((nc // LANES, ng2, tl // FFT_N2, FFT_G, LANES), lambda i: (0, 0, i, 0, 0)),
            pl.BlockSpec((1, nc), lambda i: (0, 0)),
        ],
        out_shape=[jax.ShapeDtypeStruct((nc // LANES, ng2, L // FFT_N2, FFT_G, LANES), F32),
                   jax.ShapeDtypeStruct((1, nc), F32)],
        compiler_params=_cparams(("arbitrary",), 48),
        name="filt",
    )(feats, w1, b1, w2, b2, w3, fq, dec)


FFT_CB = FFT_W // LANES


def _pack_c(y):
    n = y.shape[0] // 2
    r = lax.bitcast_convert_type(y[:n], U32)
    i = lax.bitcast_convert_type(y[n:], U32)
    half = jnp.uint32(0x8000)
    return ((r + half) & jnp.uint32(0xFFFF0000)) | ((i + half) >> 16)


def _unpack_c(w):
    re = lax.bitcast_convert_type(w & jnp.uint32(0xFFFF0000), F32)
    im = lax.bitcast_convert_type(w << 16, F32)
    return re, im


def _unpack_rows(w):
    re, im = _unpack_c(w)
    return jnp.concatenate([re, im], axis=0).astype(BF16)


def _flat_scratch(n, dtype, lead=()):
    return pltpu.VMEM(lead + (FFT_CB, n * FFT_G, LANES), dtype)


def _flatten(flat_ref, block_ref, pre=()):
    rows = flat_ref.shape[-2]
    for h in range(FFT_CB):
        flat_ref[pre + (h,)] = block_ref[(h,) + pre].reshape(rows, LANES)


def _unflatten(block_ref, flat_ref, pre=()):
    n = flat_ref.shape[-2] // FFT_G
    for h in range(FFT_CB):
        block_ref[(h,) + pre] = flat_ref[pre + (h,)].reshape(n, FFT_G, LANES)


def _gather_rows(flat_ref, j, pre=()):
    n = flat_ref.shape[-2] // FFT_G
    return jnp.concatenate(
        [flat_ref[pre + (h, pl.ds(j, n, stride=FFT_G), slice(None))] for h in range(FFT_CB)], axis=1)


def _scatter_rows(flat_ref, j, val, pre=()):
    n = flat_ref.shape[-2] // FFT_G
    for h in range(FFT_CB):
        flat_ref[pre + (h, pl.ds(j, n, stride=FFT_G), slice(None))] = val[:, h * LANES:(h + 1) * LANES]


def _store_blocks(ref, j, val):
    for h in range(FFT_CB):
        ref[h, j] = val[:, h * LANES:(h + 1) * LANES]


def _store_grouped(ref, j, val):
    groups = ref.shape[1]
    for h in range(FFT_CB):
        ref[h, :, j] = val[:, h * LANES:(h + 1) * LANES].reshape(groups, FFT_G, LANES)


def _fft_a_body(x_ref, fa_ref, o_ref, xf_ref):
    for b in range(2):
        _flatten(xf_ref, x_ref, (b,))
    for j in range(FFT_G):
        xs = jnp.concatenate([_gather_rows(xf_ref, j, (b,)) for b in range(2)], axis=0).astype(BF16)
        _store_grouped(o_ref, j, _pack_c(jnp.dot(fa_ref[j], xs, preferred_element_type=F32)))


def _fft_a(xv, fa, *, ncb, cb0=0):
    _, P, _, ng2, n1h, _, _ = xv.shape
    n2 = ng2 * FFT_G
    two_n1 = fa.shape[1]
    n1 = two_n1 // 2
    c0 = cb0 // FFT_CB
    return pl.pallas_call(
        _fft_a_body,
        grid=(ng2, P, ncb // FFT_CB),
        in_specs=[
            pl.BlockSpec((FFT_CB, None, 2, None, n1h, FFT_G, LANES), lambda g, p, c: (c + c0, p, 0, g, 0, 0, 0)),
            pl.BlockSpec((FFT_G, two_n1, 2 * n1h), lambda g, p, c: (g, 0, 0)),
        ],
        out_specs=pl.BlockSpec((None, FFT_CB, n1 // FFT_G, FFT_G, FFT_G, LANES),
                               lambda g, p, c: (p, c, 0, g, 0, 0)),
        out_shape=jax.ShapeDtypeStruct((P, ncb, n1 // FFT_G, n2, FFT_G, LANES), U32),
        scratch_shapes=[_flat_scratch(n1h, F32, (2,))],
        compiler_params=_cparams(("arbitrary", "arbitrary", "arbitrary"), 32),
        name="fft_a",
    )(xv, fa)


def _fft_a_real_body(x_ref, fa_ref, o_ref, xf_ref):
    _flatten(xf_ref, x_ref)
    for j in range(FFT_G):
        xs = _gather_rows(xf_ref, j).astype(BF16)
        _store_grouped(o_ref, j, _pack_c(jnp.dot(fa_ref[j], xs, preferred_element_type=F32)))


def _fft_a_real(xv, fa_re):
    ncb, ng2, n1h, _, _ = xv.shape
    n2 = ng2 * FFT_G
    two_n1 = fa_re.shape[1]
    n1 = two_n1 // 2
    return pl.pallas_call(
        _fft_a_real_body,
        grid=(ng2, ncb // FFT_CB),
        in_specs=[
            pl.BlockSpec((FFT_CB, None, n1h, FFT_G, LANES), lambda g, c: (c, g, 0, 0, 0)),
            pl.BlockSpec((FFT_G, two_n1, n1h), lambda g, c: (g, 0, 0)),
        ],
        out_specs=pl.BlockSpec((FFT_CB, n1 // FFT_G, FFT_G, FFT_G, LANES), lambda g, c: (c, 0, g, 0, 0)),
        out_shape=jax.ShapeDtypeStruct((ncb, n1 // FFT_G, n2, FFT_G, LANES), U32),
        scratch_shapes=[_flat_scratch(n1h, F32)],
        compiler_params=_cparams(("arbitrary", "arbitrary"), 32),
        name="fft_a_real",
    )(xv, fa_re)


def _fft_b_filt_body(f_ref, b_ref, fb_ref, asf_ref, asb_ref, o_ref, ff_ref, bf_ref):
    n2 = FFT_N2
    _flatten(ff_ref, f_ref)
    _flatten(bf_ref, b_ref)
    inv = 1.0 / (asf_ref[...] + asb_ref[...])
    fb = fb_ref[...]
    for j in range(FFT_G):
        u = jnp.dot(fb, _unpack_rows(_gather_rows(ff_ref, j)), preferred_element_type=F32)
        w = jnp.dot(fb, _unpack_rows(_gather_rows(bf_ref, j)), preferred_element_type=F32)
        kr = (u[:n2] + w[:n2]) * inv
        ki = (u[n2:] - w[n2:]) * inv
        _store_blocks(o_ref, j, _pack_c(jnp.concatenate([kr, ki], axis=0)))


def _fft_b_filt(s1f, fb, asum):
    cb2, ng1, n2, _, _ = s1f.shape
    n1 = ng1 * FFT_G
    ncb = cb2 // 2
    nb = ncb // FFT_CB
    blk = (FFT_CB, None, n2, FFT_G, LANES)
    return pl.pallas_call(
        _fft_b_filt_body,
        grid=(ng1, nb),
        in_specs=[
            pl.BlockSpec(blk, lambda g, c: (c, g, 0, 0, 0)),
            pl.BlockSpec(blk, lambda g, c: (c + nb, g, 0, 0, 0)),
            _resident(fb.shape),
            pl.BlockSpec((1, FFT_W), lambda g, c: (0, c)),
            pl.BlockSpec((1, FFT_W), lambda g, c: (0, c + nb)),
        ],
        out_specs=pl.BlockSpec((FFT_CB, FFT_G, n2, LANES), lambda g, c: (c, g, 0, 0)),
        out_shape=jax.ShapeDtypeStruct((ncb, n1, n2, LANES), U32),
        scratch_shapes=[_flat_scratch(n2, U32), _flat_scratch(n2, U32)],
        compiler_params=_cparams(("parallel", "parallel"), 32),
        name="fft_b_filt",
    )(s1f, s1f, fb, asum, asum)


def _fft_b_body(s_ref, kf_ref, fb_ref, fbi_ref, o_ref, sf_ref):
    n2 = FFT_N2
    _flatten(sf_ref, s_ref)
    fb = fb_ref[...]
    fbi = fbi_ref[...]
    for j in range(FFT_G):
        x = jnp.dot(fb, _unpack_rows(_gather_rows(sf_ref, j)), preferred_element_type=F32)
        xr, xi = x[:n2], x[n2:]
        kr, ki = _unpack_c(jnp.concatenate([kf_ref[h, j] for h in range(FFT_CB)], axis=1))
        pm = jnp.concatenate([xr * kr - xi * ki, xr * ki + xi * kr], axis=0).astype(BF16)
        _store_grouped(o_ref, j, _pack_c(jnp.dot(fbi, pm, preferred_element_type=F32)))


def _fft_b(s1, kf, fb, fbi, *, kcb0):
    P, ncb, ng1, n2, _, _ = s1.shape
    n1 = ng1 * FFT_G
    k0 = kcb0 // FFT_CB
    return pl.pallas_call(
        _fft_b_body,
        grid=(ng1, ncb // FFT_CB, P),
        in_specs=[
            pl.BlockSpec((None, FFT_CB, None, n2, FFT_G, LANES), lambda g, c, p: (p, c, g, 0, 0, 0)),
            pl.BlockSpec((FFT_CB, FFT_G, n2, LANES), lambda g, c, p: (c + k0, g, 0, 0)),
            _resident(fb.shape),
            _resident(fbi.shape),
        ],
        out_specs=pl.BlockSpec((None, FFT_CB, n2 // FFT_G, FFT_G, FFT_G, LANES),
                               lambda g, c, p: (p, c, 0, g, 0, 0)),
        out_shape=jax.ShapeDtypeStruct((P, ncb, n2 // FFT_G, n1, FFT_G, LANES), U32),
        scratch_shapes=[_flat_scratch(n2, U32)],
        compiler_params=_cparams(("parallel", "parallel", "arbitrary"), 32),
        name="fft_b",
    )(s1, kf, fb, fbi)


def _fft_c_body(z_ref, fc_ref, u_ref, gate_ref, d_ref, o_ref, zf_ref, uf_ref, gf_ref, of_ref):
    n1h = o_ref.shape[2]
    d = d_ref[...]
    _flatten(zf_ref, z_ref)
    for b in range(2):
        _flatten(uf_ref, u_ref, (b,))
        _flatten(gf_ref, gate_ref, (b,))
    for j in range(FFT_G):
        y = jnp.dot(fc_ref[j], _unpack_rows(_gather_rows(zf_ref, j)), preferred_element_type=F32)
        for b in range(2):
            yb = y[b * n1h:(b + 1) * n1h]
            res = _gather_rows(gf_ref, j, (b,)) * (yb + _gather_rows(uf_ref, j, (b,)) * d)
            _scatter_rows(of_ref, j, res, (b,))
    for b in range(2):
        _unflatten(o_ref, of_ref, (b,))


def _fft_c(s2, fc, uv, gv, d, *, ucb0, gcb0):
    P, ncb, ng2, n1, _, _ = s2.shape
    n1h = n1 // 2
    u0 = ucb0 // FFT_CB
    g0 = gcb0 // FFT_CB
    tblk = (FFT_CB, None, 2, None, n1h, FFT_G, LANES)
    return pl.pallas_call(
        _fft_c_body,
        grid=(ng2, P, ncb // FFT_CB),
        in_specs=[
            pl.BlockSpec((None, FFT_CB, None, n1, FFT_G, LANES), lambda g, p, c: (p, c, g, 0, 0, 0)),
            pl.BlockSpec((FFT_G, 2 * n1h, 2 * n1), lambda g, p, c: (g, 0, 0)),
            pl.BlockSpec(tblk, lambda g, p, c: (c + u0, p, 0, g, 0, 0, 0)),
            pl.BlockSpec(tblk, lambda g, p, c: (c + g0, p, 0, g, 0, 0, 0)),
            pl.BlockSpec((1, FFT_W), lambda g, p, c: (0, c)),
        ],
        out_specs=pl.BlockSpec(tblk, lambda g, p, c: (c, p, 0, g, 0, 0, 0)),
        out_shape=jax.ShapeDtypeStruct((ncb, P, 2, ng2, n1h, FFT_G, LANES), F32),
        scratch_shapes=[_flat_scratch(n1, U32), _flat_scratch(n1h, F32, (2,)), _flat_scratch(n1h, F32, (2,)),
                        _flat_scratch(n1h, F32, (2,))],
        compiler_params=_cparams(("arbitrary", "arbitrary", "arbitrary"), 32),
        name="fft_c",
    )(s2, fc, uv, gv, d)


def _oproj_body(x_ref, a_ref, h_ref, ga_ref, gh_ref, wa_ref, wh_ref, o_ref):
    ma = _rms(a_ref[...], ga_ref[...]).astype(BF16)
    hy = jnp.concatenate([_load_time_grouped(h_ref, cb) for cb in range(h_ref.shape[0])], axis=1)
    mh = _rms(hy, gh_ref[...]).astype(BF16)
    o_ref[...] = (x_ref[...] + jnp.dot(ma, wa_ref[...], preferred_element_type=F32)
                  + jnp.dot(mh, wh_ref[...], preferred_element_type=F32))


def _out_proj(x, a, h, ga, gh, wa, wh, *, tm=512):
    T, D = x.shape
    ncb, _, ng2, n1h, _, _ = h.shape
    npos = n1h * FFT_N2 // tm
    return pl.pallas_call(
        _oproj_body,
        grid=(T // tm,),
        in_specs=[
            pl.BlockSpec((tm, D), lambda i: (i, 0)),
            pl.BlockSpec((tm, D_ATTN), lambda i: (i, 0)),
            pl.BlockSpec((ncb, None, ng2, tm // FFT_N2, FFT_G, LANES), lambda i: (0, i // npos, 0, i % npos, 0, 0)),
            _resident((1, D_ATTN)), _resident((1, D_HYENA)),
            _resident(wa.shape), _resident(wh.shape),
        ],
        out_specs=pl.BlockSpec((tm, D), lambda i: (i, 0)),
        out_shape=jax.ShapeDtypeStruct((T, D), F32),
        compiler_params=_cparams(("parallel",), 48),
        name="oproj",
    )(x, a, h, ga, gh, wa, wh)


def _rope_tables(L):
    rows = L // GRID_W
    row = jnp.repeat(jnp.arange(rows, dtype=F32), GRID_W)
    col = jnp.tile(jnp.arange(GRID_W, dtype=F32), rows)
    inv = ROPE_THETA ** (-jnp.arange(0, ROPE_HALF, 2, dtype=F32) / ROPE_HALF)
    ar = row[:, None] * inv[None]
    ac = col[:, None] * inv[None]
    cr, sr, cc, sc = jnp.cos(ar), jnp.sin(ar), jnp.cos(ac), jnp.sin(ac)
    cos_t = jnp.concatenate([cr, cr, cc, cc], axis=-1)
    sin_t = jnp.concatenate([-sr, sr, -sc, sc], axis=-1)
    return cos_t, sin_t


def _filter_feats(L):
    t01 = jnp.linspace(0.0, 1.0, L, dtype=F32)[:, None]
    bands = (FILTER_EMB - 1) // 2
    fr = jnp.linspace(1e-4, bands - 1, bands, dtype=F32)[None]
    w = 2.0 * math.pi * jnp.arange(L, dtype=F32)[:, None] / L
    feats = jnp.concatenate([t01, jnp.cos(fr * w), -jnp.sin(fr * w)], axis=-1)
    return jnp.pad(feats, ((0, 0), (0, FILT_PAD - FILTER_EMB)))


def _dft_tables(L):
    n = 2 * L
    n2 = FFT_N2
    n1 = n // n2
    n1h = n1 // 2
    k1 = jnp.arange(n1, dtype=jnp.int32)[None, :, None]
    t1 = jnp.arange(n1h, dtype=jnp.int32)[None, None, :]
    t2 = jnp.arange(n2, dtype=jnp.int32)[:, None, None]
    ang = (-2.0 * math.pi / n) * ((k1 * (n2 * t1 + t2)) % n).astype(F32)
    cr, ci = jnp.cos(ang), jnp.sin(ang)
    fa = jnp.concatenate([jnp.concatenate([cr, -ci], -1), jnp.concatenate([ci, cr], -1)], -2)
    fa_re = jnp.concatenate([cr, ci], -2)
    crt, cit = jnp.swapaxes(cr, 1, 2), jnp.swapaxes(ci, 1, 2)
    fc = jnp.concatenate([jnp.concatenate([crt, cit], -1), jnp.concatenate([-cit, crt], -1)], -2) / n
    a = jnp.arange(n2, dtype=jnp.int32)
    ang2 = (-2.0 * math.pi / n2) * ((a[:, None] * a[None, :]) % n2).astype(F32)
    fr, fi = jnp.cos(ang2), jnp.sin(ang2)
    fb = jnp.concatenate([jnp.concatenate([fr, -fi], -1), jnp.concatenate([fi, fr], -1)], -2)
    fbi = jnp.concatenate([jnp.concatenate([fr, fi], -1), jnp.concatenate([-fi, fr], -1)], -2)
    return (fa.astype(BF16), fa_re.astype(BF16), fb.astype(BF16), fbi.astype(BF16), fc.astype(BF16))


def _trunk(x, p):
    B, L, D = x.shape
    T = B * L
    P = B // 2
    n2 = FFT_N2
    n1h = L // n2
    C = D_HYENA
    ncb = C // LANES

    x0 = x.reshape(T, D)
    x1 = _ffn(x0, p["ffn1_norm"], p["ffn1_w13"], p["ffn1_w2"], p["final_norm"], final_norm=False)

    cos_t, sin_t = _rope_tables(L)
    q, k, v, hyc = _mix_in(x1, p["mix_norm"], p["wq"], p["wkv"], p["wh"], p["q_norm"], p["k_norm"],
                           cos_t, sin_t, p["conv_w"], p["conv_b"], L=L)
    attn = _attention(q.reshape(B, L, D_ATTN), k.reshape(B, L, D_KV), v.reshape(B, L, D_KV))
    ng2 = n2 // FFT_G
    hyv = hyc.reshape(3 * ncb, P, 2, ng2, n1h, FFT_G, LANES)

    fa, fa_re, fb, fbi, fc = _dft_tables(L)
    hf, asum = _filters(_filter_feats(L), p["filt_w1"], p["filt_b1"], p["filt_w2"], p["filt_b2"],
                        p["filt_w3"], p["filt_freq"], p["decay"])
    s1f = _fft_a_real(hf, fa_re)
    kf = _fft_b_filt(s1f, fb, asum)

    s1 = _fft_a(hyv, fa, ncb=ncb, cb0=0)
    s2 = _fft_b(s1, kf, fb, fbi, kcb0=0)
    z = _fft_c(s2, fc, hyv, hyv, p["bias0"], ucb0=0, gcb0=ncb)
    s1 = _fft_a(z, fa, ncb=ncb, cb0=0)
    s2 = _fft_b(s1, kf, fb, fbi, kcb0=ncb)
    ho = _fft_c(s2, fc, z, hyv, p["bias1"], ucb0=0, gcb0=2 * ncb)

    x2 = _out_proj(x1, attn.reshape(T, D_ATTN), ho.reshape(ncb, B, ng2, n1h, FFT_G, LANES),
                   p["gon_a"], p["gon_h"], p["wo_a"], p["wo_h"])
    x3 = _ffn(x2, p["ffn2_norm"], p["ffn2_w13"], p["ffn2_w2"], p["final_norm"], final_norm=True)
    return x3.reshape(B, L, D)


def kernel(x_prompt, x_sample, ffn1_norm, ffn1_w13, ffn1_w2, mix_norm, w_in, q_norm, k_norm, conv_w, conv_b, filt_w1, filt_b1, filt_w2, filt_b2, filt_w3, filt_freq, hyena_decay, hyena_bias, group_out_norm, w_out, ffn2_norm, ffn2_w13, ffn2_w2, final_norm):
    hp = FILT_PAD - FILTER_HIDDEN
    w_in0 = w_in[0]
    p = {
        "ffn1_norm": ffn1_norm[0][None], "ffn2_norm": ffn2_norm[0][None],
        "ffn1_w13": ffn1_w13[0].astype(BF16), "ffn1_w2": ffn1_w2[0].astype(BF16),
        "ffn2_w13": ffn2_w13[0].astype(BF16), "ffn2_w2": ffn2_w2[0].astype(BF16),
        "final_norm": final_norm[None],
        "mix_norm": mix_norm[0][None],
        "wq": w_in0[:, :D_ATTN].astype(BF16),
        "wkv": w_in0[:, D_ATTN:D_ATTN + 2 * D_KV].astype(BF16),
        "wh": w_in0[:, D_ATTN + 2 * D_KV:].astype(BF16),
        "q_norm": q_norm[0][None], "k_norm": k_norm[0][None],
        "conv_w": conv_w[0], "conv_b": conv_b[0][None],
        "filt_w1": jnp.pad(filt_w1[0], ((0, FILT_PAD - FILTER_EMB), (0, hp))),
        "filt_b1": jnp.pad(filt_b1[0], (0, hp))[None],
        "filt_w2": jnp.pad(filt_w2[0], ((0, hp), (0, hp))),
        "filt_b2": jnp.pad(filt_b2[0], (0, hp))[None],
        "filt_w3": jnp.pad(filt_w3[0], ((0, hp), (0, 0))),
        "filt_freq": jnp.pad(filt_freq[0], (0, hp))[None],
        "decay": hyena_decay[0].reshape(1, 4 * D_HYENA),
        "bias0": hyena_bias[0, 0][None], "bias1": hyena_bias[0, 1][None],
        "gon_a": group_out_norm[0, :D_ATTN][None], "gon_h": group_out_norm[0, D_ATTN:][None],
        "wo_a": w_out[0, :D_ATTN].astype(BF16), "wo_h": w_out[0, D_ATTN:].astype(BF16),
    }
    return (_trunk(x_prompt, p), _trunk(x_sample, p))
```

```python
import functools
import math

import jax
import jax.numpy as jnp
from jax import lax
from jax.experimental import pallas as pl
from jax.experimental.pallas import tpu as pltpu

F32 = jnp.float32
BF16 = jnp.bfloat16
U32 = jnp.uint32

D_MODEL = 2048
GRID_W = 64
D_ATTN = D_MODEL // 2
D_HYENA = D_MODEL - D_ATTN
HEAD_DIM = 128
N_Q_HEADS = D_ATTN // HEAD_DIM
N_KV_HEADS = 2
Q_PER_KV = N_Q_HEADS // N_KV_HEADS
ROPE_HALF = HEAD_DIM // 2
ROPE_THETA = 10000.0
FILTER_EMB = 33
FILTER_HIDDEN = 64
D_FF = 5632
EPS = 1e-6
D_KV = N_KV_HEADS * HEAD_DIM

LANES = 128
FILT_PAD = 128
FFT_N2 = 128
FFT_W = 512
FFT_G = 8
MIB = 1024 * 1024
LOG2_E = math.log2(math.e)


def _cparams(semantics, vmem_mib):
    return pltpu.CompilerParams(dimension_semantics=semantics,
                                vmem_limit_bytes=vmem_mib * MIB)


def _rms(x, g):
    ms = jnp.mean(x * x, axis=-1, keepdims=True)
    return x * lax.rsqrt(ms + EPS) * g


def _resident(shape):
    nd = len(shape)
    return pl.BlockSpec(shape, lambda *_: (0,) * nd, pipeline_mode=pl.Buffered(1))


def _ffn_body(*refs, nj, final_norm, mix):
    if mix:
        x_ref, a_ref, h_ref, ga_ref, gh_ref, wa_ref, wh_ref = refs[:7]
        refs = refs[:1] + refs[7:]
    x_ref, g_ref, wg_ref, wu_ref, w2_ref, gf_ref, o_ref, xn_ref = refs
    j = pl.program_id(1)

    @pl.when(j == 0)
    def _init():
        x = x_ref[...]
        if mix:
            ma = _rms(a_ref[...], ga_ref[...]).astype(BF16)
            hy = jnp.concatenate([_load_time_grouped(h_ref, cb) for cb in range(h_ref.shape[0])], axis=1)
            mh = _rms(hy, gh_ref[...]).astype(BF16)
            x = (x + jnp.dot(ma, wa_ref[...], preferred_element_type=F32)
                 + jnp.dot(mh, wh_ref[...], preferred_element_type=F32))
        xn_ref[...] = _rms(x, g_ref[...]).astype(BF16)
        o_ref[...] = x

    xn = xn_ref[...]
    hg = jnp.dot(xn, wg_ref[...], preferred_element_type=F32)
    hu = jnp.dot(xn, wu_ref[...], preferred_element_type=F32)
    a = (hg * (0.5 / (1.0 + jnp.exp(-hg))) * hu).astype(BF16)
    o_ref[...] += jnp.dot(a, w2_ref[...], preferred_element_type=F32)

    if final_norm:
        @pl.when(j == nj - 1)
        def _fin():
            o_ref[...] = _rms(o_ref[...], gf_ref[...])


def _ffn(x, g, w13, w2, gf, *, final_norm, mix=None, tm=512, tf=512):
    T, D = x.shape
    ff = w2.shape[0]
    nj = ff // tf
    body = functools.partial(_ffn_body, nj=nj, final_norm=final_norm, mix=mix is not None)
    in_specs = [pl.BlockSpec((tm, D), lambda i, j: (i, 0))]
    args = [x]
    if mix is not None:
        a, h, ga, gh, wa, wh = mix
        ncb, _, ng2, n1h, _, _ = h.shape
        npos = n1h * FFT_N2 // tm
        in_specs += [
            pl.BlockSpec((tm, D_ATTN), lambda i, j: (i, 0)),
            pl.BlockSpec((ncb, None, ng2, tm // FFT_N2, FFT_G, LANES),
                         lambda i, j: (0, i // npos, 0, i % npos, 0, 0)),
            _resident(ga.shape), _resident(gh.shape), _resident(wa.shape), _resident(wh.shape),
        ]
        args += [a, h, ga, gh, wa, wh]
    in_specs += [
        pl.BlockSpec((1, D), lambda i, j: (0, 0)),
        pl.BlockSpec((D, tf), lambda i, j: (0, j)),
        pl.BlockSpec((D, tf), lambda i, j: (0, j + nj)),
        pl.BlockSpec((tf, D), lambda i, j: (j, 0)),
        pl.BlockSpec((1, D), lambda i, j: (0, 0)),
    ]
    return pl.pallas_call(
        body,
        grid=(T // tm, nj),
        in_specs=in_specs,
        out_specs=pl.BlockSpec((tm, D), lambda i, j: (i, 0)),
        out_shape=jax.ShapeDtypeStruct((T, D), F32),
        scratch_shapes=[pltpu.VMEM((tm, D), BF16)],
        compiler_params=_cparams(("parallel", "arbitrary"), 56 if mix is not None else 48),
        name="ffn",
    )(*args, g, w13, w13, w2, gf)


def _store_time_grouped(ref, cb, val):
    n1t = ref.shape[2]
    v = val.reshape(n1t * (FFT_N2 // FFT_G), FFT_G, LANES)
    for a in range(n1t):
        ref[cb, :, a] = v[a * (FFT_N2 // FFT_G):(a + 1) * (FFT_N2 // FFT_G)]


def _load_time_grouped(ref, cb):
    n1t = ref.shape[2]
    return jnp.concatenate([ref[cb, :, a].reshape(FFT_N2, LANES) for a in range(n1t)], axis=0)


MIX_HALO = 16


def _mix_body(xb_ref, x_ref, xa_ref, g_ref, wq_ref, wkv_ref, wh_ref, qn_ref, kn_ref, cos_ref, sin_ref,
              cw_ref, cb_ref, q_ref, k_ref, v_ref, hy_ref, *, npos):
    tm = x_ref.shape[0]
    x_ext = jnp.concatenate([xb_ref[...], x_ref[...], xa_ref[...]], axis=0)
    h_ext = _rms(x_ext, g_ref[...]).astype(BF16)
    h = h_ext[MIX_HALO:MIX_HALO + tm]
    lane = lax.broadcasted_iota(jnp.int32, (tm, HEAD_DIM), 1)
    low_half = (lane % (2 * (ROPE_HALF // 2))) < (ROPE_HALF // 2)
    c = cos_ref[...]
    s = sin_ref[...]

    def head(p, gain, scale):
        y = _rms(p, gain)
        partner = jnp.where(low_half, pltpu.roll(y, HEAD_DIM - ROPE_HALF // 2, 1),
                            pltpu.roll(y, ROPE_HALF // 2, 1))
        y = y * c + partner * s
        return y if scale is None else y * scale

    q = jnp.dot(h, wq_ref[...], preferred_element_type=F32)
    for hd in range(N_Q_HEADS):
        sl = slice(hd * HEAD_DIM, (hd + 1) * HEAD_DIM)
        q_ref[:, sl] = head(q[:, sl], qn_ref[...], HEAD_DIM ** -0.5 * LOG2_E).astype(BF16)
    kv = jnp.dot(h, wkv_ref[...], preferred_element_type=F32)
    for hd in range(N_KV_HEADS):
        sl = slice(hd * HEAD_DIM, (hd + 1) * HEAD_DIM)
        k_ref[:, sl] = head(kv[:, sl], kn_ref[...], None).astype(BF16)
    v_ref[...] = kv[:, D_KV:].astype(BF16)

    hy = jnp.dot(h_ext, wh_ref[...], preferred_element_type=F32)
    pos = pl.program_id(0) % npos
    rid = lax.broadcasted_iota(jnp.int32, (tm, 1), 0)
    before = jnp.where((rid == 0) & (pos == 0), 0.0, hy[MIX_HALO - 1:MIX_HALO - 1 + tm])
    after = jnp.where((rid == tm - 1) & (pos == npos - 1), 0.0, hy[MIX_HALO + 1:MIX_HALO + 1 + tm])
    hyc = (before * cw_ref[0:1, :] + hy[MIX_HALO:MIX_HALO + tm] * cw_ref[1:2, :]
           + after * cw_ref[2:3, :] + cb_ref[...])
    for cb in range(hy_ref.shape[0]):
        _store_time_grouped(hy_ref, cb, hyc[:, cb * LANES:(cb + 1) * LANES])


def _mix_in(x, g, wq, wkv, wh, qn, kn, cos_t, sin_t, cw, cb, *, L, tm=256):
    T, D = x.shape
    npos = L // tm
    ng2 = FFT_N2 // FFT_G
    hb = tm // MIX_HALO
    nhb = T // MIX_HALO
    return pl.pallas_call(
        functools.partial(_mix_body, npos=npos),
        grid=(T // tm,),
        in_specs=[
            pl.BlockSpec((MIX_HALO, D), lambda i: (jnp.maximum(i * hb - 1, 0), 0)),
            pl.BlockSpec((tm, D), lambda i: (i, 0)),
            pl.BlockSpec((MIX_HALO, D), lambda i: (jnp.minimum((i + 1) * hb, nhb - 1), 0)),
            _resident((1, D)),
            _resident(wq.shape),
            _resident(wkv.shape),
            _resident(wh.shape),
            _resident((1, HEAD_DIM)),
            _resident((1, HEAD_DIM)),
            pl.BlockSpec((tm, HEAD_DIM), lambda i: (i % npos, 0)),
            pl.BlockSpec((tm, HEAD_DIM), lambda i: (i % npos, 0)),
            _resident(cw.shape),
            _resident(cb.shape),
        ],
        out_specs=[
            pl.BlockSpec((tm, D_ATTN), lambda i: (i, 0)),
            pl.BlockSpec((tm, D_KV), lambda i: (i, 0)),
            pl.BlockSpec((tm, D_KV), lambda i: (i, 0)),
            pl.BlockSpec((3 * D_HYENA // LANES, None, ng2, tm // FFT_N2, FFT_G, LANES),
                         lambda i: (0, i // npos, 0, i % npos, 0, 0)),
        ],
        out_shape=[
            jax.ShapeDtypeStruct((T, D_ATTN), BF16),
            jax.ShapeDtypeStruct((T, D_KV), BF16),
            jax.ShapeDtypeStruct((T, D_KV), BF16),
            jax.ShapeDtypeStruct((3 * D_HYENA // LANES, T // L, ng2, L // FFT_N2, FFT_G, LANES), F32),
        ],
        compiler_params=_cparams(("parallel",), 48),
        name="mix_in",
    )(x, x, x, g, wq, wkv, wh, qn, kn, cos_t, sin_t, cw, cb)


ATTN_ROWS = 32
ATTN_RING = 4
ATTN_ONES = 16


def _attn_body(q_ref, k_ref, v_ref, o_ref, vt_ref, *bufs, tq, tk, nk):
    s_refs, mx_refs, p_refs = (bufs[k * ATTN_RING:(k + 1) * ATTN_RING] for k in range(3))

    @pl.when(pl.program_id(2) == 0)
    def _transpose_v():
        for c in range(nk):
            vt_ref[c, :HEAD_DIM] = v_ref[pl.ds(c * tk, tk), :].astype(F32).T.astype(BF16)
            vt_ref[c, HEAD_DIM:] = jnp.ones((ATTN_ONES, tk), BF16)

    q = jnp.concatenate([q_ref[:, h * HEAD_DIM:(h + 1) * HEAD_DIM] for h in range(Q_PER_KV)], axis=0)
    m_cols = Q_PER_KV * tq
    nchunk = tk // ATTN_ROWS

    def stage_a(i, s_ref, mx_ref):
        kc = k_ref[pl.ds(pl.multiple_of(i * tk, tk), tk), :]
        s = lax.dot_general(kc, q, (((1,), (1,)), ((), ())), preferred_element_type=F32)
        s_ref[...] = s
        mx_ref[...] = jnp.max(s, axis=0, keepdims=True)

    def stage_b(s_ref, mx_ref, p_ref, m):
        m_new = jnp.maximum(m, mx_ref[...])
        alpha = jnp.exp2(m - m_new)
        for c in range(nchunk):
            rows = slice(c * ATTN_ROWS, (c + 1) * ATTN_ROWS)
            p_ref[rows, :] = jnp.exp2((s_ref[rows, :] - m_new).astype(BF16))
        return m_new, alpha

    def stage_d(i, p_ref, alpha, acc):
        return alpha * acc + jnp.dot(vt_ref[i], p_ref[...], preferred_element_type=F32)

    def ring(i, m, al0, al1, acc, n_a, n_b):
        alphas = {0: al0, 1: al1}
        for u in range(ATTN_RING):
            if u < n_a:
                stage_a(i + u + 4, s_refs[u], mx_refs[u])
            if u < n_b:
                v = (u + 2) % ATTN_RING
                m, alphas[u + 2] = stage_b(s_refs[v], mx_refs[v], p_refs[v], m)
            acc = stage_d(i + u, p_refs[u], alphas[u], acc)
        return m, alphas.get(4), alphas.get(5), acc

    m = jnp.full((1, m_cols), -jnp.inf, F32)
    acc = jnp.zeros((HEAD_DIM + ATTN_ONES, m_cols), F32)
    for u in range(ATTN_RING):
        stage_a(u, s_refs[u], mx_refs[u])
    m, al0 = stage_b(s_refs[0], mx_refs[0], p_refs[0], m)
    m, al1 = stage_b(s_refs[1], mx_refs[1], p_refs[1], m)

    def trip(j, carry):
        return ring(ATTN_RING * j, *carry, ATTN_RING, ATTN_RING)

    m, al0, al1, acc = lax.fori_loop(0, nk // ATTN_RING - 1, trip, (m, al0, al1, acc))
    _, _, _, acc = ring(nk - ATTN_RING, m, al0, al1, acc, 0, 2)
    o = acc[:HEAD_DIM] / acc[HEAD_DIM:HEAD_DIM + 1]
    for h in range(Q_PER_KV):
        o_ref[:, h * HEAD_DIM:(h + 1) * HEAD_DIM] = o[:, h * tq:(h + 1) * tq].T


def _attention(q, k, v, *, tq=128, tk=512):
    B, L, _ = q.shape
    qw = Q_PER_KV * HEAD_DIM
    nk = L // tk
    assert nk % ATTN_RING == 0 and nk >= 2 * ATTN_RING
    body = functools.partial(_attn_body, tq=tq, tk=tk, nk=nk)
    m_cols = Q_PER_KV * tq
    return pl.pallas_call(
        body,
        grid=(B, N_KV_HEADS, L // tq),
        in_specs=[
            pl.BlockSpec((None, tq, qw), lambda b, g, i: (b, i, g)),
            pl.BlockSpec((None, L, HEAD_DIM), lambda b, g, i: (b, 0, g)),
            pl.BlockSpec((None, L, HEAD_DIM), lambda b, g, i: (b, 0, g)),
        ],
        out_specs=pl.BlockSpec((None, tq, qw), lambda b, g, i: (b, i, g)),
        out_shape=jax.ShapeDtypeStruct((B, L, D_ATTN), F32),
        scratch_shapes=([pltpu.VMEM((nk, HEAD_DIM + ATTN_ONES, tk), BF16)]
                        + [pltpu.VMEM((tk, m_cols), F32)] * ATTN_RING
                        + [pltpu.VMEM((1, m_cols), F32)] * ATTN_RING
                        + [pltpu.VMEM((tk, m_cols), BF16)] * ATTN_RING),
        compiler_params=_cparams(("parallel", "parallel", "arbitrary"), 32),
        name="attn",
    )(q, k, v)


def _filt_body(feat_ref, w1_ref, b1_ref, w2_ref, b2_ref, w3_ref, fq_ref, dec_ref, hf_ref, asum_ref, *, tl):
    i = pl.program_id(0)
    hp = lax.Precision.HIGHEST
    fq = fq_ref[...]
    feat = feat_ref[...]
    h = jnp.sin(fq * (jnp.dot(feat, w1_ref[...], precision=hp, preferred_element_type=F32) + b1_ref[...]))
    h = jnp.sin(fq * (jnp.dot(h, w2_ref[...], precision=hp, preferred_element_type=F32) + b2_ref[...]))
    h3 = jnp.dot(h, w3_ref[...], precision=hp, preferred_element_type=F32)
    t01 = feat[:, 0:1]
    hf = h3 * jnp.exp(-t01 * jnp.abs(dec_ref[...]))
    ncol = hf.shape[1]
    row = lax.broadcasted_iota(jnp.int32, hf.shape, 0) + i * tl
    col = lax.broadcasted_iota(jnp.int32, hf.shape, 1)
    hf = jnp.where((row == 0) & (col >= ncol // 2), 0.0, hf)
    for cb in range(ncol // LANES):
        _store_time_grouped(hf_ref, cb, hf[:, cb * LANES:(cb + 1) * LANES])
    part = jnp.sum(jnp.abs(hf), axis=0, keepdims=True)

    @pl.when(i == 0)
    def _first():
        asum_ref[...] = part

    @pl.when(i > 0)
    def _rest():
        asum_ref[...] += part


def _filters(feats, w1, b1, w2, b2, w3, fq, dec, *, tl=256):
    L = feats.shape[0]
    nc = w3.shape[1]
    ng2 = FFT_N2 // FFT_G
    body = functools.partial(_filt_body, tl=tl)
    return pl.pallas_call(
        body,
        grid=(L // tl,),
        in_specs=[
            pl.BlockSpec((tl, FILT_PAD), lambda i: (i, 0)),
            _resident(w1.shape), _resident(b1.shape), _resident(w2.shape), _resident(b2.shape),
            _resident(w3.shape), _resident(fq.shape), _resident(dec.shape),
        ],
        out_specs=[
            pl.BlockSpec((nc // LANES, ng2, tl // FFT_N2, FFT_G, LANES), lambda i: (0, 0, i, 0, 0)),
            pl.BlockSpec((1, nc), lambda i: (0, 0)),
        ],
        out_shape=[jax.ShapeDtypeStruct((nc // LANES, ng2, L // FFT_N2, FFT_G, LANES), F32),
                   jax.ShapeDtypeStruct((1, nc), F32)],
        compiler_params=_cparams(("arbitrary",), 48),
        name="filt",
    )(feats, w1, b1, w2, b2, w3, fq, dec)


FFT_CB = FFT_W // LANES
FFT_CB_B = 2


def _pack_c(y):
    n = y.shape[0] // 2
    r = lax.bitcast_convert_type(y[:n], U32)
    i = lax.bitcast_convert_type(y[n:], U32)
    half = jnp.uint32(0x8000)
    return ((r + half) & jnp.uint32(0xFFFF0000)) | ((i + half) >> 16)


def _unpack_c(w):
    re = lax.bitcast_convert_type(w & jnp.uint32(0xFFFF0000), F32)
    im = lax.bitcast_convert_type(w << 16, F32)
    return re, im


def _unpack_rows(w):
    re, im = _unpack_c(w)
    return jnp.concatenate([re, im], axis=0).astype(BF16)


def _flat_scratch(n, dtype, cbs, lead=()):
    return pltpu.VMEM(lead + (cbs, n * FFT_G, LANES), dtype)


def _flatten(flat_ref, block_ref, pre=()):
    rows = flat_ref.shape[-2]
    for h in range(block_ref.shape[0]):
        flat_ref[pre + (h,)] = block_ref[(h,) + pre].reshape(rows, LANES)


def _unflatten(block_ref, flat_ref, pre=()):
    n = flat_ref.shape[-2] // FFT_G
    for h in range(block_ref.shape[0]):
        block_ref[(h,) + pre] = flat_ref[pre + (h,)].reshape(n, FFT_G, LANES)


def _gather_rows(flat_ref, j, pre=()):
    n = flat_ref.shape[-2] // FFT_G
    return jnp.concatenate(
        [flat_ref[pre + (h, pl.ds(j, n, stride=FFT_G), slice(None))] for h in range(flat_ref.shape[len(pre)])],
        axis=1)


def _scatter_rows(flat_ref, j, val, pre=()):
    n = flat_ref.shape[-2] // FFT_G
    for h in range(flat_ref.shape[len(pre)]):
        flat_ref[pre + (h, pl.ds(j, n, stride=FFT_G), slice(None))] = val[:, h * LANES:(h + 1) * LANES]


def _store_blocks(ref, j, val):
    for h in range(ref.shape[0]):
        ref[h, j] = val[:, h * LANES:(h + 1) * LANES]


def _store_grouped(ref, j, val):
    groups = ref.shape[1]
    for h in range(ref.shape[0]):
        ref[h, :, j] = val[:, h * LANES:(h + 1) * LANES].reshape(groups, FFT_G, LANES)


def _fft_a_body(x_ref, fa_ref, o_ref, xf_ref):
    for b in range(2):
        _flatten(xf_ref, x_ref, (b,))
    for j in range(FFT_G):
        xs = jnp.concatenate([_gather_rows(xf_ref, j, (b,)) for b in range(2)], axis=0).astype(BF16)
        _store_grouped(o_ref, j, _pack_c(jnp.dot(fa_ref[j], xs, preferred_element_type=F32)))


def _fft_a(xv, fa, *, ncb, cb0=0):
    _, P, _, ng2, n1h, _, _ = xv.shape
    n2 = ng2 * FFT_G
    two_n1 = fa.shape[1]
    n1 = two_n1 // 2
    c0 = cb0 // FFT_CB
    return pl.pallas_call(
        _fft_a_body,
        grid=(ng2, P, ncb // FFT_CB),
        in_specs=[
            pl.BlockSpec((FFT_CB, None, 2, None, n1h, FFT_G, LANES), lambda g, p, c: (c + c0, p, 0, g, 0, 0, 0)),
            pl.BlockSpec((FFT_G, two_n1, 2 * n1h), lambda g, p, c: (g, 0, 0)),
        ],
        out_specs=pl.BlockSpec((None, FFT_CB, n1 // FFT_G, FFT_G, FFT_G, LANES),
                               lambda g, p, c: (p, c, 0, g, 0, 0)),
        out_shape=jax.ShapeDtypeStruct((P, ncb, n1 // FFT_G, n2, FFT_G, LANES), U32),
        scratch_shapes=[_flat_scratch(n1h, F32, FFT_CB, (2,))],
        compiler_params=_cparams(("arbitrary", "arbitrary", "arbitrary"), 32),
        name="fft_a",
    )(xv, fa)


def _fft_a_real_body(x_ref, fa_ref, o_ref, xf_ref):
    _flatten(xf_ref, x_ref)
    for j in range(FFT_G):
        xs = _gather_rows(xf_ref, j).astype(BF16)
        _store_grouped(o_ref, j, _pack_c(jnp.dot(fa_ref[j], xs, preferred_element_type=F32)))


def _fft_a_real(xv, fa_re):
    ncb, ng2, n1h, _, _ = xv.shape
    n2 = ng2 * FFT_G
    two_n1 = fa_re.shape[1]
    n1 = two_n1 // 2
    return pl.pallas_call(
        _fft_a_real_body,
        grid=(ng2, ncb // FFT_CB),
        in_specs=[
            pl.BlockSpec((FFT_CB, None, n1h, FFT_G, LANES), lambda g, c: (c, g, 0, 0, 0)),
            pl.BlockSpec((FFT_G, two_n1, n1h), lambda g, c: (g, 0, 0)),
        ],
        out_specs=pl.BlockSpec((FFT_CB, n1 // FFT_G, FFT_G, FFT_G, LANES), lambda g, c: (c, 0, g, 0, 0)),
        out_shape=jax.ShapeDtypeStruct((ncb, n1 // FFT_G, n2, FFT_G, LANES), U32),
        scratch_shapes=[_flat_scratch(n1h, F32, FFT_CB)],
        compiler_params=_cparams(("arbitrary", "arbitrary"), 32),
        name="fft_a_real",
    )(xv, fa_re)


def _fft_b_filt_body(f_ref, b_ref, fb_ref, asf_ref, asb_ref, o_ref, ff_ref, bf_ref):
    n2 = FFT_N2
    _flatten(ff_ref, f_ref)
    _flatten(bf_ref, b_ref)
    inv = 1.0 / (asf_ref[...] + asb_ref[...])
    fb = fb_ref[...]
    for j in range(FFT_G):
        u = jnp.dot(fb, _unpack_rows(_gather_rows(ff_ref, j)), preferred_element_type=F32)
        w = jnp.dot(fb, _unpack_rows(_gather_rows(bf_ref, j)), preferred_element_type=F32)
        kr = (u[:n2] + w[:n2]) * inv
        ki = (u[n2:] - w[n2:]) * inv
        _store_blocks(o_ref, j, _pack_c(jnp.concatenate([kr, ki], axis=0)))


def _fft_b_filt(s1f, fb, asum):
    cb2, ng1, n2, _, _ = s1f.shape
    n1 = ng1 * FFT_G
    ncb = cb2 // 2
    nb = ncb // FFT_CB
    blk = (FFT_CB, None, n2, FFT_G, LANES)
    return pl.pallas_call(
        _fft_b_filt_body,
        grid=(ng1, nb),
        in_specs=[
            pl.BlockSpec(blk, lambda g, c: (c, g, 0, 0, 0)),
            pl.BlockSpec(blk, lambda g, c: (c + nb, g, 0, 0, 0)),
            _resident(fb.shape),
            pl.BlockSpec((1, FFT_W), lambda g, c: (0, c)),
            pl.BlockSpec((1, FFT_W), lambda g, c: (0, c + nb)),
        ],
        out_specs=pl.BlockSpec((FFT_CB, FFT_G, n2, LANES), lambda g, c: (c, g, 0, 0)),
        out_shape=jax.ShapeDtypeStruct((ncb, n1, n2, LANES), U32),
        scratch_shapes=[_flat_scratch(n2, U32, FFT_CB), _flat_scratch(n2, U32, FFT_CB)],
        compiler_params=_cparams(("parallel", "parallel"), 32),
        name="fft_b_filt",
    )(s1f, s1f, fb, asum, asum)


def _fft_b_body(s_ref, kf_ref, fb_ref, fbi_ref, o_ref, sf_ref):
    n2 = FFT_N2
    _flatten(sf_ref, s_ref)
    fb = fb_ref[...]
    fbi = fbi_ref[...]
    for j in range(FFT_G):
        x = jnp.dot(fb, _unpack_rows(_gather_rows(sf_ref, j)), preferred_element_type=F32)
        xr, xi = x[:n2], x[n2:]
        kr, ki = _unpack_c(jnp.concatenate([kf_ref[h, j] for h in range(kf_ref.shape[0])], axis=1))
        pm = jnp.concatenate([xr * kr - xi * ki, xr * ki + xi * kr], axis=0).astype(BF16)
        _store_grouped(o_ref, j, _pack_c(jnp.dot(fbi, pm, preferred_element_type=F32)))


def _fft_b(s1, kf, fb, fbi, *, kcb0):
    P, ncb, ng1, n2, _, _ = s1.shape
    n1 = ng1 * FFT_G
    cbs = FFT_CB_B
    k0 = kcb0 // cbs
    return pl.pallas_call(
        _fft_b_body,
        grid=(ng1, ncb // cbs, P),
        in_specs=[
            pl.BlockSpec((None, cbs, None, n2, FFT_G, LANES), lambda g, c, p: (p, c, g, 0, 0, 0)),
            pl.BlockSpec((cbs, FFT_G, n2, LANES), lambda g, c, p: (c + k0, g, 0, 0)),
            _resident(fb.shape),
            _resident(fbi.shape),
        ],
        out_specs=pl.BlockSpec((None, cbs, n2 // FFT_G, FFT_G, FFT_G, LANES),
                               lambda g, c, p: (p, c, 0, g, 0, 0)),
        out_shape=jax.ShapeDtypeStruct((P, ncb, n2 // FFT_G, n1, FFT_G, LANES), U32),
        scratch_shapes=[_flat_scratch(n2, U32, cbs)],
        compiler_params=_cparams(("parallel", "parallel", "arbitrary"), 32),
        name="fft_b",
    )(s1, kf, fb, fbi)


def _fft_c_body(z_ref, fc_ref, u_ref, gate_ref, d_ref, *rest, chain):
    if chain:
        fa_ref, o_ref, s_ref, zf_ref, uf_ref, gf_ref, of_ref = rest
    else:
        o_ref, zf_ref, uf_ref, gf_ref, of_ref = rest
    n1h = o_ref.shape[2]
    d = d_ref[...]
    _flatten(zf_ref, z_ref)
    for b in range(2):
        _flatten(uf_ref, u_ref, (b,))
        _flatten(gf_ref, gate_ref, (b,))
    for j in range(FFT_G):
        y = jnp.dot(fc_ref[j], _unpack_rows(_gather_rows(zf_ref, j)), preferred_element_type=F32)
        res = []
        for b in range(2):
            yb = y[b * n1h:(b + 1) * n1h]
            res.append(_gather_rows(gf_ref, j, (b,)) * (yb + _gather_rows(uf_ref, j, (b,)) * d))
            _scatter_rows(of_ref, j, res[b], (b,))
        if chain:
            xs = jnp.concatenate(res, axis=0).astype(BF16)
            _store_grouped(s_ref, j, _pack_c(jnp.dot(fa_ref[j], xs, preferred_element_type=F32)))
    for b in range(2):
        _unflatten(o_ref, of_ref, (b,))


def _fft_c(s2, fc, uv, gv, d, *, ucb0, gcb0, fa=None):
    P, ncb, ng2, n1, _, _ = s2.shape
    n1h = n1 // 2
    n2 = ng2 * FFT_G
    u0 = ucb0 // FFT_CB
    g0 = gcb0 // FFT_CB
    chain = fa is not None
    tblk = (FFT_CB, None, 2, None, n1h, FFT_G, LANES)
    in_specs = [
        pl.BlockSpec((None, FFT_CB, None, n1, FFT_G, LANES), lambda g, p, c: (p, c, g, 0, 0, 0)),
        pl.BlockSpec((FFT_G, 2 * n1h, 2 * n1), lambda g, p, c: (g, 0, 0)),
        pl.BlockSpec(tblk, lambda g, p, c: (c + u0, p, 0, g, 0, 0, 0)),
        pl.BlockSpec(tblk, lambda g, p, c: (c + g0, p, 0, g, 0, 0, 0)),
        pl.BlockSpec((1, FFT_W), lambda g, p, c: (0, c)),
    ]
    out_specs = [pl.BlockSpec(tblk, lambda g, p, c: (c, p, 0, g, 0, 0, 0))]
    out_shape = [jax.ShapeDtypeStruct((ncb, P, 2, ng2, n1h, FFT_G, LANES), F32)]
    args = [s2, fc, uv, gv, d]
    if chain:
        in_specs.append(pl.BlockSpec((FFT_G, 2 * n1, 2 * n1h), lambda g, p, c: (g, 0, 0)))
        out_specs.append(pl.BlockSpec((None, FFT_CB, n1 // FFT_G, FFT_G, FFT_G, LANES),
                                      lambda g, p, c: (p, c, 0, g, 0, 0)))
        out_shape.append(jax.ShapeDtypeStruct((P, ncb, n1 // FFT_G, n2, FFT_G, LANES), U32))
        args.append(fa)
    return pl.pallas_call(
        functools.partial(_fft_c_body, chain=chain),
        grid=(ng2, P, ncb // FFT_CB),
        in_specs=in_specs,
        out_specs=out_specs,
        out_shape=out_shape,
        scratch_shapes=[_flat_scratch(n1, U32, FFT_CB), _flat_scratch(n1h, F32, FFT_CB, (2,)),
                        _flat_scratch(n1h, F32, FFT_CB, (2,)), _flat_scratch(n1h, F32, FFT_CB, (2,))],
        compiler_params=_cparams(("arbitrary", "arbitrary", "arbitrary"), 40),
        name="fft_c",
    )(*args)


def _rope_tables(L):
    rows = L // GRID_W
    row = jnp.repeat(jnp.arange(rows, dtype=F32), GRID_W)
    col = jnp.tile(jnp.arange(GRID_W, dtype=F32), rows)
    inv = ROPE_THETA ** (-jnp.arange(0, ROPE_HALF, 2, dtype=F32) / ROPE_HALF)
    ar = row[:, None] * inv[None]
    ac = col[:, None] * inv[None]
    cr, sr, cc, sc = jnp.cos(ar), jnp.sin(ar), jnp.cos(ac), jnp.sin(ac)
    cos_t = jnp.concatenate([cr, cr, cc, cc], axis=-1)
    sin_t = jnp.concatenate([-sr, sr, -sc, sc], axis=-1)
    return cos_t, sin_t


def _filter_feats(L):
    t01 = jnp.linspace(0.0, 1.0, L, dtype=F32)[:, None]
    bands = (FILTER_EMB - 1) // 2
    fr = jnp.linspace(1e-4, bands - 1, bands, dtype=F32)[None]
    w = 2.0 * math.pi * jnp.arange(L, dtype=F32)[:, None] / L
    feats = jnp.concatenate([t01, jnp.cos(fr * w), -jnp.sin(fr * w)], axis=-1)
    return jnp.pad(feats, ((0, 0), (0, FILT_PAD - FILTER_EMB)))


def _dft_tables(L):
    n = 2 * L
    n2 = FFT_N2
    n1 = n // n2
    n1h = n1 // 2
    k1 = jnp.arange(n1, dtype=jnp.int32)[None, :, None]
    t1 = jnp.arange(n1h, dtype=jnp.int32)[None, None, :]
    t2 = jnp.arange(n2, dtype=jnp.int32)[:, None, None]
    ang = (-2.0 * math.pi / n) * ((k1 * (n2 * t1 + t2)) % n).astype(F32)
    cr, ci = jnp.cos(ang), jnp.sin(ang)
    fa = jnp.concatenate([jnp.concatenate([cr, -ci], -1), jnp.concatenate([ci, cr], -1)], -2)
    fa_re = jnp.concatenate([cr, ci], -2)
    crt, cit = jnp.swapaxes(cr, 1, 2), jnp.swapaxes(ci, 1, 2)
    fc = jnp.concatenate([jnp.concatenate([crt, cit], -1), jnp.concatenate([-cit, crt], -1)], -2) / n
    a = jnp.arange(n2, dtype=jnp.int32)
    ang2 = (-2.0 * math.pi / n2) * ((a[:, None] * a[None, :]) % n2).astype(F32)
    fr, fi = jnp.cos(ang2), jnp.sin(ang2)
    fb = jnp.concatenate([jnp.concatenate([fr, -fi], -1), jnp.concatenate([fi, fr], -1)], -2)
    fbi = jnp.concatenate([jnp.concatenate([fr, fi], -1), jnp.concatenate([-fi, fr], -1)], -2)
    return (fa.astype(BF16), fa_re.astype(BF16), fb.astype(BF16), fbi.astype(BF16), fc.astype(BF16))


def _trunk(x, p):
    B, L, D = x.shape
    T = B * L
    P = B // 2
    n2 = FFT_N2
    n1h = L // n2
    C = D_HYENA
    ncb = C // LANES

    x0 = x.reshape(T, D)
    x1 = _ffn(x0, p["ffn1_norm"], p["ffn1_w13"], p["ffn1_w2"], p["final_norm"], final_norm=False)

    cos_t, sin_t = _rope_tables(L)
    q, k, v, hyc = _mix_in(x1, p["mix_norm"], p["wq"], p["wkv"], p["wh"], p["q_norm"], p["k_norm"],
                           cos_t, sin_t, p["conv_w"], p["conv_b"], L=L)
    attn = _attention(q.reshape(B, L, D_ATTN), k.reshape(B, L, D_KV), v.reshape(B, L, D_KV))
    ng2 = n2 // FFT_G
    hyv = hyc.reshape(3 * ncb, P, 2, ng2, n1h, FFT_G, LANES)

    fa, fa_re, fb, fbi, fc = _dft_tables(L)
    hf, asum = _filters(_filter_feats(L), p["filt_w1"], p["filt_b1"], p["filt_w2"], p["filt_b2"],
                        p["filt_w3"], p["filt_freq"], p["decay"])
    s1f = _fft_a_real(hf, fa_re)
    kf = _fft_b_filt(s1f, fb, asum)

    s1 = _fft_a(hyv, fa, ncb=ncb, cb0=0)
    s2 = _fft_b(s1, kf, fb, fbi, kcb0=0)
    z, s1 = _fft_c(s2, fc, hyv, hyv, p["bias0"], ucb0=0, gcb0=ncb, fa=fa)
    s2 = _fft_b(s1, kf, fb, fbi, kcb0=ncb)
    ho, = _fft_c(s2, fc, z, hyv, p["bias1"], ucb0=0, gcb0=2 * ncb)

    mix = (attn.reshape(T, D_ATTN), ho.reshape(ncb, B, ng2, n1h, FFT_G, LANES),
           p["gon_a"], p["gon_h"], p["wo_a"], p["wo_h"])
    x3 = _ffn(x1, p["ffn2_norm"], p["ffn2_w13"], p["ffn2_w2"], p["final_norm"], final_norm=True, mix=mix)
    return x3.reshape(B, L, D)


def kernel(x_prompt, x_sample, ffn1_norm, ffn1_w13, ffn1_w2, mix_norm, w_in, q_norm, k_norm, conv_w, conv_b, filt_w1, filt_b1, filt_w2, filt_b2, filt_w3, filt_freq, hyena_decay, hyena_bias, group_out_norm, w_out, ffn2_norm, ffn2_w13, ffn2_w2, final_norm):
    hp = FILT_PAD - FILTER_HIDDEN
    w_in0 = w_in[0]
    p = {
        "ffn1_norm": ffn1_norm[0][None], "ffn2_norm": ffn2_norm[0][None],
        "ffn1_w13": ffn1_w13[0].astype(BF16), "ffn1_w2": ffn1_w2[0].astype(BF16),
        "ffn2_w13": ffn2_w13[0].astype(BF16), "ffn2_w2": ffn2_w2[0].astype(BF16),
        "final_norm": final_norm[None],
        "mix_norm": mix_norm[0][None],
        "wq": w_in0[:, :D_ATTN].astype(BF16),
        "wkv": w_in0[:, D_ATTN:D_ATTN + 2 * D_KV].astype(BF16),
        "wh": w_in0[:, D_ATTN + 2 * D_KV:].astype(BF16),
        "q_norm": q_norm[0][None], "k_norm": k_norm[0][None],
        "conv_w": conv_w[0], "conv_b": conv_b[0][None],
        "filt_w1": jnp.pad(filt_w1[0], ((0, FILT_PAD - FILTER_EMB), (0, hp))),
        "filt_b1": jnp.pad(filt_b1[0], (0, hp))[None],
        "filt_w2": jnp.pad(filt_w2[0], ((0, hp), (0, hp))),
        "filt_b2": jnp.pad(filt_b2[0], (0, hp))[None],
        "filt_w3": jnp.pad(filt_w3[0], ((0, hp), (0, 0))),
        "filt_freq": jnp.pad(filt_freq[0], (0, hp))[None],
        "decay": hyena_decay[0].reshape(1, 4 * D_HYENA),
        "bias0": hyena_bias[0, 0][None], "bias1": hyena_bias[0, 1][None],
        "gon_a": group_out_norm[0, :D_ATTN][None], "gon_h": group_out_norm[0, D_ATTN:][None],
        "wo_a": w_out[0, :D_ATTN].astype(BF16), "wo_h": w_out[0, D_ATTN:].astype(BF16),
    }
    return (_trunk(x_prompt, p), _trunk(x_sample, p))
```

```python
import functools
import math

import jax
import jax.numpy as jnp
from jax import lax
from jax.experimental import pallas as pl
from jax.experimental.pallas import tpu as pltpu

F32 = jnp.float32
BF16 = jnp.bfloat16
U32 = jnp.uint32

D_MODEL = 2048
GRID_W = 64
D_ATTN = D_MODEL // 2
D_HYENA = D_MODEL - D_ATTN
HEAD_DIM = 128
N_Q_HEADS = D_ATTN // HEAD_DIM
N_KV_HEADS = 2
Q_PER_KV = N_Q_HEADS // N_KV_HEADS
ROPE_HALF = HEAD_DIM // 2
ROPE_THETA = 10000.0
FILTER_EMB = 33
FILTER_HIDDEN = 64
D_FF = 5632
EPS = 1e-6
D_KV = N_KV_HEADS * HEAD_DIM

LANES = 128
FILT_PAD = 128
FFT_N2 = 128
FFT_W = 512
FFT_G = 8
MIB = 1024 * 1024
LOG2_E = math.log2(math.e)


def _cparams(semantics, vmem_mib):
    return pltpu.CompilerParams(dimension_semantics=semantics,
                                vmem_limit_bytes=vmem_mib * MIB)


def _rms(x, g):
    ms = jnp.mean(x * x, axis=-1, keepdims=True)
    return x * lax.rsqrt(ms + EPS) * g


def _resident(shape):
    nd = len(shape)
    return pl.BlockSpec(shape, lambda *_: (0,) * nd, pipeline_mode=pl.Buffered(1))


def _ffn_body(x_ref, g_ref, wg_ref, wu_ref, w2_ref, gf_ref, o_ref, xn_ref, *, nj, final_norm):
    j = pl.program_id(1)

    @pl.when(j == 0)
    def _init():
        x = x_ref[...]
        xn_ref[...] = _rms(x, g_ref[...]).astype(BF16)
        o_ref[...] = x

    xn = xn_ref[...]
    hg = jnp.dot(xn, wg_ref[...], preferred_element_type=F32)
    hu = jnp.dot(xn, wu_ref[...], preferred_element_type=F32)
    a = (hg * (0.5 / (1.0 + jnp.exp(-hg))) * hu).astype(BF16)
    o_ref[...] += jnp.dot(a, w2_ref[...], preferred_element_type=F32)

    if final_norm:
        @pl.when(j == nj - 1)
        def _fin():
            o_ref[...] = _rms(o_ref[...], gf_ref[...])


def _ffn(x, g, w13, w2, gf, *, final_norm, tm=512, tf=512):
    T, D = x.shape
    ff = w2.shape[0]
    nj = ff // tf
    body = functools.partial(_ffn_body, nj=nj, final_norm=final_norm)
    return pl.pallas_call(
        body,
        grid=(T // tm, nj),
        in_specs=[
            pl.BlockSpec((tm, D), lambda i, j: (i, 0)),
            pl.BlockSpec((1, D), lambda i, j: (0, 0)),
            pl.BlockSpec((D, tf), lambda i, j: (0, j)),
            pl.BlockSpec((D, tf), lambda i, j: (0, j + nj)),
            pl.BlockSpec((tf, D), lambda i, j: (j, 0)),
            pl.BlockSpec((1, D), lambda i, j: (0, 0)),
        ],
        out_specs=pl.BlockSpec((tm, D), lambda i, j: (i, 0)),
        out_shape=jax.ShapeDtypeStruct((T, D), F32),
        scratch_shapes=[pltpu.VMEM((tm, D), BF16)],
        compiler_params=_cparams(("parallel", "arbitrary"), 48),
        name="ffn",
    )(x, g, w13, w13, w2, gf)


def _store_time_grouped(ref, cb, val):
    n1t = ref.shape[2]
    v = val.reshape(n1t * (FFT_N2 // FFT_G), FFT_G, LANES)
    for a in range(n1t):
        ref[cb, :, a] = v[a * (FFT_N2 // FFT_G):(a + 1) * (FFT_N2 // FFT_G)]


def _load_time_grouped(ref, cb):
    n1t = ref.shape[2]
    return jnp.concatenate([ref[cb, :, a].reshape(FFT_N2, LANES) for a in range(n1t)], axis=0)


MIX_HALO = 16


def _mix_body(xb_ref, x_ref, xa_ref, g_ref, wq_ref, wkv_ref, wh_ref, qn_ref, kn_ref, cos_ref, sin_ref,
              cw_ref, cb_ref, q_ref, k_ref, v_ref, hy_ref, *, npos):
    tm = x_ref.shape[0]
    x_ext = jnp.concatenate([xb_ref[...], x_ref[...], xa_ref[...]], axis=0)
    h_ext = _rms(x_ext, g_ref[...]).astype(BF16)
    h = h_ext[MIX_HALO:MIX_HALO + tm]
    lane = lax.broadcasted_iota(jnp.int32, (tm, HEAD_DIM), 1)
    low_half = (lane % (2 * (ROPE_HALF // 2))) < (ROPE_HALF // 2)
    c = cos_ref[...]
    s = sin_ref[...]

    def head(p, gain, scale):
        y = _rms(p, gain)
        partner = jnp.where(low_half, pltpu.roll(y, HEAD_DIM - ROPE_HALF // 2, 1),
                            pltpu.roll(y, ROPE_HALF // 2, 1))
        y = y * c + partner * s
        return y if scale is None else y * scale

    q = jnp.dot(h, wq_ref[...], preferred_element_type=F32)
    for hd in range(N_Q_HEADS):
        sl = slice(hd * HEAD_DIM, (hd + 1) * HEAD_DIM)
        q_ref[:, sl] = head(q[:, sl], qn_ref[...], HEAD_DIM ** -0.5 * LOG2_E).astype(BF16)
    kv = jnp.dot(h, wkv_ref[...], preferred_element_type=F32)
    for hd in range(N_KV_HEADS):
        sl = slice(hd * HEAD_DIM, (hd + 1) * HEAD_DIM)
        k_ref[:, sl] = head(kv[:, sl], kn_ref[...], None).astype(BF16)
    v_ref[...] = kv[:, D_KV:].astype(BF16)

    hy = jnp.dot(h_ext, wh_ref[...], preferred_element_type=F32)
    pos = pl.program_id(0) % npos
    rid = lax.broadcasted_iota(jnp.int32, (tm, 1), 0)
    before = jnp.where((rid == 0) & (pos == 0), 0.0, hy[MIX_HALO - 1:MIX_HALO - 1 + tm])
    after = jnp.where((rid == tm - 1) & (pos == npos - 1), 0.0, hy[MIX_HALO + 1:MIX_HALO + 1 + tm])
    hyc = (before * cw_ref[0:1, :] + hy[MIX_HALO:MIX_HALO + tm] * cw_ref[1:2, :]
           + after * cw_ref[2:3, :] + cb_ref[...])
    for cb in range(hy_ref.shape[0]):
        _store_time_grouped(hy_ref, cb, hyc[:, cb * LANES:(cb + 1) * LANES])


def _mix_in(x, g, wq, wkv, wh, qn, kn, cos_t, sin_t, cw, cb, *, L, tm=256):
    T, D = x.shape
    npos = L // tm
    ng2 = FFT_N2 // FFT_G
    hb = tm // MIX_HALO
    nhb = T // MIX_HALO
    return pl.pallas_call(
        functools.partial(_mix_body, npos=npos),
        grid=(T // tm,),
        in_specs=[
            pl.BlockSpec((MIX_HALO, D), lambda i: (jnp.maximum(i * hb - 1, 0), 0)),
            pl.BlockSpec((tm, D), lambda i: (i, 0)),
            pl.BlockSpec((MIX_HALO, D), lambda i: (jnp.minimum((i + 1) * hb, nhb - 1), 0)),
            _resident((1, D)),
            _resident(wq.shape),
            _resident(wkv.shape),
            _resident(wh.shape),
            _resident((1, HEAD_DIM)),
            _resident((1, HEAD_DIM)),
            pl.BlockSpec((tm, HEAD_DIM), lambda i: (i % npos, 0)),
            pl.BlockSpec((tm, HEAD_DIM), lambda i: (i % npos, 0)),
            _resident(cw.shape),
            _resident(cb.shape),
        ],
        out_specs=[
            pl.BlockSpec((tm, D_ATTN), lambda i: (i, 0)),
            pl.BlockSpec((tm, D_KV), lambda i: (i, 0)),
            pl.BlockSpec((tm, D_KV), lambda i: (i, 0)),
            pl.BlockSpec((3 * D_HYENA // LANES, None, ng2, tm // FFT_N2, FFT_G, LANES),
                         lambda i: (0, i // npos, 0, i % npos, 0, 0)),
        ],
        out_shape=[
            jax.ShapeDtypeStruct((T, D_ATTN), BF16),
            jax.ShapeDtypeStruct((T, D_KV), BF16),
            jax.ShapeDtypeStruct((T, D_KV), BF16),
            jax.ShapeDtypeStruct((3 * D_HYENA // LANES, T // L, ng2, L // FFT_N2, FFT_G, LANES), F32),
        ],
        compiler_params=_cparams(("parallel",), 48),
        name="mix_in",
    )(x, x, x, g, wq, wkv, wh, qn, kn, cos_t, sin_t, cw, cb)


ATTN_ROWS = 32
ATTN_RING = 4
ATTN_ONES = 16


def _attn_body(q_ref, k_ref, v_ref, o_ref, vt_ref, *bufs, tq, tk, nk):
    s_refs, mx_refs, p_refs = (bufs[k * ATTN_RING:(k + 1) * ATTN_RING] for k in range(3))

    @pl.when(pl.program_id(2) == 0)
    def _transpose_v():
        for c in range(nk):
            vt_ref[c, :HEAD_DIM] = v_ref[pl.ds(c * tk, tk), :].astype(F32).T.astype(BF16)
            vt_ref[c, HEAD_DIM:] = jnp.ones((ATTN_ONES, tk), BF16)

    q = jnp.concatenate([q_ref[:, h * HEAD_DIM:(h + 1) * HEAD_DIM] for h in range(Q_PER_KV)], axis=0)
    m_cols = Q_PER_KV * tq
    nchunk = tk // ATTN_ROWS

    def stage_a(i, s_ref, mx_ref):
        kc = k_ref[pl.ds(pl.multiple_of(i * tk, tk), tk), :]
        s = lax.dot_general(kc, q, (((1,), (1,)), ((), ())), preferred_element_type=F32)
        s_ref[...] = s
        mx_ref[...] = jnp.max(s, axis=0, keepdims=True)

    def stage_b(s_ref, mx_ref, p_ref, m):
        m_new = jnp.maximum(m, mx_ref[...])
        alpha = jnp.exp2(m - m_new)
        for c in range(nchunk):
            rows = slice(c * ATTN_ROWS, (c + 1) * ATTN_ROWS)
            p_ref[rows, :] = jnp.exp2((s_ref[rows, :] - m_new).astype(BF16))
        return m_new, alpha

    def stage_d(i, p_ref, alpha, acc):
        return alpha * acc + jnp.dot(vt_ref[i], p_ref[...], preferred_element_type=F32)

    def ring(i, m, al0, al1, acc, n_a, n_b):
        alphas = {0: al0, 1: al1}
        for u in range(ATTN_RING):
            if u < n_a:
                stage_a(i + u + 4, s_refs[u], mx_refs[u])
            if u < n_b:
                v = (u + 2) % ATTN_RING
                m, alphas[u + 2] = stage_b(s_refs[v], mx_refs[v], p_refs[v], m)
            acc = stage_d(i + u, p_refs[u], alphas[u], acc)
        return m, alphas.get(4), alphas.get(5), acc

    m = jnp.full((1, m_cols), -jnp.inf, F32)
    acc = jnp.zeros((HEAD_DIM + ATTN_ONES, m_cols), F32)
    for u in range(ATTN_RING):
        stage_a(u, s_refs[u], mx_refs[u])
    m, al0 = stage_b(s_refs[0], mx_refs[0], p_refs[0], m)
    m, al1 = stage_b(s_refs[1], mx_refs[1], p_refs[1], m)

    def trip(j, carry):
        return ring(ATTN_RING * j, *carry, ATTN_RING, ATTN_RING)

    m, al0, al1, acc = lax.fori_loop(0, nk // ATTN_RING - 1, trip, (m, al0, al1, acc))
    _, _, _, acc = ring(nk - ATTN_RING, m, al0, al1, acc, 0, 2)
    o = acc[:HEAD_DIM] / acc[HEAD_DIM:HEAD_DIM + 1]
    for h in range(Q_PER_KV):
        o_ref[:, h * HEAD_DIM:(h + 1) * HEAD_DIM] = o[:, h * tq:(h + 1) * tq].T


def _attention(q, k, v, *, tq=256, tk=512):
    B, L, _ = q.shape
    qw = Q_PER_KV * HEAD_DIM
    nk = L // tk
    assert nk % ATTN_RING == 0 and nk >= 2 * ATTN_RING
    body = functools.partial(_attn_body, tq=tq, tk=tk, nk=nk)
    m_cols = Q_PER_KV * tq
    return pl.pallas_call(
        body,
        grid=(B, N_KV_HEADS, L // tq),
        in_specs=[
            pl.BlockSpec((None, tq, qw), lambda b, g, i: (b, i, g)),
            pl.BlockSpec((None, L, HEAD_DIM), lambda b, g, i: (b, 0, g)),
            pl.BlockSpec((None, L, HEAD_DIM), lambda b, g, i: (b, 0, g)),
        ],
        out_specs=pl.BlockSpec((None, tq, qw), lambda b, g, i: (b, i, g)),
        out_shape=jax.ShapeDtypeStruct((B, L, D_ATTN), F32),
        scratch_shapes=([pltpu.VMEM((nk, HEAD_DIM + ATTN_ONES, tk), BF16)]
                        + [pltpu.VMEM((tk, m_cols), F32)] * ATTN_RING
                        + [pltpu.VMEM((1, m_cols), F32)] * ATTN_RING
                        + [pltpu.VMEM((tk, m_cols), BF16)] * ATTN_RING),
        compiler_params=_cparams(("parallel", "parallel", "arbitrary"), 32),
        name="attn",
    )(q, k, v)


def _filt_body(feat_ref, w1_ref, b1_ref, w2_ref, b2_ref, w3_ref, fq_ref, dec_ref, hf_ref, asum_ref, *, tl):
    i = pl.program_id(0)
    hp = lax.Precision.HIGHEST
    fq = fq_ref[...]
    feat = feat_ref[...]
    h = jnp.sin(fq * (jnp.dot(feat, w1_ref[...], precision=hp, preferred_element_type=F32) + b1_ref[...]))
    h = jnp.sin(fq * (jnp.dot(h, w2_ref[...], precision=hp, preferred_element_type=F32) + b2_ref[...]))
    h3 = jnp.dot(h, w3_ref[...], precision=hp, preferred_element_type=F32)
    t01 = feat[:, 0:1]
    hf = h3 * jnp.exp(-t01 * jnp.abs(dec_ref[...]))
    ncol = hf.shape[1]
    row = lax.broadcasted_iota(jnp.int32, hf.shape, 0) + i * tl
    col = lax.broadcasted_iota(jnp.int32, hf.shape, 1)
    hf = jnp.where((row == 0) & (col >= ncol // 2), 0.0, hf)
    for cb in range(ncol // LANES):
        _store_time_grouped(hf_ref, cb, hf[:, cb * LANES:(cb + 1) * LANES])
    part = jnp.sum(jnp.abs(hf), axis=0, keepdims=True)

    @pl.when(i == 0)
    def _first():
        asum_ref[...] = part

    @pl.when(i > 0)
    def _rest():
        asum_ref[...] += part


def _filters(feats, w1, b1, w2, b2, w3, fq, dec, *, tl=256):
    L = feats.shape[0]
    nc = w3.shape[1]
    ng2 = FFT_N2 // FFT_G
    body = functools.partial(_filt_body, tl=tl)
    return pl.pallas_call(
        body,
        grid=(L // tl,),
        in_specs=[
            pl.BlockSpec((tl, FILT_PAD), lambda i: (i, 0)),
            _resident(w1.shape), _resident(b1.shape), _resident(w2.shape), _resident(b2.shape),
            _resident(w3.shape), _resident(fq.shape), _resident(dec.shape),
        ],
        out_specs=[
            pl.BlockSpec((nc // LANES, ng2, tl // FFT_N2, FFT_G, LANES), lambda i: (0, 0, i, 0, 0)),
            pl.BlockSpec((1, nc), lambda i: (0, 0)),
        ],
        out_shape=[jax.ShapeDtypeStruct((nc // LANES, ng2, L // FFT_N2, FFT_G, LANES), F32),
                   jax.ShapeDtypeStruct((1, nc), F32)],
        compiler_params=_cparams(("arbitrary",), 48),
        name="filt",
    )(feats, w1, b1, w2, b2, w3, fq, dec)


FFT_CB = FFT_W // LANES
FFT_CB_B = 2


def _pack_c(y):
    n = y.shape[0] // 2
    r = lax.bitcast_convert_type(y[:n], U32)
    i = lax.bitcast_convert_type(y[n:], U32)
    half = jnp.uint32(0x8000)
    return ((r + half) & jnp.uint32(0xFFFF0000)) | ((i + half) >> 16)


def _unpack_c(w):
    re = lax.bitcast_convert_type(w & jnp.uint32(0xFFFF0000), F32)
    im = lax.bitcast_convert_type(w << 16, F32)
    return re, im


def _unpack_rows(w):
    re, im = _unpack_c(w)
    return jnp.concatenate([re, im], axis=0).astype(BF16)


def _flat_scratch(n, dtype, cbs, lead=()):
    return pltpu.VMEM(lead + (cbs, n * FFT_G, LANES), dtype)


def _flatten(flat_ref, block_ref, pre=()):
    rows = flat_ref.shape[-2]
    for h in range(block_ref.shape[0]):
        flat_ref[pre + (h,)] = block_ref[(h,) + pre].reshape(rows, LANES)


def _unflatten(block_ref, flat_ref, pre=()):
    n = flat_ref.shape[-2] // FFT_G
    for h in range(block_ref.shape[0]):
        block_ref[(h,) + pre] = flat_ref[pre + (h,)].reshape(n, FFT_G, LANES)


def _gather_rows(flat_ref, j, pre=()):
    n = flat_ref.shape[-2] // FFT_G
    return jnp.concatenate(
        [flat_ref[pre + (h, pl.ds(j, n, stride=FFT_G), slice(None))] for h in range(flat_ref.shape[len(pre)])],
        axis=1)


def _scatter_rows(flat_ref, j, val, pre=()):
    n = flat_ref.shape[-2] // FFT_G
    for h in range(flat_ref.shape[len(pre)]):
        flat_ref[pre + (h, pl.ds(j, n, stride=FFT_G), slice(None))] = val[:, h * LANES:(h + 1) * LANES]


def _store_blocks(ref, j, val):
    for h in range(ref.shape[0]):
        ref[h, j] = val[:, h * LANES:(h + 1) * LANES]


def _store_grouped(ref, j, val):
    groups = ref.shape[1]
    for h in range(ref.shape[0]):
        ref[h, :, j] = val[:, h * LANES:(h + 1) * LANES].reshape(groups, FFT_G, LANES)


def _fft_a_body(x_ref, fa_ref, o_ref, xf_ref):
    for b in range(2):
        _flatten(xf_ref, x_ref, (b,))
    for j in range(FFT_G):
        xs = jnp.concatenate([_gather_rows(xf_ref, j, (b,)) for b in range(2)], axis=0).astype(BF16)
        _store_grouped(o_ref, j, _pack_c(jnp.dot(fa_ref[j], xs, preferred_element_type=F32)))


def _fft_a(xv, fa, *, ncb, cb0=0):
    _, P, _, ng2, n1h, _, _ = xv.shape
    n2 = ng2 * FFT_G
    two_n1 = fa.shape[1]
    n1 = two_n1 // 2
    c0 = cb0 // FFT_CB
    return pl.pallas_call(
        _fft_a_body,
        grid=(ng2, P, ncb // FFT_CB),
        in_specs=[
            pl.BlockSpec((FFT_CB, None, 2, None, n1h, FFT_G, LANES), lambda g, p, c: (c + c0, p, 0, g, 0, 0, 0)),
            pl.BlockSpec((FFT_G, two_n1, 2 * n1h), lambda g, p, c: (g, 0, 0)),
        ],
        out_specs=pl.BlockSpec((None, FFT_CB, n1 // FFT_G, FFT_G, FFT_G, LANES),
                               lambda g, p, c: (p, c, 0, g, 0, 0)),
        out_shape=jax.ShapeDtypeStruct((P, ncb, n1 // FFT_G, n2, FFT_G, LANES), U32),
        scratch_shapes=[_flat_scratch(n1h, F32, FFT_CB, (2,))],
        compiler_params=_cparams(("arbitrary", "arbitrary", "arbitrary"), 32),
        name="fft_a",
    )(xv, fa)


def _fft_a_real_body(x_ref, fa_ref, o_ref, xf_ref):
    _flatten(xf_ref, x_ref)
    for j in range(FFT_G):
        xs = _gather_rows(xf_ref, j).astype(BF16)
        _store_grouped(o_ref, j, _pack_c(jnp.dot(fa_ref[j], xs, preferred_element_type=F32)))


def _fft_a_real(xv, fa_re):
    ncb, ng2, n1h, _, _ = xv.shape
    n2 = ng2 * FFT_G
    two_n1 = fa_re.shape[1]
    n1 = two_n1 // 2
    return pl.pallas_call(
        _fft_a_real_body,
        grid=(ng2, ncb // FFT_CB),
        in_specs=[
            pl.BlockSpec((FFT_CB, None, n1h, FFT_G, LANES), lambda g, c: (c, g, 0, 0, 0)),
            pl.BlockSpec((FFT_G, two_n1, n1h), lambda g, c: (g, 0, 0)),
        ],
        out_specs=pl.BlockSpec((FFT_CB, n1 // FFT_G, FFT_G, FFT_G, LANES), lambda g, c: (c, 0, g, 0, 0)),
        out_shape=jax.ShapeDtypeStruct((ncb, n1 // FFT_G, n2, FFT_G, LANES), U32),
        scratch_shapes=[_flat_scratch(n1h, F32, FFT_CB)],
        compiler_params=_cparams(("arbitrary", "arbitrary"), 32),
        name="fft_a_real",
    )(xv, fa_re)


def _fft_b_filt_body(f_ref, b_ref, fb_ref, asf_ref, asb_ref, o_ref, ff_ref, bf_ref):
    n2 = FFT_N2
    _flatten(ff_ref, f_ref)
    _flatten(bf_ref, b_ref)
    inv = 1.0 / (asf_ref[...] + asb_ref[...])
    fb = fb_ref[...]
    for j in range(FFT_G):
        u = jnp.dot(fb, _unpack_rows(_gather_rows(ff_ref, j)), preferred_element_type=F32)
        w = jnp.dot(fb, _unpack_rows(_gather_rows(bf_ref, j)), preferred_element_type=F32)
        kr = (u[:n2] + w[:n2]) * inv
        ki = (u[n2:] - w[n2:]) * inv
        _store_blocks(o_ref, j, _pack_c(jnp.concatenate([kr, ki], axis=0)))


def _fft_b_filt(s1f, fb, asum):
    cb2, ng1, n2, _, _ = s1f.shape
    n1 = ng1 * FFT_G
    ncb = cb2 // 2
    nb = ncb // FFT_CB
    blk = (FFT_CB, None, n2, FFT_G, LANES)
    return pl.pallas_call(
        _fft_b_filt_body,
        grid=(ng1, nb),
        in_specs=[
            pl.BlockSpec(blk, lambda g, c: (c, g, 0, 0, 0)),
            pl.BlockSpec(blk, lambda g, c: (c + nb, g, 0, 0, 0)),
            _resident(fb.shape),
            pl.BlockSpec((1, FFT_W), lambda g, c: (0, c)),
            pl.BlockSpec((1, FFT_W), lambda g, c: (0, c + nb)),
        ],
        out_specs=pl.BlockSpec((FFT_CB, FFT_G, n2, LANES), lambda g, c: (c, g, 0, 0)),
        out_shape=jax.ShapeDtypeStruct((ncb, n1, n2, LANES), U32),
        scratch_shapes=[_flat_scratch(n2, U32, FFT_CB), _flat_scratch(n2, U32, FFT_CB)],
        compiler_params=_cparams(("parallel", "parallel"), 32),
        name="fft_b_filt",
    )(s1f, s1f, fb, asum, asum)


def _fft_b_body(s_ref, kf_ref, fb_ref, fbi_ref, o_ref, sf_ref):
    n2 = FFT_N2
    _flatten(sf_ref, s_ref)
    fb = fb_ref[...]
    fbi = fbi_ref[...]
    for j in range(FFT_G):
        x = jnp.dot(fb, _unpack_rows(_gather_rows(sf_ref, j)), preferred_element_type=F32)
        xr, xi = x[:n2], x[n2:]
        kr, ki = _unpack_c(jnp.concatenate([kf_ref[h, j] for h in range(kf_ref.shape[0])], axis=1))
        pm = jnp.concatenate([xr * kr - xi * ki, xr * ki + xi * kr], axis=0).astype(BF16)
        _store_grouped(o_ref, j, _pack_c(jnp.dot(fbi, pm, preferred_element_type=F32)))


def _fft_b(s1, kf, fb, fbi, *, kcb0):
    P, ncb, ng1, n2, _, _ = s1.shape
    n1 = ng1 * FFT_G
    cbs = FFT_CB_B
    k0 = kcb0 // cbs
    return pl.pallas_call(
        _fft_b_body,
        grid=(ng1, ncb // cbs, P),
        in_specs=[
            pl.BlockSpec((None, cbs, None, n2, FFT_G, LANES), lambda g, c, p: (p, c, g, 0, 0, 0)),
            pl.BlockSpec((cbs, FFT_G, n2, LANES), lambda g, c, p: (c + k0, g, 0, 0)),
            _resident(fb.shape),
            _resident(fbi.shape),
        ],
        out_specs=pl.BlockSpec((None, cbs, n2 // FFT_G, FFT_G, FFT_G, LANES),
                               lambda g, c, p: (p, c, 0, g, 0, 0)),
        out_shape=jax.ShapeDtypeStruct((P, ncb, n2 // FFT_G, n1, FFT_G, LANES), U32),
        scratch_shapes=[_flat_scratch(n2, U32, cbs)],
        compiler_params=_cparams(("parallel", "parallel", "arbitrary"), 32),
        name="fft_b",
    )(s1, kf, fb, fbi)


def _fft_c_body(z_ref, fc_ref, u_ref, gate_ref, d_ref, *rest, chain):
    if chain:
        fa_ref, o_ref, s_ref, zf_ref, uf_ref, gf_ref, of_ref = rest
    else:
        o_ref, zf_ref, uf_ref, gf_ref, of_ref = rest
    n1h = o_ref.shape[2]
    d = d_ref[...]
    _flatten(zf_ref, z_ref)
    for b in range(2):
        _flatten(uf_ref, u_ref, (b,))
        _flatten(gf_ref, gate_ref, (b,))
    for j in range(FFT_G):
        y = jnp.dot(fc_ref[j], _unpack_rows(_gather_rows(zf_ref, j)), preferred_element_type=F32)
        res = []
        for b in range(2):
            yb = y[b * n1h:(b + 1) * n1h]
            res.append(_gather_rows(gf_ref, j, (b,)) * (yb + _gather_rows(uf_ref, j, (b,)) * d))
            _scatter_rows(of_ref, j, res[b], (b,))
        if chain:
            xs = jnp.concatenate(res, axis=0).astype(BF16)
            _store_grouped(s_ref, j, _pack_c(jnp.dot(fa_ref[j], xs, preferred_element_type=F32)))
    for b in range(2):
        _unflatten(o_ref, of_ref, (b,))


def _fft_c(s2, fc, uv, gv, d, *, ucb0, gcb0, fa=None):
    P, ncb, ng2, n1, _, _ = s2.shape
    n1h = n1 // 2
    n2 = ng2 * FFT_G
    u0 = ucb0 // FFT_CB
    g0 = gcb0 // FFT_CB
    chain = fa is not None
    tblk = (FFT_CB, None, 2, None, n1h, FFT_G, LANES)
    in_specs = [
        pl.BlockSpec((None, FFT_CB, None, n1, FFT_G, LANES), lambda g, p, c: (p, c, g, 0, 0, 0)),
        pl.BlockSpec((FFT_G, 2 * n1h, 2 * n1), lambda g, p, c: (g, 0, 0)),
        pl.BlockSpec(tblk, lambda g, p, c: (c + u0, p, 0, g, 0, 0, 0)),
        pl.BlockSpec(tblk, lambda g, p, c: (c + g0, p, 0, g, 0, 0, 0)),
        pl.BlockSpec((1, FFT_W), lambda g, p, c: (0, c)),
    ]
    out_specs = [pl.BlockSpec(tblk, lambda g, p, c: (c, p, 0, g, 0, 0, 0))]
    out_shape = [jax.ShapeDtypeStruct((ncb, P, 2, ng2, n1h, FFT_G, LANES), F32)]
    args = [s2, fc, uv, gv, d]
    if chain:
        in_specs.append(pl.BlockSpec((FFT_G, 2 * n1, 2 * n1h), lambda g, p, c: (g, 0, 0)))
        out_specs.append(pl.BlockSpec((None, FFT_CB, n1 // FFT_G, FFT_G, FFT_G, LANES),
                                      lambda g, p, c: (p, c, 0, g, 0, 0)))
        out_shape.append(jax.ShapeDtypeStruct((P, ncb, n1 // FFT_G, n2, FFT_G, LANES), U32))
        args.append(fa)
    return pl.pallas_call(
        functools.partial(_fft_c_body, chain=chain),
        grid=(ng2, P, ncb // FFT_CB),
        in_specs=in_specs,
        out_specs=out_specs,
        out_shape=out_shape,
        scratch_shapes=[_flat_scratch(n1, U32, FFT_CB), _flat_scratch(n1h, F32, FFT_CB, (2,)),
                        _flat_scratch(n1h, F32, FFT_CB, (2,)), _flat_scratch(n1h, F32, FFT_CB, (2,))],
        compiler_params=_cparams(("arbitrary", "arbitrary", "arbitrary"), 40),
        name="fft_c",
    )(*args)


def _oproj_body(x_ref, a_ref, h_ref, ga_ref, gh_ref, wa_ref, wh_ref, o_ref):
    ma = _rms(a_ref[...], ga_ref[...]).astype(BF16)
    hy = jnp.concatenate([_load_time_grouped(h_ref, cb) for cb in range(h_ref.shape[0])], axis=1)
    mh = _rms(hy, gh_ref[...]).astype(BF16)
    o_ref[...] = (x_ref[...] + jnp.dot(ma, wa_ref[...], preferred_element_type=F32)
                  + jnp.dot(mh, wh_ref[...], preferred_element_type=F32))


def _out_proj(x, a, h, ga, gh, wa, wh, *, tm=512):
    T, D = x.shape
    ncb, _, ng2, n1h, _, _ = h.shape
    npos = n1h * FFT_N2 // tm
    return pl.pallas_call(
        _oproj_body,
        grid=(T // tm,),
        in_specs=[
            pl.BlockSpec((tm, D), lambda i: (i, 0)),
            pl.BlockSpec((tm, D_ATTN), lambda i: (i, 0)),
            pl.BlockSpec((ncb, None, ng2, tm // FFT_N2, FFT_G, LANES), lambda i: (0, i // npos, 0, i % npos, 0, 0)),
            _resident((1, D_ATTN)), _resident((1, D_HYENA)),
            _resident(wa.shape), _resident(wh.shape),
        ],
        out_specs=pl.BlockSpec((tm, D), lambda i: (i, 0)),
        out_shape=jax.ShapeDtypeStruct((T, D), F32),
        compiler_params=_cparams(("parallel",), 48),
        name="oproj",
    )(x, a, h, ga, gh, wa, wh)


def _rope_tables(L):
    rows = L // GRID_W
    row = jnp.repeat(jnp.arange(rows, dtype=F32), GRID_W)
    col = jnp.tile(jnp.arange(GRID_W, dtype=F32), rows)
    inv = ROPE_THETA ** (-jnp.arange(0, ROPE_HALF, 2, dtype=F32) / ROPE_HALF)
    ar = row[:, None] * inv[None]
    ac = col[:, None] * inv[None]
    cr, sr, cc, sc = jnp.cos(ar), jnp.sin(ar), jnp.cos(ac), jnp.sin(ac)
    cos_t = jnp.concatenate([cr, cr, cc, cc], axis=-1)
    sin_t = jnp.concatenate([-sr, sr, -sc, sc], axis=-1)
    return cos_t, sin_t


def _filter_feats(L):
    t01 = jnp.linspace(0.0, 1.0, L, dtype=F32)[:, None]
    bands = (FILTER_EMB - 1) // 2
    fr = jnp.linspace(1e-4, bands - 1, bands, dtype=F32)[None]
    w = 2.0 * math.pi * jnp.arange(L, dtype=F32)[:, None] / L
    feats = jnp.concatenate([t01, jnp.cos(fr * w), -jnp.sin(fr * w)], axis=-1)
    return jnp.pad(feats, ((0, 0), (0, FILT_PAD - FILTER_EMB)))


def _dft_tables(L):
    n = 2 * L
    n2 = FFT_N2
    n1 = n // n2
    n1h = n1 // 2
    k1 = jnp.arange(n1, dtype=jnp.int32)[None, :, None]
    t1 = jnp.arange(n1h, dtype=jnp.int32)[None, None, :]
    t2 = jnp.arange(n2, dtype=jnp.int32)[:, None, None]
    ang = (-2.0 * math.pi / n) * ((k1 * (n2 * t1 + t2)) % n).astype(F32)
    cr, ci = jnp.cos(ang), jnp.sin(ang)
    fa = jnp.concatenate([jnp.concatenate([cr, -ci], -1), jnp.concatenate([ci, cr], -1)], -2)
    fa_re = jnp.concatenate([cr, ci], -2)
    crt, cit = jnp.swapaxes(cr, 1, 2), jnp.swapaxes(ci, 1, 2)
    fc = jnp.concatenate([jnp.concatenate([crt, cit], -1), jnp.concatenate([-cit, crt], -1)], -2) / n
    a = jnp.arange(n2, dtype=jnp.int32)
    ang2 = (-2.0 * math.pi / n2) * ((a[:, None] * a[None, :]) % n2).astype(F32)
    fr, fi = jnp.cos(ang2), jnp.sin(ang2)
    fb = jnp.concatenate([jnp.concatenate([fr, -fi], -1), jnp.concatenate([fi, fr], -1)], -2)
    fbi = jnp.concatenate([jnp.concatenate([fr, fi], -1), jnp.concatenate([-fi, fr], -1)], -2)
    return (fa.astype(BF16), fa_re.astype(BF16), fb.astype(BF16), fbi.astype(BF16), fc.astype(BF16))


def _trunk(x, p):
    B, L, D = x.shape
    T = B * L
    P = B // 2
    n2 = FFT_N2
    n1h = L // n2
    C = D_HYENA
    ncb = C // LANES

    x0 = x.reshape(T, D)
    x1 = _ffn(x0, p["ffn1_norm"], p["ffn1_w13"], p["ffn1_w2"], p["final_norm"], final_norm=False)

    cos_t, sin_t = _rope_tables(L)
    q, k, v, hyc = _mix_in(x1, p["mix_norm"], p["wq"], p["wkv"], p["wh"], p["q_norm"], p["k_norm"],
                           cos_t, sin_t, p["conv_w"], p["conv_b"], L=L)
    attn = _attention(q.reshape(B, L, D_ATTN), k.reshape(B, L, D_KV), v.reshape(B, L, D_KV))
    ng2 = n2 // FFT_G
    hyv = hyc.reshape(3 * ncb, P, 2, ng2, n1h, FFT_G, LANES)

    fa, fa_re, fb, fbi, fc = _dft_tables(L)
    hf, asum = _filters(_filter_feats(L), p["filt_w1"], p["filt_b1"], p["filt_w2"], p["filt_b2"],
                        p["filt_w3"], p["filt_freq"], p["decay"])
    s1f = _fft_a_real(hf, fa_re)
    kf = _fft_b_filt(s1f, fb, asum)

    s1 = _fft_a(hyv, fa, ncb=ncb, cb0=0)
    s2 = _fft_b(s1, kf, fb, fbi, kcb0=0)
    z, s1 = _fft_c(s2, fc, hyv, hyv, p["bias0"], ucb0=0, gcb0=ncb, fa=fa)
    s2 = _fft_b(s1, kf, fb, fbi, kcb0=ncb)
    ho, = _fft_c(s2, fc, z, hyv, p["bias1"], ucb0=0, gcb0=2 * ncb)

    x2 = _out_proj(x1, attn.reshape(T, D_ATTN), ho.reshape(ncb, B, ng2, n1h, FFT_G, LANES),
                   p["gon_a"], p["gon_h"], p["wo_a"], p["wo_h"])
    x3 = _ffn(x2, p["ffn2_norm"], p["ffn2_w13"], p["ffn2_w2"], p["final_norm"], final_norm=True)
    return x3.reshape(B, L, D)


def kernel(x_prompt, x_sample, ffn1_norm, ffn1_w13, ffn1_w2, mix_norm, w_in, q_norm, k_norm, conv_w, conv_b, filt_w1, filt_b1, filt_w2, filt_b2, filt_w3, filt_freq, hyena_decay, hyena_bias, group_out_norm, w_out, ffn2_norm, ffn2_w13, ffn2_w2, final_norm):
    hp = FILT_PAD - FILTER_HIDDEN
    w_in0 = w_in[0]
    p = {
        "ffn1_norm": ffn1_norm[0][None], "ffn2_norm": ffn2_norm[0][None],
        "ffn1_w13": ffn1_w13[0].astype(BF16), "ffn1_w2": ffn1_w2[0].astype(BF16),
        "ffn2_w13": ffn2_w13[0].astype(BF16), "ffn2_w2": ffn2_w2[0].astype(BF16),
        "final_norm": final_norm[None],
        "mix_norm": mix_norm[0][None],
        "wq": w_in0[:, :D_ATTN].astype(BF16),
        "wkv": w_in0[:, D_ATTN:D_ATTN + 2 * D_KV].astype(BF16),
        "wh": w_in0[:, D_ATTN + 2 * D_KV:].astype(BF16),
        "q_norm": q_norm[0][None], "k_norm": k_norm[0][None],
        "conv_w": conv_w[0], "conv_b": conv_b[0][None],
        "filt_w1": jnp.pad(filt_w1[0], ((0, FILT_PAD - FILTER_EMB), (0, hp))),
        "filt_b1": jnp.pad(filt_b1[0], (0, hp))[None],
        "filt_w2": jnp.pad(filt_w2[0], ((0, hp), (0, hp))),
        "filt_b2": jnp.pad(filt_b2[0], (0, hp))[None],
        "filt_w3": jnp.pad(filt_w3[0], ((0, hp), (0, 0))),
        "filt_freq": jnp.pad(filt_freq[0], (0, hp))[None],
        "decay": hyena_decay[0].reshape(1, 4 * D_HYENA),
        "bias0": hyena_bias[0, 0][None], "bias1": hyena_bias[0, 1][None],
        "gon_a": group_out_norm[0, :D_ATTN][None], "gon_h": group_out_norm[0, D_ATTN:][None],
        "wo_a": w_out[0, :D_ATTN].astype(BF16), "wo_h": w_out[0, D_ATTN:].astype(BF16),
    }
    return (_trunk(x_prompt, p), _trunk(x_sample, p))
```

```python
import functools
import math

import jax
import jax.numpy as jnp
from jax import lax
from jax.experimental import pallas as pl
from jax.experimental.pallas import tpu as pltpu

F32 = jnp.float32
BF16 = jnp.bfloat16
U32 = jnp.uint32

D_MODEL = 2048
GRID_W = 64
D_ATTN = D_MODEL // 2
D_HYENA = D_MODEL - D_ATTN
HEAD_DIM = 128
N_Q_HEADS = D_ATTN // HEAD_DIM
N_KV_HEADS = 2
Q_PER_KV = N_Q_HEADS // N_KV_HEADS
ROPE_HALF = HEAD_DIM // 2
ROPE_THETA = 10000.0
FILTER_EMB = 33
FILTER_HIDDEN = 64
D_FF = 5632
EPS = 1e-6
D_KV = N_KV_HEADS * HEAD_DIM

LANES = 128
FILT_PAD = 128
FFT_N2 = 128
FFT_W = 512
FFT_G = 8
MIB = 1024 * 1024
LOG2_E = math.log2(math.e)


def _cparams(semantics, vmem_mib):
    return pltpu.CompilerParams(dimension_semantics=semantics,
                                vmem_limit_bytes=vmem_mib * MIB)


def _rms(x, g):
    ms = jnp.mean(x * x, axis=-1, keepdims=True)
    return x * lax.rsqrt(ms + EPS) * g


def _resident(shape):
    nd = len(shape)
    return pl.BlockSpec(shape, lambda *_: (0,) * nd, pipeline_mode=pl.Buffered(1))


def _ffn_body(x_ref, g_ref, wg_ref, wu_ref, w2_ref, gf_ref, o_ref, xn_ref, *, nj, final_norm):
    j = pl.program_id(1)

    @pl.when(j == 0)
    def _init():
        x = x_ref[...]
        xn_ref[...] = _rms(x, g_ref[...]).astype(BF16)
        o_ref[...] = x

    xn = xn_ref[...]
    hg = jnp.dot(xn, wg_ref[...], preferred_element_type=F32)
    hu = jnp.dot(xn, wu_ref[...], preferred_element_type=F32)
    a = (hg * (0.5 / (1.0 + jnp.exp(-hg))) * hu).astype(BF16)
    o_ref[...] += jnp.dot(a, w2_ref[...], preferred_element_type=F32)

    if final_norm:
        @pl.when(j == nj - 1)
        def _fin():
            o_ref[...] = _rms(o_ref[...], gf_ref[...])


def _ffn(x, g, w13, w2, gf, *, final_norm, tm=1024, tf=512):
    T, D = x.shape
    ff = w2.shape[0]
    nj = ff // tf
    body = functools.partial(_ffn_body, nj=nj, final_norm=final_norm)
    return pl.pallas_call(
        body,
        grid=(T // tm, nj),
        in_specs=[
            pl.BlockSpec((tm, D), lambda i, j: (i, 0)),
            pl.BlockSpec((1, D), lambda i, j: (0, 0)),
            pl.BlockSpec((D, tf), lambda i, j: (0, j)),
            pl.BlockSpec((D, tf), lambda i, j: (0, j + nj)),
            pl.BlockSpec((tf, D), lambda i, j: (j, 0)),
            pl.BlockSpec((1, D), lambda i, j: (0, 0)),
        ],
        out_specs=pl.BlockSpec((tm, D), lambda i, j: (i, 0)),
        out_shape=jax.ShapeDtypeStruct((T, D), F32),
        scratch_shapes=[pltpu.VMEM((tm, D), BF16)],
        compiler_params=_cparams(("parallel", "arbitrary"), 58),
        name="ffn",
    )(x, g, w13, w13, w2, gf)


def _store_time_grouped(ref, cb, val):
    n1t = ref.shape[2]
    v = val.reshape(n1t * (FFT_N2 // FFT_G), FFT_G, LANES)
    for a in range(n1t):
        ref[cb, :, a] = v[a * (FFT_N2 // FFT_G):(a + 1) * (FFT_N2 // FFT_G)]


def _load_time_grouped(ref, cb):
    n1t = ref.shape[2]
    return jnp.concatenate([ref[cb, :, a].reshape(FFT_N2, LANES) for a in range(n1t)], axis=0)


MIX_HALO = 16


def _mix_body(xb_ref, x_ref, xa_ref, g_ref, wq_ref, wkv_ref, wh_ref, qn_ref, kn_ref, cos_ref, sin_ref,
              cw_ref, cb_ref, q_ref, k_ref, v_ref, hy_ref, *, npos):
    tm = x_ref.shape[0]
    x_ext = jnp.concatenate([xb_ref[...], x_ref[...], xa_ref[...]], axis=0)
    h_ext = _rms(x_ext, g_ref[...]).astype(BF16)
    h = h_ext[MIX_HALO:MIX_HALO + tm]
    lane = lax.broadcasted_iota(jnp.int32, (tm, HEAD_DIM), 1)
    low_half = (lane % (2 * (ROPE_HALF // 2))) < (ROPE_HALF // 2)
    c = cos_ref[...]
    s = sin_ref[...]

    def head(p, gain, scale):
        y = _rms(p, gain)
        partner = jnp.where(low_half, pltpu.roll(y, HEAD_DIM - ROPE_HALF // 2, 1),
                            pltpu.roll(y, ROPE_HALF // 2, 1))
        y = y * c + partner * s
        return y if scale is None else y * scale

    q = jnp.dot(h, wq_ref[...], preferred_element_type=F32)
    for hd in range(N_Q_HEADS):
        sl = slice(hd * HEAD_DIM, (hd + 1) * HEAD_DIM)
        q_ref[:, sl] = head(q[:, sl], qn_ref[...], HEAD_DIM ** -0.5 * LOG2_E).astype(BF16)
    kv = jnp.dot(h, wkv_ref[...], preferred_element_type=F32)
    for hd in range(N_KV_HEADS):
        sl = slice(hd * HEAD_DIM, (hd + 1) * HEAD_DIM)
        k_ref[:, sl] = head(kv[:, sl], kn_ref[...], None).astype(BF16)
    v_ref[...] = kv[:, D_KV:].astype(BF16)

    hy = jnp.dot(h_ext, wh_ref[...], preferred_element_type=F32)
    pos = pl.program_id(0) % npos
    rid = lax.broadcasted_iota(jnp.int32, (tm, 1), 0)
    before = jnp.where((rid == 0) & (pos == 0), 0.0, hy[MIX_HALO - 1:MIX_HALO - 1 + tm])
    after = jnp.where((rid == tm - 1) & (pos == npos - 1), 0.0, hy[MIX_HALO + 1:MIX_HALO + 1 + tm])
    hyc = (before * cw_ref[0:1, :] + hy[MIX_HALO:MIX_HALO + tm] * cw_ref[1:2, :]
           + after * cw_ref[2:3, :] + cb_ref[...])
    for cb in range(hy_ref.shape[0]):
        _store_time_grouped(hy_ref, cb, hyc[:, cb * LANES:(cb + 1) * LANES])


def _mix_in(x, g, wq, wkv, wh, qn, kn, cos_t, sin_t, cw, cb, *, L, tm=256):
    T, D = x.shape
    npos = L // tm
    ng2 = FFT_N2 // FFT_G
    hb = tm // MIX_HALO
    nhb = T // MIX_HALO
    return pl.pallas_call(
        functools.partial(_mix_body, npos=npos),
        grid=(T // tm,),
        in_specs=[
            pl.BlockSpec((MIX_HALO, D), lambda i: (jnp.maximum(i * hb - 1, 0), 0)),
            pl.BlockSpec((tm, D), lambda i: (i, 0)),
            pl.BlockSpec((MIX_HALO, D), lambda i: (jnp.minimum((i + 1) * hb, nhb - 1), 0)),
            _resident((1, D)),
            _resident(wq.shape),
            _resident(wkv.shape),
            _resident(wh.shape),
            _resident((1, HEAD_DIM)),
            _resident((1, HEAD_DIM)),
            pl.BlockSpec((tm, HEAD_DIM), lambda i: (i % npos, 0)),
            pl.BlockSpec((tm, HEAD_DIM), lambda i: (i % npos, 0)),
            _resident(cw.shape),
            _resident(cb.shape),
        ],
        out_specs=[
            pl.BlockSpec((tm, D_ATTN), lambda i: (i, 0)),
            pl.BlockSpec((tm, D_KV), lambda i: (i, 0)),
            pl.BlockSpec((tm, D_KV), lambda i: (i, 0)),
            pl.BlockSpec((3 * D_HYENA // LANES, None, ng2, tm // FFT_N2, FFT_G, LANES),
                         lambda i: (0, i // npos, 0, i % npos, 0, 0)),
        ],
        out_shape=[
            jax.ShapeDtypeStruct((T, D_ATTN), BF16),
            jax.ShapeDtypeStruct((T, D_KV), BF16),
            jax.ShapeDtypeStruct((T, D_KV), BF16),
            jax.ShapeDtypeStruct((3 * D_HYENA // LANES, T // L, ng2, L // FFT_N2, FFT_G, LANES), F32),
        ],
        compiler_params=_cparams(("parallel",), 48),
        name="mix_in",
    )(x, x, x, g, wq, wkv, wh, qn, kn, cos_t, sin_t, cw, cb)


ATTN_ROWS = 32
ATTN_RING = 4
ATTN_ONES = 16


def _attn_body(q_ref, k_ref, v_ref, o_ref, vt_ref, *bufs, tq, tk, nk):
    s_refs, mx_refs, p_refs = (bufs[k * ATTN_RING:(k + 1) * ATTN_RING] for k in range(3))

    @pl.when(pl.program_id(2) == 0)
    def _transpose_v():
        for c in range(nk):
            vt_ref[c, :HEAD_DIM] = v_ref[pl.ds(c * tk, tk), :].astype(F32).T.astype(BF16)
            vt_ref[c, HEAD_DIM:] = jnp.ones((ATTN_ONES, tk), BF16)

    q = jnp.concatenate([q_ref[:, h * HEAD_DIM:(h + 1) * HEAD_DIM] for h in range(Q_PER_KV)], axis=0)
    m_cols = Q_PER_KV * tq
    nchunk = tk // ATTN_ROWS

    def stage_a(i, s_ref, mx_ref):
        kc = k_ref[pl.ds(pl.multiple_of(i * tk, tk), tk), :]
        s = lax.dot_general(kc, q, (((1,), (1,)), ((), ())), preferred_element_type=F32)
        s_ref[...] = s
        mx_ref[...] = jnp.max(s, axis=0, keepdims=True)

    def stage_b(s_ref, mx_ref, p_ref, m):
        m_new = jnp.maximum(m, mx_ref[...])
        alpha = jnp.exp2(m - m_new)
        for c in range(nchunk):
            rows = slice(c * ATTN_ROWS, (c + 1) * ATTN_ROWS)
            p_ref[rows, :] = jnp.exp2((s_ref[rows, :] - m_new).astype(BF16))
        return m_new, alpha

    def stage_d(i, p_ref, alpha, acc):
        return alpha * acc + jnp.dot(vt_ref[i], p_ref[...], preferred_element_type=F32)

    def ring(i, m, al0, al1, acc, n_a, n_b):
        alphas = {0: al0, 1: al1}
        for u in range(ATTN_RING):
            if u < n_a:
                stage_a(i + u + 4, s_refs[u], mx_refs[u])
            if u < n_b:
                v = (u + 2) % ATTN_RING
                m, alphas[u + 2] = stage_b(s_refs[v], mx_refs[v], p_refs[v], m)
            acc = stage_d(i + u, p_refs[u], alphas[u], acc)
        return m, alphas.get(4), alphas.get(5), acc

    m = jnp.full((1, m_cols), -jnp.inf, F32)
    acc = jnp.zeros((HEAD_DIM + ATTN_ONES, m_cols), F32)
    for u in range(ATTN_RING):
        stage_a(u, s_refs[u], mx_refs[u])
    m, al0 = stage_b(s_refs[0], mx_refs[0], p_refs[0], m)
    m, al1 = stage_b(s_refs[1], mx_refs[1], p_refs[1], m)

    def trip(j, carry):
        return ring(ATTN_RING * j, *carry, ATTN_RING, ATTN_RING)

    m, al0, al1, acc = lax.fori_loop(0, nk // ATTN_RING - 1, trip, (m, al0, al1, acc))
    _, _, _, acc = ring(nk - ATTN_RING, m, al0, al1, acc, 0, 2)
    o = acc[:HEAD_DIM] / acc[HEAD_DIM:HEAD_DIM + 1]
    for h in range(Q_PER_KV):
        o_ref[:, h * HEAD_DIM:(h + 1) * HEAD_DIM] = o[:, h * tq:(h + 1) * tq].T


def _attention(q, k, v, *, tq=512, tk=512):
    B, L, _ = q.shape
    qw = Q_PER_KV * HEAD_DIM
    nk = L // tk
    assert nk % ATTN_RING == 0 and nk >= 2 * ATTN_RING
    body = functools.partial(_attn_body, tq=tq, tk=tk, nk=nk)
    m_cols = Q_PER_KV * tq
    return pl.pallas_call(
        body,
        grid=(B, N_KV_HEADS, L // tq),
        in_specs=[
            pl.BlockSpec((None, tq, qw), lambda b, g, i: (b, i, g)),
            pl.BlockSpec((None, L, HEAD_DIM), lambda b, g, i: (b, 0, g)),
            pl.BlockSpec((None, L, HEAD_DIM), lambda b, g, i: (b, 0, g)),
        ],
        out_specs=pl.BlockSpec((None, tq, qw), lambda b, g, i: (b, i, g)),
        out_shape=jax.ShapeDtypeStruct((B, L, D_ATTN), F32),
        scratch_shapes=([pltpu.VMEM((nk, HEAD_DIM + ATTN_ONES, tk), BF16)]
                        + [pltpu.VMEM((tk, m_cols), F32)] * ATTN_RING
                        + [pltpu.VMEM((1, m_cols), F32)] * ATTN_RING
                        + [pltpu.VMEM((tk, m_cols), BF16)] * ATTN_RING),
        compiler_params=_cparams(("parallel", "parallel", "arbitrary"), 48),
        name="attn",
    )(q, k, v)


def _filt_body(feat_ref, w1_ref, b1_ref, w2_ref, b2_ref, w3_ref, fq_ref, dec_ref, hf_ref, asum_ref, *, tl):
    i = pl.program_id(0)
    hp = lax.Precision.HIGHEST
    fq = fq_ref[...]
    feat = feat_ref[...]
    h = jnp.sin(fq * (jnp.dot(feat, w1_ref[...], precision=hp, preferred_element_type=F32) + b1_ref[...]))
    h = jnp.sin(fq * (jnp.dot(h, w2_ref[...], precision=hp, preferred_element_type=F32) + b2_ref[...]))
    h3 = jnp.dot(h, w3_ref[...], precision=hp, preferred_element_type=F32)
    t01 = feat[:, 0:1]
    hf = h3 * jnp.exp(-t01 * jnp.abs(dec_ref[...]))
    ncol = hf.shape[1]
    row = lax.broadcasted_iota(jnp.int32, hf.shape, 0) + i * tl
    col = lax.broadcasted_iota(jnp.int32, hf.shape, 1)
    hf = jnp.where((row == 0) & (col >= ncol // 2), 0.0, hf)
    for cb in range(ncol // LANES):
        _store_time_grouped(hf_ref, cb, hf[:, cb * LANES:(cb + 1) * LANES])
    part = jnp.sum(jnp.abs(hf), axis=0, keepdims=True)

    @pl.when(i == 0)
    def _first():
        asum_ref[...] = part

    @pl.when(i > 0)
    def _rest():
        asum_ref[...] += part


def _filters(feats, w1, b1, w2, b2, w3, fq, dec, *, tl=256):
    L = feats.shape[0]
    nc = w3.shape[1]
    ng2 = FFT_N2 // FFT_G
    body = functools.partial(_filt_body, tl=tl)
    return pl.pallas_call(
        body,
        grid=(L // tl,),
        in_specs=[
            pl.BlockSpec((tl, FILT_PAD), lambda i: (i, 0)),
            _resident(w1.shape), _resident(b1.shape), _resident(w2.shape), _resident(b2.shape),
            _resident(w3.shape), _resident(fq.shape), _resident(dec.shape),
        ],
        out_specs=[
            pl.BlockSpec((nc // LANES, ng2, tl // FFT_N2, FFT_G, LANES), lambda i: (0, 0, i, 0, 0)),
            pl.BlockSpec((1, nc), lambda i: (0, 0)),
        ],
        out_shape=[jax.ShapeDtypeStruct((nc // LANES, ng2, L // FFT_N2, FFT_G, LANES), F32),
                   jax.ShapeDtypeStruct((1, nc), F32)],
        compiler_params=_cparams(("arbitrary",), 48),
        name="filt",
    )(feats, w1, b1, w2, b2, w3, fq, dec)


FFT_CB = FFT_W // LANES
FFT_CB_B = 2


def _pack_c(y):
    n = y.shape[0] // 2
    r = lax.bitcast_convert_type(y[:n], U32)
    i = lax.bitcast_convert_type(y[n:], U32)
    half = jnp.uint32(0x8000)
    return ((r + half) & jnp.uint32(0xFFFF0000)) | ((i + half) >> 16)


def _unpack_c(w):
    re = lax.bitcast_convert_type(w & jnp.uint32(0xFFFF0000), F32)
    im = lax.bitcast_convert_type(w << 16, F32)
    return re, im


def _unpack_rows(w):
    re, im = _unpack_c(w)
    return jnp.concatenate([re, im], axis=0).astype(BF16)


def _flat_scratch(n, dtype, cbs, lead=()):
    return pltpu.VMEM(lead + (cbs, n * FFT_G, LANES), dtype)


def _flatten(flat_ref, block_ref, pre=()):
    rows = flat_ref.shape[-2]
    for h in range(block_ref.shape[0]):
        flat_ref[pre + (h,)] = block_ref[(h,) + pre].reshape(rows, LANES)


def _unflatten(block_ref, flat_ref, pre=()):
    n = flat_ref.shape[-2] // FFT_G
    for h in range(block_ref.shape[0]):
        block_ref[(h,) + pre] = flat_ref[pre + (h,)].reshape(n, FFT_G, LANES)


def _gather_rows(flat_ref, j, pre=()):
    n = flat_ref.shape[-2] // FFT_G
    return jnp.concatenate(
        [flat_ref[pre + (h, pl.ds(j, n, stride=FFT_G), slice(None))] for h in range(flat_ref.shape[len(pre)])],
        axis=1)


def _scatter_rows(flat_ref, j, val, pre=()):
    n = flat_ref.shape[-2] // FFT_G
    for h in range(flat_ref.shape[len(pre)]):
        flat_ref[pre + (h, pl.ds(j, n, stride=FFT_G), slice(None))] = val[:, h * LANES:(h + 1) * LANES]


def _store_blocks(ref, j, val):
    for h in range(ref.shape[0]):
        ref[h, j] = val[:, h * LANES:(h + 1) * LANES]


def _store_grouped(ref, j, val):
    groups = ref.shape[1]
    for h in range(ref.shape[0]):
        ref[h, :, j] = val[:, h * LANES:(h + 1) * LANES].reshape(groups, FFT_G, LANES)


def _fft_a_body(x_ref, fa_ref, o_ref, xf_ref):
    for b in range(2):
        _flatten(xf_ref, x_ref, (b,))
    for j in range(FFT_G):
        xs = jnp.concatenate([_gather_rows(xf_ref, j, (b,)) for b in range(2)], axis=0).astype(BF16)
        _store_grouped(o_ref, j, _pack_c(jnp.dot(fa_ref[j], xs, preferred_element_type=F32)))


def _fft_a(xv, fa, *, ncb, cb0=0):
    _, P, _, ng2, n1h, _, _ = xv.shape
    n2 = ng2 * FFT_G
    two_n1 = fa.shape[1]
    n1 = two_n1 // 2
    c0 = cb0 // FFT_CB
    return pl.pallas_call(
        _fft_a_body,
        grid=(ng2, P, ncb // FFT_CB),
        in_specs=[
            pl.BlockSpec((FFT_CB, None, 2, None, n1h, FFT_G, LANES), lambda g, p, c: (c + c0, p, 0, g, 0, 0, 0)),
            pl.BlockSpec((FFT_G, two_n1, 2 * n1h), lambda g, p, c: (g, 0, 0)),
        ],
        out_specs=pl.BlockSpec((None, FFT_CB, n1 // FFT_G, FFT_G, FFT_G, LANES),
                               lambda g, p, c: (p, c, 0, g, 0, 0)),
        out_shape=jax.ShapeDtypeStruct((P, ncb, n1 // FFT_G, n2, FFT_G, LANES), U32),
        scratch_shapes=[_flat_scratch(n1h, F32, FFT_CB, (2,))],
        compiler_params=_cparams(("arbitrary", "arbitrary", "arbitrary"), 32),
        name="fft_a",
    )(xv, fa)


def _fft_a_real_body(x_ref, fa_ref, o_ref, xf_ref):
    _flatten(xf_ref, x_ref)
    for j in range(FFT_G):
        xs = _gather_rows(xf_ref, j).astype(BF16)
        _store_grouped(o_ref, j, _pack_c(jnp.dot(fa_ref[j], xs, preferred_element_type=F32)))


def _fft_a_real(xv, fa_re):
    ncb, ng2, n1h, _, _ = xv.shape
    n2 = ng2 * FFT_G
    two_n1 = fa_re.shape[1]
    n1 = two_n1 // 2
    return pl.pallas_call(
        _fft_a_real_body,
        grid=(ng2, ncb // FFT_CB),
        in_specs=[
            pl.BlockSpec((FFT_CB, None, n1h, FFT_G, LANES), lambda g, c: (c, g, 0, 0, 0)),
            pl.BlockSpec((FFT_G, two_n1, n1h), lambda g, c: (g, 0, 0)),
        ],
        out_specs=pl.BlockSpec((FFT_CB, n1 // FFT_G, FFT_G, FFT_G, LANES), lambda g, c: (c, 0, g, 0, 0)),
        out_shape=jax.ShapeDtypeStruct((ncb, n1 // FFT_G, n2, FFT_G, LANES), U32),
        scratch_shapes=[_flat_scratch(n1h, F32, FFT_CB)],
        compiler_params=_cparams(("arbitrary", "arbitrary"), 32),
        name="fft_a_real",
    )(xv, fa_re)


def _fft_b_filt_body(f_ref, b_ref, fb_ref, asf_ref, asb_ref, o_ref, ff_ref, bf_ref):
    n2 = FFT_N2
    _flatten(ff_ref, f_ref)
    _flatten(bf_ref, b_ref)
    inv = 1.0 / (asf_ref[...] + asb_ref[...])
    fb = fb_ref[...]
    for j in range(FFT_G):
        u = jnp.dot(fb, _unpack_rows(_gather_rows(ff_ref, j)), preferred_element_type=F32)
        w = jnp.dot(fb, _unpack_rows(_gather_rows(bf_ref, j)), preferred_element_type=F32)
        kr = (u[:n2] + w[:n2]) * inv
        ki = (u[n2:] - w[n2:]) * inv
        _store_blocks(o_ref, j, _pack_c(jnp.concatenate([kr, ki], axis=0)))


def _fft_b_filt(s1f, fb, asum):
    cb2, ng1, n2, _, _ = s1f.shape
    n1 = ng1 * FFT_G
    ncb = cb2 // 2
    nb = ncb // FFT_CB
    blk = (FFT_CB, None, n2, FFT_G, LANES)
    return pl.pallas_call(
        _fft_b_filt_body,
        grid=(ng1, nb),
        in_specs=[
            pl.BlockSpec(blk, lambda g, c: (c, g, 0, 0, 0)),
            pl.BlockSpec(blk, lambda g, c: (c + nb, g, 0, 0, 0)),
            _resident(fb.shape),
            pl.BlockSpec((1, FFT_W), lambda g, c: (0, c)),
            pl.BlockSpec((1, FFT_W), lambda g, c: (0, c + nb)),
        ],
        out_specs=pl.BlockSpec((FFT_CB, FFT_G, n2, LANES), lambda g, c: (c, g, 0, 0)),
        out_shape=jax.ShapeDtypeStruct((ncb, n1, n2, LANES), U32),
        scratch_shapes=[_flat_scratch(n2, U32, FFT_CB), _flat_scratch(n2, U32, FFT_CB)],
        compiler_params=_cparams(("parallel", "parallel"), 32),
        name="fft_b_filt",
    )(s1f, s1f, fb, asum, asum)


def _fft_b_body(s_ref, kf_ref, fb_ref, fbi_ref, o_ref, sf_ref):
    n2 = FFT_N2
    _flatten(sf_ref, s_ref)
    fb = fb_ref[...]
    fbi = fbi_ref[...]
    for j in range(FFT_G):
        x = jnp.dot(fb, _unpack_rows(_gather_rows(sf_ref, j)), preferred_element_type=F32)
        xr, xi = x[:n2], x[n2:]
        kr, ki = _unpack_c(jnp.concatenate([kf_ref[h, j] for h in range(kf_ref.shape[0])], axis=1))
        pm = jnp.concatenate([xr * kr - xi * ki, xr * ki + xi * kr], axis=0).astype(BF16)
        _store_grouped(o_ref, j, _pack_c(jnp.dot(fbi, pm, preferred_element_type=F32)))


def _fft_b(s1, kf, fb, fbi, *, kcb0):
    P, ncb, ng1, n2, _, _ = s1.shape
    n1 = ng1 * FFT_G
    cbs = FFT_CB_B
    k0 = kcb0 // cbs
    return pl.pallas_call(
        _fft_b_body,
        grid=(ng1, ncb // cbs, P),
        in_specs=[
            pl.BlockSpec((None, cbs, None, n2, FFT_G, LANES), lambda g, c, p: (p, c, g, 0, 0, 0)),
            pl.BlockSpec((cbs, FFT_G, n2, LANES), lambda g, c, p: (c + k0, g, 0, 0)),
            _resident(fb.shape),
            _resident(fbi.shape),
        ],
        out_specs=pl.BlockSpec((None, cbs, n2 // FFT_G, FFT_G, FFT_G, LANES),
                               lambda g, c, p: (p, c, 0, g, 0, 0)),
        out_shape=jax.ShapeDtypeStruct((P, ncb, n2 // FFT_G, n1, FFT_G, LANES), U32),
        scratch_shapes=[_flat_scratch(n2, U32, cbs)],
        compiler_params=_cparams(("parallel", "parallel", "arbitrary"), 32),
        name="fft_b",
    )(s1, kf, fb, fbi)


def _fft_c_body(z_ref, fc_ref, u_ref, gate_ref, d_ref, *rest, chain):
    if chain:
        fa_ref, o_ref, s_ref, zf_ref, uf_ref, gf_ref, of_ref = rest
    else:
        o_ref, zf_ref, uf_ref, gf_ref, of_ref = rest
    n1h = o_ref.shape[2]
    d = d_ref[...]
    _flatten(zf_ref, z_ref)
    for b in range(2):
        _flatten(uf_ref, u_ref, (b,))
        _flatten(gf_ref, gate_ref, (b,))
    for j in range(FFT_G):
        y = jnp.dot(fc_ref[j], _unpack_rows(_gather_rows(zf_ref, j)), preferred_element_type=F32)
        res = []
        for b in range(2):
            yb = y[b * n1h:(b + 1) * n1h]
            res.append(_gather_rows(gf_ref, j, (b,)) * (yb + _gather_rows(uf_ref, j, (b,)) * d))
            _scatter_rows(of_ref, j, res[b], (b,))
        if chain:
            xs = jnp.concatenate(res, axis=0).astype(BF16)
            _store_grouped(s_ref, j, _pack_c(jnp.dot(fa_ref[j], xs, preferred_element_type=F32)))
    for b in range(2):
        _unflatten(o_ref, of_ref, (b,))


def _fft_c(s2, fc, uv, gv, d, *, ucb0, gcb0, fa=None):
    P, ncb, ng2, n1, _, _ = s2.shape
    n1h = n1 // 2
    n2 = ng2 * FFT_G
    u0 = ucb0 // FFT_CB
    g0 = gcb0 // FFT_CB
    chain = fa is not None
    tblk = (FFT_CB, None, 2, None, n1h, FFT_G, LANES)
    in_specs = [
        pl.BlockSpec((None, FFT_CB, None, n1, FFT_G, LANES), lambda g, p, c: (p, c, g, 0, 0, 0)),
        pl.BlockSpec((FFT_G, 2 * n1h, 2 * n1), lambda g, p, c: (g, 0, 0)),
        pl.BlockSpec(tblk, lambda g, p, c: (c + u0, p, 0, g, 0, 0, 0)),
        pl.BlockSpec(tblk, lambda g, p, c: (c + g0, p, 0, g, 0, 0, 0)),
        pl.BlockSpec((1, FFT_W), lambda g, p, c: (0, c)),
    ]
    out_specs = [pl.BlockSpec(tblk, lambda g, p, c: (c, p, 0, g, 0, 0, 0))]
    out_shape = [jax.ShapeDtypeStruct((ncb, P, 2, ng2, n1h, FFT_G, LANES), F32)]
    args = [s2, fc, uv, gv, d]
    if chain:
        in_specs.append(pl.BlockSpec((FFT_G, 2 * n1, 2 * n1h), lambda g, p, c: (g, 0, 0)))
        out_specs.append(pl.BlockSpec((None, FFT_CB, n1 // FFT_G, FFT_G, FFT_G, LANES),
                                      lambda g, p, c: (p, c, 0, g, 0, 0)))
        out_shape.append(jax.ShapeDtypeStruct((P, ncb, n1 // FFT_G, n2, FFT_G, LANES), U32))
        args.append(fa)
    return pl.pallas_call(
        functools.partial(_fft_c_body, chain=chain),
        grid=(ng2, P, ncb // FFT_CB),
        in_specs=in_specs,
        out_specs=out_specs,
        out_shape=out_shape,
        scratch_shapes=[_flat_scratch(n1, U32, FFT_CB), _flat_scratch(n1h, F32, FFT_CB, (2,)),
                        _flat_scratch(n1h, F32, FFT_CB, (2,)), _flat_scratch(n1h, F32, FFT_CB, (2,))],
        compiler_params=_cparams(("arbitrary", "arbitrary", "arbitrary"), 40),
        name="fft_c",
    )(*args)


def _oproj_body(x_ref, a_ref, h_ref, ga_ref, gh_ref, wa_ref, wh_ref, o_ref):
    ma = _rms(a_ref[...], ga_ref[...]).astype(BF16)
    hy = jnp.concatenate([_load_time_grouped(h_ref, cb) for cb in range(h_ref.shape[0])], axis=1)
    mh = _rms(hy, gh_ref[...]).astype(BF16)
    o_ref[...] = (x_ref[...] + jnp.dot(ma, wa_ref[...], preferred_element_type=F32)
                  + jnp.dot(mh, wh_ref[...], preferred_element_type=F32))


def _out_proj(x, a, h, ga, gh, wa, wh, *, tm=512):
    T, D = x.shape
    ncb, _, ng2, n1h, _, _ = h.shape
    npos = n1h * FFT_N2 // tm
    return pl.pallas_call(
        _oproj_body,
        grid=(T // tm,),
        in_specs=[
            pl.BlockSpec((tm, D), lambda i: (i, 0)),
            pl.BlockSpec((tm, D_ATTN), lambda i: (i, 0)),
            pl.BlockSpec((ncb, None, ng2, tm // FFT_N2, FFT_G, LANES), lambda i: (0, i // npos, 0, i % npos, 0, 0)),
            _resident((1, D_ATTN)), _resident((1, D_HYENA)),
            _resident(wa.shape), _resident(wh.shape),
        ],
        out_specs=pl.BlockSpec((tm, D), lambda i: (i, 0)),
        out_shape=jax.ShapeDtypeStruct((T, D), F32),
        compiler_params=_cparams(("parallel",), 48),
        name="oproj",
    )(x, a, h, ga, gh, wa, wh)


def _rope_tables(L):
    rows = L // GRID_W
    row = jnp.repeat(jnp.arange(rows, dtype=F32), GRID_W)
    col = jnp.tile(jnp.arange(GRID_W, dtype=F32), rows)
    inv = ROPE_THETA ** (-jnp.arange(0, ROPE_HALF, 2, dtype=F32) / ROPE_HALF)
    ar = row[:, None] * inv[None]
    ac = col[:, None] * inv[None]
    cr, sr, cc, sc = jnp.cos(ar), jnp.sin(ar), jnp.cos(ac), jnp.sin(ac)
    cos_t = jnp.concatenate([cr, cr, cc, cc], axis=-1)
    sin_t = jnp.concatenate([-sr, sr, -sc, sc], axis=-1)
    return cos_t, sin_t


def _filter_feats(L):
    t01 = jnp.linspace(0.0, 1.0, L, dtype=F32)[:, None]
    bands = (FILTER_EMB - 1) // 2
    fr = jnp.linspace(1e-4, bands - 1, bands, dtype=F32)[None]
    w = 2.0 * math.pi * jnp.arange(L, dtype=F32)[:, None] / L
    feats = jnp.concatenate([t01, jnp.cos(fr * w), -jnp.sin(fr * w)], axis=-1)
    return jnp.pad(feats, ((0, 0), (0, FILT_PAD - FILTER_EMB)))


def _dft_tables(L):
    n = 2 * L
    n2 = FFT_N2
    n1 = n // n2
    n1h = n1 // 2
    k1 = jnp.arange(n1, dtype=jnp.int32)[None, :, None]
    t1 = jnp.arange(n1h, dtype=jnp.int32)[None, None, :]
    t2 = jnp.arange(n2, dtype=jnp.int32)[:, None, None]
    ang = (-2.0 * math.pi / n) * ((k1 * (n2 * t1 + t2)) % n).astype(F32)
    cr, ci = jnp.cos(ang), jnp.sin(ang)
    fa = jnp.concatenate([jnp.concatenate([cr, -ci], -1), jnp.concatenate([ci, cr], -1)], -2)
    fa_re = jnp.concatenate([cr, ci], -2)
    crt, cit = jnp.swapaxes(cr, 1, 2), jnp.swapaxes(ci, 1, 2)
    fc = jnp.concatenate([jnp.concatenate([crt, cit], -1), jnp.concatenate([-cit, crt], -1)], -2) / n
    a = jnp.arange(n2, dtype=jnp.int32)
    ang2 = (-2.0 * math.pi / n2) * ((a[:, None] * a[None, :]) % n2).astype(F32)
    fr, fi = jnp.cos(ang2), jnp.sin(ang2)
    fb = jnp.concatenate([jnp.concatenate([fr, -fi], -1), jnp.concatenate([fi, fr], -1)], -2)
    fbi = jnp.concatenate([jnp.concatenate([fr, fi], -1), jnp.concatenate([-fi, fr], -1)], -2)
    return (fa.astype(BF16), fa_re.astype(BF16), fb.astype(BF16), fbi.astype(BF16), fc.astype(BF16))


def _trunk(x, p):
    B, L, D = x.shape
    T = B * L
    P = B // 2
    n2 = FFT_N2
    n1h = L // n2
    C = D_HYENA
    ncb = C // LANES

    x0 = x.reshape(T, D)
    x1 = _ffn(x0, p["ffn1_norm"], p["ffn1_w13"], p["ffn1_w2"], p["final_norm"], final_norm=False)

    cos_t, sin_t = _rope_tables(L)
    q, k, v, hyc = _mix_in(x1, p["mix_norm"], p["wq"], p["wkv"], p["wh"], p["q_norm"], p["k_norm"],
                           cos_t, sin_t, p["conv_w"], p["conv_b"], L=L)
    attn = _attention(q.reshape(B, L, D_ATTN), k.reshape(B, L, D_KV), v.reshape(B, L, D_KV))
    ng2 = n2 // FFT_G
    hyv = hyc.reshape(3 * ncb, P, 2, ng2, n1h, FFT_G, LANES)

    fa, fa_re, fb, fbi, fc = _dft_tables(L)
    hf, asum = _filters(_filter_feats(L), p["filt_w1"], p["filt_b1"], p["filt_w2"], p["filt_b2"],
                        p["filt_w3"], p["filt_freq"], p["decay"])
    s1f = _fft_a_real(hf, fa_re)
    kf = _fft_b_filt(s1f, fb, asum)

    s1 = _fft_a(hyv, fa, ncb=ncb, cb0=0)
    s2 = _fft_b(s1, kf, fb, fbi, kcb0=0)
    z, s1 = _fft_c(s2, fc, hyv, hyv, p["bias0"], ucb0=0, gcb0=ncb, fa=fa)
    s2 = _fft_b(s1, kf, fb, fbi, kcb0=ncb)
    ho, = _fft_c(s2, fc, z, hyv, p["bias1"], ucb0=0, gcb0=2 * ncb)

    x2 = _out_proj(x1, attn.reshape(T, D_ATTN), ho.reshape(ncb, B, ng2, n1h, FFT_G, LANES),
                   p["gon_a"], p["gon_h"], p["wo_a"], p["wo_h"])
    x3 = _ffn(x2, p["ffn2_norm"], p["ffn2_w13"], p["ffn2_w2"], p["final_norm"], final_norm=True)
    return x3.reshape(B, L, D)


def kernel(x_prompt, x_sample, ffn1_norm, ffn1_w13, ffn1_w2, mix_norm, w_in, q_norm, k_norm, conv_w, conv_b, filt_w1, filt_b1, filt_w2, filt_b2, filt_w3, filt_freq, hyena_decay, hyena_bias, group_out_norm, w_out, ffn2_norm, ffn2_w13, ffn2_w2, final_norm):
    hp = FILT_PAD - FILTER_HIDDEN
    w_in0 = w_in[0]
    p = {
        "ffn1_norm": ffn1_norm[0][None], "ffn2_norm": ffn2_norm[0][None],
        "ffn1_w13": ffn1_w13[0].astype(BF16), "ffn1_w2": ffn1_w2[0].astype(BF16),
        "ffn2_w13": ffn2_w13[0].astype(BF16), "ffn2_w2": ffn2_w2[0].astype(BF16),
        "final_norm": final_norm[None],
        "mix_norm": mix_norm[0][None],
        "wq": w_in0[:, :D_ATTN].astype(BF16),
        "wkv": w_in0[:, D_ATTN:D_ATTN + 2 * D_KV].astype(BF16),
        "wh": w_in0[:, D_ATTN + 2 * D_KV:].astype(BF16),
        "q_norm": q_norm[0][None], "k_norm": k_norm[0][None],
        "conv_w": conv_w[0], "conv_b": conv_b[0][None],
        "filt_w1": jnp.pad(filt_w1[0], ((0, FILT_PAD - FILTER_EMB), (0, hp))),
        "filt_b1": jnp.pad(filt_b1[0], (0, hp))[None],
        "filt_w2": jnp.pad(filt_w2[0], ((0, hp), (0, hp))),
        "filt_b2": jnp.pad(filt_b2[0], (0, hp))[None],
        "filt_w3": jnp.pad(filt_w3[0], ((0, hp), (0, 0))),
        "filt_freq": jnp.pad(filt_freq[0], (0, hp))[None],
        "decay": hyena_decay[0].reshape(1, 4 * D_HYENA),
        "bias0": hyena_bias[0, 0][None], "bias1": hyena_bias[0, 1][None],
        "gon_a": group_out_norm[0, :D_ATTN][None], "gon_h": group_out_norm[0, D_ATTN:][None],
        "wo_a": w_out[0, :D_ATTN].astype(BF16), "wo_h": w_out[0, D_ATTN:].astype(BF16),
    }
    return (_trunk(x_prompt, p), _trunk(x_sample, p))
```

```python
import functools
import math

import jax
import jax.numpy as jnp
from jax import lax
from jax.experimental import pallas as pl
from jax.experimental.pallas import tpu as pltpu

F32 = jnp.float32
BF16 = jnp.bfloat16
U32 = jnp.uint32

D_MODEL = 2048
GRID_W = 64
D_ATTN = D_MODEL // 2
D_HYENA = D_MODEL - D_ATTN
HEAD_DIM = 128
N_Q_HEADS = D_ATTN // HEAD_DIM
N_KV_HEADS = 2
Q_PER_KV = N_Q_HEADS // N_KV_HEADS
ROPE_HALF = HEAD_DIM // 2
ROPE_THETA = 10000.0
FILTER_EMB = 33
FILTER_HIDDEN = 64
D_FF = 5632
EPS = 1e-6
D_KV = N_KV_HEADS * HEAD_DIM

LANES = 128
FILT_PAD = 128
FFT_N2 = 128
FFT_W = 512
FFT_G = 8
MIB = 1024 * 1024
LOG2_E = math.log2(math.e)


def _cparams(semantics, vmem_mib):
    return pltpu.CompilerParams(dimension_semantics=semantics,
                                vmem_limit_bytes=vmem_mib * MIB)


def _rms(x, g):
    ms = jnp.mean(x * x, axis=-1, keepdims=True)
    return x * lax.rsqrt(ms + EPS) * g


def _resident(shape):
    nd = len(shape)
    return pl.BlockSpec(shape, lambda *_: (0,) * nd, pipeline_mode=pl.Buffered(1))


def _ffn_body(x_ref, g_ref, wg_ref, wu_ref, w2_ref, gf_ref, o_ref, xn_ref, *, nj, final_norm):
    j = pl.program_id(1)

    @pl.when(j == 0)
    def _init():
        x = x_ref[...]
        xn_ref[...] = _rms(x, g_ref[...]).astype(BF16)
        o_ref[...] = x

    xn = xn_ref[...]
    hg = jnp.dot(xn, wg_ref[...], preferred_element_type=F32)
    hu = jnp.dot(xn, wu_ref[...], preferred_element_type=F32)
    a = (hg * (0.5 / (1.0 + jnp.exp(-hg))) * hu).astype(BF16)
    o_ref[...] += jnp.dot(a, w2_ref[...], preferred_element_type=F32)

    if final_norm:
        @pl.when(j == nj - 1)
        def _fin():
            o_ref[...] = _rms(o_ref[...], gf_ref[...])


def _ffn(x, g, w13, w2, gf, *, final_norm, tm=1024, tf=512):
    T, D = x.shape
    ff = w2.shape[0]
    nj = ff // tf
    body = functools.partial(_ffn_body, nj=nj, final_norm=final_norm)
    return pl.pallas_call(
        body,
        grid=(T // tm, nj),
        in_specs=[
            pl.BlockSpec((tm, D), lambda i, j: (i, 0)),
            pl.BlockSpec((1, D), lambda i, j: (0, 0)),
            pl.BlockSpec((D, tf), lambda i, j: (0, j)),
            pl.BlockSpec((D, tf), lambda i, j: (0, j + nj)),
            pl.BlockSpec((tf, D), lambda i, j: (j, 0)),
            pl.BlockSpec((1, D), lambda i, j: (0, 0)),
        ],
        out_specs=pl.BlockSpec((tm, D), lambda i, j: (i, 0)),
        out_shape=jax.ShapeDtypeStruct((T, D), F32),
        scratch_shapes=[pltpu.VMEM((tm, D), BF16)],
        compiler_params=_cparams(("parallel", "arbitrary"), 58),
        name="ffn",
    )(x, g, w13, w13, w2, gf)


def _store_time_grouped(ref, cb, val):
    n1t = ref.shape[2]
    v = val.reshape(n1t * (FFT_N2 // FFT_G), FFT_G, LANES)
    for a in range(n1t):
        ref[cb, :, a] = v[a * (FFT_N2 // FFT_G):(a + 1) * (FFT_N2 // FFT_G)]


def _load_time_grouped(ref, cb):
    n1t = ref.shape[2]
    return jnp.concatenate([ref[cb, :, a].reshape(FFT_N2, LANES) for a in range(n1t)], axis=0)


MIX_HALO = 16


def _mix_body(xb_ref, x_ref, xa_ref, g_ref, wq_ref, wkv_ref, wh_ref, qn_ref, kn_ref, cos_ref, sin_ref,
              cw_ref, cb_ref, q_ref, k_ref, v_ref, hy_ref, *, npos):
    tm = x_ref.shape[0]
    x_ext = jnp.concatenate([xb_ref[...], x_ref[...], xa_ref[...]], axis=0)
    h_ext = _rms(x_ext, g_ref[...]).astype(BF16)
    h = h_ext[MIX_HALO:MIX_HALO + tm]
    lane = lax.broadcasted_iota(jnp.int32, (tm, HEAD_DIM), 1)
    low_half = (lane % (2 * (ROPE_HALF // 2))) < (ROPE_HALF // 2)
    c = cos_ref[...]
    s = sin_ref[...]

    def head(p, gain, scale):
        y = _rms(p, gain)
        partner = jnp.where(low_half, pltpu.roll(y, HEAD_DIM - ROPE_HALF // 2, 1),
                            pltpu.roll(y, ROPE_HALF // 2, 1))
        y = y * c + partner * s
        return y if scale is None else y * scale

    q = jnp.dot(h, wq_ref[...], preferred_element_type=F32)
    for hd in range(N_Q_HEADS):
        sl = slice(hd * HEAD_DIM, (hd + 1) * HEAD_DIM)
        q_ref[:, sl] = head(q[:, sl], qn_ref[...], HEAD_DIM ** -0.5 * LOG2_E).astype(BF16)
    kv = jnp.dot(h, wkv_ref[...], preferred_element_type=F32)
    for hd in range(N_KV_HEADS):
        sl = slice(hd * HEAD_DIM, (hd + 1) * HEAD_DIM)
        k_ref[:, sl] = head(kv[:, sl], kn_ref[...], None).astype(BF16)
    v_ref[...] = kv[:, D_KV:].astype(BF16)

    hy = jnp.dot(h_ext, wh_ref[...], preferred_element_type=F32)
    pos = pl.program_id(0) % npos
    rid = lax.broadcasted_iota(jnp.int32, (tm, 1), 0)
    before = jnp.where((rid == 0) & (pos == 0), 0.0, hy[MIX_HALO - 1:MIX_HALO - 1 + tm])
    after = jnp.where((rid == tm - 1) & (pos == npos - 1), 0.0, hy[MIX_HALO + 1:MIX_HALO + 1 + tm])
    hyc = (before * cw_ref[0:1, :] + hy[MIX_HALO:MIX_HALO + tm] * cw_ref[1:2, :]
           + after * cw_ref[2:3, :] + cb_ref[...])
    for cb in range(hy_ref.shape[0]):
        _store_time_grouped(hy_ref, cb, hyc[:, cb * LANES:(cb + 1) * LANES])


def _mix_in(x, g, wq, wkv, wh, qn, kn, cos_t, sin_t, cw, cb, *, L, tm=256):
    T, D = x.shape
    npos = L // tm
    ng2 = FFT_N2 // FFT_G
    hb = tm // MIX_HALO
    nhb = T // MIX_HALO
    return pl.pallas_call(
        functools.partial(_mix_body, npos=npos),
        grid=(T // tm,),
        in_specs=[
            pl.BlockSpec((MIX_HALO, D), lambda i: (jnp.maximum(i * hb - 1, 0), 0)),
            pl.BlockSpec((tm, D), lambda i: (i, 0)),
            pl.BlockSpec((MIX_HALO, D), lambda i: (jnp.minimum((i + 1) * hb, nhb - 1), 0)),
            _resident((1, D)),
            _resident(wq.shape),
            _resident(wkv.shape),
            _resident(wh.shape),
            _resident((1, HEAD_DIM)),
            _resident((1, HEAD_DIM)),
            pl.BlockSpec((tm, HEAD_DIM), lambda i: (i % npos, 0)),
            pl.BlockSpec((tm, HEAD_DIM), lambda i: (i % npos, 0)),
            _resident(cw.shape),
            _resident(cb.shape),
        ],
        out_specs=[
            pl.BlockSpec((tm, D_ATTN), lambda i: (i, 0)),
            pl.BlockSpec((tm, D_KV), lambda i: (i, 0)),
            pl.BlockSpec((tm, D_KV), lambda i: (i, 0)),
            pl.BlockSpec((3 * D_HYENA // LANES, None, ng2, tm // FFT_N2, FFT_G, LANES),
                         lambda i: (0, i // npos, 0, i % npos, 0, 0)),
        ],
        out_shape=[
            jax.ShapeDtypeStruct((T, D_ATTN), BF16),
            jax.ShapeDtypeStruct((T, D_KV), BF16),
            jax.ShapeDtypeStruct((T, D_KV), BF16),
            jax.ShapeDtypeStruct((3 * D_HYENA // LANES, T // L, ng2, L // FFT_N2, FFT_G, LANES), F32),
        ],
        compiler_params=_cparams(("parallel",), 48),
        name="mix_in",
    )(x, x, x, g, wq, wkv, wh, qn, kn, cos_t, sin_t, cw, cb)


ATTN_ROWS = 32
ATTN_RING = 4
ATTN_ONES = 16


def _attn_body(q_ref, k_ref, v_ref, o_ref, vt_ref, *bufs, tq, tk, nk):
    s_refs, mx_refs, p_refs = (bufs[k * ATTN_RING:(k + 1) * ATTN_RING] for k in range(3))

    @pl.when(pl.program_id(2) == 0)
    def _transpose_v():
        for c in range(nk):
            vt_ref[c, :HEAD_DIM] = v_ref[pl.ds(c * tk, tk), :].astype(F32).T.astype(BF16)
            vt_ref[c, HEAD_DIM:] = jnp.ones((ATTN_ONES, tk), BF16)

    q = jnp.concatenate([q_ref[:, h * HEAD_DIM:(h + 1) * HEAD_DIM] for h in range(Q_PER_KV)], axis=0)
    m_cols = Q_PER_KV * tq
    nchunk = tk // ATTN_ROWS

    def stage_a(i, s_ref, mx_ref):
        kc = k_ref[pl.ds(pl.multiple_of(i * tk, tk), tk), :]
        s = lax.dot_general(kc, q, (((1,), (1,)), ((), ())), preferred_element_type=F32)
        s_ref[...] = s
        mx_ref[...] = jnp.max(s, axis=0, keepdims=True)

    def stage_b(s_ref, mx_ref, p_ref, m):
        m_new = jnp.maximum(m, mx_ref[...])
        alpha = jnp.exp2(m - m_new)
        for c in range(nchunk):
            rows = slice(c * ATTN_ROWS, (c + 1) * ATTN_ROWS)
            p_ref[rows, :] = jnp.exp2((s_ref[rows, :] - m_new).astype(BF16))
        return m_new, alpha

    def stage_d(i, p_ref, alpha, acc):
        return alpha * acc + jnp.dot(vt_ref[i], p_ref[...], preferred_element_type=F32)

    def ring(i, m, al0, al1, acc, n_a, n_b):
        alphas = {0: al0, 1: al1}
        for u in range(ATTN_RING):
            if u < n_a:
                stage_a(i + u + 4, s_refs[u], mx_refs[u])
            if u < n_b:
                v = (u + 2) % ATTN_RING
                m, alphas[u + 2] = stage_b(s_refs[v], mx_refs[v], p_refs[v], m)
            acc = stage_d(i + u, p_refs[u], alphas[u], acc)
        return m, alphas.get(4), alphas.get(5), acc

    m = jnp.full((1, m_cols), -jnp.inf, F32)
    acc = jnp.zeros((HEAD_DIM + ATTN_ONES, m_cols), F32)
    for u in range(ATTN_RING):
        stage_a(u, s_refs[u], mx_refs[u])
    m, al0 = stage_b(s_refs[0], mx_refs[0], p_refs[0], m)
    m, al1 = stage_b(s_refs[1], mx_refs[1], p_refs[1], m)

    def trip(j, carry):
        return ring(ATTN_RING * j, *carry, ATTN_RING, ATTN_RING)

    m, al0, al1, acc = lax.fori_loop(0, nk // ATTN_RING - 1, trip, (m, al0, al1, acc))
    _, _, _, acc = ring(nk - ATTN_RING, m, al0, al1, acc, 0, 2)
    o = acc[:HEAD_DIM] / acc[HEAD_DIM:HEAD_DIM + 1]
    for h in range(Q_PER_KV):
        o_ref[:, h * HEAD_DIM:(h + 1) * HEAD_DIM] = o[:, h * tq:(h + 1) * tq].T


def _attention(q, k, v, *, tq=512, tk=512):
    B, L, _ = q.shape
    qw = Q_PER_KV * HEAD_DIM
    nk = L // tk
    assert nk % ATTN_RING == 0 and nk >= 2 * ATTN_RING
    body = functools.partial(_attn_body, tq=tq, tk=tk, nk=nk)
    m_cols = Q_PER_KV * tq
    return pl.pallas_call(
        body,
        grid=(B, N_KV_HEADS, L // tq),
        in_specs=[
            pl.BlockSpec((None, tq, qw), lambda b, g, i: (b, i, g)),
            pl.BlockSpec((None, L, HEAD_DIM), lambda b, g, i: (b, 0, g)),
            pl.BlockSpec((None, L, HEAD_DIM), lambda b, g, i: (b, 0, g)),
        ],
        out_specs=pl.BlockSpec((None, tq, qw), lambda b, g, i: (b, i, g)),
        out_shape=jax.ShapeDtypeStruct((B, L, D_ATTN), F32),
        scratch_shapes=([pltpu.VMEM((nk, HEAD_DIM + ATTN_ONES, tk), BF16)]
                        + [pltpu.VMEM((tk, m_cols), F32)] * ATTN_RING
                        + [pltpu.VMEM((1, m_cols), F32)] * ATTN_RING
                        + [pltpu.VMEM((tk, m_cols), BF16)] * ATTN_RING),
        compiler_params=_cparams(("parallel", "parallel", "arbitrary"), 48),
        name="attn",
    )(q, k, v)


def _filt_body(feat_ref, w1_ref, b1_ref, w2_ref, b2_ref, w3_ref, fq_ref, dec_ref, kf_ref, asum_ref, *, nt):
    i = pl.program_id(0)
    hp = lax.Precision.HIGHEST
    fq = fq_ref[...]
    feat = feat_ref[...]
    h = jnp.sin(fq * (jnp.dot(feat, w1_ref[...], precision=hp, preferred_element_type=F32) + b1_ref[...]))
    h = jnp.sin(fq * (jnp.dot(h, w2_ref[...], precision=hp, preferred_element_type=F32) + b2_ref[...]))
    h3 = jnp.dot(h, w3_ref[...], precision=hp, preferred_element_type=F32)
    t01 = feat[:, 0:1]
    taps = h3 * jnp.exp(-t01 * jnp.abs(dec_ref[...]))
    row = lax.broadcasted_iota(jnp.int32, (taps.shape[0], 1), 0)
    taps = jnp.where((row == 0) & (i == nt), 0.0, taps)
    for cb in range(taps.shape[1] // LANES):
        _store_time_grouped(kf_ref, cb, taps[:, cb * LANES:(cb + 1) * LANES])
    part = jnp.sum(jnp.abs(taps), axis=0, keepdims=True)

    @pl.when(i == 0)
    def _first():
        asum_ref[...] = part

    @pl.when(i > 0)
    def _rest():
        asum_ref[...] += part


def _filters(feats2, w1, b1, w2, b2, w3, fq, dec, *, tl=512):
    _, L, _ = feats2.shape
    nc = w3.shape[2]
    ng2 = FFT_N2 // FFT_G
    nt = L // tl
    body = functools.partial(_filt_body, nt=nt)
    return pl.pallas_call(
        body,
        grid=(2 * nt,),
        in_specs=[
            pl.BlockSpec((None, tl, FILT_PAD), lambda i: (i // nt, i % nt, 0)),
            _resident(w1.shape), _resident(b1.shape), _resident(w2.shape), _resident(b2.shape),
            pl.BlockSpec((None, FILT_PAD, nc), lambda i: (i // nt, 0, 0)),
            _resident(fq.shape),
            pl.BlockSpec((None, 1, nc), lambda i: (i // nt, 0, 0)),
        ],
        out_specs=[
            pl.BlockSpec((nc // LANES, ng2, tl // FFT_N2, FFT_G, LANES), lambda i: (0, 0, i, 0, 0)),
            pl.BlockSpec((1, nc), lambda i: (0, 0)),
        ],
        out_shape=[jax.ShapeDtypeStruct((nc // LANES, ng2, 2 * L // FFT_N2, FFT_G, LANES), F32),
                   jax.ShapeDtypeStruct((1, nc), F32)],
        compiler_params=_cparams(("arbitrary",), 48),
        name="filt",
    )(feats2, w1, b1, w2, b2, w3, fq, dec)


FFT_CB = FFT_W // LANES
FFT_CB_B = 2


def _pack_c(y):
    n = y.shape[0] // 2
    r = lax.bitcast_convert_type(y[:n], U32)
    i = lax.bitcast_convert_type(y[n:], U32)
    half = jnp.uint32(0x8000)
    return ((r + half) & jnp.uint32(0xFFFF0000)) | ((i + half) >> 16)


def _unpack_c(w):
    re = lax.bitcast_convert_type(w & jnp.uint32(0xFFFF0000), F32)
    im = lax.bitcast_convert_type(w << 16, F32)
    return re, im


def _unpack_rows(w):
    re, im = _unpack_c(w)
    return jnp.concatenate([re, im], axis=0).astype(BF16)


def _flat_scratch(n, dtype, cbs, lead=()):
    return pltpu.VMEM(lead + (cbs, n * FFT_G, LANES), dtype)


def _flatten(flat_ref, block_ref, pre=()):
    rows = flat_ref.shape[-2]
    for h in range(block_ref.shape[0]):
        flat_ref[pre + (h,)] = block_ref[(h,) + pre].reshape(rows, LANES)


def _unflatten(block_ref, flat_ref, pre=()):
    n = flat_ref.shape[-2] // FFT_G
    for h in range(block_ref.shape[0]):
        block_ref[(h,) + pre] = flat_ref[pre + (h,)].reshape(n, FFT_G, LANES)


def _gather_rows(flat_ref, j, pre=()):
    n = flat_ref.shape[-2] // FFT_G
    return jnp.concatenate(
        [flat_ref[pre + (h, pl.ds(j, n, stride=FFT_G), slice(None))] for h in range(flat_ref.shape[len(pre)])],
        axis=1)


def _scatter_rows(flat_ref, j, val, pre=()):
    n = flat_ref.shape[-2] // FFT_G
    for h in range(flat_ref.shape[len(pre)]):
        flat_ref[pre + (h, pl.ds(j, n, stride=FFT_G), slice(None))] = val[:, h * LANES:(h + 1) * LANES]


def _store_blocks(ref, j, val):
    for h in range(ref.shape[0]):
        ref[h, j] = val[:, h * LANES:(h + 1) * LANES]


def _store_grouped(ref, j, val):
    groups = ref.shape[1]
    for h in range(ref.shape[0]):
        ref[h, :, j] = val[:, h * LANES:(h + 1) * LANES].reshape(groups, FFT_G, LANES)


def _fft_a_body(x_ref, fa_ref, o_ref, xf_ref):
    for b in range(2):
        _flatten(xf_ref, x_ref, (b,))
    for j in range(FFT_G):
        xs = jnp.concatenate([_gather_rows(xf_ref, j, (b,)) for b in range(2)], axis=0).astype(BF16)
        _store_grouped(o_ref, j, _pack_c(jnp.dot(fa_ref[j], xs, preferred_element_type=F32)))


def _fft_a(xv, fa, *, ncb, cb0=0):
    _, P, _, ng2, n1h, _, _ = xv.shape
    n2 = ng2 * FFT_G
    two_n1 = fa.shape[1]
    n1 = two_n1 // 2
    c0 = cb0 // FFT_CB
    return pl.pallas_call(
        _fft_a_body,
        grid=(ng2, P, ncb // FFT_CB),
        in_specs=[
            pl.BlockSpec((FFT_CB, None, 2, None, n1h, FFT_G, LANES), lambda g, p, c: (c + c0, p, 0, g, 0, 0, 0)),
            pl.BlockSpec((FFT_G, two_n1, 2 * n1h), lambda g, p, c: (g, 0, 0)),
        ],
        out_specs=pl.BlockSpec((None, FFT_CB, n1 // FFT_G, FFT_G, FFT_G, LANES),
                               lambda g, p, c: (p, c, 0, g, 0, 0)),
        out_shape=jax.ShapeDtypeStruct((P, ncb, n1 // FFT_G, n2, FFT_G, LANES), U32),
        scratch_shapes=[_flat_scratch(n1h, F32, FFT_CB, (2,))],
        compiler_params=_cparams(("arbitrary", "arbitrary", "arbitrary"), 32),
        name="fft_a",
    )(xv, fa)


def _fft_a_real_body(x_ref, fa_ref, o_ref, xf_ref):
    _flatten(xf_ref, x_ref)
    for j in range(FFT_G):
        xs = _gather_rows(xf_ref, j).astype(BF16)
        _store_grouped(o_ref, j, _pack_c(jnp.dot(fa_ref[j], xs, preferred_element_type=F32)))


def _fft_a_real(xv, fa_re):
    ncb, ng2, n1, _, _ = xv.shape
    n2 = ng2 * FFT_G
    return pl.pallas_call(
        _fft_a_real_body,
        grid=(ng2, ncb // FFT_CB),
        in_specs=[
            pl.BlockSpec((FFT_CB, None, n1, FFT_G, LANES), lambda g, c: (c, g, 0, 0, 0)),
            pl.BlockSpec((FFT_G, 2 * n1, n1), lambda g, c: (g, 0, 0)),
        ],
        out_specs=pl.BlockSpec((FFT_CB, n1 // FFT_G, FFT_G, FFT_G, LANES), lambda g, c: (c, 0, g, 0, 0)),
        out_shape=jax.ShapeDtypeStruct((ncb, n1 // FFT_G, n2, FFT_G, LANES), U32),
        scratch_shapes=[_flat_scratch(n1, F32, FFT_CB)],
        compiler_params=_cparams(("arbitrary", "arbitrary"), 32),
        name="fft_a_real",
    )(xv, fa_re)


def _fft_b_filt_body(s_ref, fb_ref, as_ref, o_ref, sf_ref):
    _flatten(sf_ref, s_ref)
    inv = 1.0 / as_ref[...]
    fb = fb_ref[...]
    for j in range(FFT_G):
        k = jnp.dot(fb, _unpack_rows(_gather_rows(sf_ref, j)), preferred_element_type=F32)
        _store_blocks(o_ref, j, _pack_c(k * inv))


def _fft_b_filt(s1f, fb, asum):
    ncb, ng1, n2, _, _ = s1f.shape
    n1 = ng1 * FFT_G
    return pl.pallas_call(
        _fft_b_filt_body,
        grid=(ng1, ncb // FFT_CB),
        in_specs=[
            pl.BlockSpec((FFT_CB, None, n2, FFT_G, LANES), lambda g, c: (c, g, 0, 0, 0)),
            _resident(fb.shape),
            pl.BlockSpec((1, FFT_W), lambda g, c: (0, c)),
        ],
        out_specs=pl.BlockSpec((FFT_CB, FFT_G, n2, LANES), lambda g, c: (c, g, 0, 0)),
        out_shape=jax.ShapeDtypeStruct((ncb, n1, n2, LANES), U32),
        scratch_shapes=[_flat_scratch(n2, U32, FFT_CB)],
        compiler_params=_cparams(("parallel", "parallel"), 32),
        name="fft_b_filt",
    )(s1f, fb, asum)


def _fft_b_body(s_ref, kf_ref, fb_ref, fbi_ref, o_ref, sf_ref):
    n2 = FFT_N2
    _flatten(sf_ref, s_ref)
    fb = fb_ref[...]
    fbi = fbi_ref[...]
    for j in range(FFT_G):
        x = jnp.dot(fb, _unpack_rows(_gather_rows(sf_ref, j)), preferred_element_type=F32)
        xr, xi = x[:n2], x[n2:]
        kr, ki = _unpack_c(jnp.concatenate([kf_ref[h, j] for h in range(kf_ref.shape[0])], axis=1))
        pm = jnp.concatenate([xr * kr - xi * ki, xr * ki + xi * kr], axis=0).astype(BF16)
        _store_grouped(o_ref, j, _pack_c(jnp.dot(fbi, pm, preferred_element_type=F32)))


def _fft_b(s1, kf, fb, fbi, *, kcb0):
    P, ncb, ng1, n2, _, _ = s1.shape
    n1 = ng1 * FFT_G
    cbs = FFT_CB_B
    k0 = kcb0 // cbs
    return pl.pallas_call(
        _fft_b_body,
        grid=(ng1, ncb // cbs, P),
        in_specs=[
            pl.BlockSpec((None, cbs, None, n2, FFT_G, LANES), lambda g, c, p: (p, c, g, 0, 0, 0)),
            pl.BlockSpec((cbs, FFT_G, n2, LANES), lambda g, c, p: (c + k0, g, 0, 0)),
            _resident(fb.shape),
            _resident(fbi.shape),
        ],
        out_specs=pl.BlockSpec((None, cbs, n2 // FFT_G, FFT_G, FFT_G, LANES),
                               lambda g, c, p: (p, c, 0, g, 0, 0)),
        out_shape=jax.ShapeDtypeStruct((P, ncb, n2 // FFT_G, n1, FFT_G, LANES), U32),
        scratch_shapes=[_flat_scratch(n2, U32, cbs)],
        compiler_params=_cparams(("parallel", "parallel", "arbitrary"), 32),
        name="fft_b",
    )(s1, kf, fb, fbi)


def _fft_c_body(z_ref, fc_ref, u_ref, gate_ref, d_ref, *rest, chain):
    if chain:
        fa_ref, o_ref, s_ref, zf_ref, uf_ref, gf_ref, of_ref = rest
    else:
        o_ref, zf_ref, uf_ref, gf_ref, of_ref = rest
    n1h = o_ref.shape[2]
    d = d_ref[...]
    _flatten(zf_ref, z_ref)
    for b in range(2):
        _flatten(uf_ref, u_ref, (b,))
        _flatten(gf_ref, gate_ref, (b,))
    for j in range(FFT_G):
        y = jnp.dot(fc_ref[j], _unpack_rows(_gather_rows(zf_ref, j)), preferred_element_type=F32)
        res = []
        for b in range(2):
            yb = y[b * n1h:(b + 1) * n1h]
            res.append(_gather_rows(gf_ref, j, (b,)) * (yb + _gather_rows(uf_ref, j, (b,)) * d))
            _scatter_rows(of_ref, j, res[b], (b,))
        if chain:
            xs = jnp.concatenate(res, axis=0).astype(BF16)
            _store_grouped(s_ref, j, _pack_c(jnp.dot(fa_ref[j], xs, preferred_element_type=F32)))
    for b in range(2):
        _unflatten(o_ref, of_ref, (b,))


def _fft_c(s2, fc, uv, gv, d, *, ucb0, gcb0, fa=None):
    P, ncb, ng2, n1, _, _ = s2.shape
    n1h = n1 // 2
    n2 = ng2 * FFT_G
    u0 = ucb0 // FFT_CB
    g0 = gcb0 // FFT_CB
    chain = fa is not None
    tblk = (FFT_CB, None, 2, None, n1h, FFT_G, LANES)
    in_specs = [
        pl.BlockSpec((None, FFT_CB, None, n1, FFT_G, LANES), lambda g, p, c: (p, c, g, 0, 0, 0)),
        pl.BlockSpec((FFT_G, 2 * n1h, 2 * n1), lambda g, p, c: (g, 0, 0)),
        pl.BlockSpec(tblk, lambda g, p, c: (c + u0, p, 0, g, 0, 0, 0)),
        pl.BlockSpec(tblk, lambda g, p, c: (c + g0, p, 0, g, 0, 0, 0)),
        pl.BlockSpec((1, FFT_W), lambda g, p, c: (0, c)),
    ]
    out_specs = [pl.BlockSpec(tblk, lambda g, p, c: (c, p, 0, g, 0, 0, 0))]
    out_shape = [jax.ShapeDtypeStruct((ncb, P, 2, ng2, n1h, FFT_G, LANES), F32)]
    args = [s2, fc, uv, gv, d]
    if chain:
        in_specs.append(pl.BlockSpec((FFT_G, 2 * n1, 2 * n1h), lambda g, p, c: (g, 0, 0)))
        out_specs.append(pl.BlockSpec((None, FFT_CB, n1 // FFT_G, FFT_G, FFT_G, LANES),
                                      lambda g, p, c: (p, c, 0, g, 0, 0)))
        out_shape.append(jax.ShapeDtypeStruct((P, ncb, n1 // FFT_G, n2, FFT_G, LANES), U32))
        args.append(fa)
    return pl.pallas_call(
        functools.partial(_fft_c_body, chain=chain),
        grid=(ng2, P, ncb // FFT_CB),
        in_specs=in_specs,
        out_specs=out_specs,
        out_shape=out_shape,
        scratch_shapes=[_flat_scratch(n1, U32, FFT_CB), _flat_scratch(n1h, F32, FFT_CB, (2,)),
                        _flat_scratch(n1h, F32, FFT_CB, (2,)), _flat_scratch(n1h, F32, FFT_CB, (2,))],
        compiler_params=_cparams(("arbitrary", "arbitrary", "arbitrary"), 40),
        name="fft_c",
    )(*args)


def _oproj_body(x_ref, a_ref, h_ref, ga_ref, gh_ref, wa_ref, wh_ref, o_ref):
    ma = _rms(a_ref[...], ga_ref[...]).astype(BF16)
    hy = jnp.concatenate([_load_time_grouped(h_ref, cb) for cb in range(h_ref.shape[0])], axis=1)
    mh = _rms(hy, gh_ref[...]).astype(BF16)
    o_ref[...] = (x_ref[...] + jnp.dot(ma, wa_ref[...], preferred_element_type=F32)
                  + jnp.dot(mh, wh_ref[...], preferred_element_type=F32))


def _out_proj(x, a, h, ga, gh, wa, wh, *, tm=512):
    T, D = x.shape
    ncb, _, ng2, n1h, _, _ = h.shape
    npos = n1h * FFT_N2 // tm
    return pl.pallas_call(
        _oproj_body,
        grid=(T // tm,),
        in_specs=[
            pl.BlockSpec((tm, D), lambda i: (i, 0)),
            pl.BlockSpec((tm, D_ATTN), lambda i: (i, 0)),
            pl.BlockSpec((ncb, None, ng2, tm // FFT_N2, FFT_G, LANES), lambda i: (0, i // npos, 0, i % npos, 0, 0)),
            _resident((1, D_ATTN)), _resident((1, D_HYENA)),
            _resident(wa.shape), _resident(wh.shape),
        ],
        out_specs=pl.BlockSpec((tm, D), lambda i: (i, 0)),
        out_shape=jax.ShapeDtypeStruct((T, D), F32),
        compiler_params=_cparams(("parallel",), 48),
        name="oproj",
    )(x, a, h, ga, gh, wa, wh)


def _rope_tables(L):
    rows = L // GRID_W
    row = jnp.repeat(jnp.arange(rows, dtype=F32), GRID_W)
    col = jnp.tile(jnp.arange(GRID_W, dtype=F32), rows)
    inv = ROPE_THETA ** (-jnp.arange(0, ROPE_HALF, 2, dtype=F32) / ROPE_HALF)
    ar = row[:, None] * inv[None]
    ac = col[:, None] * inv[None]
    cr, sr, cc, sc = jnp.cos(ar), jnp.sin(ar), jnp.cos(ac), jnp.sin(ac)
    cos_t = jnp.concatenate([cr, cr, cc, cc], axis=-1)
    sin_t = jnp.concatenate([-sr, sr, -sc, sc], axis=-1)
    return cos_t, sin_t


def _filter_feats(L):
    t01 = jnp.linspace(0.0, 1.0, L, dtype=F32)[:, None]
    bands = (FILTER_EMB - 1) // 2
    fr = jnp.linspace(1e-4, bands - 1, bands, dtype=F32)[None]
    w = 2.0 * math.pi * jnp.arange(L, dtype=F32)[:, None] / L
    feats = jnp.concatenate([t01, jnp.cos(fr * w), -jnp.sin(fr * w)], axis=-1)
    feats = jnp.pad(feats, ((0, 0), (0, FILT_PAD - FILTER_EMB)))
    mirrored = jnp.concatenate([feats[:1], feats[1:][::-1]], axis=0)
    return jnp.stack([feats, mirrored])


def _dft_tables(L):
    n = 2 * L
    n2 = FFT_N2
    n1 = n // n2
    n1h = n1 // 2
    k1 = jnp.arange(n1, dtype=jnp.int32)[None, :, None]
    t1 = jnp.arange(n1, dtype=jnp.int32)[None, None, :]
    t2 = jnp.arange(n2, dtype=jnp.int32)[:, None, None]
    ang = (-2.0 * math.pi / n) * ((k1 * (n2 * t1 + t2)) % n).astype(F32)
    cr_full, ci_full = jnp.cos(ang), jnp.sin(ang)
    fa_re = jnp.concatenate([cr_full, ci_full], -2)
    cr, ci = cr_full[..., :n1h], ci_full[..., :n1h]
    fa = jnp.concatenate([jnp.concatenate([cr, -ci], -1), jnp.concatenate([ci, cr], -1)], -2)
    crt, cit = jnp.swapaxes(cr, 1, 2), jnp.swapaxes(ci, 1, 2)
    fc = jnp.concatenate([jnp.concatenate([crt, cit], -1), jnp.concatenate([-cit, crt], -1)], -2) / n
    a = jnp.arange(n2, dtype=jnp.int32)
    ang2 = (-2.0 * math.pi / n2) * ((a[:, None] * a[None, :]) % n2).astype(F32)
    fr, fi = jnp.cos(ang2), jnp.sin(ang2)
    fb = jnp.concatenate([jnp.concatenate([fr, -fi], -1), jnp.concatenate([fi, fr], -1)], -2)
    fbi = jnp.concatenate([jnp.concatenate([fr, fi], -1), jnp.concatenate([-fi, fr], -1)], -2)
    return (fa.astype(BF16), fa_re.astype(BF16), fb.astype(BF16), fbi.astype(BF16), fc.astype(BF16))


def _trunk(x, p):
    B, L, D = x.shape
    T = B * L
    P = B // 2
    n2 = FFT_N2
    n1h = L // n2
    C = D_HYENA
    ncb = C // LANES

    x0 = x.reshape(T, D)
    x1 = _ffn(x0, p["ffn1_norm"], p["ffn1_w13"], p["ffn1_w2"], p["final_norm"], final_norm=False)

    cos_t, sin_t = _rope_tables(L)
    q, k, v, hyc = _mix_in(x1, p["mix_norm"], p["wq"], p["wkv"], p["wh"], p["q_norm"], p["k_norm"],
                           cos_t, sin_t, p["conv_w"], p["conv_b"], L=L)
    attn = _attention(q.reshape(B, L, D_ATTN), k.reshape(B, L, D_KV), v.reshape(B, L, D_KV))
    ng2 = n2 // FFT_G
    hyv = hyc.reshape(3 * ncb, P, 2, ng2, n1h, FFT_G, LANES)

    fa, fa_re, fb, fbi, fc = _dft_tables(L)
    taps, asum = _filters(_filter_feats(L), p["filt_w1"], p["filt_b1"], p["filt_w2"], p["filt_b2"],
                          p["filt_w3"], p["filt_freq"], p["decay"])
    kf = _fft_b_filt(_fft_a_real(taps, fa_re), fb, asum)

    s1 = _fft_a(hyv, fa, ncb=ncb, cb0=0)
    s2 = _fft_b(s1, kf, fb, fbi, kcb0=0)
    z, s1 = _fft_c(s2, fc, hyv, hyv, p["bias0"], ucb0=0, gcb0=ncb, fa=fa)
    s2 = _fft_b(s1, kf, fb, fbi, kcb0=ncb)
    ho, = _fft_c(s2, fc, z, hyv, p["bias1"], ucb0=0, gcb0=2 * ncb)

    x2 = _out_proj(x1, attn.reshape(T, D_ATTN), ho.reshape(ncb, B, ng2, n1h, FFT_G, LANES),
                   p["gon_a"], p["gon_h"], p["wo_a"], p["wo_h"])
    x3 = _ffn(x2, p["ffn2_norm"], p["ffn2_w13"], p["ffn2_w2"], p["final_norm"], final_norm=True)
    return x3.reshape(B, L, D)


def kernel(x_prompt, x_sample, ffn1_norm, ffn1_w13, ffn1_w2, mix_norm, w_in, q_norm, k_norm, conv_w, conv_b, filt_w1, filt_b1, filt_w2, filt_b2, filt_w3, filt_freq, hyena_decay, hyena_bias, group_out_norm, w_out, ffn2_norm, ffn2_w13, ffn2_w2, final_norm):
    hp = FILT_PAD - FILTER_HIDDEN
    w_in0 = w_in[0]
    p = {
        "ffn1_norm": ffn1_norm[0][None], "ffn2_norm": ffn2_norm[0][None],
        "ffn1_w13": ffn1_w13[0].astype(BF16), "ffn1_w2": ffn1_w2[0].astype(BF16),
        "ffn2_w13": ffn2_w13[0].astype(BF16), "ffn2_w2": ffn2_w2[0].astype(BF16),
        "final_norm": final_norm[None],
        "mix_norm": mix_norm[0][None],
        "wq": w_in0[:, :D_ATTN].astype(BF16),
        "wkv": w_in0[:, D_ATTN:D_ATTN + 2 * D_KV].astype(BF16),
        "wh": w_in0[:, D_ATTN + 2 * D_KV:].astype(BF16),
        "q_norm": q_norm[0][None], "k_norm": k_norm[0][None],
        "conv_w": conv_w[0], "conv_b": conv_b[0][None],
        "filt_w1": jnp.pad(filt_w1[0], ((0, FILT_PAD - FILTER_EMB), (0, hp))),
        "filt_b1": jnp.pad(filt_b1[0], (0, hp))[None],
        "filt_w2": jnp.pad(filt_w2[0], ((0, hp), (0, hp))),
        "filt_b2": jnp.pad(filt_b2[0], (0, hp))[None],
        "filt_w3": jnp.pad(filt_w3[0], ((0, hp), (0, 0))).reshape(FILT_PAD, 2, 2 * D_HYENA).transpose(1, 0, 2),
        "filt_freq": jnp.pad(filt_freq[0], (0, hp))[None],
        "decay": hyena_decay[0].reshape(2, 1, 2 * D_HYENA),
        "bias0": hyena_bias[0, 0][None], "bias1": hyena_bias[0, 1][None],
        "gon_a": group_out_norm[0, :D_ATTN][None], "gon_h": group_out_norm[0, D_ATTN:][None],
        "wo_a": w_out[0, :D_ATTN].astype(BF16), "wo_h": w_out[0, D_ATTN:].astype(BF16),
    }
    return (_trunk(x_prompt, p), _trunk(x_sample, p))
```

```python
import functools
import math

import jax
import jax.numpy as jnp
from jax import lax
from jax.experimental import pallas as pl
from jax.experimental.pallas import tpu as pltpu

F32 = jnp.float32
BF16 = jnp.bfloat16
U32 = jnp.uint32

D_MODEL = 2048
GRID_W = 64
D_ATTN = D_MODEL // 2
D_HYENA = D_MODEL - D_ATTN
HEAD_DIM = 128
N_Q_HEADS = D_ATTN // HEAD_DIM
N_KV_HEADS = 2
Q_PER_KV = N_Q_HEADS // N_KV_HEADS
ROPE_HALF = HEAD_DIM // 2
ROPE_THETA = 10000.0
FILTER_EMB = 33
FILTER_HIDDEN = 64
D_FF = 5632
EPS = 1e-6
D_KV = N_KV_HEADS * HEAD_DIM

LANES = 128
FILT_PAD = 128
FFT_N2 = 128
FFT_W = 512
FFT_G = 8
MIB = 1024 * 1024
LOG2_E = math.log2(math.e)


def _cparams(semantics, vmem_mib):
    return pltpu.CompilerParams(dimension_semantics=semantics,
                                vmem_limit_bytes=vmem_mib * MIB)


def _rms(x, g):
    ms = jnp.mean(x * x, axis=-1, keepdims=True)
    return x * lax.rsqrt(ms + EPS) * g


def _resident(shape):
    nd = len(shape)
    return pl.BlockSpec(shape, lambda *_: (0,) * nd, pipeline_mode=pl.Buffered(1))


def _pack2(a, b):
    ua = lax.bitcast_convert_type(a, U32)
    ub = lax.bitcast_convert_type(b, U32)
    half = jnp.uint32(0x8000)
    return ((ua + half) & jnp.uint32(0xFFFF0000)) | ((ub + half) >> 16)


def _unpack_c(w):
    hi = lax.bitcast_convert_type(w & jnp.uint32(0xFFFF0000), F32)
    lo = lax.bitcast_convert_type(w << 16, F32)
    return hi, lo


def _pair_blocks(pb):
    hi = 4 * (pb // 2) + pb % 2
    return hi, hi + 2


def _ffn_body(x_ref, g_ref, wg_ref, wu_ref, w2_ref, gf_ref, o_ref, xn_ref, *, nj, final_norm):
    j = pl.program_id(1)

    @pl.when(j == 0)
    def _init():
        x = x_ref[...]
        xn_ref[...] = _rms(x, g_ref[...]).astype(BF16)
        o_ref[...] = x

    xn = xn_ref[...]
    hg = jnp.dot(xn, wg_ref[...], preferred_element_type=F32)
    hu = jnp.dot(xn, wu_ref[...], preferred_element_type=F32)
    a = (hg * (0.5 / (1.0 + jnp.exp(-hg))) * hu).astype(BF16)
    o_ref[...] += jnp.dot(a, w2_ref[...], preferred_element_type=F32)

    if final_norm:
        @pl.when(j == nj - 1)
        def _fin():
            o_ref[...] = _rms(o_ref[...], gf_ref[...])


def _ffn(x, g, w13, w2, gf, *, final_norm, tm=1024, tf=512):
    T, D = x.shape
    ff = w2.shape[0]
    nj = ff // tf
    body = functools.partial(_ffn_body, nj=nj, final_norm=final_norm)
    return pl.pallas_call(
        body,
        grid=(T // tm, nj),
        in_specs=[
            pl.BlockSpec((tm, D), lambda i, j: (i, 0)),
            pl.BlockSpec((1, D), lambda i, j: (0, 0)),
            pl.BlockSpec((D, tf), lambda i, j: (0, j)),
            pl.BlockSpec((D, tf), lambda i, j: (0, j + nj)),
            pl.BlockSpec((tf, D), lambda i, j: (j, 0)),
            pl.BlockSpec((1, D), lambda i, j: (0, 0)),
        ],
        out_specs=pl.BlockSpec((tm, D), lambda i, j: (i, 0)),
        out_shape=jax.ShapeDtypeStruct((T, D), F32),
        scratch_shapes=[pltpu.VMEM((tm, D), BF16)],
        compiler_params=_cparams(("parallel", "arbitrary"), 58),
        name="ffn",
    )(x, g, w13, w13, w2, gf)


def _store_time_grouped(ref, cb, val):
    n1t = ref.shape[2]
    v = val.reshape(n1t * (FFT_N2 // FFT_G), FFT_G, LANES)
    for a in range(n1t):
        ref[cb, :, a] = v[a * (FFT_N2 // FFT_G):(a + 1) * (FFT_N2 // FFT_G)]


def _load_time_grouped(ref, cb):
    n1t = ref.shape[2]
    return jnp.concatenate([ref[cb, :, a].reshape(FFT_N2, LANES) for a in range(n1t)], axis=0)


MIX_HALO = 16


def _mix_body(xb_ref, x_ref, xa_ref, g_ref, wq_ref, wkv_ref, wh_ref, qn_ref, kn_ref, cos_ref, sin_ref,
              cw_ref, cb_ref, q_ref, k_ref, v_ref, hy_ref, *, npos):
    tm = x_ref.shape[0]
    x_ext = jnp.concatenate([xb_ref[...], x_ref[...], xa_ref[...]], axis=0)
    h_ext = _rms(x_ext, g_ref[...]).astype(BF16)
    h = h_ext[MIX_HALO:MIX_HALO + tm]
    lane = lax.broadcasted_iota(jnp.int32, (tm, HEAD_DIM), 1)
    low_half = (lane % (2 * (ROPE_HALF // 2))) < (ROPE_HALF // 2)
    c = cos_ref[...]
    s = sin_ref[...]

    def head(p, gain, scale):
        y = _rms(p, gain)
        partner = jnp.where(low_half, pltpu.roll(y, HEAD_DIM - ROPE_HALF // 2, 1),
                            pltpu.roll(y, ROPE_HALF // 2, 1))
        y = y * c + partner * s
        return y if scale is None else y * scale

    q = jnp.dot(h, wq_ref[...], preferred_element_type=F32)
    for hd in range(N_Q_HEADS):
        sl = slice(hd * HEAD_DIM, (hd + 1) * HEAD_DIM)
        q_ref[:, sl] = head(q[:, sl], qn_ref[...], HEAD_DIM ** -0.5 * LOG2_E).astype(BF16)
    kv = jnp.dot(h, wkv_ref[...], preferred_element_type=F32)
    for hd in range(N_KV_HEADS):
        sl = slice(hd * HEAD_DIM, (hd + 1) * HEAD_DIM)
        k_ref[:, sl] = head(kv[:, sl], kn_ref[...], None).astype(BF16)
    v_ref[...] = kv[:, D_KV:].astype(BF16)

    hy = jnp.dot(h_ext, wh_ref[...], preferred_element_type=F32)
    pos = pl.program_id(0) % npos
    rid = lax.broadcasted_iota(jnp.int32, (tm, 1), 0)
    before = jnp.where((rid == 0) & (pos == 0), 0.0, hy[MIX_HALO - 1:MIX_HALO - 1 + tm])
    after = jnp.where((rid == tm - 1) & (pos == npos - 1), 0.0, hy[MIX_HALO + 1:MIX_HALO + 1 + tm])
    hyc = (before * cw_ref[0:1, :] + hy[MIX_HALO:MIX_HALO + tm] * cw_ref[1:2, :]
           + after * cw_ref[2:3, :] + cb_ref[...])
    for pb in range(hy_ref.shape[0]):
        hi, lo = _pair_blocks(pb)
        _store_time_grouped(hy_ref, pb, _pack2(hyc[:, hi * LANES:(hi + 1) * LANES],
                                               hyc[:, lo * LANES:(lo + 1) * LANES]))


def _mix_in(x, g, wq, wkv, wh, qn, kn, cos_t, sin_t, cw, cb, *, L, tm=256):
    T, D = x.shape
    npos = L // tm
    ng2 = FFT_N2 // FFT_G
    hb = tm // MIX_HALO
    nhb = T // MIX_HALO
    return pl.pallas_call(
        functools.partial(_mix_body, npos=npos),
        grid=(T // tm,),
        in_specs=[
            pl.BlockSpec((MIX_HALO, D), lambda i: (jnp.maximum(i * hb - 1, 0), 0)),
            pl.BlockSpec((tm, D), lambda i: (i, 0)),
            pl.BlockSpec((MIX_HALO, D), lambda i: (jnp.minimum((i + 1) * hb, nhb - 1), 0)),
            _resident((1, D)),
            _resident(wq.shape),
            _resident(wkv.shape),
            _resident(wh.shape),
            _resident((1, HEAD_DIM)),
            _resident((1, HEAD_DIM)),
            pl.BlockSpec((tm, HEAD_DIM), lambda i: (i % npos, 0)),
            pl.BlockSpec((tm, HEAD_DIM), lambda i: (i % npos, 0)),
            _resident(cw.shape),
            _resident(cb.shape),
        ],
        out_specs=[
            pl.BlockSpec((tm, D_ATTN), lambda i: (i, 0)),
            pl.BlockSpec((tm, D_KV), lambda i: (i, 0)),
            pl.BlockSpec((tm, D_KV), lambda i: (i, 0)),
            pl.BlockSpec((3 * D_HYENA // (2 * LANES), None, ng2, tm // FFT_N2, FFT_G, LANES),
                         lambda i: (0, i // npos, 0, i % npos, 0, 0)),
        ],
        out_shape=[
            jax.ShapeDtypeStruct((T, D_ATTN), BF16),
            jax.ShapeDtypeStruct((T, D_KV), BF16),
            jax.ShapeDtypeStruct((T, D_KV), BF16),
            jax.ShapeDtypeStruct((3 * D_HYENA // (2 * LANES), T // L, ng2, L // FFT_N2, FFT_G, LANES), U32),
        ],
        compiler_params=_cparams(("parallel",), 48),
        name="mix_in",
    )(x, x, x, g, wq, wkv, wh, qn, kn, cos_t, sin_t, cw, cb)


ATTN_ROWS = 32
ATTN_RING = 4
ATTN_ONES = 16


def _attn_body(q_ref, k_ref, v_ref, o_ref, vt_ref, *bufs, tq, tk, nk):
    s_refs, mx_refs, p_refs = (bufs[k * ATTN_RING:(k + 1) * ATTN_RING] for k in range(3))

    @pl.when(pl.program_id(2) == 0)
    def _transpose_v():
        for c in range(nk):
            vt_ref[c, :HEAD_DIM] = v_ref[pl.ds(c * tk, tk), :].astype(F32).T.astype(BF16)
            vt_ref[c, HEAD_DIM:] = jnp.ones((ATTN_ONES, tk), BF16)

    q = jnp.concatenate([q_ref[:, h * HEAD_DIM:(h + 1) * HEAD_DIM] for h in range(Q_PER_KV)], axis=0)
    m_cols = Q_PER_KV * tq
    nchunk = tk // ATTN_ROWS

    def stage_a(i, s_ref, mx_ref):
        kc = k_ref[pl.ds(pl.multiple_of(i * tk, tk), tk), :]
        s = lax.dot_general(kc, q, (((1,), (1,)), ((), ())), preferred_element_type=F32)
        s_ref[...] = s
        mx_ref[...] = jnp.max(s, axis=0, keepdims=True)

    def stage_b(s_ref, mx_ref, p_ref, m):
        m_new = jnp.maximum(m, mx_ref[...])
        alpha = jnp.exp2(m - m_new)
        for c in range(nchunk):
            rows = slice(c * ATTN_ROWS, (c + 1) * ATTN_ROWS)
            p_ref[rows, :] = jnp.exp2((s_ref[rows, :] - m_new).astype(BF16))
        return m_new, alpha

    def stage_d(i, p_ref, alpha, acc):
        return alpha * acc + jnp.dot(vt_ref[i], p_ref[...], preferred_element_type=F32)

    def ring(i, m, al0, al1, acc, n_a, n_b):
        alphas = {0: al0, 1: al1}
        for u in range(ATTN_RING):
            if u < n_a:
                stage_a(i + u + 4, s_refs[u], mx_refs[u])
            if u < n_b:
                v = (u + 2) % ATTN_RING
                m, alphas[u + 2] = stage_b(s_refs[v], mx_refs[v], p_refs[v], m)
            acc = stage_d(i + u, p_refs[u], alphas[u], acc)
        return m, alphas.get(4), alphas.get(5), acc

    m = jnp.full((1, m_cols), -jnp.inf, F32)
    acc = jnp.zeros((HEAD_DIM + ATTN_ONES, m_cols), F32)
    for u in range(ATTN_RING):
        stage_a(u, s_refs[u], mx_refs[u])
    m, al0 = stage_b(s_refs[0], mx_refs[0], p_refs[0], m)
    m, al1 = stage_b(s_refs[1], mx_refs[1], p_refs[1], m)

    def trip(j, carry):
        return ring(ATTN_RING * j, *carry, ATTN_RING, ATTN_RING)

    m, al0, al1, acc = lax.fori_loop(0, nk // ATTN_RING - 1, trip, (m, al0, al1, acc))
    _, _, _, acc = ring(nk - ATTN_RING, m, al0, al1, acc, 0, 2)
    o = acc[:HEAD_DIM] / acc[HEAD_DIM:HEAD_DIM + 1]
    for h in range(Q_PER_KV):
        o_ref[:, h * HEAD_DIM:(h + 1) * HEAD_DIM] = o[:, h * tq:(h + 1) * tq].T


def _attention(q, k, v, *, tq=512, tk=512):
    B, L, _ = q.shape
    qw = Q_PER_KV * HEAD_DIM
    nk = L // tk
    assert nk % ATTN_RING == 0 and nk >= 2 * ATTN_RING
    body = functools.partial(_attn_body, tq=tq, tk=tk, nk=nk)
    m_cols = Q_PER_KV * tq
    return pl.pallas_call(
        body,
        grid=(B, N_KV_HEADS, L // tq),
        in_specs=[
            pl.BlockSpec((None, tq, qw), lambda b, g, i: (b, i, g)),
            pl.BlockSpec((None, L, HEAD_DIM), lambda b, g, i: (b, 0, g)),
            pl.BlockSpec((None, L, HEAD_DIM), lambda b, g, i: (b, 0, g)),
        ],
        out_specs=pl.BlockSpec((None, tq, qw), lambda b, g, i: (b, i, g)),
        out_shape=jax.ShapeDtypeStruct((B, L, D_ATTN), F32),
        scratch_shapes=([pltpu.VMEM((nk, HEAD_DIM + ATTN_ONES, tk), BF16)]
                        + [pltpu.VMEM((tk, m_cols), F32)] * ATTN_RING
                        + [pltpu.VMEM((1, m_cols), F32)] * ATTN_RING
                        + [pltpu.VMEM((tk, m_cols), BF16)] * ATTN_RING),
        compiler_params=_cparams(("parallel", "parallel", "arbitrary"), 48),
        name="attn",
    )(q, k, v)


def _filt_body(feat_ref, w1_ref, b1_ref, w2_ref, b2_ref, w3_ref, fq_ref, dec_ref, kf_ref, asum_ref, *, nt):
    i = pl.program_id(0)
    hp = lax.Precision.HIGHEST
    fq = fq_ref[...]
    feat = feat_ref[...]
    h = jnp.sin(fq * (jnp.dot(feat, w1_ref[...], precision=hp, preferred_element_type=F32) + b1_ref[...]))
    h = jnp.sin(fq * (jnp.dot(h, w2_ref[...], precision=hp, preferred_element_type=F32) + b2_ref[...]))
    h3 = jnp.dot(h, w3_ref[...], precision=hp, preferred_element_type=F32)
    t01 = feat[:, 0:1]
    taps = h3 * jnp.exp(-t01 * jnp.abs(dec_ref[...]))
    row = lax.broadcasted_iota(jnp.int32, (taps.shape[0], 1), 0)
    taps = jnp.where((row == 0) & (i == nt), 0.0, taps)
    for cb in range(taps.shape[1] // LANES):
        _store_time_grouped(kf_ref, cb, taps[:, cb * LANES:(cb + 1) * LANES])
    part = jnp.sum(jnp.abs(taps), axis=0, keepdims=True)

    @pl.when(i == 0)
    def _first():
        asum_ref[...] = part

    @pl.when(i > 0)
    def _rest():
        asum_ref[...] += part


def _filters(feats2, w1, b1, w2, b2, w3, fq, dec, *, tl=512):
    _, L, _ = feats2.shape
    nc = w3.shape[2]
    ng2 = FFT_N2 // FFT_G
    nt = L // tl
    body = functools.partial(_filt_body, nt=nt)
    return pl.pallas_call(
        body,
        grid=(2 * nt,),
        in_specs=[
            pl.BlockSpec((None, tl, FILT_PAD), lambda i: (i // nt, i % nt, 0)),
            _resident(w1.shape), _resident(b1.shape), _resident(w2.shape), _resident(b2.shape),
            pl.BlockSpec((None, FILT_PAD, nc), lambda i: (i // nt, 0, 0)),
            _resident(fq.shape),
            pl.BlockSpec((None, 1, nc), lambda i: (i // nt, 0, 0)),
        ],
        out_specs=[
            pl.BlockSpec((nc // LANES, ng2, tl // FFT_N2, FFT_G, LANES), lambda i: (0, 0, i, 0, 0)),
            pl.BlockSpec((1, nc), lambda i: (0, 0)),
        ],
        out_shape=[jax.ShapeDtypeStruct((nc // LANES, ng2, 2 * L // FFT_N2, FFT_G, LANES), F32),
                   jax.ShapeDtypeStruct((1, nc), F32)],
        compiler_params=_cparams(("arbitrary",), 48),
        name="filt",
    )(feats2, w1, b1, w2, b2, w3, fq, dec)


FFT_CB = FFT_W // LANES
FFT_CB_B = 2


def _pack_c(y):
    n = y.shape[0] // 2
    return _pack2(y[:n], y[n:])


def _unpack_lanes(w):
    hi, lo = _unpack_c(w)
    return jnp.concatenate([hi, lo], axis=1)


def _unpack_rows(w):
    re, im = _unpack_c(w)
    return jnp.concatenate([re, im], axis=0).astype(BF16)


def _flat_scratch(n, dtype, cbs, lead=()):
    return pltpu.VMEM(lead + (cbs, n * FFT_G, LANES), dtype)


def _flatten(flat_ref, block_ref, pre=()):
    rows = flat_ref.shape[-2]
    for h in range(block_ref.shape[0]):
        flat_ref[pre + (h,)] = block_ref[(h,) + pre].reshape(rows, LANES)


def _unflatten(block_ref, flat_ref, pre=()):
    n = flat_ref.shape[-2] // FFT_G
    for h in range(block_ref.shape[0]):
        block_ref[(h,) + pre] = flat_ref[pre + (h,)].reshape(n, FFT_G, LANES)


def _gather_rows(flat_ref, j, pre=()):
    n = flat_ref.shape[-2] // FFT_G
    return jnp.concatenate(
        [flat_ref[pre + (h, pl.ds(j, n, stride=FFT_G), slice(None))] for h in range(flat_ref.shape[len(pre)])],
        axis=1)


def _scatter_rows(flat_ref, j, val, pre=()):
    n = flat_ref.shape[-2] // FFT_G
    for h in range(flat_ref.shape[len(pre)]):
        flat_ref[pre + (h, pl.ds(j, n, stride=FFT_G), slice(None))] = val[:, h * LANES:(h + 1) * LANES]


def _store_blocks(ref, j, val):
    for h in range(ref.shape[0]):
        ref[h, j] = val[:, h * LANES:(h + 1) * LANES]


def _store_grouped(ref, j, val):
    groups = ref.shape[1]
    for h in range(ref.shape[0]):
        ref[h, :, j] = val[:, h * LANES:(h + 1) * LANES].reshape(groups, FFT_G, LANES)


def _fft_a_body(x_ref, fa_ref, o_ref, xf_ref):
    for b in range(2):
        _flatten(xf_ref, x_ref, (b,))
    for j in range(FFT_G):
        xs = jnp.concatenate([_unpack_lanes(_gather_rows(xf_ref, j, (b,))) for b in range(2)],
                             axis=0).astype(BF16)
        _store_grouped(o_ref, j, _pack_c(jnp.dot(fa_ref[j], xs, preferred_element_type=F32)))


def _fft_a(xv, fa, *, ncb, pb0=0):
    _, P, _, ng2, n1h, _, _ = xv.shape
    n2 = ng2 * FFT_G
    two_n1 = fa.shape[1]
    n1 = two_n1 // 2
    pbs = FFT_CB // 2
    c0 = pb0 // pbs
    return pl.pallas_call(
        _fft_a_body,
        grid=(ng2, P, ncb // FFT_CB),
        in_specs=[
            pl.BlockSpec((pbs, None, 2, None, n1h, FFT_G, LANES), lambda g, p, c: (c + c0, p, 0, g, 0, 0, 0)),
            pl.BlockSpec((FFT_G, two_n1, 2 * n1h), lambda g, p, c: (g, 0, 0)),
        ],
        out_specs=pl.BlockSpec((None, FFT_CB, n1 // FFT_G, FFT_G, FFT_G, LANES),
                               lambda g, p, c: (p, c, 0, g, 0, 0)),
        out_shape=jax.ShapeDtypeStruct((P, ncb, n1 // FFT_G, n2, FFT_G, LANES), U32),
        scratch_shapes=[_flat_scratch(n1h, U32, pbs, (2,))],
        compiler_params=_cparams(("arbitrary", "arbitrary", "arbitrary"), 32),
        name="fft_a",
    )(xv, fa)


def _fft_a_real_body(x_ref, fa_ref, o_ref, xf_ref):
    _flatten(xf_ref, x_ref)
    for j in range(FFT_G):
        xs = _gather_rows(xf_ref, j).astype(BF16)
        _store_grouped(o_ref, j, _pack_c(jnp.dot(fa_ref[j], xs, preferred_element_type=F32)))


def _fft_a_real(xv, fa_re):
    ncb, ng2, n1, _, _ = xv.shape
    n2 = ng2 * FFT_G
    return pl.pallas_call(
        _fft_a_real_body,
        grid=(ng2, ncb // FFT_CB),
        in_specs=[
            pl.BlockSpec((FFT_CB, None, n1, FFT_G, LANES), lambda g, c: (c, g, 0, 0, 0)),
            pl.BlockSpec((FFT_G, 2 * n1, n1), lambda g, c: (g, 0, 0)),
        ],
        out_specs=pl.BlockSpec((FFT_CB, n1 // FFT_G, FFT_G, FFT_G, LANES), lambda g, c: (c, 0, g, 0, 0)),
        out_shape=jax.ShapeDtypeStruct((ncb, n1 // FFT_G, n2, FFT_G, LANES), U32),
        scratch_shapes=[_flat_scratch(n1, F32, FFT_CB)],
        compiler_params=_cparams(("arbitrary", "arbitrary"), 32),
        name="fft_a_real",
    )(xv, fa_re)


def _fft_b_filt_body(s_ref, fb_ref, as_ref, o_ref, sf_ref):
    _flatten(sf_ref, s_ref)
    inv = 1.0 / as_ref[...]
    fb = fb_ref[...]
    for j in range(FFT_G):
        k = jnp.dot(fb, _unpack_rows(_gather_rows(sf_ref, j)), preferred_element_type=F32)
        _store_blocks(o_ref, j, _pack_c(k * inv))


def _fft_b_filt(s1f, fb, asum):
    ncb, ng1, n2, _, _ = s1f.shape
    n1 = ng1 * FFT_G
    return pl.pallas_call(
        _fft_b_filt_body,
        grid=(ng1, ncb // FFT_CB),
        in_specs=[
            pl.BlockSpec((FFT_CB, None, n2, FFT_G, LANES), lambda g, c: (c, g, 0, 0, 0)),
            _resident(fb.shape),
            pl.BlockSpec((1, FFT_W), lambda g, c: (0, c)),
        ],
        out_specs=pl.BlockSpec((FFT_CB, FFT_G, n2, LANES), lambda g, c: (c, g, 0, 0)),
        out_shape=jax.ShapeDtypeStruct((ncb, n1, n2, LANES), U32),
        scratch_shapes=[_flat_scratch(n2, U32, FFT_CB)],
        compiler_params=_cparams(("parallel", "parallel"), 32),
        name="fft_b_filt",
    )(s1f, fb, asum)


def _fft_b_body(s_ref, kf_ref, fb_ref, fbi_ref, o_ref, sf_ref):
    n2 = FFT_N2
    _flatten(sf_ref, s_ref)
    fb = fb_ref[...]
    fbi = fbi_ref[...]
    for j in range(FFT_G):
        x = jnp.dot(fb, _unpack_rows(_gather_rows(sf_ref, j)), preferred_element_type=F32)
        xr, xi = x[:n2], x[n2:]
        kr, ki = _unpack_c(jnp.concatenate([kf_ref[h, j] for h in range(kf_ref.shape[0])], axis=1))
        pm = jnp.concatenate([xr * kr - xi * ki, xr * ki + xi * kr], axis=0).astype(BF16)
        _store_grouped(o_ref, j, _pack_c(jnp.dot(fbi, pm, preferred_element_type=F32)))


def _fft_b(s1, kf, fb, fbi, *, kcb0):
    P, ncb, ng1, n2, _, _ = s1.shape
    n1 = ng1 * FFT_G
    cbs = FFT_CB_B
    k0 = kcb0 // cbs
    return pl.pallas_call(
        _fft_b_body,
        grid=(ng1, ncb // cbs, P),
        in_specs=[
            pl.BlockSpec((None, cbs, None, n2, FFT_G, LANES), lambda g, c, p: (p, c, g, 0, 0, 0)),
            pl.BlockSpec((cbs, FFT_G, n2, LANES), lambda g, c, p: (c + k0, g, 0, 0)),
            _resident(fb.shape),
            _resident(fbi.shape),
        ],
        out_specs=pl.BlockSpec((None, cbs, n2 // FFT_G, FFT_G, FFT_G, LANES),
                               lambda g, c, p: (p, c, 0, g, 0, 0)),
        out_shape=jax.ShapeDtypeStruct((P, ncb, n2 // FFT_G, n1, FFT_G, LANES), U32),
        scratch_shapes=[_flat_scratch(n2, U32, cbs)],
        compiler_params=_cparams(("parallel", "parallel", "arbitrary"), 32),
        name="fft_b",
    )(s1, kf, fb, fbi)


def _fft_c_body(z_ref, fc_ref, u_ref, gate_ref, d_ref, *rest, chain):
    if chain:
        fa_ref, o_ref, s_ref, zf_ref, uf_ref, gf_ref, of_ref = rest
    else:
        o_ref, zf_ref, uf_ref, gf_ref, of_ref = rest
    n1h = o_ref.shape[2]
    d = d_ref[...]
    _flatten(zf_ref, z_ref)
    for b in range(2):
        _flatten(uf_ref, u_ref, (b,))
        _flatten(gf_ref, gate_ref, (b,))
    for j in range(FFT_G):
        y = jnp.dot(fc_ref[j], _unpack_rows(_gather_rows(zf_ref, j)), preferred_element_type=F32)
        res = []
        for b in range(2):
            yb = y[b * n1h:(b + 1) * n1h]
            gate = _unpack_lanes(_gather_rows(gf_ref, j, (b,)))
            skip = _unpack_lanes(_gather_rows(uf_ref, j, (b,)))
            res.append(gate * (yb + skip * d))
            if chain:
                half = res[b].shape[1] // 2
                _scatter_rows(of_ref, j, _pack2(res[b][:, :half], res[b][:, half:]), (b,))
            else:
                _scatter_rows(of_ref, j, res[b], (b,))
        if chain:
            xs = jnp.concatenate(res, axis=0).astype(BF16)
            _store_grouped(s_ref, j, _pack_c(jnp.dot(fa_ref[j], xs, preferred_element_type=F32)))
    for b in range(2):
        _unflatten(o_ref, of_ref, (b,))


def _fft_c(s2, fc, uv, gv, d, *, upb0, gpb0, fa=None):
    P, ncb, ng2, n1, _, _ = s2.shape
    n1h = n1 // 2
    n2 = ng2 * FFT_G
    pbs = FFT_CB // 2
    u0 = upb0 // pbs
    g0 = gpb0 // pbs
    chain = fa is not None
    pblk = (pbs, None, 2, None, n1h, FFT_G, LANES)
    ocb, odt = (pbs, U32) if chain else (FFT_CB, F32)
    in_specs = [
        pl.BlockSpec((None, FFT_CB, None, n1, FFT_G, LANES), lambda g, p, c: (p, c, g, 0, 0, 0)),
        pl.BlockSpec((FFT_G, 2 * n1h, 2 * n1), lambda g, p, c: (g, 0, 0)),
        pl.BlockSpec(pblk, lambda g, p, c: (c + u0, p, 0, g, 0, 0, 0)),
        pl.BlockSpec(pblk, lambda g, p, c: (c + g0, p, 0, g, 0, 0, 0)),
        pl.BlockSpec((1, FFT_W), lambda g, p, c: (0, c)),
    ]
    out_specs = [pl.BlockSpec((ocb, None, 2, None, n1h, FFT_G, LANES), lambda g, p, c: (c, p, 0, g, 0, 0, 0))]
    out_shape = [jax.ShapeDtypeStruct((ncb * ocb // FFT_CB, P, 2, ng2, n1h, FFT_G, LANES), odt)]
    args = [s2, fc, uv, gv, d]
    if chain:
        in_specs.append(pl.BlockSpec((FFT_G, 2 * n1, 2 * n1h), lambda g, p, c: (g, 0, 0)))
        out_specs.append(pl.BlockSpec((None, FFT_CB, n1 // FFT_G, FFT_G, FFT_G, LANES),
                                      lambda g, p, c: (p, c, 0, g, 0, 0)))
        out_shape.append(jax.ShapeDtypeStruct((P, ncb, n1 // FFT_G, n2, FFT_G, LANES), U32))
        args.append(fa)
    return pl.pallas_call(
        functools.partial(_fft_c_body, chain=chain),
        grid=(ng2, P, ncb // FFT_CB),
        in_specs=in_specs,
        out_specs=out_specs,
        out_shape=out_shape,
        scratch_shapes=[_flat_scratch(n1, U32, FFT_CB), _flat_scratch(n1h, U32, pbs, (2,)),
                        _flat_scratch(n1h, U32, pbs, (2,)), _flat_scratch(n1h, odt, ocb, (2,))],
        compiler_params=_cparams(("arbitrary", "arbitrary", "arbitrary"), 40),
        name="fft_c",
    )(*args)


def _oproj_body(x_ref, a_ref, h_ref, ga_ref, gh_ref, wa_ref, wh_ref, o_ref):
    ma = _rms(a_ref[...], ga_ref[...]).astype(BF16)
    hy = jnp.concatenate([_load_time_grouped(h_ref, cb) for cb in range(h_ref.shape[0])], axis=1)
    mh = _rms(hy, gh_ref[...]).astype(BF16)
    o_ref[...] = (x_ref[...] + jnp.dot(ma, wa_ref[...], preferred_element_type=F32)
                  + jnp.dot(mh, wh_ref[...], preferred_element_type=F32))


def _out_proj(x, a, h, ga, gh, wa, wh, *, tm=512):
    T, D = x.shape
    ncb, _, ng2, n1h, _, _ = h.shape
    npos = n1h * FFT_N2 // tm
    return pl.pallas_call(
        _oproj_body,
        grid=(T // tm,),
        in_specs=[
            pl.BlockSpec((tm, D), lambda i: (i, 0)),
            pl.BlockSpec((tm, D_ATTN), lambda i: (i, 0)),
            pl.BlockSpec((ncb, None, ng2, tm // FFT_N2, FFT_G, LANES), lambda i: (0, i // npos, 0, i % npos, 0, 0)),
            _resident((1, D_ATTN)), _resident((1, D_HYENA)),
            _resident(wa.shape), _resident(wh.shape),
        ],
        out_specs=pl.BlockSpec((tm, D), lambda i: (i, 0)),
        out_shape=jax.ShapeDtypeStruct((T, D), F32),
        compiler_params=_cparams(("parallel",), 48),
        name="oproj",
    )(x, a, h, ga, gh, wa, wh)


def _rope_tables(L):
    rows = L // GRID_W
    row = jnp.repeat(jnp.arange(rows, dtype=F32), GRID_W)
    col = jnp.tile(jnp.arange(GRID_W, dtype=F32), rows)
    inv = ROPE_THETA ** (-jnp.arange(0, ROPE_HALF, 2, dtype=F32) / ROPE_HALF)
    ar = row[:, None] * inv[None]
    ac = col[:, None] * inv[None]
    cr, sr, cc, sc = jnp.cos(ar), jnp.sin(ar), jnp.cos(ac), jnp.sin(ac)
    cos_t = jnp.concatenate([cr, cr, cc, cc], axis=-1)
    sin_t = jnp.concatenate([-sr, sr, -sc, sc], axis=-1)
    return cos_t, sin_t


def _filter_feats(L):
    t01 = jnp.linspace(0.0, 1.0, L, dtype=F32)[:, None]
    bands = (FILTER_EMB - 1) // 2
    fr = jnp.linspace(1e-4, bands - 1, bands, dtype=F32)[None]
    w = 2.0 * math.pi * jnp.arange(L, dtype=F32)[:, None] / L
    feats = jnp.concatenate([t01, jnp.cos(fr * w), -jnp.sin(fr * w)], axis=-1)
    feats = jnp.pad(feats, ((0, 0), (0, FILT_PAD - FILTER_EMB)))
    mirrored = jnp.concatenate([feats[:1], feats[1:][::-1]], axis=0)
    return jnp.stack([feats, mirrored])


def _dft_tables(L):
    n = 2 * L
    n2 = FFT_N2
    n1 = n // n2
    n1h = n1 // 2
    k1 = jnp.arange(n1, dtype=jnp.int32)[None, :, None]
    t1 = jnp.arange(n1, dtype=jnp.int32)[None, None, :]
    t2 = jnp.arange(n2, dtype=jnp.int32)[:, None, None]
    ang = (-2.0 * math.pi / n) * ((k1 * (n2 * t1 + t2)) % n).astype(F32)
    cr_full, ci_full = jnp.cos(ang), jnp.sin(ang)
    fa_re = jnp.concatenate([cr_full, ci_full], -2)
    cr, ci = cr_full[..., :n1h], ci_full[..., :n1h]
    fa = jnp.concatenate([jnp.concatenate([cr, -ci], -1), jnp.concatenate([ci, cr], -1)], -2)
    crt, cit = jnp.swapaxes(cr, 1, 2), jnp.swapaxes(ci, 1, 2)
    fc = jnp.concatenate([jnp.concatenate([crt, cit], -1), jnp.concatenate([-cit, crt], -1)], -2) / n
    a = jnp.arange(n2, dtype=jnp.int32)
    ang2 = (-2.0 * math.pi / n2) * ((a[:, None] * a[None, :]) % n2).astype(F32)
    fr, fi = jnp.cos(ang2), jnp.sin(ang2)
    fb = jnp.concatenate([jnp.concatenate([fr, -fi], -1), jnp.concatenate([fi, fr], -1)], -2)
    fbi = jnp.concatenate([jnp.concatenate([fr, fi], -1), jnp.concatenate([-fi, fr], -1)], -2)
    return (fa.astype(BF16), fa_re.astype(BF16), fb.astype(BF16), fbi.astype(BF16), fc.astype(BF16))


def _trunk(x, p):
    B, L, D = x.shape
    T = B * L
    P = B // 2
    n2 = FFT_N2
    n1h = L // n2
    C = D_HYENA
    ncb = C // LANES

    x0 = x.reshape(T, D)
    x1 = _ffn(x0, p["ffn1_norm"], p["ffn1_w13"], p["ffn1_w2"], p["final_norm"], final_norm=False)

    cos_t, sin_t = _rope_tables(L)
    q, k, v, hyc = _mix_in(x1, p["mix_norm"], p["wq"], p["wkv"], p["wh"], p["q_norm"], p["k_norm"],
                           cos_t, sin_t, p["conv_w"], p["conv_b"], L=L)
    attn = _attention(q.reshape(B, L, D_ATTN), k.reshape(B, L, D_KV), v.reshape(B, L, D_KV))
    ng2 = n2 // FFT_G
    npb = ncb // 2
    hyv = hyc.reshape(3 * npb, P, 2, ng2, n1h, FFT_G, LANES)

    fa, fa_re, fb, fbi, fc = _dft_tables(L)
    taps, asum = _filters(_filter_feats(L), p["filt_w1"], p["filt_b1"], p["filt_w2"], p["filt_b2"],
                          p["filt_w3"], p["filt_freq"], p["decay"])
    kf = _fft_b_filt(_fft_a_real(taps, fa_re), fb, asum)

    s1 = _fft_a(hyv, fa, ncb=ncb, pb0=0)
    s2 = _fft_b(s1, kf, fb, fbi, kcb0=0)
    z, s1 = _fft_c(s2, fc, hyv, hyv, p["bias0"], upb0=0, gpb0=npb, fa=fa)
    s2 = _fft_b(s1, kf, fb, fbi, kcb0=ncb)
    ho, = _fft_c(s2, fc, z, hyv, p["bias1"], upb0=0, gpb0=2 * npb)

    x2 = _out_proj(x1, attn.reshape(T, D_ATTN), ho.reshape(ncb, B, ng2, n1h, FFT_G, LANES),
                   p["gon_a"], p["gon_h"], p["wo_a"], p["wo_h"])
    x3 = _ffn(x2, p["ffn2_norm"], p["ffn2_w13"], p["ffn2_w2"], p["final_norm"], final_norm=True)
    return x3.reshape(B, L, D)


def kernel(x_prompt, x_sample, ffn1_norm, ffn1_w13, ffn1_w2, mix_norm, w_in, q_norm, k_norm, conv_w, conv_b, filt_w1, filt_b1, filt_w2, filt_b2, filt_w3, filt_freq, hyena_decay, hyena_bias, group_out_norm, w_out, ffn2_norm, ffn2_w13, ffn2_w2, final_norm):
    hp = FILT_PAD - FILTER_HIDDEN
    w_in0 = w_in[0]
    p = {
        "ffn1_norm": ffn1_norm[0][None], "ffn2_norm": ffn2_norm[0][None],
        "ffn1_w13": ffn1_w13[0].astype(BF16), "ffn1_w2": ffn1_w2[0].astype(BF16),
        "ffn2_w13": ffn2_w13[0].astype(BF16), "ffn2_w2": ffn2_w2[0].astype(BF16),
        "final_norm": final_norm[None],
        "mix_norm": mix_norm[0][None],
        "wq": w_in0[:, :D_ATTN].astype(BF16),
        "wkv": w_in0[:, D_ATTN:D_ATTN + 2 * D_KV].astype(BF16),
        "wh": w_in0[:, D_ATTN + 2 * D_KV:].astype(BF16),
        "q_norm": q_norm[0][None], "k_norm": k_norm[0][None],
        "conv_w": conv_w[0], "conv_b": conv_b[0][None],
        "filt_w1": jnp.pad(filt_w1[0], ((0, FILT_PAD - FILTER_EMB), (0, hp))),
        "filt_b1": jnp.pad(filt_b1[0], (0, hp))[None],
        "filt_w2": jnp.pad(filt_w2[0], ((0, hp), (0, hp))),
        "filt_b2": jnp.pad(filt_b2[0], (0, hp))[None],
        "filt_w3": jnp.pad(filt_w3[0], ((0, hp), (0, 0))).reshape(FILT_PAD, 2, 2 * D_HYENA).transpose(1, 0, 2),
        "filt_freq": jnp.pad(filt_freq[0], (0, hp))[None],
        "decay": hyena_decay[0].reshape(2, 1, 2 * D_HYENA),
        "bias0": hyena_bias[0, 0][None], "bias1": hyena_bias[0, 1][None],
        "gon_a": group_out_norm[0, :D_ATTN][None], "gon_h": group_out_norm[0, D_ATTN:][None],
        "wo_a": w_out[0, :D_ATTN].astype(BF16), "wo_h": w_out[0, D_ATTN:].astype(BF16),
    }
    return (_trunk(x_prompt, p), _trunk(x_sample, p))
```

```python
import functools
import math

import jax
import jax.numpy as jnp
import numpy as np
from jax import lax
from jax.experimental import pallas as pl
from jax.experimental.pallas import tpu as pltpu

F32 = jnp.float32
BF16 = jnp.bfloat16
U32 = jnp.uint32

D_MODEL = 2048
GRID_W = 64
D_ATTN = D_MODEL // 2
D_HYENA = D_MODEL - D_ATTN
HEAD_DIM = 128
N_Q_HEADS = D_ATTN // HEAD_DIM
N_KV_HEADS = 2
Q_PER_KV = N_Q_HEADS // N_KV_HEADS
ROPE_HALF = HEAD_DIM // 2
ROPE_THETA = 10000.0
FILTER_EMB = 33
FILTER_HIDDEN = 64
D_FF = 5632
EPS = 1e-6
D_KV = N_KV_HEADS * HEAD_DIM

LANES = 128
FILT_PAD = 128
FFT_N2 = 128
FFT_W = 512
FFT_G = 8
MIB = 1024 * 1024
LOG2_E = math.log2(math.e)


def _cparams(semantics, vmem_mib):
    return pltpu.CompilerParams(dimension_semantics=semantics,
                                vmem_limit_bytes=vmem_mib * MIB)


def _rms(x, g):
    ms = jnp.mean(x * x, axis=-1, keepdims=True)
    return x * lax.rsqrt(ms + EPS) * g


def _resident(shape):
    nd = len(shape)
    return pl.BlockSpec(shape, lambda *_: (0,) * nd, pipeline_mode=pl.Buffered(1))


def _pack2(a, b):
    ua = lax.bitcast_convert_type(a, U32)
    ub = lax.bitcast_convert_type(b, U32)
    half = jnp.uint32(0x8000)
    return ((ua + half) & jnp.uint32(0xFFFF0000)) | ((ub + half) >> 16)


def _unpack_c(w):
    hi = lax.bitcast_convert_type(w & jnp.uint32(0xFFFF0000), F32)
    lo = lax.bitcast_convert_type(w << 16, F32)
    return hi, lo


def _pair_blocks(pb):
    hi = 4 * (pb // 2) + pb % 2
    return hi, hi + 2


def _ffn_body(x_ref, g_ref, wg_ref, wu_ref, w2_ref, gf_ref, o_ref, xn_ref, *, nj, final_norm):
    j = pl.program_id(1)

    @pl.when(j == 0)
    def _init():
        x = x_ref[...]
        xn_ref[...] = _rms(x, g_ref[...]).astype(BF16)
        o_ref[...] = x

    xn = xn_ref[...]
    hg = jnp.dot(xn, wg_ref[...], preferred_element_type=F32)
    hu = jnp.dot(xn, wu_ref[...], preferred_element_type=F32)
    a = (hg * (0.5 / (1.0 + jnp.exp(-hg))) * hu).astype(BF16)
    o_ref[...] += jnp.dot(a, w2_ref[...], preferred_element_type=F32)

    if final_norm:
        @pl.when(j == nj - 1)
        def _fin():
            o_ref[...] = _rms(o_ref[...], gf_ref[...])


def _ffn(x, g, w13, w2, gf, *, final_norm, tm=1024, tf=512):
    T, D = x.shape
    ff = w2.shape[0]
    nj = ff // tf
    body = functools.partial(_ffn_body, nj=nj, final_norm=final_norm)
    return pl.pallas_call(
        body,
        grid=(T // tm, nj),
        in_specs=[
            pl.BlockSpec((tm, D), lambda i, j: (i, 0)),
            pl.BlockSpec((1, D), lambda i, j: (0, 0)),
            pl.BlockSpec((D, tf), lambda i, j: (0, j)),
            pl.BlockSpec((D, tf), lambda i, j: (0, j + nj)),
            pl.BlockSpec((tf, D), lambda i, j: (j, 0)),
            pl.BlockSpec((1, D), lambda i, j: (0, 0)),
        ],
        out_specs=pl.BlockSpec((tm, D), lambda i, j: (i, 0)),
        out_shape=jax.ShapeDtypeStruct((T, D), F32),
        scratch_shapes=[pltpu.VMEM((tm, D), BF16)],
        compiler_params=_cparams(("parallel", "arbitrary"), 58),
        name="ffn",
    )(x, g, w13, w13, w2, gf)


def _store_time_grouped(ref, cb, val):
    n1t = ref.shape[2]
    v = val.reshape(n1t * (FFT_N2 // FFT_G), FFT_G, LANES)
    for a in range(n1t):
        ref[cb, :, a] = v[a * (FFT_N2 // FFT_G):(a + 1) * (FFT_N2 // FFT_G)]


def _load_time_grouped(ref, cb):
    n1t = ref.shape[2]
    return jnp.concatenate([ref[cb, :, a].reshape(FFT_N2, LANES) for a in range(n1t)], axis=0)


MIX_HALO = 16


def _mix_body(xb_ref, x_ref, xa_ref, g_ref, wq_ref, wkv_ref, wh_ref, qn_ref, kn_ref, cos_ref, sin_ref,
              cw_ref, cb_ref, q_ref, k_ref, v_ref, hy_ref, *, npos):
    tm = x_ref.shape[0]
    x_ext = jnp.concatenate([xb_ref[...], x_ref[...], xa_ref[...]], axis=0)
    h_ext = _rms(x_ext, g_ref[...]).astype(BF16)
    h = h_ext[MIX_HALO:MIX_HALO + tm]
    lane = lax.broadcasted_iota(jnp.int32, (tm, HEAD_DIM), 1)
    low_half = (lane % (2 * (ROPE_HALF // 2))) < (ROPE_HALF // 2)
    c = cos_ref[...]
    s = sin_ref[...]

    def head(p, gain, scale):
        y = _rms(p, gain)
        partner = jnp.where(low_half, pltpu.roll(y, HEAD_DIM - ROPE_HALF // 2, 1),
                            pltpu.roll(y, ROPE_HALF // 2, 1))
        y = y * c + partner * s
        return y if scale is None else y * scale

    q = jnp.dot(h, wq_ref[...], preferred_element_type=F32)
    for hd in range(N_Q_HEADS):
        sl = slice(hd * HEAD_DIM, (hd + 1) * HEAD_DIM)
        q_ref[:, sl] = head(q[:, sl], qn_ref[...], HEAD_DIM ** -0.5 * LOG2_E).astype(BF16)
    kv = jnp.dot(h, wkv_ref[...], preferred_element_type=F32)
    for hd in range(N_KV_HEADS):
        sl = slice(hd * HEAD_DIM, (hd + 1) * HEAD_DIM)
        k_ref[:, sl] = head(kv[:, sl], kn_ref[...], None).astype(BF16)
    v_ref[...] = kv[:, D_KV:].astype(BF16)

    hy = jnp.dot(h_ext, wh_ref[...], preferred_element_type=F32)
    pos = pl.program_id(0) % npos
    rid = lax.broadcasted_iota(jnp.int32, (tm, 1), 0)
    before = jnp.where((rid == 0) & (pos == 0), 0.0, hy[MIX_HALO - 1:MIX_HALO - 1 + tm])
    after = jnp.where((rid == tm - 1) & (pos == npos - 1), 0.0, hy[MIX_HALO + 1:MIX_HALO + 1 + tm])
    hyc = (before * cw_ref[0:1, :] + hy[MIX_HALO:MIX_HALO + tm] * cw_ref[1:2, :]
           + after * cw_ref[2:3, :] + cb_ref[...])
    for pb in range(hy_ref.shape[0]):
        hi, lo = _pair_blocks(pb)
        _store_time_grouped(hy_ref, pb, _pack2(hyc[:, hi * LANES:(hi + 1) * LANES],
                                               hyc[:, lo * LANES:(lo + 1) * LANES]))


def _mix_in(x, g, wq, wkv, wh, qn, kn, cos_t, sin_t, cw, cb, *, L, tm=512):
    T, D = x.shape
    npos = L // tm
    ng2 = FFT_N2 // FFT_G
    hb = tm // MIX_HALO
    nhb = T // MIX_HALO
    return pl.pallas_call(
        functools.partial(_mix_body, npos=npos),
        grid=(T // tm,),
        in_specs=[
            pl.BlockSpec((MIX_HALO, D), lambda i: (jnp.maximum(i * hb - 1, 0), 0)),
            pl.BlockSpec((tm, D), lambda i: (i, 0)),
            pl.BlockSpec((MIX_HALO, D), lambda i: (jnp.minimum((i + 1) * hb, nhb - 1), 0)),
            _resident((1, D)),
            _resident(wq.shape),
            _resident(wkv.shape),
            _resident(wh.shape),
            _resident((1, HEAD_DIM)),
            _resident((1, HEAD_DIM)),
            pl.BlockSpec((tm, HEAD_DIM), lambda i: (i % npos, 0)),
            pl.BlockSpec((tm, HEAD_DIM), lambda i: (i % npos, 0)),
            _resident(cw.shape),
            _resident(cb.shape),
        ],
        out_specs=[
            pl.BlockSpec((tm, D_ATTN), lambda i: (i, 0)),
            pl.BlockSpec((tm, D_KV), lambda i: (i, 0)),
            pl.BlockSpec((tm, D_KV), lambda i: (i, 0)),
            pl.BlockSpec((3 * D_HYENA // (2 * LANES), None, ng2, tm // FFT_N2, FFT_G, LANES),
                         lambda i: (0, i // npos, 0, i % npos, 0, 0)),
        ],
        out_shape=[
            jax.ShapeDtypeStruct((T, D_ATTN), BF16),
            jax.ShapeDtypeStruct((T, D_KV), BF16),
            jax.ShapeDtypeStruct((T, D_KV), BF16),
            jax.ShapeDtypeStruct((3 * D_HYENA // (2 * LANES), T // L, ng2, L // FFT_N2, FFT_G, LANES), U32),
        ],
        compiler_params=_cparams(("parallel",), 58),
        name="mix_in",
    )(x, x, x, g, wq, wkv, wh, qn, kn, cos_t, sin_t, cw, cb)


ATTN_ROWS = 32
ATTN_RING = 4
ATTN_ONES = 16


def _attn_body(q_ref, k_ref, v_ref, o_ref, vt_ref, *bufs, tq, tk, nk):
    s_refs, mx_refs, p_refs = (bufs[k * ATTN_RING:(k + 1) * ATTN_RING] for k in range(3))

    @pl.when(pl.program_id(2) == 0)
    def _transpose_v():
        for c in range(nk):
            vt_ref[c, :HEAD_DIM] = v_ref[pl.ds(c * tk, tk), :].astype(F32).T.astype(BF16)
            vt_ref[c, HEAD_DIM:] = jnp.ones((ATTN_ONES, tk), BF16)

    q = jnp.concatenate([q_ref[:, h * HEAD_DIM:(h + 1) * HEAD_DIM] for h in range(Q_PER_KV)], axis=0)
    m_cols = Q_PER_KV * tq
    nchunk = tk // ATTN_ROWS

    def stage_a(i, s_ref, mx_ref):
        kc = k_ref[pl.ds(pl.multiple_of(i * tk, tk), tk), :]
        s = lax.dot_general(kc, q, (((1,), (1,)), ((), ())), preferred_element_type=F32)
        s_ref[...] = s
        mx_ref[...] = jnp.max(s, axis=0, keepdims=True)

    def stage_b(s_ref, mx_ref, p_ref, m):
        m_new = jnp.maximum(m, mx_ref[...])
        alpha = jnp.exp2(m - m_new)
        for c in range(nchunk):
            rows = slice(c * ATTN_ROWS, (c + 1) * ATTN_ROWS)
            p_ref[rows, :] = jnp.exp2((s_ref[rows, :] - m_new).astype(BF16))
        return m_new, alpha

    def stage_d(i, p_ref, alpha, acc):
        return alpha * acc + jnp.dot(vt_ref[i], p_ref[...], preferred_element_type=F32)

    def ring(i, m, al0, al1, acc, n_a, n_b):
        alphas = {0: al0, 1: al1}
        for u in range(ATTN_RING):
            if u < n_a:
                stage_a(i + u + 4, s_refs[u], mx_refs[u])
            if u < n_b:
                v = (u + 2) % ATTN_RING
                m, alphas[u + 2] = stage_b(s_refs[v], mx_refs[v], p_refs[v], m)
            acc = stage_d(i + u, p_refs[u], alphas[u], acc)
        return m, alphas.get(4), alphas.get(5), acc

    m = jnp.full((1, m_cols), -jnp.inf, F32)
    acc = jnp.zeros((HEAD_DIM + ATTN_ONES, m_cols), F32)
    for u in range(ATTN_RING):
        stage_a(u, s_refs[u], mx_refs[u])
    m, al0 = stage_b(s_refs[0], mx_refs[0], p_refs[0], m)
    m, al1 = stage_b(s_refs[1], mx_refs[1], p_refs[1], m)

    def trip(j, carry):
        return ring(ATTN_RING * j, *carry, ATTN_RING, ATTN_RING)

    m, al0, al1, acc = lax.fori_loop(0, nk // ATTN_RING - 1, trip, (m, al0, al1, acc))
    _, _, _, acc = ring(nk - ATTN_RING, m, al0, al1, acc, 0, 2)
    o = acc[:HEAD_DIM] / acc[HEAD_DIM:HEAD_DIM + 1]
    for h in range(Q_PER_KV):
        o_ref[:, h * HEAD_DIM:(h + 1) * HEAD_DIM] = o[:, h * tq:(h + 1) * tq].T


def _attention(q, k, v, *, tq=512, tk=512):
    B, L, _ = q.shape
    qw = Q_PER_KV * HEAD_DIM
    nk = L // tk
    assert nk % ATTN_RING == 0 and nk >= 2 * ATTN_RING
    body = functools.partial(_attn_body, tq=tq, tk=tk, nk=nk)
    m_cols = Q_PER_KV * tq
    return pl.pallas_call(
        body,
        grid=(B, N_KV_HEADS, L // tq),
        in_specs=[
            pl.BlockSpec((None, tq, qw), lambda b, g, i: (b, i, g)),
            pl.BlockSpec((None, L, HEAD_DIM), lambda b, g, i: (b, 0, g)),
            pl.BlockSpec((None, L, HEAD_DIM), lambda b, g, i: (b, 0, g)),
        ],
        out_specs=pl.BlockSpec((None, tq, qw), lambda b, g, i: (b, i, g)),
        out_shape=jax.ShapeDtypeStruct((B, L, D_ATTN), F32),
        scratch_shapes=([pltpu.VMEM((nk, HEAD_DIM + ATTN_ONES, tk), BF16)]
                        + [pltpu.VMEM((tk, m_cols), F32)] * ATTN_RING
                        + [pltpu.VMEM((1, m_cols), F32)] * ATTN_RING
                        + [pltpu.VMEM((tk, m_cols), BF16)] * ATTN_RING),
        compiler_params=_cparams(("parallel", "parallel", "arbitrary"), 48),
        name="attn",
    )(q, k, v)


def _filt_body(feat_ref, w1_ref, b1_ref, w2_ref, b2_ref, w3_ref, fq_ref, dec_ref, kf_ref, asum_ref, *, nt):
    i = pl.program_id(0)
    hp = lax.Precision.HIGHEST
    fq = fq_ref[...]
    feat = feat_ref[...]
    h = jnp.sin(fq * (jnp.dot(feat, w1_ref[...], precision=hp, preferred_element_type=F32) + b1_ref[...]))
    h = jnp.sin(fq * (jnp.dot(h, w2_ref[...], precision=hp, preferred_element_type=F32) + b2_ref[...]))
    h3 = jnp.dot(h, w3_ref[...], precision=hp, preferred_element_type=F32)
    t01 = feat[:, 0:1]
    taps = h3 * jnp.exp(-t01 * jnp.abs(dec_ref[...]))
    row = lax.broadcasted_iota(jnp.int32, (taps.shape[0], 1), 0)
    taps = jnp.where((row == 0) & (i == nt), 0.0, taps)
    for cb in range(taps.shape[1] // LANES):
        _store_time_grouped(kf_ref, cb, taps[:, cb * LANES:(cb + 1) * LANES])
    part = jnp.sum(jnp.abs(taps), axis=0, keepdims=True)

    @pl.when(i == 0)
    def _first():
        asum_ref[...] = part

    @pl.when(i > 0)
    def _rest():
        asum_ref[...] += part


def _filters(feats2, w1, b1, w2, b2, w3, fq, dec, *, tl=512):
    _, L, _ = feats2.shape
    nc = w3.shape[2]
    ng2 = FFT_N2 // FFT_G
    nt = L // tl
    body = functools.partial(_filt_body, nt=nt)
    return pl.pallas_call(
        body,
        grid=(2 * nt,),
        in_specs=[
            pl.BlockSpec((None, tl, FILT_PAD), lambda i: (i // nt, i % nt, 0)),
            _resident(w1.shape), _resident(b1.shape), _resident(w2.shape), _resident(b2.shape),
            pl.BlockSpec((None, FILT_PAD, nc), lambda i: (i // nt, 0, 0)),
            _resident(fq.shape),
            pl.BlockSpec((None, 1, nc), lambda i: (i // nt, 0, 0)),
        ],
        out_specs=[
            pl.BlockSpec((nc // LANES, ng2, tl // FFT_N2, FFT_G, LANES), lambda i: (0, 0, i, 0, 0)),
            pl.BlockSpec((1, nc), lambda i: (0, 0)),
        ],
        out_shape=[jax.ShapeDtypeStruct((nc // LANES, ng2, 2 * L // FFT_N2, FFT_G, LANES), F32),
                   jax.ShapeDtypeStruct((1, nc), F32)],
        compiler_params=_cparams(("arbitrary",), 48),
        name="filt",
    )(feats2, w1, b1, w2, b2, w3, fq, dec)


FFT_CB = FFT_W // LANES
FFT_CB_B = 2


def _pack_c(y):
    n = y.shape[0] // 2
    return _pack2(y[:n], y[n:])


def _unpack_lanes(w):
    hi, lo = _unpack_c(w)
    return jnp.concatenate([hi, lo], axis=1)


def _unpack_rows(w):
    re, im = _unpack_c(w)
    return jnp.concatenate([re, im], axis=0).astype(BF16)


def _flat_scratch(n, dtype, cbs, lead=()):
    return pltpu.VMEM(lead + (cbs, n * FFT_G, LANES), dtype)


def _flatten(flat_ref, block_ref, pre=()):
    rows = flat_ref.shape[-2]
    for h in range(block_ref.shape[0]):
        flat_ref[pre + (h,)] = block_ref[(h,) + pre].reshape(rows, LANES)


def _unflatten(block_ref, flat_ref, pre=()):
    n = flat_ref.shape[-2] // FFT_G
    for h in range(block_ref.shape[0]):
        block_ref[(h,) + pre] = flat_ref[pre + (h,)].reshape(n, FFT_G, LANES)


def _gather_rows(flat_ref, j, pre=()):
    n = flat_ref.shape[-2] // FFT_G
    return jnp.concatenate(
        [flat_ref[pre + (h, pl.ds(j, n, stride=FFT_G), slice(None))] for h in range(flat_ref.shape[len(pre)])],
        axis=1)


def _scatter_rows(flat_ref, j, val, pre=()):
    n = flat_ref.shape[-2] // FFT_G
    for h in range(flat_ref.shape[len(pre)]):
        flat_ref[pre + (h, pl.ds(j, n, stride=FFT_G), slice(None))] = val[:, h * LANES:(h + 1) * LANES]


def _store_blocks(ref, j, val):
    for h in range(ref.shape[0]):
        ref[h, j] = val[:, h * LANES:(h + 1) * LANES]


def _store_grouped(ref, j, val):
    groups = ref.shape[1]
    for h in range(ref.shape[0]):
        ref[h, :, j] = val[:, h * LANES:(h + 1) * LANES].reshape(groups, FFT_G, LANES)


def _fft_a_body(x_ref, fa_ref, o_ref, xf_ref):
    for b in range(2):
        _flatten(xf_ref, x_ref, (b,))
    for j in range(FFT_G):
        xs = jnp.concatenate([_unpack_lanes(_gather_rows(xf_ref, j, (b,))) for b in range(2)],
                             axis=0).astype(BF16)
        _store_grouped(o_ref, j, _pack_c(jnp.dot(fa_ref[j], xs, preferred_element_type=F32)))


def _fft_a(xv, fa, *, ncb, pb0=0):
    _, P, _, ng2, n1h, _, _ = xv.shape
    n2 = ng2 * FFT_G
    two_n1 = fa.shape[1]
    n1 = two_n1 // 2
    pbs = FFT_CB // 2
    c0 = pb0 // pbs
    return pl.pallas_call(
        _fft_a_body,
        grid=(ng2, P, ncb // FFT_CB),
        in_specs=[
            pl.BlockSpec((pbs, None, 2, None, n1h, FFT_G, LANES), lambda g, p, c: (c + c0, p, 0, g, 0, 0, 0)),
            pl.BlockSpec((FFT_G, two_n1, 2 * n1h), lambda g, p, c: (g, 0, 0)),
        ],
        out_specs=pl.BlockSpec((None, FFT_CB, n1 // FFT_G, FFT_G, FFT_G, LANES),
                               lambda g, p, c: (p, c, 0, g, 0, 0)),
        out_shape=jax.ShapeDtypeStruct((P, ncb, n1 // FFT_G, n2, FFT_G, LANES), U32),
        scratch_shapes=[_flat_scratch(n1h, U32, pbs, (2,))],
        compiler_params=_cparams(("arbitrary", "arbitrary", "arbitrary"), 32),
        name="fft_a",
    )(xv, fa)


def _fft_a_real_body(x_ref, fa_ref, o_ref, xf_ref):
    _flatten(xf_ref, x_ref)
    for j in range(FFT_G):
        xs = _gather_rows(xf_ref, j).astype(BF16)
        _store_grouped(o_ref, j, _pack_c(jnp.dot(fa_ref[j], xs, preferred_element_type=F32)))


def _fft_a_real(xv, fa_re):
    ncb, ng2, n1, _, _ = xv.shape
    n2 = ng2 * FFT_G
    return pl.pallas_call(
        _fft_a_real_body,
        grid=(ng2, ncb // FFT_CB),
        in_specs=[
            pl.BlockSpec((FFT_CB, None, n1, FFT_G, LANES), lambda g, c: (c, g, 0, 0, 0)),
            pl.BlockSpec((FFT_G, 2 * n1, n1), lambda g, c: (g, 0, 0)),
        ],
        out_specs=pl.BlockSpec((FFT_CB, n1 // FFT_G, FFT_G, FFT_G, LANES), lambda g, c: (c, 0, g, 0, 0)),
        out_shape=jax.ShapeDtypeStruct((ncb, n1 // FFT_G, n2, FFT_G, LANES), U32),
        scratch_shapes=[_flat_scratch(n1, F32, FFT_CB)],
        compiler_params=_cparams(("arbitrary", "arbitrary"), 32),
        name="fft_a_real",
    )(xv, fa_re)


def _fft_b_filt_body(s_ref, fb_ref, as_ref, o_ref, sf_ref):
    _flatten(sf_ref, s_ref)
    inv = 1.0 / as_ref[...]
    fb = fb_ref[...]
    for j in range(FFT_G):
        k = jnp.dot(fb, _unpack_rows(_gather_rows(sf_ref, j)), preferred_element_type=F32)
        _store_blocks(o_ref, j, _pack_c(k * inv))


def _fft_b_filt(s1f, fb, asum):
    ncb, ng1, n2, _, _ = s1f.shape
    n1 = ng1 * FFT_G
    return pl.pallas_call(
        _fft_b_filt_body,
        grid=(ng1, ncb // FFT_CB),
        in_specs=[
            pl.BlockSpec((FFT_CB, None, n2, FFT_G, LANES), lambda g, c: (c, g, 0, 0, 0)),
            _resident(fb.shape),
            pl.BlockSpec((1, FFT_W), lambda g, c: (0, c)),
        ],
        out_specs=pl.BlockSpec((FFT_CB, FFT_G, n2, LANES), lambda g, c: (c, g, 0, 0)),
        out_shape=jax.ShapeDtypeStruct((ncb, n1, n2, LANES), U32),
        scratch_shapes=[_flat_scratch(n2, U32, FFT_CB)],
        compiler_params=_cparams(("parallel", "parallel"), 32),
        name="fft_b_filt",
    )(s1f, fb, asum)


def _fft_b_body(s_ref, kf_ref, fb_ref, fbi_ref, o_ref, sf_ref):
    n2 = FFT_N2
    _flatten(sf_ref, s_ref)
    fb = fb_ref[...]
    fbi = fbi_ref[...]
    for j in range(FFT_G):
        x = jnp.dot(fb, _unpack_rows(_gather_rows(sf_ref, j)), preferred_element_type=F32)
        xr, xi = x[:n2], x[n2:]
        kr, ki = _unpack_c(jnp.concatenate([kf_ref[h, j] for h in range(kf_ref.shape[0])], axis=1))
        pm = jnp.concatenate([xr * kr - xi * ki, xr * ki + xi * kr], axis=0).astype(BF16)
        _store_grouped(o_ref, j, _pack_c(jnp.dot(fbi, pm, preferred_element_type=F32)))


def _fft_b(s1, kf, fb, fbi, *, kcb0):
    P, ncb, ng1, n2, _, _ = s1.shape
    n1 = ng1 * FFT_G
    cbs = FFT_CB_B
    k0 = kcb0 // cbs
    return pl.pallas_call(
        _fft_b_body,
        grid=(ng1, ncb // cbs, P),
        in_specs=[
            pl.BlockSpec((None, cbs, None, n2, FFT_G, LANES), lambda g, c, p: (p, c, g, 0, 0, 0)),
            pl.BlockSpec((cbs, FFT_G, n2, LANES), lambda g, c, p: (c + k0, g, 0, 0)),
            _resident(fb.shape),
            _resident(fbi.shape),
        ],
        out_specs=pl.BlockSpec((None, cbs, n2 // FFT_G, FFT_G, FFT_G, LANES),
                               lambda g, c, p: (p, c, 0, g, 0, 0)),
        out_shape=jax.ShapeDtypeStruct((P, ncb, n2 // FFT_G, n1, FFT_G, LANES), U32),
        scratch_shapes=[_flat_scratch(n2, U32, cbs)],
        compiler_params=_cparams(("parallel", "parallel", "arbitrary"), 32),
        name="fft_b",
    )(s1, kf, fb, fbi)


def _fft_c_body(z_ref, fc_ref, u_ref, gate_ref, d_ref, *rest, chain):
    if chain:
        fa_ref, o_ref, s_ref, zf_ref, uf_ref, gf_ref, of_ref = rest
    else:
        o_ref, zf_ref, uf_ref, gf_ref, of_ref = rest
    n1h = o_ref.shape[2]
    d = d_ref[...]
    _flatten(zf_ref, z_ref)
    for b in range(2):
        _flatten(uf_ref, u_ref, (b,))
        _flatten(gf_ref, gate_ref, (b,))
    for j in range(FFT_G):
        y = jnp.dot(fc_ref[j], _unpack_rows(_gather_rows(zf_ref, j)), preferred_element_type=F32)
        res = []
        for b in range(2):
            yb = y[b * n1h:(b + 1) * n1h]
            gate = _unpack_lanes(_gather_rows(gf_ref, j, (b,)))
            skip = _unpack_lanes(_gather_rows(uf_ref, j, (b,)))
            res.append(gate * (yb + skip * d))
            if chain:
                half = res[b].shape[1] // 2
                _scatter_rows(of_ref, j, _pack2(res[b][:, :half], res[b][:, half:]), (b,))
            else:
                _scatter_rows(of_ref, j, res[b], (b,))
        if chain:
            xs = jnp.concatenate(res, axis=0).astype(BF16)
            _store_grouped(s_ref, j, _pack_c(jnp.dot(fa_ref[j], xs, preferred_element_type=F32)))
    for b in range(2):
        _unflatten(o_ref, of_ref, (b,))


def _fft_c(s2, fc, uv, gv, d, *, upb0, gpb0, fa=None):
    P, ncb, ng2, n1, _, _ = s2.shape
    n1h = n1 // 2
    n2 = ng2 * FFT_G
    pbs = FFT_CB // 2
    u0 = upb0 // pbs
    g0 = gpb0 // pbs
    chain = fa is not None
    pblk = (pbs, None, 2, None, n1h, FFT_G, LANES)
    ocb, odt = (pbs, U32) if chain else (FFT_CB, F32)
    in_specs = [
        pl.BlockSpec((None, FFT_CB, None, n1, FFT_G, LANES), lambda g, p, c: (p, c, g, 0, 0, 0)),
        pl.BlockSpec((FFT_G, 2 * n1h, 2 * n1), lambda g, p, c: (g, 0, 0)),
        pl.BlockSpec(pblk, lambda g, p, c: (c + u0, p, 0, g, 0, 0, 0)),
        pl.BlockSpec(pblk, lambda g, p, c: (c + g0, p, 0, g, 0, 0, 0)),
        pl.BlockSpec((1, FFT_W), lambda g, p, c: (0, c)),
    ]
    out_specs = [pl.BlockSpec((ocb, None, 2, None, n1h, FFT_G, LANES), lambda g, p, c: (c, p, 0, g, 0, 0, 0))]
    out_shape = [jax.ShapeDtypeStruct((ncb * ocb // FFT_CB, P, 2, ng2, n1h, FFT_G, LANES), odt)]
    args = [s2, fc, uv, gv, d]
    if chain:
        in_specs.append(pl.BlockSpec((FFT_G, 2 * n1, 2 * n1h), lambda g, p, c: (g, 0, 0)))
        out_specs.append(pl.BlockSpec((None, FFT_CB, n1 // FFT_G, FFT_G, FFT_G, LANES),
                                      lambda g, p, c: (p, c, 0, g, 0, 0)))
        out_shape.append(jax.ShapeDtypeStruct((P, ncb, n1 // FFT_G, n2, FFT_G, LANES), U32))
        args.append(fa)
    return pl.pallas_call(
        functools.partial(_fft_c_body, chain=chain),
        grid=(ng2, P, ncb // FFT_CB),
        in_specs=in_specs,
        out_specs=out_specs,
        out_shape=out_shape,
        scratch_shapes=[_flat_scratch(n1, U32, FFT_CB), _flat_scratch(n1h, U32, pbs, (2,)),
                        _flat_scratch(n1h, U32, pbs, (2,)), _flat_scratch(n1h, odt, ocb, (2,))],
        compiler_params=_cparams(("arbitrary", "arbitrary", "arbitrary"), 40),
        name="fft_c",
    )(*args)


def _oproj_body(x_ref, a_ref, h_ref, ga_ref, gh_ref, wa_ref, wh_ref, o_ref):
    ma = _rms(a_ref[...], ga_ref[...]).astype(BF16)
    hy = jnp.concatenate([_load_time_grouped(h_ref, cb) for cb in range(h_ref.shape[0])], axis=1)
    mh = _rms(hy, gh_ref[...]).astype(BF16)
    o_ref[...] = (x_ref[...] + jnp.dot(ma, wa_ref[...], preferred_element_type=F32)
                  + jnp.dot(mh, wh_ref[...], preferred_element_type=F32))


def _out_proj(x, a, h, ga, gh, wa, wh, *, tm=512):
    T, D = x.shape
    ncb, _, ng2, n1h, _, _ = h.shape
    npos = n1h * FFT_N2 // tm
    return pl.pallas_call(
        _oproj_body,
        grid=(T // tm,),
        in_specs=[
            pl.BlockSpec((tm, D), lambda i: (i, 0)),
            pl.BlockSpec((tm, D_ATTN), lambda i: (i, 0)),
            pl.BlockSpec((ncb, None, ng2, tm // FFT_N2, FFT_G, LANES), lambda i: (0, i // npos, 0, i % npos, 0, 0)),
            _resident((1, D_ATTN)), _resident((1, D_HYENA)),
            _resident(wa.shape), _resident(wh.shape),
        ],
        out_specs=pl.BlockSpec((tm, D), lambda i: (i, 0)),
        out_shape=jax.ShapeDtypeStruct((T, D), F32),
        compiler_params=_cparams(("parallel",), 48),
        name="oproj",
    )(x, a, h, ga, gh, wa, wh)


@functools.lru_cache(maxsize=None)
def _rope_tables_np(L):
    rows = L // GRID_W
    row = np.repeat(np.arange(rows, dtype=np.float32), GRID_W)
    col = np.tile(np.arange(GRID_W, dtype=np.float32), rows)
    inv = (ROPE_THETA ** (-np.arange(0, ROPE_HALF, 2, dtype=np.float32) / ROPE_HALF)).astype(np.float32)
    ar = (row[:, None] * inv[None]).astype(np.float64)
    ac = (col[:, None] * inv[None]).astype(np.float64)
    cr, sr, cc, sc = np.cos(ar), np.sin(ar), np.cos(ac), np.sin(ac)
    cos_t = np.concatenate([cr, cr, cc, cc], axis=-1).astype(np.float32)
    sin_t = np.concatenate([-sr, sr, -sc, sc], axis=-1).astype(np.float32)
    return cos_t, sin_t


def _rope_tables(L):
    return tuple(jnp.asarray(t) for t in _rope_tables_np(L))


@functools.lru_cache(maxsize=None)
def _filter_feats_np(L):
    t01 = np.linspace(0.0, 1.0, L, dtype=np.float32)[:, None]
    bands = (FILTER_EMB - 1) // 2
    fr = np.linspace(1e-4, bands - 1, bands, dtype=np.float32)[None]
    w = (np.float32(2.0 * math.pi) * np.arange(L, dtype=np.float32)[:, None] / np.float32(L)).astype(np.float32)
    arg = (fr * w).astype(np.float32).astype(np.float64)
    feats = np.concatenate([t01, np.cos(arg), -np.sin(arg)], axis=-1).astype(np.float32)
    feats = np.pad(feats, ((0, 0), (0, FILT_PAD - FILTER_EMB)))
    mirrored = np.concatenate([feats[:1], feats[1:][::-1]], axis=0)
    return np.stack([feats, mirrored])


def _filter_feats(L):
    return jnp.asarray(_filter_feats_np(L))


@functools.lru_cache(maxsize=None)
def _dft_tables_np(L):
    n = 2 * L
    n2 = FFT_N2
    n1 = n // n2
    n1h = n1 // 2
    k1 = np.arange(n1, dtype=np.int64)[None, :, None]
    t1 = np.arange(n1, dtype=np.int64)[None, None, :]
    t2 = np.arange(n2, dtype=np.int64)[:, None, None]
    ang = (-2.0 * math.pi / n) * ((k1 * (n2 * t1 + t2)) % n)
    cr_full, ci_full = np.cos(ang), np.sin(ang)
    fa_re = np.concatenate([cr_full, ci_full], -2)
    cr, ci = cr_full[..., :n1h], ci_full[..., :n1h]
    fa = np.concatenate([np.concatenate([cr, -ci], -1), np.concatenate([ci, cr], -1)], -2)
    crt, cit = np.swapaxes(cr, 1, 2), np.swapaxes(ci, 1, 2)
    fc = np.concatenate([np.concatenate([crt, cit], -1), np.concatenate([-cit, crt], -1)], -2) / n
    a = np.arange(n2, dtype=np.int64)
    ang2 = (-2.0 * math.pi / n2) * ((a[:, None] * a[None, :]) % n2)
    fr, fi = np.cos(ang2), np.sin(ang2)
    fb = np.concatenate([np.concatenate([fr, -fi], -1), np.concatenate([fi, fr], -1)], -2)
    fbi = np.concatenate([np.concatenate([fr, fi], -1), np.concatenate([-fi, fr], -1)], -2)
    return tuple(t.astype(np.float32) for t in (fa, fa_re, fb, fbi, fc))


def _dft_tables(L):
    return tuple(jnp.asarray(t, dtype=BF16) for t in _dft_tables_np(L))


def _trunk(x, p):
    B, L, D = x.shape
    T = B * L
    P = B // 2
    n2 = FFT_N2
    n1h = L // n2
    C = D_HYENA
    ncb = C // LANES

    x0 = x.reshape(T, D)
    x1 = _ffn(x0, p["ffn1_norm"], p["ffn1_w13"], p["ffn1_w2"], p["final_norm"], final_norm=False)

    cos_t, sin_t = _rope_tables(L)
    q, k, v, hyc = _mix_in(x1, p["mix_norm"], p["wq"], p["wkv"], p["wh"], p["q_norm"], p["k_norm"],
                           cos_t, sin_t, p["conv_w"], p["conv_b"], L=L)
    attn = _attention(q.reshape(B, L, D_ATTN), k.reshape(B, L, D_KV), v.reshape(B, L, D_KV))
    ng2 = n2 // FFT_G
    npb = ncb // 2
    hyv = hyc.reshape(3 * npb, P, 2, ng2, n1h, FFT_G, LANES)

    fa, fa_re, fb, fbi, fc = _dft_tables(L)
    taps, asum = _filters(_filter_feats(L), p["filt_w1"], p["filt_b1"], p["filt_w2"], p["filt_b2"],
                          p["filt_w3"], p["filt_freq"], p["decay"])
    kf = _fft_b_filt(_fft_a_real(taps, fa_re), fb, asum)

    s1 = _fft_a(hyv, fa, ncb=ncb, pb0=0)
    s2 = _fft_b(s1, kf, fb, fbi, kcb0=0)
    z, s1 = _fft_c(s2, fc, hyv, hyv, p["bias0"], upb0=0, gpb0=npb, fa=fa)
    s2 = _fft_b(s1, kf, fb, fbi, kcb0=ncb)
    ho, = _fft_c(s2, fc, z, hyv, p["bias1"], upb0=0, gpb0=2 * npb)

    x2 = _out_proj(x1, attn.reshape(T, D_ATTN), ho.reshape(ncb, B, ng2, n1h, FFT_G, LANES),
                   p["gon_a"], p["gon_h"], p["wo_a"], p["wo_h"])
    x3 = _ffn(x2, p["ffn2_norm"], p["ffn2_w13"], p["ffn2_w2"], p["final_norm"], final_norm=True)
    return x3.reshape(B, L, D)


def kernel(x_prompt, x_sample, ffn1_norm, ffn1_w13, ffn1_w2, mix_norm, w_in, q_norm, k_norm, conv_w, conv_b, filt_w1, filt_b1, filt_w2, filt_b2, filt_w3, filt_freq, hyena_decay, hyena_bias, group_out_norm, w_out, ffn2_norm, ffn2_w13, ffn2_w2, final_norm):
    hp = FILT_PAD - FILTER_HIDDEN
    w_in0 = w_in[0]
    p = {
        "ffn1_norm": ffn1_norm[0][None], "ffn2_norm": ffn2_norm[0][None],
        "ffn1_w13": ffn1_w13[0].astype(BF16), "ffn1_w2": ffn1_w2[0].astype(BF16),
        "ffn2_w13": ffn2_w13[0].astype(BF16), "ffn2_w2": ffn2_w2[0].astype(BF16),
        "final_norm": final_norm[None],
        "mix_norm": mix_norm[0][None],
        "wq": w_in0[:, :D_ATTN].astype(BF16),
        "wkv": w_in0[:, D_ATTN:D_ATTN + 2 * D_KV].astype(BF16),
        "wh": w_in0[:, D_ATTN + 2 * D_KV:].astype(BF16),
        "q_norm": q_norm[0][None], "k_norm": k_norm[0][None],
        "conv_w": conv_w[0], "conv_b": conv_b[0][None],
        "filt_w1": jnp.pad(filt_w1[0], ((0, FILT_PAD - FILTER_EMB), (0, hp))),
        "filt_b1": jnp.pad(filt_b1[0], (0, hp))[None],
        "filt_w2": jnp.pad(filt_w2[0], ((0, hp), (0, hp))),
        "filt_b2": jnp.pad(filt_b2[0], (0, hp))[None],
        "filt_w3": jnp.pad(filt_w3[0], ((0, hp), (0, 0))).reshape(FILT_PAD, 2, 2 * D_HYENA).transpose(1, 0, 2),
        "filt_freq": jnp.pad(filt_freq[0], (0, hp))[None],
        "decay": hyena_decay[0].reshape(2, 1, 2 * D_HYENA),
        "bias0": hyena_bias[0, 0][None], "bias1": hyena_bias[0, 1][None],
        "gon_a": group_out_norm[0, :D_ATTN][None], "gon_h": group_out_norm[0, D_ATTN:][None],
        "wo_a": w_out[0, :D_ATTN].astype(BF16), "wo_h": w_out[0, D_ATTN:].astype(BF16),
    }
    return (_trunk(x_prompt, p), _trunk(x_sample, p))
```

```python
import functools
import math

import jax
import jax.numpy as jnp
import numpy as np
from jax import lax
from jax.experimental import pallas as pl
from jax.experimental.pallas import tpu as pltpu

F32 = jnp.float32
BF16 = jnp.bfloat16
U32 = jnp.uint32

D_MODEL = 2048
GRID_W = 64
D_ATTN = D_MODEL // 2
D_HYENA = D_MODEL - D_ATTN
HEAD_DIM = 128
N_Q_HEADS = D_ATTN // HEAD_DIM
N_KV_HEADS = 2
Q_PER_KV = N_Q_HEADS // N_KV_HEADS
ROPE_HALF = HEAD_DIM // 2
ROPE_THETA = 10000.0
FILTER_EMB = 33
FILTER_HIDDEN = 64
D_FF = 5632
EPS = 1e-6
D_KV = N_KV_HEADS * HEAD_DIM

LANES = 128
FILT_PAD = 128
FFT_N2 = 128
FFT_W = 512
FFT_G = 8
MIB = 1024 * 1024
LOG2_E = math.log2(math.e)


def _cparams(semantics, vmem_mib):
    return pltpu.CompilerParams(dimension_semantics=semantics,
                                vmem_limit_bytes=vmem_mib * MIB)


def _rms(x, g):
    ms = jnp.mean(x * x, axis=-1, keepdims=True)
    return x * lax.rsqrt(ms + EPS) * g


def _resident(shape):
    nd = len(shape)
    return pl.BlockSpec(shape, lambda *_: (0,) * nd, pipeline_mode=pl.Buffered(1))


def _pack2(a, b):
    ua = lax.bitcast_convert_type(a, U32)
    ub = lax.bitcast_convert_type(b, U32)
    half = jnp.uint32(0x8000)
    return ((ua + half) & jnp.uint32(0xFFFF0000)) | ((ub + half) >> 16)


def _unpack_c(w):
    hi = lax.bitcast_convert_type(w & jnp.uint32(0xFFFF0000), F32)
    lo = lax.bitcast_convert_type(w << 16, F32)
    return hi, lo


def _pair_blocks(pb):
    hi = 4 * (pb // 2) + pb % 2
    return hi, hi + 2


def _ffn_body(x_ref, g_ref, wg_ref, wu_ref, w2_ref, gf_ref, o_ref, xn_ref, *, nj, final_norm):
    j = pl.program_id(1)

    @pl.when(j == 0)
    def _init():
        x = x_ref[...]
        xn_ref[...] = _rms(x, g_ref[...]).astype(BF16)
        o_ref[...] = x

    xn = xn_ref[...]
    hg = jnp.dot(xn, wg_ref[...], preferred_element_type=F32)
    hu = jnp.dot(xn, wu_ref[...], preferred_element_type=F32)
    a = (hg * (0.5 / (1.0 + jnp.exp(-hg))) * hu).astype(BF16)
    o_ref[...] += jnp.dot(a, w2_ref[...], preferred_element_type=F32)

    if final_norm:
        @pl.when(j == nj - 1)
        def _fin():
            o_ref[...] = _rms(o_ref[...], gf_ref[...])


def _ffn(x, g, w13, w2, gf, *, final_norm, tm=1024, tf=512):
    T, D = x.shape
    ff = w2.shape[0]
    nj = ff // tf
    body = functools.partial(_ffn_body, nj=nj, final_norm=final_norm)
    return pl.pallas_call(
        body,
        grid=(T // tm, nj),
        in_specs=[
            pl.BlockSpec((tm, D), lambda i, j: (i, 0)),
            pl.BlockSpec((1, D), lambda i, j: (0, 0)),
            pl.BlockSpec((D, tf), lambda i, j: (0, j)),
            pl.BlockSpec((D, tf), lambda i, j: (0, j + nj)),
            pl.BlockSpec((tf, D), lambda i, j: (j, 0)),
            pl.BlockSpec((1, D), lambda i, j: (0, 0)),
        ],
        out_specs=pl.BlockSpec((tm, D), lambda i, j: (i, 0)),
        out_shape=jax.ShapeDtypeStruct((T, D), F32),
        scratch_shapes=[pltpu.VMEM((tm, D), BF16)],
        compiler_params=_cparams(("parallel", "arbitrary"), 58),
        name="ffn",
    )(x, g, w13, w13, w2, gf)


def _store_time_grouped(ref, cb, val):
    n1t = ref.shape[2]
    v = val.reshape(n1t * (FFT_N2 // FFT_G), FFT_G, LANES)
    for a in range(n1t):
        ref[cb, :, a] = v[a * (FFT_N2 // FFT_G):(a + 1) * (FFT_N2 // FFT_G)]


def _load_time_grouped(ref, cb):
    n1t = ref.shape[2]
    return jnp.concatenate([ref[cb, :, a].reshape(FFT_N2, LANES) for a in range(n1t)], axis=0)


MIX_HALO = 16


def _mix_body(xb_ref, x_ref, xa_ref, g_ref, wq_ref, wkv_ref, wh_ref, qn_ref, kn_ref, cos_ref, sin_ref,
              cw_ref, cb_ref, q_ref, k_ref, v_ref, hy_ref, *, npos):
    tm = x_ref.shape[0]
    x_ext = jnp.concatenate([xb_ref[...], x_ref[...], xa_ref[...]], axis=0)
    h_ext = _rms(x_ext, g_ref[...]).astype(BF16)
    h = h_ext[MIX_HALO:MIX_HALO + tm]
    lane = lax.broadcasted_iota(jnp.int32, (tm, HEAD_DIM), 1)
    low_half = (lane % (2 * (ROPE_HALF // 2))) < (ROPE_HALF // 2)
    c = cos_ref[...]
    s = sin_ref[...]

    def head(p, gain, scale):
        y = _rms(p, gain)
        partner = jnp.where(low_half, pltpu.roll(y, HEAD_DIM - ROPE_HALF // 2, 1),
                            pltpu.roll(y, ROPE_HALF // 2, 1))
        y = y * c + partner * s
        return y if scale is None else y * scale

    q = jnp.dot(h, wq_ref[...], preferred_element_type=F32)
    for hd in range(N_Q_HEADS):
        sl = slice(hd * HEAD_DIM, (hd + 1) * HEAD_DIM)
        q_ref[:, sl] = head(q[:, sl], qn_ref[...], HEAD_DIM ** -0.5 * LOG2_E).astype(BF16)
    kv = jnp.dot(h, wkv_ref[...], preferred_element_type=F32)
    for hd in range(N_KV_HEADS):
        sl = slice(hd * HEAD_DIM, (hd + 1) * HEAD_DIM)
        k_ref[:, sl] = head(kv[:, sl], kn_ref[...], None).astype(BF16)
    v_ref[...] = kv[:, D_KV:].astype(BF16)

    hy = jnp.dot(h_ext, wh_ref[...], preferred_element_type=F32)
    pos = pl.program_id(0) % npos
    rid = lax.broadcasted_iota(jnp.int32, (tm, 1), 0)
    before = jnp.where((rid == 0) & (pos == 0), 0.0, hy[MIX_HALO - 1:MIX_HALO - 1 + tm])
    after = jnp.where((rid == tm - 1) & (pos == npos - 1), 0.0, hy[MIX_HALO + 1:MIX_HALO + 1 + tm])
    hyc = (before * cw_ref[0:1, :] + hy[MIX_HALO:MIX_HALO + tm] * cw_ref[1:2, :]
           + after * cw_ref[2:3, :] + cb_ref[...])
    for pb in range(hy_ref.shape[0]):
        hi, lo = _pair_blocks(pb)
        _store_time_grouped(hy_ref, pb, _pack2(hyc[:, hi * LANES:(hi + 1) * LANES],
                                               hyc[:, lo * LANES:(lo + 1) * LANES]))


def _mix_in(x, g, wq, wkv, wh, qn, kn, cos_t, sin_t, cw, cb, *, L, tm=512):
    T, D = x.shape
    npos = L // tm
    ng2 = FFT_N2 // FFT_G
    hb = tm // MIX_HALO
    nhb = T // MIX_HALO
    return pl.pallas_call(
        functools.partial(_mix_body, npos=npos),
        grid=(T // tm,),
        in_specs=[
            pl.BlockSpec((MIX_HALO, D), lambda i: (jnp.maximum(i * hb - 1, 0), 0)),
            pl.BlockSpec((tm, D), lambda i: (i, 0)),
            pl.BlockSpec((MIX_HALO, D), lambda i: (jnp.minimum((i + 1) * hb, nhb - 1), 0)),
            _resident((1, D)),
            _resident(wq.shape),
            _resident(wkv.shape),
            _resident(wh.shape),
            _resident((1, HEAD_DIM)),
            _resident((1, HEAD_DIM)),
            pl.BlockSpec((tm, HEAD_DIM), lambda i: (i % npos, 0)),
            pl.BlockSpec((tm, HEAD_DIM), lambda i: (i % npos, 0)),
            _resident(cw.shape),
            _resident(cb.shape),
        ],
        out_specs=[
            pl.BlockSpec((tm, D_ATTN), lambda i: (i, 0)),
            pl.BlockSpec((tm, D_KV), lambda i: (i, 0)),
            pl.BlockSpec((tm, D_KV), lambda i: (i, 0)),
            pl.BlockSpec((3 * D_HYENA // (2 * LANES), None, ng2, tm // FFT_N2, FFT_G, LANES),
                         lambda i: (0, i // npos, 0, i % npos, 0, 0)),
        ],
        out_shape=[
            jax.ShapeDtypeStruct((T, D_ATTN), BF16),
            jax.ShapeDtypeStruct((T, D_KV), BF16),
            jax.ShapeDtypeStruct((T, D_KV), BF16),
            jax.ShapeDtypeStruct((3 * D_HYENA // (2 * LANES), T // L, ng2, L // FFT_N2, FFT_G, LANES), U32),
        ],
        compiler_params=_cparams(("parallel",), 58),
        name="mix_in",
    )(x, x, x, g, wq, wkv, wh, qn, kn, cos_t, sin_t, cw, cb)


ATTN_ROWS = 32
ATTN_RING = 4
ATTN_ONES = 16


def _attn_body(q_ref, k_ref, v_ref, o_ref, vt_ref, *bufs, tq, tk, nk):
    s_refs, mx_refs, p_refs = (bufs[k * ATTN_RING:(k + 1) * ATTN_RING] for k in range(3))

    @pl.when(pl.program_id(2) == 0)
    def _transpose_v():
        for c in range(nk):
            vt_ref[c, :HEAD_DIM] = v_ref[pl.ds(c * tk, tk), :].astype(F32).T.astype(BF16)
            vt_ref[c, HEAD_DIM:] = jnp.ones((ATTN_ONES, tk), BF16)

    q = jnp.concatenate([q_ref[:, h * HEAD_DIM:(h + 1) * HEAD_DIM] for h in range(Q_PER_KV)], axis=0)
    m_cols = Q_PER_KV * tq
    nchunk = tk // ATTN_ROWS

    def stage_a(i, s_ref, mx_ref):
        kc = k_ref[pl.ds(pl.multiple_of(i * tk, tk), tk), :]
        s = lax.dot_general(kc, q, (((1,), (1,)), ((), ())), preferred_element_type=F32)
        s_ref[...] = s
        mx_ref[...] = jnp.max(s, axis=0, keepdims=True)

    def stage_b(s_ref, mx_ref, p_ref, m):
        m_new = jnp.maximum(m, mx_ref[...])
        alpha = jnp.exp2(m - m_new)
        for c in range(nchunk):
            rows = slice(c * ATTN_ROWS, (c + 1) * ATTN_ROWS)
            p_ref[rows, :] = jnp.exp2((s_ref[rows, :] - m_new).astype(BF16))
        return m_new, alpha

    def stage_d(i, p_ref, alpha, acc):
        return alpha * acc + jnp.dot(vt_ref[i], p_ref[...], preferred_element_type=F32)

    def ring(i, m, al0, al1, acc, n_a, n_b):
        alphas = {0: al0, 1: al1}
        for u in range(ATTN_RING):
            if u < n_a:
                stage_a(i + u + 4, s_refs[u], mx_refs[u])
            if u < n_b:
                v = (u + 2) % ATTN_RING
                m, alphas[u + 2] = stage_b(s_refs[v], mx_refs[v], p_refs[v], m)
            acc = stage_d(i + u, p_refs[u], alphas[u], acc)
        return m, alphas.get(4), alphas.get(5), acc

    m = jnp.full((1, m_cols), -jnp.inf, F32)
    acc = jnp.zeros((HEAD_DIM + ATTN_ONES, m_cols), F32)
    for u in range(ATTN_RING):
        stage_a(u, s_refs[u], mx_refs[u])
    m, al0 = stage_b(s_refs[0], mx_refs[0], p_refs[0], m)
    m, al1 = stage_b(s_refs[1], mx_refs[1], p_refs[1], m)

    def trip(j, carry):
        return ring(ATTN_RING * j, *carry, ATTN_RING, ATTN_RING)

    m, al0, al1, acc = lax.fori_loop(0, nk // ATTN_RING - 1, trip, (m, al0, al1, acc))
    _, _, _, acc = ring(nk - ATTN_RING, m, al0, al1, acc, 0, 2)
    o = acc[:HEAD_DIM] / acc[HEAD_DIM:HEAD_DIM + 1]
    for h in range(Q_PER_KV):
        o_ref[:, h * HEAD_DIM:(h + 1) * HEAD_DIM] = o[:, h * tq:(h + 1) * tq].T


def _attention(q, k, v, *, tq=512, tk=512):
    B, L, _ = q.shape
    qw = Q_PER_KV * HEAD_DIM
    nk = L // tk
    assert nk % ATTN_RING == 0 and nk >= 2 * ATTN_RING
    body = functools.partial(_attn_body, tq=tq, tk=tk, nk=nk)
    m_cols = Q_PER_KV * tq
    return pl.pallas_call(
        body,
        grid=(B, N_KV_HEADS, L // tq),
        in_specs=[
            pl.BlockSpec((None, tq, qw), lambda b, g, i: (b, i, g)),
            pl.BlockSpec((None, L, HEAD_DIM), lambda b, g, i: (b, 0, g)),
            pl.BlockSpec((None, L, HEAD_DIM), lambda b, g, i: (b, 0, g)),
        ],
        out_specs=pl.BlockSpec((None, tq, qw), lambda b, g, i: (b, i, g)),
        out_shape=jax.ShapeDtypeStruct((B, L, D_ATTN), F32),
        scratch_shapes=([pltpu.VMEM((nk, HEAD_DIM + ATTN_ONES, tk), BF16)]
                        + [pltpu.VMEM((tk, m_cols), F32)] * ATTN_RING
                        + [pltpu.VMEM((1, m_cols), F32)] * ATTN_RING
                        + [pltpu.VMEM((tk, m_cols), BF16)] * ATTN_RING),
        compiler_params=_cparams(("parallel", "parallel", "arbitrary"), 48),
        name="attn",
    )(q, k, v)


def _filt_body(feat_ref, w1_ref, b1_ref, w2_ref, b2_ref, w3_ref, fq_ref, dec_ref, kf_ref, asum_ref, *, nt):
    i = pl.program_id(0)
    hp = lax.Precision.HIGHEST
    fq = fq_ref[...]
    feat = feat_ref[...]
    h = jnp.sin(fq * (jnp.dot(feat, w1_ref[...], precision=hp, preferred_element_type=F32) + b1_ref[...]))
    h = jnp.sin(fq * (jnp.dot(h, w2_ref[...], precision=hp, preferred_element_type=F32) + b2_ref[...]))
    h_hi = h.astype(BF16)
    h_lo = (h - h_hi.astype(F32)).astype(BF16)
    h3 = (jnp.dot(h_hi, w3_ref[0], preferred_element_type=F32)
          + jnp.dot(h_lo, w3_ref[0], preferred_element_type=F32)
          + jnp.dot(h_hi, w3_ref[1], preferred_element_type=F32))
    t01 = feat[:, 0:1]
    taps = h3 * jnp.exp(-t01 * jnp.abs(dec_ref[...]))
    row = lax.broadcasted_iota(jnp.int32, (taps.shape[0], 1), 0)
    taps = jnp.where((row == 0) & (i == nt), 0.0, taps)
    for cb in range(taps.shape[1] // LANES):
        _store_time_grouped(kf_ref, cb, taps[:, cb * LANES:(cb + 1) * LANES])
    part = jnp.sum(jnp.abs(taps), axis=0, keepdims=True)

    @pl.when(i == 0)
    def _first():
        asum_ref[...] = part

    @pl.when(i > 0)
    def _rest():
        asum_ref[...] += part


def _filters(feats2, w1, b1, w2, b2, w3, fq, dec, *, tl=512):
    _, L, _ = feats2.shape
    nc = w3.shape[3]
    ng2 = FFT_N2 // FFT_G
    nt = L // tl
    body = functools.partial(_filt_body, nt=nt)
    return pl.pallas_call(
        body,
        grid=(2 * nt,),
        in_specs=[
            pl.BlockSpec((None, tl, FILT_PAD), lambda i: (i // nt, i % nt, 0)),
            _resident(w1.shape), _resident(b1.shape), _resident(w2.shape), _resident(b2.shape),
            pl.BlockSpec((None, 2, FILT_PAD, nc), lambda i: (i // nt, 0, 0, 0)),
            _resident(fq.shape),
            pl.BlockSpec((None, 1, nc), lambda i: (i // nt, 0, 0)),
        ],
        out_specs=[
            pl.BlockSpec((nc // LANES, ng2, tl // FFT_N2, FFT_G, LANES), lambda i: (0, 0, i, 0, 0)),
            pl.BlockSpec((1, nc), lambda i: (0, 0)),
        ],
        out_shape=[jax.ShapeDtypeStruct((nc // LANES, ng2, 2 * L // FFT_N2, FFT_G, LANES), F32),
                   jax.ShapeDtypeStruct((1, nc), F32)],
        compiler_params=_cparams(("arbitrary",), 48),
        name="filt",
    )(feats2, w1, b1, w2, b2, w3, fq, dec)


FFT_CB = FFT_W // LANES
FFT_CB_B = 2


def _pack_c(y):
    n = y.shape[0] // 2
    return _pack2(y[:n], y[n:])


def _unpack_lanes(w):
    hi, lo = _unpack_c(w)
    return jnp.concatenate([hi, lo], axis=1)


def _unpack_rows(w):
    re, im = _unpack_c(w)
    return jnp.concatenate([re, im], axis=0).astype(BF16)


def _flat_scratch(n, dtype, cbs, lead=()):
    return pltpu.VMEM(lead + (cbs, n * FFT_G, LANES), dtype)


def _flatten(flat_ref, block_ref, pre=()):
    rows = flat_ref.shape[-2]
    for h in range(block_ref.shape[0]):
        flat_ref[pre + (h,)] = block_ref[(h,) + pre].reshape(rows, LANES)


def _unflatten(block_ref, flat_ref, pre=()):
    n = flat_ref.shape[-2] // FFT_G
    for h in range(block_ref.shape[0]):
        block_ref[(h,) + pre] = flat_ref[pre + (h,)].reshape(n, FFT_G, LANES)


def _gather_rows(flat_ref, j, pre=()):
    n = flat_ref.shape[-2] // FFT_G
    return jnp.concatenate(
        [flat_ref[pre + (h, pl.ds(j, n, stride=FFT_G), slice(None))] for h in range(flat_ref.shape[len(pre)])],
        axis=1)


def _scatter_rows(flat_ref, j, val, pre=()):
    n = flat_ref.shape[-2] // FFT_G
    for h in range(flat_ref.shape[len(pre)]):
        flat_ref[pre + (h, pl.ds(j, n, stride=FFT_G), slice(None))] = val[:, h * LANES:(h + 1) * LANES]


def _store_blocks(ref, j, val):
    for h in range(ref.shape[0]):
        ref[h, j] = val[:, h * LANES:(h + 1) * LANES]


def _store_grouped(ref, j, val):
    groups = ref.shape[1]
    for h in range(ref.shape[0]):
        ref[h, :, j] = val[:, h * LANES:(h + 1) * LANES].reshape(groups, FFT_G, LANES)


def _fft_a_body(x_ref, fa_ref, o_ref, xf_ref):
    for b in range(2):
        _flatten(xf_ref, x_ref, (b,))
    for j in range(FFT_G):
        xs = jnp.concatenate([_unpack_lanes(_gather_rows(xf_ref, j, (b,))) for b in range(2)],
                             axis=0).astype(BF16)
        _store_grouped(o_ref, j, _pack_c(jnp.dot(fa_ref[j], xs, preferred_element_type=F32)))


def _fft_a(xv, fa, *, ncb, pb0=0):
    _, P, _, ng2, n1h, _, _ = xv.shape
    n2 = ng2 * FFT_G
    two_n1 = fa.shape[1]
    n1 = two_n1 // 2
    pbs = FFT_CB // 2
    c0 = pb0 // pbs
    return pl.pallas_call(
        _fft_a_body,
        grid=(ng2, P, ncb // FFT_CB),
        in_specs=[
            pl.BlockSpec((pbs, None, 2, None, n1h, FFT_G, LANES), lambda g, p, c: (c + c0, p, 0, g, 0, 0, 0)),
            pl.BlockSpec((FFT_G, two_n1, 2 * n1h), lambda g, p, c: (g, 0, 0)),
        ],
        out_specs=pl.BlockSpec((None, FFT_CB, n1 // FFT_G, FFT_G, FFT_G, LANES),
                               lambda g, p, c: (p, c, 0, g, 0, 0)),
        out_shape=jax.ShapeDtypeStruct((P, ncb, n1 // FFT_G, n2, FFT_G, LANES), U32),
        scratch_shapes=[_flat_scratch(n1h, U32, pbs, (2,))],
        compiler_params=_cparams(("arbitrary", "arbitrary", "arbitrary"), 32),
        name="fft_a",
    )(xv, fa)


def _fft_a_real_body(x_ref, fa_ref, o_ref, xf_ref):
    _flatten(xf_ref, x_ref)
    for j in range(FFT_G):
        xs = _gather_rows(xf_ref, j).astype(BF16)
        _store_grouped(o_ref, j, _pack_c(jnp.dot(fa_ref[j], xs, preferred_element_type=F32)))


def _fft_a_real(xv, fa_re):
    ncb, ng2, n1, _, _ = xv.shape
    n2 = ng2 * FFT_G
    return pl.pallas_call(
        _fft_a_real_body,
        grid=(ng2, ncb // FFT_CB),
        in_specs=[
            pl.BlockSpec((FFT_CB, None, n1, FFT_G, LANES), lambda g, c: (c, g, 0, 0, 0)),
            pl.BlockSpec((FFT_G, 2 * n1, n1), lambda g, c: (g, 0, 0)),
        ],
        out_specs=pl.BlockSpec((FFT_CB, n1 // FFT_G, FFT_G, FFT_G, LANES), lambda g, c: (c, 0, g, 0, 0)),
        out_shape=jax.ShapeDtypeStruct((ncb, n1 // FFT_G, n2, FFT_G, LANES), U32),
        scratch_shapes=[_flat_scratch(n1, F32, FFT_CB)],
        compiler_params=_cparams(("arbitrary", "arbitrary"), 32),
        name="fft_a_real",
    )(xv, fa_re)


def _fft_b_filt_body(s_ref, fb_ref, as_ref, o_ref, sf_ref):
    _flatten(sf_ref, s_ref)
    inv = 1.0 / as_ref[...]
    fb = fb_ref[...]
    for j in range(FFT_G):
        k = jnp.dot(fb, _unpack_rows(_gather_rows(sf_ref, j)), preferred_element_type=F32)
        _store_blocks(o_ref, j, _pack_c(k * inv))


def _fft_b_filt(s1f, fb, asum):
    ncb, ng1, n2, _, _ = s1f.shape
    n1 = ng1 * FFT_G
    return pl.pallas_call(
        _fft_b_filt_body,
        grid=(ng1, ncb // FFT_CB),
        in_specs=[
            pl.BlockSpec((FFT_CB, None, n2, FFT_G, LANES), lambda g, c: (c, g, 0, 0, 0)),
            _resident(fb.shape),
            pl.BlockSpec((1, FFT_W), lambda g, c: (0, c)),
        ],
        out_specs=pl.BlockSpec((FFT_CB, FFT_G, n2, LANES), lambda g, c: (c, g, 0, 0)),
        out_shape=jax.ShapeDtypeStruct((ncb, n1, n2, LANES), U32),
        scratch_shapes=[_flat_scratch(n2, U32, FFT_CB)],
        compiler_params=_cparams(("parallel", "parallel"), 32),
        name="fft_b_filt",
    )(s1f, fb, asum)


def _fft_b_body(s_ref, kf_ref, fb_ref, fbi_ref, o_ref, sf_ref):
    n2 = FFT_N2
    _flatten(sf_ref, s_ref)
    fb = fb_ref[...]
    fbi = fbi_ref[...]
    for j in range(FFT_G):
        x = jnp.dot(fb, _unpack_rows(_gather_rows(sf_ref, j)), preferred_element_type=F32)
        xr, xi = x[:n2], x[n2:]
        kr, ki = _unpack_c(jnp.concatenate([kf_ref[h, j] for h in range(kf_ref.shape[0])], axis=1))
        pm = jnp.concatenate([xr * kr - xi * ki, xr * ki + xi * kr], axis=0).astype(BF16)
        _store_grouped(o_ref, j, _pack_c(jnp.dot(fbi, pm, preferred_element_type=F32)))


def _fft_b(s1, kf, fb, fbi, *, kcb0):
    P, ncb, ng1, n2, _, _ = s1.shape
    n1 = ng1 * FFT_G
    cbs = FFT_CB_B
    k0 = kcb0 // cbs
    return pl.pallas_call(
        _fft_b_body,
        grid=(ng1, ncb // cbs, P),
        in_specs=[
            pl.BlockSpec((None, cbs, None, n2, FFT_G, LANES), lambda g, c, p: (p, c, g, 0, 0, 0)),
            pl.BlockSpec((cbs, FFT_G, n2, LANES), lambda g, c, p: (c + k0, g, 0, 0)),
            _resident(fb.shape),
            _resident(fbi.shape),
        ],
        out_specs=pl.BlockSpec((None, cbs, n2 // FFT_G, FFT_G, FFT_G, LANES),
                               lambda g, c, p: (p, c, 0, g, 0, 0)),
        out_shape=jax.ShapeDtypeStruct((P, ncb, n2 // FFT_G, n1, FFT_G, LANES), U32),
        scratch_shapes=[_flat_scratch(n2, U32, cbs)],
        compiler_params=_cparams(("parallel", "parallel", "arbitrary"), 32),
        name="fft_b",
    )(s1, kf, fb, fbi)


def _fft_c_body(z_ref, fc_ref, u_ref, gate_ref, d_ref, *rest, chain):
    if chain:
        fa_ref, o_ref, s_ref, zf_ref, uf_ref, gf_ref, of_ref = rest
    else:
        o_ref, zf_ref, uf_ref, gf_ref, of_ref = rest
    n1h = o_ref.shape[2]
    d = d_ref[...]
    _flatten(zf_ref, z_ref)
    for b in range(2):
        _flatten(uf_ref, u_ref, (b,))
        _flatten(gf_ref, gate_ref, (b,))
    for j in range(FFT_G):
        y = jnp.dot(fc_ref[j], _unpack_rows(_gather_rows(zf_ref, j)), preferred_element_type=F32)
        res = []
        for b in range(2):
            yb = y[b * n1h:(b + 1) * n1h]
            gate = _unpack_lanes(_gather_rows(gf_ref, j, (b,)))
            skip = _unpack_lanes(_gather_rows(uf_ref, j, (b,)))
            res.append(gate * (yb + skip * d))
            if chain:
                half = res[b].shape[1] // 2
                _scatter_rows(of_ref, j, _pack2(res[b][:, :half], res[b][:, half:]), (b,))
            else:
                _scatter_rows(of_ref, j, res[b], (b,))
        if chain:
            xs = jnp.concatenate(res, axis=0).astype(BF16)
            _store_grouped(s_ref, j, _pack_c(jnp.dot(fa_ref[j], xs, preferred_element_type=F32)))
    for b in range(2):
        _unflatten(o_ref, of_ref, (b,))


def _fft_c(s2, fc, uv, gv, d, *, upb0, gpb0, fa=None):
    P, ncb, ng2, n1, _, _ = s2.shape
    n1h = n1 // 2
    n2 = ng2 * FFT_G
    pbs = FFT_CB // 2
    u0 = upb0 // pbs
    g0 = gpb0 // pbs
    chain = fa is not None
    pblk = (pbs, None, 2, None, n1h, FFT_G, LANES)
    ocb, odt = (pbs, U32) if chain else (FFT_CB, F32)
    in_specs = [
        pl.BlockSpec((None, FFT_CB, None, n1, FFT_G, LANES), lambda g, p, c: (p, c, g, 0, 0, 0)),
        pl.BlockSpec((FFT_G, 2 * n1h, 2 * n1), lambda g, p, c: (g, 0, 0)),
        pl.BlockSpec(pblk, lambda g, p, c: (c + u0, p, 0, g, 0, 0, 0)),
        pl.BlockSpec(pblk, lambda g, p, c: (c + g0, p, 0, g, 0, 0, 0)),
        pl.BlockSpec((1, FFT_W), lambda g, p, c: (0, c)),
    ]
    out_specs = [pl.BlockSpec((ocb, None, 2, None, n1h, FFT_G, LANES), lambda g, p, c: (c, p, 0, g, 0, 0, 0))]
    out_shape = [jax.ShapeDtypeStruct((ncb * ocb // FFT_CB, P, 2, ng2, n1h, FFT_G, LANES), odt)]
    args = [s2, fc, uv, gv, d]
    if chain:
        in_specs.append(pl.BlockSpec((FFT_G, 2 * n1, 2 * n1h), lambda g, p, c: (g, 0, 0)))
        out_specs.append(pl.BlockSpec((None, FFT_CB, n1 // FFT_G, FFT_G, FFT_G, LANES),
                                      lambda g, p, c: (p, c, 0, g, 0, 0)))
        out_shape.append(jax.ShapeDtypeStruct((P, ncb, n1 // FFT_G, n2, FFT_G, LANES), U32))
        args.append(fa)
    return pl.pallas_call(
        functools.partial(_fft_c_body, chain=chain),
        grid=(ng2, P, ncb // FFT_CB),
        in_specs=in_specs,
        out_specs=out_specs,
        out_shape=out_shape,
        scratch_shapes=[_flat_scratch(n1, U32, FFT_CB), _flat_scratch(n1h, U32, pbs, (2,)),
                        _flat_scratch(n1h, U32, pbs, (2,)), _flat_scratch(n1h, odt, ocb, (2,))],
        compiler_params=_cparams(("arbitrary", "arbitrary", "arbitrary"), 40),
        name="fft_c",
    )(*args)


def _oproj_body(x_ref, a_ref, h_ref, ga_ref, gh_ref, wa_ref, wh_ref, o_ref):
    ma = _rms(a_ref[...], ga_ref[...]).astype(BF16)
    hy = jnp.concatenate([_load_time_grouped(h_ref, cb) for cb in range(h_ref.shape[0])], axis=1)
    mh = _rms(hy, gh_ref[...]).astype(BF16)
    o_ref[...] = (x_ref[...] + jnp.dot(ma, wa_ref[...], preferred_element_type=F32)
                  + jnp.dot(mh, wh_ref[...], preferred_element_type=F32))


def _out_proj(x, a, h, ga, gh, wa, wh, *, tm=512):
    T, D = x.shape
    ncb, _, ng2, n1h, _, _ = h.shape
    npos = n1h * FFT_N2 // tm
    return pl.pallas_call(
        _oproj_body,
        grid=(T // tm,),
        in_specs=[
            pl.BlockSpec((tm, D), lambda i: (i, 0)),
            pl.BlockSpec((tm, D_ATTN), lambda i: (i, 0)),
            pl.BlockSpec((ncb, None, ng2, tm // FFT_N2, FFT_G, LANES), lambda i: (0, i // npos, 0, i % npos, 0, 0)),
            _resident((1, D_ATTN)), _resident((1, D_HYENA)),
            _resident(wa.shape), _resident(wh.shape),
        ],
        out_specs=pl.BlockSpec((tm, D), lambda i: (i, 0)),
        out_shape=jax.ShapeDtypeStruct((T, D), F32),
        compiler_params=_cparams(("parallel",), 48),
        name="oproj",
    )(x, a, h, ga, gh, wa, wh)


@functools.lru_cache(maxsize=None)
def _rope_tables_np(L):
    rows = L // GRID_W
    row = np.repeat(np.arange(rows, dtype=np.float32), GRID_W)
    col = np.tile(np.arange(GRID_W, dtype=np.float32), rows)
    inv = (ROPE_THETA ** (-np.arange(0, ROPE_HALF, 2, dtype=np.float32) / ROPE_HALF)).astype(np.float32)
    ar = (row[:, None] * inv[None]).astype(np.float64)
    ac = (col[:, None] * inv[None]).astype(np.float64)
    cr, sr, cc, sc = np.cos(ar), np.sin(ar), np.cos(ac), np.sin(ac)
    cos_t = np.concatenate([cr, cr, cc, cc], axis=-1).astype(np.float32)
    sin_t = np.concatenate([-sr, sr, -sc, sc], axis=-1).astype(np.float32)
    return cos_t, sin_t


def _rope_tables(L):
    return tuple(jnp.asarray(t) for t in _rope_tables_np(L))


@functools.lru_cache(maxsize=None)
def _filter_feats_np(L):
    t01 = np.linspace(0.0, 1.0, L, dtype=np.float32)[:, None]
    bands = (FILTER_EMB - 1) // 2
    fr = np.linspace(1e-4, bands - 1, bands, dtype=np.float32)[None]
    w = (np.float32(2.0 * math.pi) * np.arange(L, dtype=np.float32)[:, None] / np.float32(L)).astype(np.float32)
    arg = (fr * w).astype(np.float32).astype(np.float64)
    feats = np.concatenate([t01, np.cos(arg), -np.sin(arg)], axis=-1).astype(np.float32)
    feats = np.pad(feats, ((0, 0), (0, FILT_PAD - FILTER_EMB)))
    mirrored = np.concatenate([feats[:1], feats[1:][::-1]], axis=0)
    return np.stack([feats, mirrored])


def _filter_feats(L):
    return jnp.asarray(_filter_feats_np(L))


@functools.lru_cache(maxsize=None)
def _dft_tables_np(L):
    n = 2 * L
    n2 = FFT_N2
    n1 = n // n2
    n1h = n1 // 2
    k1 = np.arange(n1, dtype=np.int64)[None, :, None]
    t1 = np.arange(n1, dtype=np.int64)[None, None, :]
    t2 = np.arange(n2, dtype=np.int64)[:, None, None]
    ang = (-2.0 * math.pi / n) * ((k1 * (n2 * t1 + t2)) % n)
    cr_full, ci_full = np.cos(ang), np.sin(ang)
    fa_re = np.concatenate([cr_full, ci_full], -2)
    cr, ci = cr_full[..., :n1h], ci_full[..., :n1h]
    fa = np.concatenate([np.concatenate([cr, -ci], -1), np.concatenate([ci, cr], -1)], -2)
    crt, cit = np.swapaxes(cr, 1, 2), np.swapaxes(ci, 1, 2)
    fc = np.concatenate([np.concatenate([crt, cit], -1), np.concatenate([-cit, crt], -1)], -2) / n
    a = np.arange(n2, dtype=np.int64)
    ang2 = (-2.0 * math.pi / n2) * ((a[:, None] * a[None, :]) % n2)
    fr, fi = np.cos(ang2), np.sin(ang2)
    fb = np.concatenate([np.concatenate([fr, -fi], -1), np.concatenate([fi, fr], -1)], -2)
    fbi = np.concatenate([np.concatenate([fr, fi], -1), np.concatenate([-fi, fr], -1)], -2)
    return tuple(t.astype(np.float32) for t in (fa, fa_re, fb, fbi, fc))


def _dft_tables(L):
    return tuple(jnp.asarray(t, dtype=BF16) for t in _dft_tables_np(L))


def _trunk(x, p):
    B, L, D = x.shape
    T = B * L
    P = B // 2
    n2 = FFT_N2
    n1h = L // n2
    C = D_HYENA
    ncb = C // LANES

    x0 = x.reshape(T, D)
    x1 = _ffn(x0, p["ffn1_norm"], p["ffn1_w13"], p["ffn1_w2"], p["final_norm"], final_norm=False)

    cos_t, sin_t = _rope_tables(L)
    q, k, v, hyc = _mix_in(x1, p["mix_norm"], p["wq"], p["wkv"], p["wh"], p["q_norm"], p["k_norm"],
                           cos_t, sin_t, p["conv_w"], p["conv_b"], L=L)
    attn = _attention(q.reshape(B, L, D_ATTN), k.reshape(B, L, D_KV), v.reshape(B, L, D_KV))
    ng2 = n2 // FFT_G
    npb = ncb // 2
    hyv = hyc.reshape(3 * npb, P, 2, ng2, n1h, FFT_G, LANES)

    fa, fa_re, fb, fbi, fc = _dft_tables(L)
    taps, asum = _filters(_filter_feats(L), p["filt_w1"], p["filt_b1"], p["filt_w2"], p["filt_b2"],
                          p["filt_w3"], p["filt_freq"], p["decay"])
    kf = _fft_b_filt(_fft_a_real(taps, fa_re), fb, asum)

    s1 = _fft_a(hyv, fa, ncb=ncb, pb0=0)
    s2 = _fft_b(s1, kf, fb, fbi, kcb0=0)
    z, s1 = _fft_c(s2, fc, hyv, hyv, p["bias0"], upb0=0, gpb0=npb, fa=fa)
    s2 = _fft_b(s1, kf, fb, fbi, kcb0=ncb)
    ho, = _fft_c(s2, fc, z, hyv, p["bias1"], upb0=0, gpb0=2 * npb)

    x2 = _out_proj(x1, attn.reshape(T, D_ATTN), ho.reshape(ncb, B, ng2, n1h, FFT_G, LANES),
                   p["gon_a"], p["gon_h"], p["wo_a"], p["wo_h"])
    x3 = _ffn(x2, p["ffn2_norm"], p["ffn2_w13"], p["ffn2_w2"], p["final_norm"], final_norm=True)
    return x3.reshape(B, L, D)


def _split_bf16(w):
    hi = w.astype(BF16)
    lo = (w - hi.astype(F32)).astype(BF16)
    return jnp.stack([hi, lo], axis=1)


def kernel(x_prompt, x_sample, ffn1_norm, ffn1_w13, ffn1_w2, mix_norm, w_in, q_norm, k_norm, conv_w, conv_b, filt_w1, filt_b1, filt_w2, filt_b2, filt_w3, filt_freq, hyena_decay, hyena_bias, group_out_norm, w_out, ffn2_norm, ffn2_w13, ffn2_w2, final_norm):
    hp = FILT_PAD - FILTER_HIDDEN
    w_in0 = w_in[0]
    p = {
        "ffn1_norm": ffn1_norm[0][None], "ffn2_norm": ffn2_norm[0][None],
        "ffn1_w13": ffn1_w13[0].astype(BF16), "ffn1_w2": ffn1_w2[0].astype(BF16),
        "ffn2_w13": ffn2_w13[0].astype(BF16), "ffn2_w2": ffn2_w2[0].astype(BF16),
        "final_norm": final_norm[None],
        "mix_norm": mix_norm[0][None],
        "wq": w_in0[:, :D_ATTN].astype(BF16),
        "wkv": w_in0[:, D_ATTN:D_ATTN + 2 * D_KV].astype(BF16),
        "wh": w_in0[:, D_ATTN + 2 * D_KV:].astype(BF16),
        "q_norm": q_norm[0][None], "k_norm": k_norm[0][None],
        "conv_w": conv_w[0], "conv_b": conv_b[0][None],
        "filt_w1": jnp.pad(filt_w1[0], ((0, FILT_PAD - FILTER_EMB), (0, hp))),
        "filt_b1": jnp.pad(filt_b1[0], (0, hp))[None],
        "filt_w2": jnp.pad(filt_w2[0], ((0, hp), (0, hp))),
        "filt_b2": jnp.pad(filt_b2[0], (0, hp))[None],
        "filt_w3": _split_bf16(jnp.pad(filt_w3[0], ((0, hp), (0, 0)))
                               .reshape(FILT_PAD, 2, 2 * D_HYENA).transpose(1, 0, 2)),
        "filt_freq": jnp.pad(filt_freq[0], (0, hp))[None],
        "decay": hyena_decay[0].reshape(2, 1, 2 * D_HYENA),
        "bias0": hyena_bias[0, 0][None], "bias1": hyena_bias[0, 1][None],
        "gon_a": group_out_norm[0, :D_ATTN][None], "gon_h": group_out_norm[0, D_ATTN:][None],
        "wo_a": w_out[0, :D_ATTN].astype(BF16), "wo_h": w_out[0, D_ATTN:].astype(BF16),
    }
    return (_trunk(x_prompt, p), _trunk(x_sample, p))
```

```python
import functools
import math

import jax
import jax.numpy as jnp
import numpy as np
from jax import lax
from jax.experimental import pallas as pl
from jax.experimental.pallas import tpu as pltpu

F32 = jnp.float32
BF16 = jnp.bfloat16
U32 = jnp.uint32

D_MODEL = 2048
GRID_W = 64
D_ATTN = D_MODEL // 2
D_HYENA = D_MODEL - D_ATTN
HEAD_DIM = 128
N_Q_HEADS = D_ATTN // HEAD_DIM
N_KV_HEADS = 2
Q_PER_KV = N_Q_HEADS // N_KV_HEADS
ROPE_HALF = HEAD_DIM // 2
ROPE_THETA = 10000.0
FILTER_EMB = 33
FILTER_HIDDEN = 64
EPS = 1e-6
D_KV = N_KV_HEADS * HEAD_DIM

LANES = 128
FILT_PAD = 128
FFT_N2 = 128
FFT_W = 512
FFT_G = 8
MIB = 1024 * 1024
LOG2_E = math.log2(math.e)


def _cparams(semantics, vmem_mib):
    return pltpu.CompilerParams(dimension_semantics=semantics,
                                vmem_limit_bytes=vmem_mib * MIB)


def _rms(x, g):
    ms = jnp.mean(x * x, axis=-1, keepdims=True)
    return x * lax.rsqrt(ms + EPS) * g


def _resident(shape):
    nd = len(shape)
    return pl.BlockSpec(shape, lambda *_: (0,) * nd, pipeline_mode=pl.Buffered(1))


def _pack2(a, b):
    ua = lax.bitcast_convert_type(a, U32)
    ub = lax.bitcast_convert_type(b, U32)
    half = jnp.uint32(0x8000)
    return ((ua + half) & jnp.uint32(0xFFFF0000)) | ((ub + half) >> 16)


def _unpack_c(w):
    hi = lax.bitcast_convert_type(w & jnp.uint32(0xFFFF0000), F32)
    lo = lax.bitcast_convert_type(w << 16, F32)
    return hi, lo


def _pair_blocks(pb):
    hi = 4 * (pb // 2) + pb % 2
    return hi, hi + 2


def _ffn_body(x_ref, g_ref, wg_ref, wu_ref, w2_ref, gf_ref, o_ref, xn_ref, *, nj, final_norm):
    j = pl.program_id(1)

    @pl.when(j == 0)
    def _init():
        x = x_ref[...]
        xn_ref[...] = _rms(x, g_ref[...]).astype(BF16)
        o_ref[...] = x

    xn = xn_ref[...]
    hg = jnp.dot(xn, wg_ref[...], preferred_element_type=F32)
    hu = jnp.dot(xn, wu_ref[...], preferred_element_type=F32)
    a = (hg * (0.5 / (1.0 + jnp.exp(-hg))) * hu).astype(BF16)
    o_ref[...] += jnp.dot(a, w2_ref[...], preferred_element_type=F32)

    if final_norm:
        @pl.when(j == nj - 1)
        def _fin():
            o_ref[...] = _rms(o_ref[...], gf_ref[...])


def _ffn(x, g, w13, w2, gf, *, final_norm, tm=1024, tf=512):
    T, D = x.shape
    ff = w2.shape[0]
    nj = ff // tf
    body = functools.partial(_ffn_body, nj=nj, final_norm=final_norm)
    return pl.pallas_call(
        body,
        grid=(T // tm, nj),
        in_specs=[
            pl.BlockSpec((tm, D), lambda i, j: (i, 0)),
            pl.BlockSpec((1, D), lambda i, j: (0, 0)),
            pl.BlockSpec((D, tf), lambda i, j: (0, j)),
            pl.BlockSpec((D, tf), lambda i, j: (0, j + nj)),
            pl.BlockSpec((tf, D), lambda i, j: (j, 0)),
            pl.BlockSpec((1, D), lambda i, j: (0, 0)),
        ],
        out_specs=pl.BlockSpec((tm, D), lambda i, j: (i, 0)),
        out_shape=jax.ShapeDtypeStruct((T, D), F32),
        scratch_shapes=[pltpu.VMEM((tm, D), BF16)],
        compiler_params=_cparams(("parallel", "arbitrary"), 58),
        name="ffn",
    )(x, g, w13, w13, w2, gf)


def _store_time_grouped(ref, cb, val):
    n1t = ref.shape[2]
    v = val.reshape(n1t * (FFT_N2 // FFT_G), FFT_G, LANES)
    for a in range(n1t):
        ref[cb, :, a] = v[a * (FFT_N2 // FFT_G):(a + 1) * (FFT_N2 // FFT_G)]


def _load_time_grouped(ref, cb):
    n1t = ref.shape[2]
    return jnp.concatenate([ref[cb, :, a].reshape(FFT_N2, LANES) for a in range(n1t)], axis=0)


MIX_HALO = 16


def _mix_body(xb_ref, x_ref, xa_ref, g_ref, wq_ref, wkv_ref, wh_ref, qn_ref, kn_ref, cos_ref, sin_ref,
              cw_ref, cb_ref, q_ref, k_ref, v_ref, hy_ref, *, npos):
    tm = x_ref.shape[0]
    x_ext = jnp.concatenate([xb_ref[...], x_ref[...], xa_ref[...]], axis=0)
    h_ext = _rms(x_ext, g_ref[...]).astype(BF16)
    h = h_ext[MIX_HALO:MIX_HALO + tm]
    lane = lax.broadcasted_iota(jnp.int32, (tm, HEAD_DIM), 1)
    low_half = (lane % (2 * (ROPE_HALF // 2))) < (ROPE_HALF // 2)
    c = cos_ref[...]
    s = sin_ref[...]

    def head(p, gain, scale):
        y = _rms(p, gain)
        partner = jnp.where(low_half, pltpu.roll(y, HEAD_DIM - ROPE_HALF // 2, 1),
                            pltpu.roll(y, ROPE_HALF // 2, 1))
        y = y * c + partner * s
        return y if scale is None else y * scale

    q = jnp.dot(h, wq_ref[...], preferred_element_type=F32)
    for hd in range(N_Q_HEADS):
        sl = slice(hd * HEAD_DIM, (hd + 1) * HEAD_DIM)
        q_ref[:, sl] = head(q[:, sl], qn_ref[...], HEAD_DIM ** -0.5 * LOG2_E).astype(BF16)
    kv = jnp.dot(h, wkv_ref[...], preferred_element_type=F32)
    for hd in range(N_KV_HEADS):
        sl = slice(hd * HEAD_DIM, (hd + 1) * HEAD_DIM)
        k_ref[:, sl] = head(kv[:, sl], kn_ref[...], None).astype(BF16)
    v_ref[...] = kv[:, D_KV:].astype(BF16)

    hy = jnp.dot(h_ext, wh_ref[...], preferred_element_type=F32)
    pos = pl.program_id(0) % npos
    rid = lax.broadcasted_iota(jnp.int32, (tm, 1), 0)
    before = jnp.where((rid == 0) & (pos == 0), 0.0, hy[MIX_HALO - 1:MIX_HALO - 1 + tm])
    after = jnp.where((rid == tm - 1) & (pos == npos - 1), 0.0, hy[MIX_HALO + 1:MIX_HALO + 1 + tm])
    hyc = (before * cw_ref[0:1, :] + hy[MIX_HALO:MIX_HALO + tm] * cw_ref[1:2, :]
           + after * cw_ref[2:3, :] + cb_ref[...])
    for pb in range(hy_ref.shape[0]):
        hi, lo = _pair_blocks(pb)
        _store_time_grouped(hy_ref, pb, _pack2(hyc[:, hi * LANES:(hi + 1) * LANES],
                                               hyc[:, lo * LANES:(lo + 1) * LANES]))


def _mix_in(x, g, wq, wkv, wh, qn, kn, cos_t, sin_t, cw, cb, *, L, tm=512):
    T, D = x.shape
    npos = L // tm
    ng2 = FFT_N2 // FFT_G
    hb = tm // MIX_HALO
    nhb = T // MIX_HALO
    return pl.pallas_call(
        functools.partial(_mix_body, npos=npos),
        grid=(T // tm,),
        in_specs=[
            pl.BlockSpec((MIX_HALO, D), lambda i: (jnp.maximum(i * hb - 1, 0), 0)),
            pl.BlockSpec((tm, D), lambda i: (i, 0)),
            pl.BlockSpec((MIX_HALO, D), lambda i: (jnp.minimum((i + 1) * hb, nhb - 1), 0)),
            _resident((1, D)),
            _resident(wq.shape),
            _resident(wkv.shape),
            _resident(wh.shape),
            _resident((1, HEAD_DIM)),
            _resident((1, HEAD_DIM)),
            pl.BlockSpec((tm, HEAD_DIM), lambda i: (i % npos, 0)),
            pl.BlockSpec((tm, HEAD_DIM), lambda i: (i % npos, 0)),
            _resident(cw.shape),
            _resident(cb.shape),
        ],
        out_specs=[
            pl.BlockSpec((tm, D_ATTN), lambda i: (i, 0)),
            pl.BlockSpec((tm, D_KV), lambda i: (i, 0)),
            pl.BlockSpec((tm, D_KV), lambda i: (i, 0)),
            pl.BlockSpec((3 * D_HYENA // (2 * LANES), None, ng2, tm // FFT_N2, FFT_G, LANES),
                         lambda i: (0, i // npos, 0, i % npos, 0, 0)),
        ],
        out_shape=[
            jax.ShapeDtypeStruct((T, D_ATTN), BF16),
            jax.ShapeDtypeStruct((T, D_KV), BF16),
            jax.ShapeDtypeStruct((T, D_KV), BF16),
            jax.ShapeDtypeStruct((3 * D_HYENA // (2 * LANES), T // L, ng2, L // FFT_N2, FFT_G, LANES), U32),
        ],
        compiler_params=_cparams(("parallel",), 58),
        name="mix_in",
    )(x, x, x, g, wq, wkv, wh, qn, kn, cos_t, sin_t, cw, cb)


ATTN_ROWS = 32
ATTN_RING = 4
ATTN_ONES = 16


def _attn_body(q_ref, k_ref, v_ref, o_ref, vt_ref, *bufs, tq, tk, nk):
    s_refs, mx_refs, p_refs = (bufs[k * ATTN_RING:(k + 1) * ATTN_RING] for k in range(3))

    @pl.when(pl.program_id(2) == 0)
    def _transpose_v():
        for c in range(nk):
            vt_ref[c, :HEAD_DIM] = v_ref[pl.ds(c * tk, tk), :].astype(F32).T.astype(BF16)
            vt_ref[c, HEAD_DIM:] = jnp.ones((ATTN_ONES, tk), BF16)

    q = jnp.concatenate([q_ref[:, h * HEAD_DIM:(h + 1) * HEAD_DIM] for h in range(Q_PER_KV)], axis=0)
    m_cols = Q_PER_KV * tq
    nchunk = tk // ATTN_ROWS

    def stage_a(i, s_ref, mx_ref):
        kc = k_ref[pl.ds(pl.multiple_of(i * tk, tk), tk), :]
        s = lax.dot_general(kc, q, (((1,), (1,)), ((), ())), preferred_element_type=F32)
        s_ref[...] = s
        mx_ref[...] = jnp.max(s, axis=0, keepdims=True)

    def stage_b(s_ref, mx_ref, p_ref, m):
        m_new = jnp.maximum(m, mx_ref[...])
        alpha = jnp.exp2(m - m_new)
        for c in range(nchunk):
            rows = slice(c * ATTN_ROWS, (c + 1) * ATTN_ROWS)
            p_ref[rows, :] = jnp.exp2((s_ref[rows, :] - m_new).astype(BF16))
        return m_new, alpha

    def stage_d(i, p_ref, alpha, acc):
        return alpha * acc + jnp.dot(vt_ref[i], p_ref[...], preferred_element_type=F32)

    def ring(i, m, al0, al1, acc, n_a, n_b):
        alphas = {0: al0, 1: al1}
        for u in range(ATTN_RING):
            if u < n_a:
                stage_a(i + u + 4, s_refs[u], mx_refs[u])
            if u < n_b:
                v = (u + 2) % ATTN_RING
                m, alphas[u + 2] = stage_b(s_refs[v], mx_refs[v], p_refs[v], m)
            acc = stage_d(i + u, p_refs[u], alphas[u], acc)
        return m, alphas.get(4), alphas.get(5), acc

    m = jnp.full((1, m_cols), -jnp.inf, F32)
    acc = jnp.zeros((HEAD_DIM + ATTN_ONES, m_cols), F32)
    for u in range(ATTN_RING):
        stage_a(u, s_refs[u], mx_refs[u])
    m, al0 = stage_b(s_refs[0], mx_refs[0], p_refs[0], m)
    m, al1 = stage_b(s_refs[1], mx_refs[1], p_refs[1], m)

    def trip(j, carry):
        return ring(ATTN_RING * j, *carry, ATTN_RING, ATTN_RING)

    m, al0, al1, acc = lax.fori_loop(0, nk // ATTN_RING - 1, trip, (m, al0, al1, acc))
    _, _, _, acc = ring(nk - ATTN_RING, m, al0, al1, acc, 0, 2)
    o = acc[:HEAD_DIM] / acc[HEAD_DIM:HEAD_DIM + 1]
    for h in range(Q_PER_KV):
        o_ref[:, h * HEAD_DIM:(h + 1) * HEAD_DIM] = o[:, h * tq:(h + 1) * tq].T


def _attention(q, k, v, *, tq=512, tk=512):
    B, L, _ = q.shape
    qw = Q_PER_KV * HEAD_DIM
    nk = L // tk
    assert nk % ATTN_RING == 0 and nk >= 2 * ATTN_RING
    body = functools.partial(_attn_body, tq=tq, tk=tk, nk=nk)
    m_cols = Q_PER_KV * tq
    return pl.pallas_call(
        body,
        grid=(B, N_KV_HEADS, L // tq),
        in_specs=[
            pl.BlockSpec((None, tq, qw), lambda b, g, i: (b, i, g)),
            pl.BlockSpec((None, L, HEAD_DIM), lambda b, g, i: (b, 0, g)),
            pl.BlockSpec((None, L, HEAD_DIM), lambda b, g, i: (b, 0, g)),
        ],
        out_specs=pl.BlockSpec((None, tq, qw), lambda b, g, i: (b, i, g)),
        out_shape=jax.ShapeDtypeStruct((B, L, D_ATTN), F32),
        scratch_shapes=([pltpu.VMEM((nk, HEAD_DIM + ATTN_ONES, tk), BF16)]
                        + [pltpu.VMEM((tk, m_cols), F32)] * ATTN_RING
                        + [pltpu.VMEM((1, m_cols), F32)] * ATTN_RING
                        + [pltpu.VMEM((tk, m_cols), BF16)] * ATTN_RING),
        compiler_params=_cparams(("parallel", "parallel", "arbitrary"), 48),
        name="attn",
    )(q, k, v)


def _filt_body(feat_ref, w1_ref, b1_ref, w2_ref, b2_ref, w3_ref, fq_ref, dec_ref, kf_ref, asum_ref, *, nt):
    i = pl.program_id(0)
    hp = lax.Precision.HIGHEST
    fq = fq_ref[...]
    feat = feat_ref[...]
    h = jnp.sin(fq * (jnp.dot(feat, w1_ref[...], precision=hp, preferred_element_type=F32) + b1_ref[...]))
    h = jnp.sin(fq * (jnp.dot(h, w2_ref[...], precision=hp, preferred_element_type=F32) + b2_ref[...]))
    h_hi = h.astype(BF16)
    h_lo = (h - h_hi.astype(F32)).astype(BF16)
    h3 = (jnp.dot(h_hi, w3_ref[0], preferred_element_type=F32)
          + jnp.dot(h_lo, w3_ref[0], preferred_element_type=F32)
          + jnp.dot(h_hi, w3_ref[1], preferred_element_type=F32))
    t01 = feat[:, 0:1]
    taps = h3 * jnp.exp(-t01 * jnp.abs(dec_ref[...]))
    row = lax.broadcasted_iota(jnp.int32, (taps.shape[0], 1), 0)
    taps = jnp.where((row == 0) & (i == nt), 0.0, taps)
    for cb in range(taps.shape[1] // LANES):
        _store_time_grouped(kf_ref, cb, taps[:, cb * LANES:(cb + 1) * LANES])
    part = jnp.sum(jnp.abs(taps), axis=0, keepdims=True)

    @pl.when(i == 0)
    def _first():
        asum_ref[...] = part

    @pl.when(i > 0)
    def _rest():
        asum_ref[...] += part


def _filters(feats2, w1, b1, w2, b2, w3, fq, dec, *, tl=512):
    _, L, _ = feats2.shape
    nc = w3.shape[3]
    ng2 = FFT_N2 // FFT_G
    nt = L // tl
    body = functools.partial(_filt_body, nt=nt)
    return pl.pallas_call(
        body,
        grid=(2 * nt,),
        in_specs=[
            pl.BlockSpec((None, tl, FILT_PAD), lambda i: (i // nt, i % nt, 0)),
            _resident(w1.shape), _resident(b1.shape), _resident(w2.shape), _resident(b2.shape),
            pl.BlockSpec((None, 2, FILT_PAD, nc), lambda i: (i // nt, 0, 0, 0)),
            _resident(fq.shape),
            pl.BlockSpec((None, 1, nc), lambda i: (i // nt, 0, 0)),
        ],
        out_specs=[
            pl.BlockSpec((nc // LANES, ng2, tl // FFT_N2, FFT_G, LANES), lambda i: (0, 0, i, 0, 0)),
            pl.BlockSpec((1, nc), lambda i: (0, 0)),
        ],
        out_shape=[jax.ShapeDtypeStruct((nc // LANES, ng2, 2 * L // FFT_N2, FFT_G, LANES), F32),
                   jax.ShapeDtypeStruct((1, nc), F32)],
        compiler_params=_cparams(("arbitrary",), 48),
        name="filt",
    )(feats2, w1, b1, w2, b2, w3, fq, dec)


FFT_CB = FFT_W // LANES
FFT_CB_B = 2


def _pack_c(y):
    n = y.shape[0] // 2
    return _pack2(y[:n], y[n:])


def _unpack_lanes(w):
    hi, lo = _unpack_c(w)
    return jnp.concatenate([hi, lo], axis=1)


def _unpack_rows(w):
    re, im = _unpack_c(w)
    return jnp.concatenate([re, im], axis=0).astype(BF16)


def _flat_scratch(n, dtype, cbs, lead=()):
    return pltpu.VMEM(lead + (cbs, n * FFT_G, LANES), dtype)


def _flatten(flat_ref, block_ref, pre=()):
    rows = flat_ref.shape[-2]
    for h in range(block_ref.shape[0]):
        flat_ref[pre + (h,)] = block_ref[(h,) + pre].reshape(rows, LANES)


def _unflatten(block_ref, flat_ref, pre=()):
    n = flat_ref.shape[-2] // FFT_G
    for h in range(block_ref.shape[0]):
        block_ref[(h,) + pre] = flat_ref[pre + (h,)].reshape(n, FFT_G, LANES)


def _gather_rows(flat_ref, j, pre=()):
    n = flat_ref.shape[-2] // FFT_G
    return jnp.concatenate(
        [flat_ref[pre + (h, pl.ds(j, n, stride=FFT_G), slice(None))] for h in range(flat_ref.shape[len(pre)])],
        axis=1)


def _scatter_rows(flat_ref, j, val, pre=()):
    n = flat_ref.shape[-2] // FFT_G
    for h in range(flat_ref.shape[len(pre)]):
        flat_ref[pre + (h, pl.ds(j, n, stride=FFT_G), slice(None))] = val[:, h * LANES:(h + 1) * LANES]


def _store_blocks(ref, j, val):
    for h in range(ref.shape[0]):
        ref[h, j] = val[:, h * LANES:(h + 1) * LANES]


def _store_grouped(ref, j, val):
    groups = ref.shape[1]
    for h in range(ref.shape[0]):
        ref[h, :, j] = val[:, h * LANES:(h + 1) * LANES].reshape(groups, FFT_G, LANES)


def _fft_a_body(x_ref, fa_ref, o_ref, xf_ref):
    for b in range(2):
        _flatten(xf_ref, x_ref, (b,))
    for j in range(FFT_G):
        xs = jnp.concatenate([_unpack_lanes(_gather_rows(xf_ref, j, (b,))) for b in range(2)],
                             axis=0).astype(BF16)
        _store_grouped(o_ref, j, _pack_c(jnp.dot(fa_ref[j], xs, preferred_element_type=F32)))


def _fft_a(xv, fa, *, ncb, pb0=0):
    _, P, _, ng2, n1h, _, _ = xv.shape
    n2 = ng2 * FFT_G
    two_n1 = fa.shape[1]
    n1 = two_n1 // 2
    pbs = FFT_CB // 2
    c0 = pb0 // pbs
    return pl.pallas_call(
        _fft_a_body,
        grid=(ng2, P, ncb // FFT_CB),
        in_specs=[
            pl.BlockSpec((pbs, None, 2, None, n1h, FFT_G, LANES), lambda g, p, c: (c + c0, p, 0, g, 0, 0, 0)),
            pl.BlockSpec((FFT_G, two_n1, 2 * n1h), lambda g, p, c: (g, 0, 0)),
        ],
        out_specs=pl.BlockSpec((None, FFT_CB, n1 // FFT_G, FFT_G, FFT_G, LANES),
                               lambda g, p, c: (p, c, 0, g, 0, 0)),
        out_shape=jax.ShapeDtypeStruct((P, ncb, n1 // FFT_G, n2, FFT_G, LANES), U32),
        scratch_shapes=[_flat_scratch(n1h, U32, pbs, (2,))],
        compiler_params=_cparams(("arbitrary", "arbitrary", "arbitrary"), 32),
        name="fft_a",
    )(xv, fa)


def _fft_a_real_body(x_ref, fa_ref, o_ref, xf_ref):
    _flatten(xf_ref, x_ref)
    for j in range(FFT_G):
        xs = _gather_rows(xf_ref, j).astype(BF16)
        _store_grouped(o_ref, j, _pack_c(jnp.dot(fa_ref[j], xs, preferred_element_type=F32)))


def _fft_a_real(xv, fa_re):
    ncb, ng2, n1, _, _ = xv.shape
    n2 = ng2 * FFT_G
    return pl.pallas_call(
        _fft_a_real_body,
        grid=(ng2, ncb // FFT_CB),
        in_specs=[
            pl.BlockSpec((FFT_CB, None, n1, FFT_G, LANES), lambda g, c: (c, g, 0, 0, 0)),
            pl.BlockSpec((FFT_G, 2 * n1, n1), lambda g, c: (g, 0, 0)),
        ],
        out_specs=pl.BlockSpec((FFT_CB, n1 // FFT_G, FFT_G, FFT_G, LANES), lambda g, c: (c, 0, g, 0, 0)),
        out_shape=jax.ShapeDtypeStruct((ncb, n1 // FFT_G, n2, FFT_G, LANES), U32),
        scratch_shapes=[_flat_scratch(n1, F32, FFT_CB)],
        compiler_params=_cparams(("arbitrary", "arbitrary"), 32),
        name="fft_a_real",
    )(xv, fa_re)


def _fft_b_filt_body(s_ref, fb_ref, as_ref, o_ref, sf_ref):
    _flatten(sf_ref, s_ref)
    inv = 1.0 / as_ref[...]
    fb = fb_ref[...]
    for j in range(FFT_G):
        k = jnp.dot(fb, _unpack_rows(_gather_rows(sf_ref, j)), preferred_element_type=F32)
        _store_blocks(o_ref, j, (k * inv).astype(BF16))


def _fft_b_filt(s1f, fb, asum):
    ncb, ng1, n2, _, _ = s1f.shape
    n1 = ng1 * FFT_G
    return pl.pallas_call(
        _fft_b_filt_body,
        grid=(ng1, ncb // FFT_CB),
        in_specs=[
            pl.BlockSpec((FFT_CB, None, n2, FFT_G, LANES), lambda g, c: (c, g, 0, 0, 0)),
            _resident(fb.shape),
            pl.BlockSpec((1, FFT_W), lambda g, c: (0, c)),
        ],
        out_specs=pl.BlockSpec((FFT_CB, FFT_G, 2 * n2, LANES), lambda g, c: (c, g, 0, 0)),
        out_shape=jax.ShapeDtypeStruct((ncb, n1, 2 * n2, LANES), BF16),
        scratch_shapes=[_flat_scratch(n2, U32, FFT_CB)],
        compiler_params=_cparams(("parallel", "parallel"), 32),
        name="fft_b_filt",
    )(s1f, fb, asum)


def _fft_b_body(s_ref, kf_ref, fb_ref, fbi_ref, o_ref, sf_ref):
    n2 = FFT_N2
    _flatten(sf_ref, s_ref)
    fb = fb_ref[...]
    fbi = fbi_ref[...]
    for j in range(FFT_G):
        x = jnp.dot(fb, _unpack_rows(_gather_rows(sf_ref, j)), preferred_element_type=F32).astype(BF16)
        k = jnp.concatenate([kf_ref[h, j] for h in range(kf_ref.shape[0])], axis=1)
        xr, xi, kr, ki = x[:n2], x[n2:], k[:n2], k[n2:]
        pm = jnp.concatenate([xr * kr - xi * ki, xr * ki + xi * kr], axis=0)
        _store_grouped(o_ref, j, _pack_c(jnp.dot(fbi, pm, preferred_element_type=F32)))


def _fft_b(s1, kf, fb, fbi, *, kcb0):
    P, ncb, ng1, n2, _, _ = s1.shape
    n1 = ng1 * FFT_G
    cbs = FFT_CB_B
    k0 = kcb0 // cbs
    return pl.pallas_call(
        _fft_b_body,
        grid=(ng1, ncb // cbs, P),
        in_specs=[
            pl.BlockSpec((None, cbs, None, n2, FFT_G, LANES), lambda g, c, p: (p, c, g, 0, 0, 0)),
            pl.BlockSpec((cbs, FFT_G, 2 * n2, LANES), lambda g, c, p: (c + k0, g, 0, 0)),
            _resident(fb.shape),
            _resident(fbi.shape),
        ],
        out_specs=pl.BlockSpec((None, cbs, n2 // FFT_G, FFT_G, FFT_G, LANES),
                               lambda g, c, p: (p, c, 0, g, 0, 0)),
        out_shape=jax.ShapeDtypeStruct((P, ncb, n2 // FFT_G, n1, FFT_G, LANES), U32),
        scratch_shapes=[_flat_scratch(n2, U32, cbs)],
        compiler_params=_cparams(("parallel", "parallel", "arbitrary"), 32),
        name="fft_b",
    )(s1, kf, fb, fbi)


def _fft_c_body(z_ref, fc_ref, u_ref, gate_ref, d_ref, *rest, chain):
    if chain:
        fa_ref, o_ref, s_ref, zf_ref, uf_ref, gf_ref, of_ref = rest
    else:
        o_ref, zf_ref, uf_ref, gf_ref, of_ref = rest
    n1h = o_ref.shape[2]
    d = d_ref[...]
    _flatten(zf_ref, z_ref)
    for b in range(2):
        _flatten(uf_ref, u_ref, (b,))
        _flatten(gf_ref, gate_ref, (b,))
    for j in range(FFT_G):
        y = jnp.dot(fc_ref[j], _unpack_rows(_gather_rows(zf_ref, j)), preferred_element_type=F32)
        res = []
        for b in range(2):
            yb = y[b * n1h:(b + 1) * n1h]
            gate = _unpack_lanes(_gather_rows(gf_ref, j, (b,)))
            skip = _unpack_lanes(_gather_rows(uf_ref, j, (b,)))
            res.append(gate * (yb + skip * d))
            if chain:
                half = res[b].shape[1] // 2
                _scatter_rows(of_ref, j, _pack2(res[b][:, :half], res[b][:, half:]), (b,))
            else:
                _scatter_rows(of_ref, j, res[b], (b,))
        if chain:
            xs = jnp.concatenate(res, axis=0).astype(BF16)
            _store_grouped(s_ref, j, _pack_c(jnp.dot(fa_ref[j], xs, preferred_element_type=F32)))
    for b in range(2):
        _unflatten(o_ref, of_ref, (b,))


def _fft_c(s2, fc, uv, gv, d, *, upb0, gpb0, fa=None):
    P, ncb, ng2, n1, _, _ = s2.shape
    n1h = n1 // 2
    n2 = ng2 * FFT_G
    pbs = FFT_CB // 2
    u0 = upb0 // pbs
    g0 = gpb0 // pbs
    chain = fa is not None
    pblk = (pbs, None, 2, None, n1h, FFT_G, LANES)
    ocb, odt = (pbs, U32) if chain else (FFT_CB, F32)
    in_specs = [
        pl.BlockSpec((None, FFT_CB, None, n1, FFT_G, LANES), lambda g, p, c: (p, c, g, 0, 0, 0)),
        pl.BlockSpec((FFT_G, 2 * n1h, 2 * n1), lambda g, p, c: (g, 0, 0)),
        pl.BlockSpec(pblk, lambda g, p, c: (c + u0, p, 0, g, 0, 0, 0)),
        pl.BlockSpec(pblk, lambda g, p, c: (c + g0, p, 0, g, 0, 0, 0)),
        pl.BlockSpec((1, FFT_W), lambda g, p, c: (0, c)),
    ]
    out_specs = [pl.BlockSpec((ocb, None, 2, None, n1h, FFT_G, LANES), lambda g, p, c: (c, p, 0, g, 0, 0, 0))]
    out_shape = [jax.ShapeDtypeStruct((ncb * ocb // FFT_CB, P, 2, ng2, n1h, FFT_G, LANES), odt)]
    args = [s2, fc, uv, gv, d]
    if chain:
        in_specs.append(pl.BlockSpec((FFT_G, 2 * n1, 2 * n1h), lambda g, p, c: (g, 0, 0)))
        out_specs.append(pl.BlockSpec((None, FFT_CB, n1 // FFT_G, FFT_G, FFT_G, LANES),
                                      lambda g, p, c: (p, c, 0, g, 0, 0)))
        out_shape.append(jax.ShapeDtypeStruct((P, ncb, n1 // FFT_G, n2, FFT_G, LANES), U32))
        args.append(fa)
    return pl.pallas_call(
        functools.partial(_fft_c_body, chain=chain),
        grid=(ng2, P, ncb // FFT_CB),
        in_specs=in_specs,
        out_specs=out_specs,
        out_shape=out_shape,
        scratch_shapes=[_flat_scratch(n1, U32, FFT_CB), _flat_scratch(n1h, U32, pbs, (2,)),
                        _flat_scratch(n1h, U32, pbs, (2,)), _flat_scratch(n1h, odt, ocb, (2,))],
        compiler_params=_cparams(("arbitrary", "arbitrary", "arbitrary"), 40),
        name="fft_c",
    )(*args)


def _oproj_body(x_ref, a_ref, h_ref, ga_ref, gh_ref, wa_ref, wh_ref, o_ref):
    ma = _rms(a_ref[...], ga_ref[...]).astype(BF16)
    hy = jnp.concatenate([_load_time_grouped(h_ref, cb) for cb in range(h_ref.shape[0])], axis=1)
    mh = _rms(hy, gh_ref[...]).astype(BF16)
    o_ref[...] = (x_ref[...] + jnp.dot(ma, wa_ref[...], preferred_element_type=F32)
                  + jnp.dot(mh, wh_ref[...], preferred_element_type=F32))


def _out_proj(x, a, h, ga, gh, wa, wh, *, tm=512):
    T, D = x.shape
    ncb, _, ng2, n1h, _, _ = h.shape
    npos = n1h * FFT_N2 // tm
    return pl.pallas_call(
        _oproj_body,
        grid=(T // tm,),
        in_specs=[
            pl.BlockSpec((tm, D), lambda i: (i, 0)),
            pl.BlockSpec((tm, D_ATTN), lambda i: (i, 0)),
            pl.BlockSpec((ncb, None, ng2, tm // FFT_N2, FFT_G, LANES), lambda i: (0, i // npos, 0, i % npos, 0, 0)),
            _resident((1, D_ATTN)), _resident((1, D_HYENA)),
            _resident(wa.shape), _resident(wh.shape),
        ],
        out_specs=pl.BlockSpec((tm, D), lambda i: (i, 0)),
        out_shape=jax.ShapeDtypeStruct((T, D), F32),
        compiler_params=_cparams(("parallel",), 48),
        name="oproj",
    )(x, a, h, ga, gh, wa, wh)


@functools.lru_cache(maxsize=None)
def _rope_tables_np(L):
    rows = L // GRID_W
    row = np.repeat(np.arange(rows, dtype=np.float32), GRID_W)
    col = np.tile(np.arange(GRID_W, dtype=np.float32), rows)
    inv = (ROPE_THETA ** (-np.arange(0, ROPE_HALF, 2, dtype=np.float32) / ROPE_HALF)).astype(np.float32)
    ar = (row[:, None] * inv[None]).astype(np.float64)
    ac = (col[:, None] * inv[None]).astype(np.float64)
    cr, sr, cc, sc = np.cos(ar), np.sin(ar), np.cos(ac), np.sin(ac)
    cos_t = np.concatenate([cr, cr, cc, cc], axis=-1).astype(np.float32)
    sin_t = np.concatenate([-sr, sr, -sc, sc], axis=-1).astype(np.float32)
    return cos_t, sin_t


def _rope_tables(L):
    return tuple(jnp.asarray(t) for t in _rope_tables_np(L))


@functools.lru_cache(maxsize=None)
def _filter_feats_np(L):
    t01 = np.linspace(0.0, 1.0, L, dtype=np.float32)[:, None]
    bands = (FILTER_EMB - 1) // 2
    fr = np.linspace(1e-4, bands - 1, bands, dtype=np.float32)[None]
    w = (np.float32(2.0 * math.pi) * np.arange(L, dtype=np.float32)[:, None] / np.float32(L)).astype(np.float32)
    arg = (fr * w).astype(np.float32).astype(np.float64)
    feats = np.concatenate([t01, np.cos(arg), -np.sin(arg)], axis=-1).astype(np.float32)
    feats = np.pad(feats, ((0, 0), (0, FILT_PAD - FILTER_EMB)))
    mirrored = np.concatenate([feats[:1], feats[1:][::-1]], axis=0)
    return np.stack([feats, mirrored])


def _filter_feats(L):
    return jnp.asarray(_filter_feats_np(L))


@functools.lru_cache(maxsize=None)
def _dft_tables_np(L):
    n = 2 * L
    n2 = FFT_N2
    n1 = n // n2
    n1h = n1 // 2
    k1 = np.arange(n1, dtype=np.int64)[None, :, None]
    t1 = np.arange(n1, dtype=np.int64)[None, None, :]
    t2 = np.arange(n2, dtype=np.int64)[:, None, None]
    ang = (-2.0 * math.pi / n) * ((k1 * (n2 * t1 + t2)) % n)
    cr_full, ci_full = np.cos(ang), np.sin(ang)
    fa_re = np.concatenate([cr_full, ci_full], -2)
    cr, ci = cr_full[..., :n1h], ci_full[..., :n1h]
    fa = np.concatenate([np.concatenate([cr, -ci], -1), np.concatenate([ci, cr], -1)], -2)
    crt, cit = np.swapaxes(cr, 1, 2), np.swapaxes(ci, 1, 2)
    fc = np.concatenate([np.concatenate([crt, cit], -1), np.concatenate([-cit, crt], -1)], -2) / n
    a = np.arange(n2, dtype=np.int64)
    ang2 = (-2.0 * math.pi / n2) * ((a[:, None] * a[None, :]) % n2)
    fr, fi = np.cos(ang2), np.sin(ang2)
    fb = np.concatenate([np.concatenate([fr, -fi], -1), np.concatenate([fi, fr], -1)], -2)
    fbi = np.concatenate([np.concatenate([fr, fi], -1), np.concatenate([-fi, fr], -1)], -2)
    return tuple(t.astype(np.float32) for t in (fa, fa_re, fb, fbi, fc))


def _dft_tables(L):
    return tuple(jnp.asarray(t, dtype=BF16) for t in _dft_tables_np(L))


def _trunk(x, p):
    B, L, D = x.shape
    T = B * L
    P = B // 2
    n2 = FFT_N2
    n1h = L // n2
    C = D_HYENA
    ncb = C // LANES

    x0 = x.reshape(T, D)
    x1 = _ffn(x0, p["ffn1_norm"], p["ffn1_w13"], p["ffn1_w2"], p["final_norm"], final_norm=False)

    cos_t, sin_t = _rope_tables(L)
    q, k, v, hyc = _mix_in(x1, p["mix_norm"], p["wq"], p["wkv"], p["wh"], p["q_norm"], p["k_norm"],
                           cos_t, sin_t, p["conv_w"], p["conv_b"], L=L)
    attn = _attention(q.reshape(B, L, D_ATTN), k.reshape(B, L, D_KV), v.reshape(B, L, D_KV))
    ng2 = n2 // FFT_G
    npb = ncb // 2
    hyv = hyc.reshape(3 * npb, P, 2, ng2, n1h, FFT_G, LANES)

    fa, fa_re, fb, fbi, fc = _dft_tables(L)
    taps, asum = _filters(_filter_feats(L), p["filt_w1"], p["filt_b1"], p["filt_w2"], p["filt_b2"],
                          p["filt_w3"], p["filt_freq"], p["decay"])
    kf = _fft_b_filt(_fft_a_real(taps, fa_re), fb, asum)

    s1 = _fft_a(hyv, fa, ncb=ncb, pb0=0)
    s2 = _fft_b(s1, kf, fb, fbi, kcb0=0)
    z, s1 = _fft_c(s2, fc, hyv, hyv, p["bias0"], upb0=0, gpb0=npb, fa=fa)
    s2 = _fft_b(s1, kf, fb, fbi, kcb0=ncb)
    ho, = _fft_c(s2, fc, z, hyv, p["bias1"], upb0=0, gpb0=2 * npb)

    x2 = _out_proj(x1, attn.reshape(T, D_ATTN), ho.reshape(ncb, B, ng2, n1h, FFT_G, LANES),
                   p["gon_a"], p["gon_h"], p["wo_a"], p["wo_h"])
    x3 = _ffn(x2, p["ffn2_norm"], p["ffn2_w13"], p["ffn2_w2"], p["final_norm"], final_norm=True)
    return x3.reshape(B, L, D)


def _split_bf16(w):
    hi = w.astype(BF16)
    lo = (w - hi.astype(F32)).astype(BF16)
    return jnp.stack([hi, lo], axis=1)


def kernel(x_prompt, x_sample, ffn1_norm, ffn1_w13, ffn1_w2, mix_norm, w_in, q_norm, k_norm, conv_w, conv_b, filt_w1, filt_b1, filt_w2, filt_b2, filt_w3, filt_freq, hyena_decay, hyena_bias, group_out_norm, w_out, ffn2_norm, ffn2_w13, ffn2_w2, final_norm):
    hp = FILT_PAD - FILTER_HIDDEN
    w_in0 = w_in[0]
    p = {
        "ffn1_norm": ffn1_norm[0][None], "ffn2_norm": ffn2_norm[0][None],
        "ffn1_w13": ffn1_w13[0].astype(BF16), "ffn1_w2": ffn1_w2[0].astype(BF16),
        "ffn2_w13": ffn2_w13[0].astype(BF16), "ffn2_w2": ffn2_w2[0].astype(BF16),
        "final_norm": final_norm[None],
        "mix_norm": mix_norm[0][None],
        "wq": w_in0[:, :D_ATTN].astype(BF16),
        "wkv": w_in0[:, D_ATTN:D_ATTN + 2 * D_KV].astype(BF16),
        "wh": w_in0[:, D_ATTN + 2 * D_KV:].astype(BF16),
        "q_norm": q_norm[0][None], "k_norm": k_norm[0][None],
        "conv_w": conv_w[0], "conv_b": conv_b[0][None],
        "filt_w1": jnp.pad(filt_w1[0], ((0, FILT_PAD - FILTER_EMB), (0, hp))),
        "filt_b1": jnp.pad(filt_b1[0], (0, hp))[None],
        "filt_w2": jnp.pad(filt_w2[0], ((0, hp), (0, hp))),
        "filt_b2": jnp.pad(filt_b2[0], (0, hp))[None],
        "filt_w3": _split_bf16(jnp.pad(filt_w3[0], ((0, hp), (0, 0)))
                               .reshape(FILT_PAD, 2, 2 * D_HYENA).transpose(1, 0, 2)),
        "filt_freq": jnp.pad(filt_freq[0], (0, hp))[None],
        "decay": hyena_decay[0].reshape(2, 1, 2 * D_HYENA),
        "bias0": hyena_bias[0, 0][None], "bias1": hyena_bias[0, 1][None],
        "gon_a": group_out_norm[0, :D_ATTN][None], "gon_h": group_out_norm[0, D_ATTN:][None],
        "wo_a": w_out[0, :D_ATTN].astype(BF16), "wo_h": w_out[0, D_ATTN:].astype(BF16),
    }
    return (_trunk(x_prompt, p), _trunk(x_sample, p))
```

```python
import functools
import math

import jax
import jax.numpy as jnp
import numpy as np
from jax import lax
from jax.experimental import pallas as pl
from jax.experimental.pallas import tpu as pltpu

F32 = jnp.float32
BF16 = jnp.bfloat16
U32 = jnp.uint32

D_MODEL = 2048
GRID_W = 64
D_ATTN = D_MODEL // 2
D_HYENA = D_MODEL - D_ATTN
HEAD_DIM = 128
N_Q_HEADS = D_ATTN // HEAD_DIM
N_KV_HEADS = 2
Q_PER_KV = N_Q_HEADS // N_KV_HEADS
ROPE_HALF = HEAD_DIM // 2
ROPE_THETA = 10000.0
FILTER_EMB = 33
FILTER_HIDDEN = 64
EPS = 1e-6
D_KV = N_KV_HEADS * HEAD_DIM

LANES = 128
FILT_PAD = 128
FFT_N2 = 128
FFT_W = 512
FFT_G = 8
MIB = 1024 * 1024
LOG2_E = math.log2(math.e)


def _cparams(semantics, vmem_mib):
    return pltpu.CompilerParams(dimension_semantics=semantics,
                                vmem_limit_bytes=vmem_mib * MIB)


def _rms(x, g):
    ms = jnp.mean(x * x, axis=-1, keepdims=True)
    return x * lax.rsqrt(ms + EPS) * g


def _resident(shape):
    nd = len(shape)
    return pl.BlockSpec(shape, lambda *_: (0,) * nd, pipeline_mode=pl.Buffered(1))


def _pack2(a, b):
    ua = lax.bitcast_convert_type(a, U32)
    ub = lax.bitcast_convert_type(b, U32)
    half = jnp.uint32(0x8000)
    return ((ua + half) & jnp.uint32(0xFFFF0000)) | ((ub + half) >> 16)


def _unpack_c(w):
    hi = lax.bitcast_convert_type(w & jnp.uint32(0xFFFF0000), F32)
    lo = lax.bitcast_convert_type(w << 16, F32)
    return hi, lo


def _pair_blocks(pb):
    hi = 4 * (pb // 2) + pb % 2
    return hi, hi + 2


def _ffn_body(x_ref, g_ref, wg_ref, wu_ref, w2_ref, gf_ref, o_ref, xn_ref, *, nj, final_norm):
    j = pl.program_id(1)

    @pl.when(j == 0)
    def _init():
        x = x_ref[...]
        xn_ref[...] = _rms(x, g_ref[...]).astype(BF16)
        o_ref[...] = x

    xn = xn_ref[...]
    hg = jnp.dot(xn, wg_ref[...], preferred_element_type=F32)
    hu = jnp.dot(xn, wu_ref[...], preferred_element_type=F32)
    a = (hg * (0.5 / (1.0 + jnp.exp(-hg))) * hu).astype(BF16)
    o_ref[...] += jnp.dot(a, w2_ref[...], preferred_element_type=F32)

    if final_norm:
        @pl.when(j == nj - 1)
        def _fin():
            o_ref[...] = _rms(o_ref[...], gf_ref[...])


def _ffn(x, g, w13, w2, gf, *, final_norm, tm=1024, tf=512):
    T, D = x.shape
    ff = w2.shape[0]
    nj = ff // tf
    body = functools.partial(_ffn_body, nj=nj, final_norm=final_norm)
    return pl.pallas_call(
        body,
        grid=(T // tm, nj),
        in_specs=[
            pl.BlockSpec((tm, D), lambda i, j: (i, 0)),
            pl.BlockSpec((1, D), lambda i, j: (0, 0)),
            pl.BlockSpec((D, tf), lambda i, j: (0, j)),
            pl.BlockSpec((D, tf), lambda i, j: (0, j + nj)),
            pl.BlockSpec((tf, D), lambda i, j: (j, 0)),
            pl.BlockSpec((1, D), lambda i, j: (0, 0)),
        ],
        out_specs=pl.BlockSpec((tm, D), lambda i, j: (i, 0)),
        out_shape=jax.ShapeDtypeStruct((T, D), F32),
        scratch_shapes=[pltpu.VMEM((tm, D), BF16)],
        compiler_params=_cparams(("parallel", "arbitrary"), 58),
        name="ffn",
    )(x, g, w13, w13, w2, gf)


def _store_time_grouped(ref, cb, val):
    n1t = ref.shape[2]
    v = val.reshape(n1t * (FFT_N2 // FFT_G), FFT_G, LANES)
    for a in range(n1t):
        ref[cb, :, a] = v[a * (FFT_N2 // FFT_G):(a + 1) * (FFT_N2 // FFT_G)]


def _load_time_grouped(ref, cb):
    n1t = ref.shape[2]
    return jnp.concatenate([ref[cb, :, a].reshape(FFT_N2, LANES) for a in range(n1t)], axis=0)


MIX_HALO = 16


def _mix_body(xb_ref, x_ref, xa_ref, g_ref, wq_ref, wkv_ref, wh_ref, qn_ref, kn_ref, cos_ref, sin_ref,
              cw_ref, cb_ref, q_ref, k_ref, v_ref, hy_ref, *, npos):
    tm = x_ref.shape[0]
    x_ext = jnp.concatenate([xb_ref[...], x_ref[...], xa_ref[...]], axis=0)
    h_ext = _rms(x_ext, g_ref[...]).astype(BF16)
    h = h_ext[MIX_HALO:MIX_HALO + tm]
    lane = lax.broadcasted_iota(jnp.int32, (tm, HEAD_DIM), 1)
    low_half = (lane % (2 * (ROPE_HALF // 2))) < (ROPE_HALF // 2)
    c = cos_ref[...]
    s = sin_ref[...]

    def head(p, gain, scale):
        y = _rms(p, gain)
        partner = jnp.where(low_half, pltpu.roll(y, HEAD_DIM - ROPE_HALF // 2, 1),
                            pltpu.roll(y, ROPE_HALF // 2, 1))
        y = y * c + partner * s
        return y if scale is None else y * scale

    q = jnp.dot(h, wq_ref[...], preferred_element_type=F32)
    for hd in range(N_Q_HEADS):
        sl = slice(hd * HEAD_DIM, (hd + 1) * HEAD_DIM)
        q_ref[:, sl] = head(q[:, sl], qn_ref[...], HEAD_DIM ** -0.5 * LOG2_E).astype(BF16)
    kv = jnp.dot(h, wkv_ref[...], preferred_element_type=F32)
    for hd in range(N_KV_HEADS):
        sl = slice(hd * HEAD_DIM, (hd + 1) * HEAD_DIM)
        k_ref[:, sl] = head(kv[:, sl], kn_ref[...], None).astype(BF16)
    v_ref[...] = kv[:, D_KV:].astype(BF16)

    hy = jnp.dot(h_ext, wh_ref[...], preferred_element_type=F32)
    pos = pl.program_id(0) % npos
    rid = lax.broadcasted_iota(jnp.int32, (tm, 1), 0)
    before = jnp.where((rid == 0) & (pos == 0), 0.0, hy[MIX_HALO - 1:MIX_HALO - 1 + tm])
    after = jnp.where((rid == tm - 1) & (pos == npos - 1), 0.0, hy[MIX_HALO + 1:MIX_HALO + 1 + tm])
    hyc = (before * cw_ref[0:1, :] + hy[MIX_HALO:MIX_HALO + tm] * cw_ref[1:2, :]
           + after * cw_ref[2:3, :] + cb_ref[...])
    for pb in range(hy_ref.shape[0]):
        hi, lo = _pair_blocks(pb)
        _store_time_grouped(hy_ref, pb, _pack2(hyc[:, hi * LANES:(hi + 1) * LANES],
                                               hyc[:, lo * LANES:(lo + 1) * LANES]))


def _mix_in(x, g, wq, wkv, wh, qn, kn, cos_t, sin_t, cw, cb, *, L, tm=512):
    T, D = x.shape
    npos = L // tm
    ng2 = FFT_N2 // FFT_G
    hb = tm // MIX_HALO
    nhb = T // MIX_HALO
    return pl.pallas_call(
        functools.partial(_mix_body, npos=npos),
        grid=(T // tm,),
        in_specs=[
            pl.BlockSpec((MIX_HALO, D), lambda i: (jnp.maximum(i * hb - 1, 0), 0)),
            pl.BlockSpec((tm, D), lambda i: (i, 0)),
            pl.BlockSpec((MIX_HALO, D), lambda i: (jnp.minimum((i + 1) * hb, nhb - 1), 0)),
            _resident((1, D)),
            _resident(wq.shape),
            _resident(wkv.shape),
            _resident(wh.shape),
            _resident((1, HEAD_DIM)),
            _resident((1, HEAD_DIM)),
            pl.BlockSpec((tm, HEAD_DIM), lambda i: (i % npos, 0)),
            pl.BlockSpec((tm, HEAD_DIM), lambda i: (i % npos, 0)),
            _resident(cw.shape),
            _resident(cb.shape),
        ],
        out_specs=[
            pl.BlockSpec((tm, D_ATTN), lambda i: (i, 0)),
            pl.BlockSpec((tm, D_KV), lambda i: (i, 0)),
            pl.BlockSpec((tm, D_KV), lambda i: (i, 0)),
            pl.BlockSpec((3 * D_HYENA // (2 * LANES), None, ng2, tm // FFT_N2, FFT_G, LANES),
                         lambda i: (0, i // npos, 0, i % npos, 0, 0)),
        ],
        out_shape=[
            jax.ShapeDtypeStruct((T, D_ATTN), BF16),
            jax.ShapeDtypeStruct((T, D_KV), BF16),
            jax.ShapeDtypeStruct((T, D_KV), BF16),
            jax.ShapeDtypeStruct((3 * D_HYENA // (2 * LANES), T // L, ng2, L // FFT_N2, FFT_G, LANES), U32),
        ],
        compiler_params=_cparams(("parallel",), 58),
        name="mix_in",
    )(x, x, x, g, wq, wkv, wh, qn, kn, cos_t, sin_t, cw, cb)


ATTN_ROWS = 32
ATTN_RING = 4
ATTN_ONES = 16


def _attn_body(q_ref, k_ref, v_ref, o_ref, vt_ref, *bufs, tq, tk, nk):
    s_refs, mx_refs, p_refs = (bufs[k * ATTN_RING:(k + 1) * ATTN_RING] for k in range(3))

    @pl.when(pl.program_id(2) == 0)
    def _transpose_v():
        for c in range(nk):
            vt_ref[c, :HEAD_DIM] = v_ref[pl.ds(c * tk, tk), :].astype(F32).T.astype(BF16)
            vt_ref[c, HEAD_DIM:] = jnp.ones((ATTN_ONES, tk), BF16)

    q = jnp.concatenate([q_ref[:, h * HEAD_DIM:(h + 1) * HEAD_DIM] for h in range(Q_PER_KV)], axis=0)
    m_cols = Q_PER_KV * tq
    nchunk = tk // ATTN_ROWS

    def stage_a(i, s_ref, mx_ref):
        kc = k_ref[pl.ds(pl.multiple_of(i * tk, tk), tk), :]
        s = lax.dot_general(kc, q, (((1,), (1,)), ((), ())), preferred_element_type=F32)
        s_ref[...] = s
        mx_ref[...] = jnp.max(s, axis=0, keepdims=True)

    def stage_b(s_ref, mx_ref, p_ref, m):
        m_new = jnp.maximum(m, mx_ref[...])
        alpha = jnp.exp2(m - m_new)
        for c in range(nchunk):
            rows = slice(c * ATTN_ROWS, (c + 1) * ATTN_ROWS)
            p_ref[rows, :] = jnp.exp2((s_ref[rows, :] - m_new).astype(BF16))
        return m_new, alpha

    def stage_d(i, p_ref, alpha, acc):
        return alpha * acc + jnp.dot(vt_ref[i], p_ref[...], preferred_element_type=F32)

    def ring(i, m, al0, al1, acc, n_a, n_b):
        alphas = {0: al0, 1: al1}
        for u in range(ATTN_RING):
            if u < n_a:
                stage_a(i + u + 4, s_refs[u], mx_refs[u])
            if u < n_b:
                v = (u + 2) % ATTN_RING
                m, alphas[u + 2] = stage_b(s_refs[v], mx_refs[v], p_refs[v], m)
            acc = stage_d(i + u, p_refs[u], alphas[u], acc)
        return m, alphas.get(4), alphas.get(5), acc

    m = jnp.full((1, m_cols), -jnp.inf, F32)
    acc = jnp.zeros((HEAD_DIM + ATTN_ONES, m_cols), F32)
    for u in range(ATTN_RING):
        stage_a(u, s_refs[u], mx_refs[u])
    m, al0 = stage_b(s_refs[0], mx_refs[0], p_refs[0], m)
    m, al1 = stage_b(s_refs[1], mx_refs[1], p_refs[1], m)

    def trip(j, carry):
        return ring(ATTN_RING * j, *carry, ATTN_RING, ATTN_RING)

    m, al0, al1, acc = lax.fori_loop(0, nk // ATTN_RING - 1, trip, (m, al0, al1, acc))
    _, _, _, acc = ring(nk - ATTN_RING, m, al0, al1, acc, 0, 2)
    o = acc[:HEAD_DIM] / acc[HEAD_DIM:HEAD_DIM + 1]
    for h in range(Q_PER_KV):
        o_ref[:, h * HEAD_DIM:(h + 1) * HEAD_DIM] = o[:, h * tq:(h + 1) * tq].T


def _attention(q, k, v, *, tq=512, tk=512):
    B, L, _ = q.shape
    qw = Q_PER_KV * HEAD_DIM
    nk = L // tk
    assert nk % ATTN_RING == 0 and nk >= 2 * ATTN_RING
    body = functools.partial(_attn_body, tq=tq, tk=tk, nk=nk)
    m_cols = Q_PER_KV * tq
    return pl.pallas_call(
        body,
        grid=(B, N_KV_HEADS, L // tq),
        in_specs=[
            pl.BlockSpec((None, tq, qw), lambda b, g, i: (b, i, g)),
            pl.BlockSpec((None, L, HEAD_DIM), lambda b, g, i: (b, 0, g)),
            pl.BlockSpec((None, L, HEAD_DIM), lambda b, g, i: (b, 0, g)),
        ],
        out_specs=pl.BlockSpec((None, tq, qw), lambda b, g, i: (b, i, g)),
        out_shape=jax.ShapeDtypeStruct((B, L, D_ATTN), F32),
        scratch_shapes=([pltpu.VMEM((nk, HEAD_DIM + ATTN_ONES, tk), BF16)]
                        + [pltpu.VMEM((tk, m_cols), F32)] * ATTN_RING
                        + [pltpu.VMEM((1, m_cols), F32)] * ATTN_RING
                        + [pltpu.VMEM((tk, m_cols), BF16)] * ATTN_RING),
        compiler_params=_cparams(("parallel", "parallel", "arbitrary"), 48),
        name="attn",
    )(q, k, v)


def _filt_body(feat_ref, w1_ref, b1_ref, w2_ref, b2_ref, w3_ref, fq_ref, dec_ref, kf_ref, asum_ref, *, nt):
    i = pl.program_id(0)
    hp = lax.Precision.HIGHEST
    fq = fq_ref[...]
    feat = feat_ref[...]
    h = jnp.sin(fq * (jnp.dot(feat, w1_ref[...], precision=hp, preferred_element_type=F32) + b1_ref[...]))
    h = jnp.sin(fq * (jnp.dot(h, w2_ref[...], precision=hp, preferred_element_type=F32) + b2_ref[...]))
    h_hi = h.astype(BF16)
    h_lo = (h - h_hi.astype(F32)).astype(BF16)
    h3 = (jnp.dot(h_hi, w3_ref[0], preferred_element_type=F32)
          + jnp.dot(h_lo, w3_ref[0], preferred_element_type=F32)
          + jnp.dot(h_hi, w3_ref[1], preferred_element_type=F32))
    t01 = feat[:, 0:1]
    taps = h3 * jnp.exp(-t01 * jnp.abs(dec_ref[...]))
    row = lax.broadcasted_iota(jnp.int32, (taps.shape[0], 1), 0)
    taps = jnp.where((row == 0) & (i == nt), 0.0, taps)
    for cb in range(taps.shape[1] // LANES):
        _store_time_grouped(kf_ref, cb, taps[:, cb * LANES:(cb + 1) * LANES])
    part = jnp.sum(jnp.abs(taps), axis=0, keepdims=True)

    @pl.when(i == 0)
    def _first():
        asum_ref[...] = part

    @pl.when(i > 0)
    def _rest():
        asum_ref[...] += part


def _filters(feats2, w1, b1, w2, b2, w3, fq, dec, *, tl=512):
    _, L, _ = feats2.shape
    nc = w3.shape[3]
    ng2 = FFT_N2 // FFT_G
    nt = L // tl
    body = functools.partial(_filt_body, nt=nt)
    return pl.pallas_call(
        body,
        grid=(2 * nt,),
        in_specs=[
            pl.BlockSpec((None, tl, FILT_PAD), lambda i: (i // nt, i % nt, 0)),
            _resident(w1.shape), _resident(b1.shape), _resident(w2.shape), _resident(b2.shape),
            pl.BlockSpec((None, 2, FILT_PAD, nc), lambda i: (i // nt, 0, 0, 0)),
            _resident(fq.shape),
            pl.BlockSpec((None, 1, nc), lambda i: (i // nt, 0, 0)),
        ],
        out_specs=[
            pl.BlockSpec((nc // LANES, ng2, tl // FFT_N2, FFT_G, LANES), lambda i: (0, 0, i, 0, 0)),
            pl.BlockSpec((1, nc), lambda i: (0, 0)),
        ],
        out_shape=[jax.ShapeDtypeStruct((nc // LANES, ng2, 2 * L // FFT_N2, FFT_G, LANES), F32),
                   jax.ShapeDtypeStruct((1, nc), F32)],
        compiler_params=_cparams(("arbitrary",), 48),
        name="filt",
    )(feats2, w1, b1, w2, b2, w3, fq, dec)


FFT_CB = FFT_W // LANES
FFT_CB_B = 2


def _pack_c(y):
    n = y.shape[0] // 2
    return _pack2(y[:n], y[n:])


def _unpack_lanes(w):
    hi, lo = _unpack_c(w)
    return jnp.concatenate([hi, lo], axis=1)


def _unpack_rows(w):
    re, im = _unpack_c(w)
    return jnp.concatenate([re, im], axis=0).astype(BF16)


def _flat_scratch(n, dtype, cbs, lead=()):
    return pltpu.VMEM(lead + (cbs, n * FFT_G, LANES), dtype)


def _flatten(flat_ref, block_ref, pre=()):
    rows = flat_ref.shape[-2]
    for h in range(block_ref.shape[0]):
        flat_ref[pre + (h,)] = block_ref[(h,) + pre].reshape(rows, LANES)


def _unflatten(block_ref, flat_ref, pre=()):
    n = flat_ref.shape[-2] // FFT_G
    for h in range(block_ref.shape[0]):
        block_ref[(h,) + pre] = flat_ref[pre + (h,)].reshape(n, FFT_G, LANES)


def _gather_rows(flat_ref, j, pre=()):
    n = flat_ref.shape[-2] // FFT_G
    return jnp.concatenate(
        [flat_ref[pre + (h, pl.ds(j, n, stride=FFT_G), slice(None))] for h in range(flat_ref.shape[len(pre)])],
        axis=1)


def _scatter_rows(flat_ref, j, val, pre=()):
    n = flat_ref.shape[-2] // FFT_G
    for h in range(flat_ref.shape[len(pre)]):
        flat_ref[pre + (h, pl.ds(j, n, stride=FFT_G), slice(None))] = val[:, h * LANES:(h + 1) * LANES]


def _store_blocks(ref, j, val):
    for h in range(ref.shape[0]):
        ref[h, j] = val[:, h * LANES:(h + 1) * LANES]


def _store_grouped(ref, j, val):
    groups = ref.shape[1]
    for h in range(ref.shape[0]):
        ref[h, :, j] = val[:, h * LANES:(h + 1) * LANES].reshape(groups, FFT_G, LANES)


def _fft_a_body(x_ref, fa_ref, o_ref, xf_ref):
    for b in range(2):
        _flatten(xf_ref, x_ref, (b,))
    for j in range(FFT_G):
        xs = jnp.concatenate([_unpack_lanes(_gather_rows(xf_ref, j, (b,))) for b in range(2)],
                             axis=0).astype(BF16)
        _store_grouped(o_ref, j, _pack_c(jnp.dot(fa_ref[j], xs, preferred_element_type=F32)))


def _fft_a(xv, fa, *, ncb, pb0=0):
    _, P, _, ng2, n1h, _, _ = xv.shape
    n2 = ng2 * FFT_G
    two_n1 = fa.shape[1]
    n1 = two_n1 // 2
    pbs = FFT_CB // 2
    c0 = pb0 // pbs
    return pl.pallas_call(
        _fft_a_body,
        grid=(ng2, P, ncb // FFT_CB),
        in_specs=[
            pl.BlockSpec((pbs, None, 2, None, n1h, FFT_G, LANES), lambda g, p, c: (c + c0, p, 0, g, 0, 0, 0)),
            pl.BlockSpec((FFT_G, two_n1, 2 * n1h), lambda g, p, c: (g, 0, 0)),
        ],
        out_specs=pl.BlockSpec((None, FFT_CB, n1 // FFT_G, FFT_G, FFT_G, LANES),
                               lambda g, p, c: (p, c, 0, g, 0, 0)),
        out_shape=jax.ShapeDtypeStruct((P, ncb, n1 // FFT_G, n2, FFT_G, LANES), U32),
        scratch_shapes=[_flat_scratch(n1h, U32, pbs, (2,))],
        compiler_params=_cparams(("arbitrary", "arbitrary", "arbitrary"), 32),
        name="fft_a",
    )(xv, fa)


def _fft_a_real_body(x_ref, fa_ref, o_ref, xf_ref):
    _flatten(xf_ref, x_ref)
    for j in range(FFT_G):
        xs = _gather_rows(xf_ref, j).astype(BF16)
        _store_grouped(o_ref, j, _pack_c(jnp.dot(fa_ref[j], xs, preferred_element_type=F32)))


def _fft_a_real(xv, fa_re):
    ncb, ng2, n1, _, _ = xv.shape
    n2 = ng2 * FFT_G
    return pl.pallas_call(
        _fft_a_real_body,
        grid=(ng2, ncb // FFT_CB),
        in_specs=[
            pl.BlockSpec((FFT_CB, None, n1, FFT_G, LANES), lambda g, c: (c, g, 0, 0, 0)),
            pl.BlockSpec((FFT_G, 2 * n1, n1), lambda g, c: (g, 0, 0)),
        ],
        out_specs=pl.BlockSpec((FFT_CB, n1 // FFT_G, FFT_G, FFT_G, LANES), lambda g, c: (c, 0, g, 0, 0)),
        out_shape=jax.ShapeDtypeStruct((ncb, n1 // FFT_G, n2, FFT_G, LANES), U32),
        scratch_shapes=[_flat_scratch(n1, F32, FFT_CB)],
        compiler_params=_cparams(("arbitrary", "arbitrary"), 32),
        name="fft_a_real",
    )(xv, fa_re)


def _fft_b_filt_body(s_ref, fb_ref, as_ref, o_ref, sf_ref):
    _flatten(sf_ref, s_ref)
    inv = 1.0 / as_ref[...]
    fb = fb_ref[...]
    for j in range(FFT_G):
        k = jnp.dot(fb, _unpack_rows(_gather_rows(sf_ref, j)), preferred_element_type=F32)
        _store_blocks(o_ref, j, (k * inv).astype(BF16))


def _fft_b_filt(s1f, fb, asum):
    ncb, ng1, n2, _, _ = s1f.shape
    n1 = ng1 * FFT_G
    return pl.pallas_call(
        _fft_b_filt_body,
        grid=(ng1, ncb // FFT_CB),
        in_specs=[
            pl.BlockSpec((FFT_CB, None, n2, FFT_G, LANES), lambda g, c: (c, g, 0, 0, 0)),
            _resident(fb.shape),
            pl.BlockSpec((1, FFT_W), lambda g, c: (0, c)),
        ],
        out_specs=pl.BlockSpec((FFT_CB, FFT_G, 2 * n2, LANES), lambda g, c: (c, g, 0, 0)),
        out_shape=jax.ShapeDtypeStruct((ncb, n1, 2 * n2, LANES), BF16),
        scratch_shapes=[_flat_scratch(n2, U32, FFT_CB)],
        compiler_params=_cparams(("parallel", "parallel"), 32),
        name="fft_b_filt",
    )(s1f, fb, asum)


def _fft_b_body(s_ref, kf_ref, fb_ref, fbi_ref, o_ref, sf_ref):
    n2 = FFT_N2
    fb = fb_ref[...]
    fbi = fbi_ref[...]
    for p in range(s_ref.shape[0]):
        _flatten(sf_ref.at[p], s_ref.at[p])
    for j in range(FFT_G):
        k = jnp.concatenate([kf_ref[h, j] for h in range(kf_ref.shape[0])], axis=1)
        kr, ki = k[:n2], k[n2:]
        for p in range(s_ref.shape[0]):
            x = jnp.dot(fb, _unpack_rows(_gather_rows(sf_ref.at[p], j)), preferred_element_type=F32).astype(BF16)
            xr, xi = x[:n2], x[n2:]
            pm = jnp.concatenate([xr * kr - xi * ki, xr * ki + xi * kr], axis=0)
            _store_grouped(o_ref.at[p], j, _pack_c(jnp.dot(fbi, pm, preferred_element_type=F32)))


def _fft_b(s1, kf, fb, fbi, *, kcb0):
    P, ncb, ng1, n2, _, _ = s1.shape
    n1 = ng1 * FFT_G
    cbs = FFT_CB_B
    k0 = kcb0 // cbs
    return pl.pallas_call(
        _fft_b_body,
        grid=(ng1, ncb // cbs),
        in_specs=[
            pl.BlockSpec((P, cbs, None, n2, FFT_G, LANES), lambda g, c: (0, c, g, 0, 0, 0)),
            pl.BlockSpec((cbs, FFT_G, 2 * n2, LANES), lambda g, c: (c + k0, g, 0, 0)),
            _resident(fb.shape),
            _resident(fbi.shape),
        ],
        out_specs=pl.BlockSpec((P, cbs, n2 // FFT_G, FFT_G, FFT_G, LANES), lambda g, c: (0, c, 0, g, 0, 0)),
        out_shape=jax.ShapeDtypeStruct((P, ncb, n2 // FFT_G, n1, FFT_G, LANES), U32),
        scratch_shapes=[_flat_scratch(n2, U32, cbs, (P,))],
        compiler_params=_cparams(("parallel", "parallel"), 32),
        name="fft_b",
    )(s1, kf, fb, fbi)


def _fft_c_body(z_ref, fc_ref, u_ref, gate_ref, d_ref, *rest, chain):
    if chain:
        fa_ref, o_ref, s_ref, zf_ref, uf_ref, gf_ref, of_ref = rest
    else:
        o_ref, zf_ref, uf_ref, gf_ref, of_ref = rest
    n1h = o_ref.shape[2]
    d = d_ref[...]
    _flatten(zf_ref, z_ref)
    for b in range(2):
        _flatten(uf_ref, u_ref, (b,))
        _flatten(gf_ref, gate_ref, (b,))
    for j in range(FFT_G):
        y = jnp.dot(fc_ref[j], _unpack_rows(_gather_rows(zf_ref, j)), preferred_element_type=F32)
        res = []
        for b in range(2):
            yb = y[b * n1h:(b + 1) * n1h]
            gate = _unpack_lanes(_gather_rows(gf_ref, j, (b,)))
            skip = _unpack_lanes(_gather_rows(uf_ref, j, (b,)))
            res.append(gate * (yb + skip * d))
            if chain:
                half = res[b].shape[1] // 2
                _scatter_rows(of_ref, j, _pack2(res[b][:, :half], res[b][:, half:]), (b,))
            else:
                _scatter_rows(of_ref, j, res[b], (b,))
        if chain:
            xs = jnp.concatenate(res, axis=0).astype(BF16)
            _store_grouped(s_ref, j, _pack_c(jnp.dot(fa_ref[j], xs, preferred_element_type=F32)))
    for b in range(2):
        _unflatten(o_ref, of_ref, (b,))


def _fft_c(s2, fc, uv, gv, d, *, upb0, gpb0, fa=None):
    P, ncb, ng2, n1, _, _ = s2.shape
    n1h = n1 // 2
    n2 = ng2 * FFT_G
    pbs = FFT_CB // 2
    u0 = upb0 // pbs
    g0 = gpb0 // pbs
    chain = fa is not None
    pblk = (pbs, None, 2, None, n1h, FFT_G, LANES)
    ocb, odt = (pbs, U32) if chain else (FFT_CB, F32)
    in_specs = [
        pl.BlockSpec((None, FFT_CB, None, n1, FFT_G, LANES), lambda g, p, c: (p, c, g, 0, 0, 0)),
        pl.BlockSpec((FFT_G, 2 * n1h, 2 * n1), lambda g, p, c: (g, 0, 0)),
        pl.BlockSpec(pblk, lambda g, p, c: (c + u0, p, 0, g, 0, 0, 0)),
        pl.BlockSpec(pblk, lambda g, p, c: (c + g0, p, 0, g, 0, 0, 0)),
        pl.BlockSpec((1, FFT_W), lambda g, p, c: (0, c)),
    ]
    out_specs = [pl.BlockSpec((ocb, None, 2, None, n1h, FFT_G, LANES), lambda g, p, c: (c, p, 0, g, 0, 0, 0))]
    out_shape = [jax.ShapeDtypeStruct((ncb * ocb // FFT_CB, P, 2, ng2, n1h, FFT_G, LANES), odt)]
    args = [s2, fc, uv, gv, d]
    if chain:
        in_specs.append(pl.BlockSpec((FFT_G, 2 * n1, 2 * n1h), lambda g, p, c: (g, 0, 0)))
        out_specs.append(pl.BlockSpec((None, FFT_CB, n1 // FFT_G, FFT_G, FFT_G, LANES),
                                      lambda g, p, c: (p, c, 0, g, 0, 0)))
        out_shape.append(jax.ShapeDtypeStruct((P, ncb, n1 // FFT_G, n2, FFT_G, LANES), U32))
        args.append(fa)
    return pl.pallas_call(
        functools.partial(_fft_c_body, chain=chain),
        grid=(ng2, P, ncb // FFT_CB),
        in_specs=in_specs,
        out_specs=out_specs,
        out_shape=out_shape,
        scratch_shapes=[_flat_scratch(n1, U32, FFT_CB), _flat_scratch(n1h, U32, pbs, (2,)),
                        _flat_scratch(n1h, U32, pbs, (2,)), _flat_scratch(n1h, odt, ocb, (2,))],
        compiler_params=_cparams(("arbitrary", "arbitrary", "arbitrary"), 40),
        name="fft_c",
    )(*args)


def _oproj_body(x_ref, a_ref, h_ref, ga_ref, gh_ref, wa_ref, wh_ref, o_ref):
    ma = _rms(a_ref[...], ga_ref[...]).astype(BF16)
    hy = jnp.concatenate([_load_time_grouped(h_ref, cb) for cb in range(h_ref.shape[0])], axis=1)
    mh = _rms(hy, gh_ref[...]).astype(BF16)
    o_ref[...] = (x_ref[...] + jnp.dot(ma, wa_ref[...], preferred_element_type=F32)
                  + jnp.dot(mh, wh_ref[...], preferred_element_type=F32))


def _out_proj(x, a, h, ga, gh, wa, wh, *, tm=512):
    T, D = x.shape
    ncb, _, ng2, n1h, _, _ = h.shape
    npos = n1h * FFT_N2 // tm
    return pl.pallas_call(
        _oproj_body,
        grid=(T // tm,),
        in_specs=[
            pl.BlockSpec((tm, D), lambda i: (i, 0)),
            pl.BlockSpec((tm, D_ATTN), lambda i: (i, 0)),
            pl.BlockSpec((ncb, None, ng2, tm // FFT_N2, FFT_G, LANES), lambda i: (0, i // npos, 0, i % npos, 0, 0)),
            _resident((1, D_ATTN)), _resident((1, D_HYENA)),
            _resident(wa.shape), _resident(wh.shape),
        ],
        out_specs=pl.BlockSpec((tm, D), lambda i: (i, 0)),
        out_shape=jax.ShapeDtypeStruct((T, D), F32),
        compiler_params=_cparams(("parallel",), 48),
        name="oproj",
    )(x, a, h, ga, gh, wa, wh)


@functools.lru_cache(maxsize=None)
def _rope_tables_np(L):
    rows = L // GRID_W
    row = np.repeat(np.arange(rows, dtype=np.float32), GRID_W)
    col = np.tile(np.arange(GRID_W, dtype=np.float32), rows)
    inv = (ROPE_THETA ** (-np.arange(0, ROPE_HALF, 2, dtype=np.float32) / ROPE_HALF)).astype(np.float32)
    ar = (row[:, None] * inv[None]).astype(np.float64)
    ac = (col[:, None] * inv[None]).astype(np.float64)
    cr, sr, cc, sc = np.cos(ar), np.sin(ar), np.cos(ac), np.sin(ac)
    cos_t = np.concatenate([cr, cr, cc, cc], axis=-1).astype(np.float32)
    sin_t = np.concatenate([-sr, sr, -sc, sc], axis=-1).astype(np.float32)
    return cos_t, sin_t


def _rope_tables(L):
    return tuple(jnp.asarray(t) for t in _rope_tables_np(L))


@functools.lru_cache(maxsize=None)
def _filter_feats_np(L):
    t01 = np.linspace(0.0, 1.0, L, dtype=np.float32)[:, None]
    bands = (FILTER_EMB - 1) // 2
    fr = np.linspace(1e-4, bands - 1, bands, dtype=np.float32)[None]
    w = (np.float32(2.0 * math.pi) * np.arange(L, dtype=np.float32)[:, None] / np.float32(L)).astype(np.float32)
    arg = (fr * w).astype(np.float32).astype(np.float64)
    feats = np.concatenate([t01, np.cos(arg), -np.sin(arg)], axis=-1).astype(np.float32)
    feats = np.pad(feats, ((0, 0), (0, FILT_PAD - FILTER_EMB)))
    mirrored = np.concatenate([feats[:1], feats[1:][::-1]], axis=0)
    return np.stack([feats, mirrored])


def _filter_feats(L):
    return jnp.asarray(_filter_feats_np(L))


@functools.lru_cache(maxsize=None)
def _dft_tables_np(L):
    n = 2 * L
    n2 = FFT_N2
    n1 = n // n2
    n1h = n1 // 2
    k1 = np.arange(n1, dtype=np.int64)[None, :, None]
    t1 = np.arange(n1, dtype=np.int64)[None, None, :]
    t2 = np.arange(n2, dtype=np.int64)[:, None, None]
    ang = (-2.0 * math.pi / n) * ((k1 * (n2 * t1 + t2)) % n)
    cr_full, ci_full = np.cos(ang), np.sin(ang)
    fa_re = np.concatenate([cr_full, ci_full], -2)
    cr, ci = cr_full[..., :n1h], ci_full[..., :n1h]
    fa = np.concatenate([np.concatenate([cr, -ci], -1), np.concatenate([ci, cr], -1)], -2)
    crt, cit = np.swapaxes(cr, 1, 2), np.swapaxes(ci, 1, 2)
    fc = np.concatenate([np.concatenate([crt, cit], -1), np.concatenate([-cit, crt], -1)], -2) / n
    a = np.arange(n2, dtype=np.int64)
    ang2 = (-2.0 * math.pi / n2) * ((a[:, None] * a[None, :]) % n2)
    fr, fi = np.cos(ang2), np.sin(ang2)
    fb = np.concatenate([np.concatenate([fr, -fi], -1), np.concatenate([fi, fr], -1)], -2)
    fbi = np.concatenate([np.concatenate([fr, fi], -1), np.concatenate([-fi, fr], -1)], -2)
    return tuple(t.astype(np.float32) for t in (fa, fa_re, fb, fbi, fc))


def _dft_tables(L):
    return tuple(jnp.asarray(t, dtype=BF16) for t in _dft_tables_np(L))


def _trunk(x, p):
    B, L, D = x.shape
    T = B * L
    P = B // 2
    n2 = FFT_N2
    n1h = L // n2
    C = D_HYENA
    ncb = C // LANES

    x0 = x.reshape(T, D)
    x1 = _ffn(x0, p["ffn1_norm"], p["ffn1_w13"], p["ffn1_w2"], p["final_norm"], final_norm=False)

    cos_t, sin_t = _rope_tables(L)
    q, k, v, hyc = _mix_in(x1, p["mix_norm"], p["wq"], p["wkv"], p["wh"], p["q_norm"], p["k_norm"],
                           cos_t, sin_t, p["conv_w"], p["conv_b"], L=L)
    attn = _attention(q.reshape(B, L, D_ATTN), k.reshape(B, L, D_KV), v.reshape(B, L, D_KV))
    ng2 = n2 // FFT_G
    npb = ncb // 2
    hyv = hyc.reshape(3 * npb, P, 2, ng2, n1h, FFT_G, LANES)

    fa, fa_re, fb, fbi, fc = _dft_tables(L)
    taps, asum = _filters(_filter_feats(L), p["filt_w1"], p["filt_b1"], p["filt_w2"], p["filt_b2"],
                          p["filt_w3"], p["filt_freq"], p["decay"])
    kf = _fft_b_filt(_fft_a_real(taps, fa_re), fb, asum)

    s1 = _fft_a(hyv, fa, ncb=ncb, pb0=0)
    s2 = _fft_b(s1, kf, fb, fbi, kcb0=0)
    z, s1 = _fft_c(s2, fc, hyv, hyv, p["bias0"], upb0=0, gpb0=npb, fa=fa)
    s2 = _fft_b(s1, kf, fb, fbi, kcb0=ncb)
    ho, = _fft_c(s2, fc, z, hyv, p["bias1"], upb0=0, gpb0=2 * npb)

    x2 = _out_proj(x1, attn.reshape(T, D_ATTN), ho.reshape(ncb, B, ng2, n1h, FFT_G, LANES),
                   p["gon_a"], p["gon_h"], p["wo_a"], p["wo_h"])
    x3 = _ffn(x2, p["ffn2_norm"], p["ffn2_w13"], p["ffn2_w2"], p["final_norm"], final_norm=True)
    return x3.reshape(B, L, D)


def _split_bf16(w):
    hi = w.astype(BF16)
    lo = (w - hi.astype(F32)).astype(BF16)
    return jnp.stack([hi, lo], axis=1)


def kernel(x_prompt, x_sample, ffn1_norm, ffn1_w13, ffn1_w2, mix_norm, w_in, q_norm, k_norm, conv_w, conv_b, filt_w1, filt_b1, filt_w2, filt_b2, filt_w3, filt_freq, hyena_decay, hyena_bias, group_out_norm, w_out, ffn2_norm, ffn2_w13, ffn2_w2, final_norm):
    hp = FILT_PAD - FILTER_HIDDEN
    w_in0 = w_in[0]
    p = {
        "ffn1_norm": ffn1_norm[0][None], "ffn2_norm": ffn2_norm[0][None],
        "ffn1_w13": ffn1_w13[0].astype(BF16), "ffn1_w2": ffn1_w2[0].astype(BF16),
        "ffn2_w13": ffn2_w13[0].astype(BF16), "ffn2_w2": ffn2_w2[0].astype(BF16),
        "final_norm": final_norm[None],
        "mix_norm": mix_norm[0][None],
        "wq": w_in0[:, :D_ATTN].astype(BF16),
        "wkv": w_in0[:, D_ATTN:D_ATTN + 2 * D_KV].astype(BF16),
        "wh": w_in0[:, D_ATTN + 2 * D_KV:].astype(BF16),
        "q_norm": q_norm[0][None], "k_norm": k_norm[0][None],
        "conv_w": conv_w[0], "conv_b": conv_b[0][None],
        "filt_w1": jnp.pad(filt_w1[0], ((0, FILT_PAD - FILTER_EMB), (0, hp))),
        "filt_b1": jnp.pad(filt_b1[0], (0, hp))[None],
        "filt_w2": jnp.pad(filt_w2[0], ((0, hp), (0, hp))),
        "filt_b2": jnp.pad(filt_b2[0], (0, hp))[None],
        "filt_w3": _split_bf16(jnp.pad(filt_w3[0], ((0, hp), (0, 0)))
                               .reshape(FILT_PAD, 2, 2 * D_HYENA).transpose(1, 0, 2)),
        "filt_freq": jnp.pad(filt_freq[0], (0, hp))[None],
        "decay": hyena_decay[0].reshape(2, 1, 2 * D_HYENA),
        "bias0": hyena_bias[0, 0][None], "bias1": hyena_bias[0, 1][None],
        "gon_a": group_out_norm[0, :D_ATTN][None], "gon_h": group_out_norm[0, D_ATTN:][None],
        "wo_a": w_out[0, :D_ATTN].astype(BF16), "wo_h": w_out[0, D_ATTN:].astype(BF16),
    }
    return (_trunk(x_prompt, p), _trunk(x_sample, p))
```

```python
import functools
import math

import jax
import jax.numpy as jnp
import numpy as np
from jax import lax
from jax.experimental import pallas as pl
from jax.experimental.pallas import tpu as pltpu

F32 = jnp.float32
BF16 = jnp.bfloat16
U32 = jnp.uint32

D_MODEL = 2048
GRID_W = 64
D_ATTN = D_MODEL // 2
D_HYENA = D_MODEL - D_ATTN
HEAD_DIM = 128
N_Q_HEADS = D_ATTN // HEAD_DIM
N_KV_HEADS = 2
Q_PER_KV = N_Q_HEADS // N_KV_HEADS
ROPE_HALF = HEAD_DIM // 2
ROPE_THETA = 10000.0
FILTER_EMB = 33
FILTER_HIDDEN = 64
EPS = 1e-6
D_KV = N_KV_HEADS * HEAD_DIM

LANES = 128
FILT_PAD = 128
FFT_N2 = 128
FFT_W = 512
FFT_G = 8
MIB = 1024 * 1024
LOG2_E = math.log2(math.e)


def _cparams(semantics, vmem_mib):
    return pltpu.CompilerParams(dimension_semantics=semantics,
                                vmem_limit_bytes=vmem_mib * MIB)


def _rms(x, g):
    ms = jnp.mean(x * x, axis=-1, keepdims=True)
    return x * lax.rsqrt(ms + EPS) * g


def _resident(shape):
    nd = len(shape)
    return pl.BlockSpec(shape, lambda *_: (0,) * nd, pipeline_mode=pl.Buffered(1))


def _pack2(a, b):
    ua = lax.bitcast_convert_type(a, U32)
    ub = lax.bitcast_convert_type(b, U32)
    half = jnp.uint32(0x8000)
    return ((ua + half) & jnp.uint32(0xFFFF0000)) | ((ub + half) >> 16)


def _unpack_c(w):
    hi = lax.bitcast_convert_type(w & jnp.uint32(0xFFFF0000), F32)
    lo = lax.bitcast_convert_type(w << 16, F32)
    return hi, lo


def _pair_blocks(pb):
    hi = 4 * (pb // 2) + pb % 2
    return hi, hi + 2


def _ffn_body(x_ref, g_ref, wg_ref, wu_ref, w2_ref, gf_ref, o_ref, xn_ref, *, nj, final_norm):
    j = pl.program_id(1)

    @pl.when(j == 0)
    def _init():
        x = x_ref[...]
        xn_ref[...] = _rms(x, g_ref[...]).astype(BF16)
        o_ref[...] = x

    xn = xn_ref[...]
    hg = jnp.dot(xn, wg_ref[...], preferred_element_type=F32)
    hu = jnp.dot(xn, wu_ref[...], preferred_element_type=F32)
    a = (hg * (0.5 / (1.0 + jnp.exp(-hg))) * hu).astype(BF16)
    o_ref[...] += jnp.dot(a, w2_ref[...], preferred_element_type=F32)

    if final_norm:
        @pl.when(j == nj - 1)
        def _fin():
            o_ref[...] = _rms(o_ref[...], gf_ref[...])


def _ffn(x, g, w13, w2, gf, *, final_norm, tm=1024, tf=512):
    T, D = x.shape
    ff = w2.shape[0]
    nj = ff // tf
    body = functools.partial(_ffn_body, nj=nj, final_norm=final_norm)
    return pl.pallas_call(
        body,
        grid=(T // tm, nj),
        in_specs=[
            pl.BlockSpec((tm, D), lambda i, j: (i, 0)),
            pl.BlockSpec((1, D), lambda i, j: (0, 0)),
            pl.BlockSpec((D, tf), lambda i, j: (0, j)),
            pl.BlockSpec((D, tf), lambda i, j: (0, j + nj)),
            pl.BlockSpec((tf, D), lambda i, j: (j, 0)),
            pl.BlockSpec((1, D), lambda i, j: (0, 0)),
        ],
        out_specs=pl.BlockSpec((tm, D), lambda i, j: (i, 0)),
        out_shape=jax.ShapeDtypeStruct((T, D), F32),
        scratch_shapes=[pltpu.VMEM((tm, D), BF16)],
        compiler_params=_cparams(("parallel", "arbitrary"), 58),
        name="ffn",
    )(x, g, w13, w13, w2, gf)


def _store_time_grouped(ref, cb, val):
    n1t = ref.shape[2]
    v = val.reshape(n1t * (FFT_N2 // FFT_G), FFT_G, LANES)
    for a in range(n1t):
        ref[cb, :, a] = v[a * (FFT_N2 // FFT_G):(a + 1) * (FFT_N2 // FFT_G)]


def _load_time_grouped(ref, cb):
    n1t = ref.shape[2]
    return jnp.concatenate([ref[cb, :, a].reshape(FFT_N2, LANES) for a in range(n1t)], axis=0)


MIX_HALO = 16


def _mix_body(xb_ref, x_ref, xa_ref, g_ref, wq_ref, wkv_ref, wh_ref, qn_ref, kn_ref, cos_ref, sin_ref,
              cw_ref, cb_ref, q_ref, k_ref, v_ref, hy_ref, *, npos):
    tm = x_ref.shape[0]
    x_ext = jnp.concatenate([xb_ref[...], x_ref[...], xa_ref[...]], axis=0)
    h_ext = _rms(x_ext, g_ref[...]).astype(BF16)
    h = h_ext[MIX_HALO:MIX_HALO + tm]
    lane = lax.broadcasted_iota(jnp.int32, (tm, HEAD_DIM), 1)
    low_half = (lane % (2 * (ROPE_HALF // 2))) < (ROPE_HALF // 2)
    c = cos_ref[...]
    s = sin_ref[...]

    def head(p, gain, scale):
        y = _rms(p, gain)
        partner = jnp.where(low_half, pltpu.roll(y, HEAD_DIM - ROPE_HALF // 2, 1),
                            pltpu.roll(y, ROPE_HALF // 2, 1))
        y = y * c + partner * s
        return y if scale is None else y * scale

    q = jnp.dot(h, wq_ref[...], preferred_element_type=F32)
    for hd in range(N_Q_HEADS):
        sl = slice(hd * HEAD_DIM, (hd + 1) * HEAD_DIM)
        q_ref[:, sl] = head(q[:, sl], qn_ref[...], HEAD_DIM ** -0.5 * LOG2_E).astype(BF16)
    kv = jnp.dot(h, wkv_ref[...], preferred_element_type=F32)
    for hd in range(N_KV_HEADS):
        sl = slice(hd * HEAD_DIM, (hd + 1) * HEAD_DIM)
        k_ref[:, sl] = head(kv[:, sl], kn_ref[...], None).astype(BF16)
    v_ref[...] = kv[:, D_KV:].astype(BF16)

    hy = jnp.dot(h_ext, wh_ref[...], preferred_element_type=F32)
    pos = pl.program_id(0) % npos
    rid = lax.broadcasted_iota(jnp.int32, (tm, 1), 0)
    before = jnp.where((rid == 0) & (pos == 0), 0.0, hy[MIX_HALO - 1:MIX_HALO - 1 + tm])
    after = jnp.where((rid == tm - 1) & (pos == npos - 1), 0.0, hy[MIX_HALO + 1:MIX_HALO + 1 + tm])
    hyc = (before * cw_ref[0:1, :] + hy[MIX_HALO:MIX_HALO + tm] * cw_ref[1:2, :]
           + after * cw_ref[2:3, :] + cb_ref[...])
    for pb in range(hy_ref.shape[0]):
        hi, lo = _pair_blocks(pb)
        _store_time_grouped(hy_ref, pb, _pack2(hyc[:, hi * LANES:(hi + 1) * LANES],
                                               hyc[:, lo * LANES:(lo + 1) * LANES]))


def _mix_in(x, g, wq, wkv, wh, qn, kn, cos_t, sin_t, cw, cb, *, L, tm=512):
    T, D = x.shape
    npos = L // tm
    ng2 = FFT_N2 // FFT_G
    hb = tm // MIX_HALO
    nhb = T // MIX_HALO
    return pl.pallas_call(
        functools.partial(_mix_body, npos=npos),
        grid=(T // tm,),
        in_specs=[
            pl.BlockSpec((MIX_HALO, D), lambda i: (jnp.maximum(i * hb - 1, 0), 0)),
            pl.BlockSpec((tm, D), lambda i: (i, 0)),
            pl.BlockSpec((MIX_HALO, D), lambda i: (jnp.minimum((i + 1) * hb, nhb - 1), 0)),
            _resident((1, D)),
            _resident(wq.shape),
            _resident(wkv.shape),
            _resident(wh.shape),
            _resident((1, HEAD_DIM)),
            _resident((1, HEAD_DIM)),
            pl.BlockSpec((tm, HEAD_DIM), lambda i: (i % npos, 0)),
            pl.BlockSpec((tm, HEAD_DIM), lambda i: (i % npos, 0)),
            _resident(cw.shape),
            _resident(cb.shape),
        ],
        out_specs=[
            pl.BlockSpec((tm, D_ATTN), lambda i: (i, 0)),
            pl.BlockSpec((tm, D_KV), lambda i: (i, 0)),
            pl.BlockSpec((tm, D_KV), lambda i: (i, 0)),
            pl.BlockSpec((3 * D_HYENA // (2 * LANES), None, ng2, tm // FFT_N2, FFT_G, LANES),
                         lambda i: (0, i // npos, 0, i % npos, 0, 0)),
        ],
        out_shape=[
            jax.ShapeDtypeStruct((T, D_ATTN), BF16),
            jax.ShapeDtypeStruct((T, D_KV), BF16),
            jax.ShapeDtypeStruct((T, D_KV), BF16),
            jax.ShapeDtypeStruct((3 * D_HYENA // (2 * LANES), T // L, ng2, L // FFT_N2, FFT_G, LANES), U32),
        ],
        compiler_params=_cparams(("parallel",), 58),
        name="mix_in",
    )(x, x, x, g, wq, wkv, wh, qn, kn, cos_t, sin_t, cw, cb)


ATTN_ROWS = 32
ATTN_RING = 4
ATTN_ONES = 16


def _attn_body(q_ref, k_ref, v_ref, o_ref, vt_ref, *bufs, tq, tk, nk):
    s_refs, mx_refs, p_refs = (bufs[k * ATTN_RING:(k + 1) * ATTN_RING] for k in range(3))

    @pl.when(pl.program_id(2) == 0)
    def _transpose_v():
        for c in range(nk):
            vt_ref[c, :HEAD_DIM] = v_ref[pl.ds(c * tk, tk), :].astype(F32).T.astype(BF16)
            vt_ref[c, HEAD_DIM:] = jnp.ones((ATTN_ONES, tk), BF16)

    q = jnp.concatenate([q_ref[:, h * HEAD_DIM:(h + 1) * HEAD_DIM] for h in range(Q_PER_KV)], axis=0)
    m_cols = Q_PER_KV * tq
    nchunk = tk // ATTN_ROWS

    def stage_a(i, s_ref, mx_ref):
        kc = k_ref[pl.ds(pl.multiple_of(i * tk, tk), tk), :]
        s = lax.dot_general(kc, q, (((1,), (1,)), ((), ())), preferred_element_type=F32)
        s_ref[...] = s
        mx_ref[...] = jnp.max(s, axis=0, keepdims=True)

    def stage_b(s_ref, mx_ref, p_ref, m):
        m_new = jnp.maximum(m, mx_ref[...])
        alpha = jnp.exp2(m - m_new)
        for c in range(nchunk):
            rows = slice(c * ATTN_ROWS, (c + 1) * ATTN_ROWS)
            p_ref[rows, :] = jnp.exp2((s_ref[rows, :] - m_new).astype(BF16))
        return m_new, alpha

    def stage_d(i, p_ref, alpha, acc):
        return alpha * acc + jnp.dot(vt_ref[i], p_ref[...], preferred_element_type=F32)

    def ring(i, m, al0, al1, acc, n_a, n_b):
        alphas = {0: al0, 1: al1}
        for u in range(ATTN_RING):
            if u < n_a:
                stage_a(i + u + 4, s_refs[u], mx_refs[u])
            if u < n_b:
                v = (u + 2) % ATTN_RING
                m, alphas[u + 2] = stage_b(s_refs[v], mx_refs[v], p_refs[v], m)
            acc = stage_d(i + u, p_refs[u], alphas[u], acc)
        return m, alphas.get(4), alphas.get(5), acc

    m = jnp.full((1, m_cols), -jnp.inf, F32)
    acc = jnp.zeros((HEAD_DIM + ATTN_ONES, m_cols), F32)
    for u in range(ATTN_RING):
        stage_a(u, s_refs[u], mx_refs[u])
    m, al0 = stage_b(s_refs[0], mx_refs[0], p_refs[0], m)
    m, al1 = stage_b(s_refs[1], mx_refs[1], p_refs[1], m)

    def trip(j, carry):
        return ring(ATTN_RING * j, *carry, ATTN_RING, ATTN_RING)

    m, al0, al1, acc = lax.fori_loop(0, nk // ATTN_RING - 1, trip, (m, al0, al1, acc))
    _, _, _, acc = ring(nk - ATTN_RING, m, al0, al1, acc, 0, 2)
    o = acc[:HEAD_DIM] / acc[HEAD_DIM:HEAD_DIM + 1]
    for h in range(Q_PER_KV):
        o_ref[:, h * HEAD_DIM:(h + 1) * HEAD_DIM] = o[:, h * tq:(h + 1) * tq].T


def _attention(q, k, v, *, tq=512, tk=512):
    B, L, _ = q.shape
    qw = Q_PER_KV * HEAD_DIM
    nk = L // tk
    assert nk % ATTN_RING == 0 and nk >= 2 * ATTN_RING
    body = functools.partial(_attn_body, tq=tq, tk=tk, nk=nk)
    m_cols = Q_PER_KV * tq
    return pl.pallas_call(
        body,
        grid=(B, N_KV_HEADS, L // tq),
        in_specs=[
            pl.BlockSpec((None, tq, qw), lambda b, g, i: (b, i, g)),
            pl.BlockSpec((None, L, HEAD_DIM), lambda b, g, i: (b, 0, g)),
            pl.BlockSpec((None, L, HEAD_DIM), lambda b, g, i: (b, 0, g)),
        ],
        out_specs=pl.BlockSpec((None, tq, qw), lambda b, g, i: (b, i, g)),
        out_shape=jax.ShapeDtypeStruct((B, L, D_ATTN), F32),
        scratch_shapes=([pltpu.VMEM((nk, HEAD_DIM + ATTN_ONES, tk), BF16)]
                        + [pltpu.VMEM((tk, m_cols), F32)] * ATTN_RING
                        + [pltpu.VMEM((1, m_cols), F32)] * ATTN_RING
                        + [pltpu.VMEM((tk, m_cols), BF16)] * ATTN_RING),
        compiler_params=_cparams(("parallel", "parallel", "arbitrary"), 48),
        name="attn",
    )(q, k, v)


def _filt_body(feat_ref, w1_ref, b1_ref, w2_ref, b2_ref, w3_ref, fq_ref, dec_ref, kf_ref, asum_ref, *, nt):
    i = pl.program_id(0)
    hp = lax.Precision.HIGHEST
    fq = fq_ref[...]
    feat = feat_ref[...]
    h = jnp.sin(fq * (jnp.dot(feat, w1_ref[...], precision=hp, preferred_element_type=F32) + b1_ref[...]))
    h = jnp.sin(fq * (jnp.dot(h, w2_ref[...], precision=hp, preferred_element_type=F32) + b2_ref[...]))
    h_hi = h.astype(BF16)
    h_lo = (h - h_hi.astype(F32)).astype(BF16)
    h3 = (jnp.dot(h_hi, w3_ref[0], preferred_element_type=F32)
          + jnp.dot(h_lo, w3_ref[0], preferred_element_type=F32)
          + jnp.dot(h_hi, w3_ref[1], preferred_element_type=F32))
    t01 = feat[:, 0:1]
    taps = h3 * jnp.exp(-t01 * jnp.abs(dec_ref[...]))
    row = lax.broadcasted_iota(jnp.int32, (taps.shape[0], 1), 0)
    taps = jnp.where((row == 0) & (i == nt), 0.0, taps)
    for cb in range(taps.shape[1] // LANES):
        _store_time_grouped(kf_ref, cb, taps[:, cb * LANES:(cb + 1) * LANES])
    part = jnp.sum(jnp.abs(taps), axis=0, keepdims=True)

    @pl.when(i == 0)
    def _first():
        asum_ref[...] = part

    @pl.when(i > 0)
    def _rest():
        asum_ref[...] += part


def _filters(feats2, w1, b1, w2, b2, w3, fq, dec, *, tl=512):
    _, L, _ = feats2.shape
    nc = w3.shape[3]
    ng2 = FFT_N2 // FFT_G
    nt = L // tl
    body = functools.partial(_filt_body, nt=nt)
    return pl.pallas_call(
        body,
        grid=(2 * nt,),
        in_specs=[
            pl.BlockSpec((None, tl, FILT_PAD), lambda i: (i // nt, i % nt, 0)),
            _resident(w1.shape), _resident(b1.shape), _resident(w2.shape), _resident(b2.shape),
            pl.BlockSpec((None, 2, FILT_PAD, nc), lambda i: (i // nt, 0, 0, 0)),
            _resident(fq.shape),
            pl.BlockSpec((None, 1, nc), lambda i: (i // nt, 0, 0)),
        ],
        out_specs=[
            pl.BlockSpec((nc // LANES, ng2, tl // FFT_N2, FFT_G, LANES), lambda i: (0, 0, i, 0, 0)),
            pl.BlockSpec((1, nc), lambda i: (0, 0)),
        ],
        out_shape=[jax.ShapeDtypeStruct((nc // LANES, ng2, 2 * L // FFT_N2, FFT_G, LANES), F32),
                   jax.ShapeDtypeStruct((1, nc), F32)],
        compiler_params=_cparams(("arbitrary",), 48),
        name="filt",
    )(feats2, w1, b1, w2, b2, w3, fq, dec)


FFT_CB = FFT_W // LANES
FFT_CB_B = 2


def _pack_c(y):
    n = y.shape[0] // 2
    return _pack2(y[:n], y[n:])


def _unpack_lanes(w):
    hi, lo = _unpack_c(w)
    return jnp.concatenate([hi, lo], axis=1)


def _unpack_rows(w):
    re, im = _unpack_c(w)
    return jnp.concatenate([re, im], axis=0).astype(BF16)


def _flat_scratch(n, dtype, cbs, lead=()):
    return pltpu.VMEM(lead + (cbs, n * FFT_G, LANES), dtype)


def _flatten(flat_ref, block_ref, pre=()):
    rows = flat_ref.shape[-2]
    for h in range(block_ref.shape[0]):
        flat_ref[pre + (h,)] = block_ref[(h,) + pre].reshape(rows, LANES)


def _unflatten(block_ref, flat_ref, pre=()):
    n = flat_ref.shape[-2] // FFT_G
    for h in range(block_ref.shape[0]):
        block_ref[(h,) + pre] = flat_ref[pre + (h,)].reshape(n, FFT_G, LANES)


def _gather_rows(flat_ref, j, pre=()):
    n = flat_ref.shape[-2] // FFT_G
    return jnp.concatenate(
        [flat_ref[pre + (h, pl.ds(j, n, stride=FFT_G), slice(None))] for h in range(flat_ref.shape[len(pre)])],
        axis=1)


def _scatter_rows(flat_ref, j, val, pre=()):
    n = flat_ref.shape[-2] // FFT_G
    for h in range(flat_ref.shape[len(pre)]):
        flat_ref[pre + (h, pl.ds(j, n, stride=FFT_G), slice(None))] = val[:, h * LANES:(h + 1) * LANES]


def _store_blocks(ref, j, val):
    for h in range(ref.shape[0]):
        ref[h, j] = val[:, h * LANES:(h + 1) * LANES]


def _store_grouped(ref, j, val):
    groups = ref.shape[1]
    for h in range(ref.shape[0]):
        ref[h, :, j] = val[:, h * LANES:(h + 1) * LANES].reshape(groups, FFT_G, LANES)


def _fft_a_body(x_ref, fa_ref, o_ref, xf_ref):
    npair = x_ref.shape[1]
    for p in range(npair):
        for b in range(2):
            _flatten(xf_ref, x_ref, (p, b))
    for j in range(FFT_G):
        for p in range(npair):
            xs = jnp.concatenate([_unpack_lanes(_gather_rows(xf_ref, j, (p, b))) for b in range(2)],
                                 axis=0).astype(BF16)
            _store_grouped(o_ref.at[p], j, _pack_c(jnp.dot(fa_ref[j], xs, preferred_element_type=F32)))


def _fft_a(xv, fa, *, ncb, pb0=0):
    _, P, _, ng2, n1h, _, _ = xv.shape
    n2 = ng2 * FFT_G
    two_n1 = fa.shape[1]
    n1 = two_n1 // 2
    pbs = FFT_CB // 2
    c0 = pb0 // pbs
    return pl.pallas_call(
        _fft_a_body,
        grid=(ng2, ncb // FFT_CB),
        in_specs=[
            pl.BlockSpec((pbs, P, 2, None, n1h, FFT_G, LANES), lambda g, c: (c + c0, 0, 0, g, 0, 0, 0)),
            pl.BlockSpec((FFT_G, two_n1, 2 * n1h), lambda g, c: (g, 0, 0)),
        ],
        out_specs=pl.BlockSpec((P, FFT_CB, n1 // FFT_G, FFT_G, FFT_G, LANES), lambda g, c: (0, c, 0, g, 0, 0)),
        out_shape=jax.ShapeDtypeStruct((P, ncb, n1 // FFT_G, n2, FFT_G, LANES), U32),
        scratch_shapes=[_flat_scratch(n1h, U32, pbs, (P, 2))],
        compiler_params=_cparams(("arbitrary", "arbitrary"), 32),
        name="fft_a",
    )(xv, fa)


def _fft_a_real_body(x_ref, fa_ref, o_ref, xf_ref):
    _flatten(xf_ref, x_ref)
    for j in range(FFT_G):
        xs = _gather_rows(xf_ref, j).astype(BF16)
        _store_grouped(o_ref, j, _pack_c(jnp.dot(fa_ref[j], xs, preferred_element_type=F32)))


def _fft_a_real(xv, fa_re):
    ncb, ng2, n1, _, _ = xv.shape
    n2 = ng2 * FFT_G
    return pl.pallas_call(
        _fft_a_real_body,
        grid=(ng2, ncb // FFT_CB),
        in_specs=[
            pl.BlockSpec((FFT_CB, None, n1, FFT_G, LANES), lambda g, c: (c, g, 0, 0, 0)),
            pl.BlockSpec((FFT_G, 2 * n1, n1), lambda g, c: (g, 0, 0)),
        ],
        out_specs=pl.BlockSpec((FFT_CB, n1 // FFT_G, FFT_G, FFT_G, LANES), lambda g, c: (c, 0, g, 0, 0)),
        out_shape=jax.ShapeDtypeStruct((ncb, n1 // FFT_G, n2, FFT_G, LANES), U32),
        scratch_shapes=[_flat_scratch(n1, F32, FFT_CB)],
        compiler_params=_cparams(("arbitrary", "arbitrary"), 32),
        name="fft_a_real",
    )(xv, fa_re)


def _fft_b_filt_body(s_ref, fb_ref, as_ref, o_ref, sf_ref):
    _flatten(sf_ref, s_ref)
    inv = 1.0 / as_ref[...]
    fb = fb_ref[...]
    for j in range(FFT_G):
        k = jnp.dot(fb, _unpack_rows(_gather_rows(sf_ref, j)), preferred_element_type=F32)
        _store_blocks(o_ref, j, (k * inv).astype(BF16))


def _fft_b_filt(s1f, fb, asum):
    ncb, ng1, n2, _, _ = s1f.shape
    n1 = ng1 * FFT_G
    return pl.pallas_call(
        _fft_b_filt_body,
        grid=(ng1, ncb // FFT_CB),
        in_specs=[
            pl.BlockSpec((FFT_CB, None, n2, FFT_G, LANES), lambda g, c: (c, g, 0, 0, 0)),
            _resident(fb.shape),
            pl.BlockSpec((1, FFT_W), lambda g, c: (0, c)),
        ],
        out_specs=pl.BlockSpec((FFT_CB, FFT_G, 2 * n2, LANES), lambda g, c: (c, g, 0, 0)),
        out_shape=jax.ShapeDtypeStruct((ncb, n1, 2 * n2, LANES), BF16),
        scratch_shapes=[_flat_scratch(n2, U32, FFT_CB)],
        compiler_params=_cparams(("parallel", "parallel"), 32),
        name="fft_b_filt",
    )(s1f, fb, asum)


def _fft_b_body(s_ref, kf_ref, fb_ref, fbi_ref, o_ref, sf_ref):
    n2 = FFT_N2
    fb = fb_ref[...]
    fbi = fbi_ref[...]
    for p in range(s_ref.shape[0]):
        _flatten(sf_ref.at[p], s_ref.at[p])
    for j in range(FFT_G):
        k = jnp.concatenate([kf_ref[h, j] for h in range(kf_ref.shape[0])], axis=1)
        kr, ki = k[:n2], k[n2:]
        for p in range(s_ref.shape[0]):
            x = jnp.dot(fb, _unpack_rows(_gather_rows(sf_ref.at[p], j)), preferred_element_type=F32).astype(BF16)
            xr, xi = x[:n2], x[n2:]
            pm = jnp.concatenate([xr * kr - xi * ki, xr * ki + xi * kr], axis=0)
            _store_grouped(o_ref.at[p], j, _pack_c(jnp.dot(fbi, pm, preferred_element_type=F32)))


def _fft_b(s1, kf, fb, fbi, *, kcb0):
    P, ncb, ng1, n2, _, _ = s1.shape
    n1 = ng1 * FFT_G
    cbs = FFT_CB_B
    k0 = kcb0 // cbs
    return pl.pallas_call(
        _fft_b_body,
        grid=(ng1, ncb // cbs),
        in_specs=[
            pl.BlockSpec((P, cbs, None, n2, FFT_G, LANES), lambda g, c: (0, c, g, 0, 0, 0)),
            pl.BlockSpec((cbs, FFT_G, 2 * n2, LANES), lambda g, c: (c + k0, g, 0, 0)),
            _resident(fb.shape),
            _resident(fbi.shape),
        ],
        out_specs=pl.BlockSpec((P, cbs, n2 // FFT_G, FFT_G, FFT_G, LANES), lambda g, c: (0, c, 0, g, 0, 0)),
        out_shape=jax.ShapeDtypeStruct((P, ncb, n2 // FFT_G, n1, FFT_G, LANES), U32),
        scratch_shapes=[_flat_scratch(n2, U32, cbs, (P,))],
        compiler_params=_cparams(("parallel", "parallel"), 32),
        name="fft_b",
    )(s1, kf, fb, fbi)


def _fft_c_body(z_ref, fc_ref, u_ref, gate_ref, d_ref, *rest, chain):
    if chain:
        fa_ref, o_ref, s_ref, zf_ref, uf_ref, gf_ref, of_ref = rest
    else:
        o_ref, zf_ref, uf_ref, gf_ref, of_ref = rest
    n1h = o_ref.shape[2]
    d = d_ref[...]
    _flatten(zf_ref, z_ref)
    for b in range(2):
        _flatten(uf_ref, u_ref, (b,))
        _flatten(gf_ref, gate_ref, (b,))
    for j in range(FFT_G):
        y = jnp.dot(fc_ref[j], _unpack_rows(_gather_rows(zf_ref, j)), preferred_element_type=F32)
        res = []
        for b in range(2):
            yb = y[b * n1h:(b + 1) * n1h]
            gate = _unpack_lanes(_gather_rows(gf_ref, j, (b,)))
            skip = _unpack_lanes(_gather_rows(uf_ref, j, (b,)))
            res.append(gate * (yb + skip * d))
            if chain:
                half = res[b].shape[1] // 2
                _scatter_rows(of_ref, j, _pack2(res[b][:, :half], res[b][:, half:]), (b,))
            else:
                _scatter_rows(of_ref, j, res[b], (b,))
        if chain:
            xs = jnp.concatenate(res, axis=0).astype(BF16)
            _store_grouped(s_ref, j, _pack_c(jnp.dot(fa_ref[j], xs, preferred_element_type=F32)))
    for b in range(2):
        _unflatten(o_ref, of_ref, (b,))


def _fft_c(s2, fc, uv, gv, d, *, upb0, gpb0, fa=None):
    P, ncb, ng2, n1, _, _ = s2.shape
    n1h = n1 // 2
    n2 = ng2 * FFT_G
    pbs = FFT_CB // 2
    u0 = upb0 // pbs
    g0 = gpb0 // pbs
    chain = fa is not None
    pblk = (pbs, None, 2, None, n1h, FFT_G, LANES)
    ocb, odt = (pbs, U32) if chain else (FFT_CB, F32)
    in_specs = [
        pl.BlockSpec((None, FFT_CB, None, n1, FFT_G, LANES), lambda g, p, c: (p, c, g, 0, 0, 0)),
        pl.BlockSpec((FFT_G, 2 * n1h, 2 * n1), lambda g, p, c: (g, 0, 0)),
        pl.BlockSpec(pblk, lambda g, p, c: (c + u0, p, 0, g, 0, 0, 0)),
        pl.BlockSpec(pblk, lambda g, p, c: (c + g0, p, 0, g, 0, 0, 0)),
        pl.BlockSpec((1, FFT_W), lambda g, p, c: (0, c)),
    ]
    out_specs = [pl.BlockSpec((ocb, None, 2, None, n1h, FFT_G, LANES), lambda g, p, c: (c, p, 0, g, 0, 0, 0))]
    out_shape = [jax.ShapeDtypeStruct((ncb * ocb // FFT_CB, P, 2, ng2, n1h, FFT_G, LANES), odt)]
    args = [s2, fc, uv, gv, d]
    if chain:
        in_specs.append(pl.BlockSpec((FFT_G, 2 * n1, 2 * n1h), lambda g, p, c: (g, 0, 0)))
        out_specs.append(pl.BlockSpec((None, FFT_CB, n1 // FFT_G, FFT_G, FFT_G, LANES),
                                      lambda g, p, c: (p, c, 0, g, 0, 0)))
        out_shape.append(jax.ShapeDtypeStruct((P, ncb, n1 // FFT_G, n2, FFT_G, LANES), U32))
        args.append(fa)
    return pl.pallas_call(
        functools.partial(_fft_c_body, chain=chain),
        grid=(ng2, P, ncb // FFT_CB),
        in_specs=in_specs,
        out_specs=out_specs,
        out_shape=out_shape,
        scratch_shapes=[_flat_scratch(n1, U32, FFT_CB), _flat_scratch(n1h, U32, pbs, (2,)),
                        _flat_scratch(n1h, U32, pbs, (2,)), _flat_scratch(n1h, odt, ocb, (2,))],
        compiler_params=_cparams(("arbitrary", "arbitrary", "arbitrary"), 40),
        name="fft_c",
    )(*args)


def _oproj_body(x_ref, a_ref, h_ref, ga_ref, gh_ref, wa_ref, wh_ref, o_ref):
    ma = _rms(a_ref[...], ga_ref[...]).astype(BF16)
    hy = jnp.concatenate([_load_time_grouped(h_ref, cb) for cb in range(h_ref.shape[0])], axis=1)
    mh = _rms(hy, gh_ref[...]).astype(BF16)
    o_ref[...] = (x_ref[...] + jnp.dot(ma, wa_ref[...], preferred_element_type=F32)
                  + jnp.dot(mh, wh_ref[...], preferred_element_type=F32))


def _out_proj(x, a, h, ga, gh, wa, wh, *, tm=512):
    T, D = x.shape
    ncb, _, ng2, n1h, _, _ = h.shape
    npos = n1h * FFT_N2 // tm
    return pl.pallas_call(
        _oproj_body,
        grid=(T // tm,),
        in_specs=[
            pl.BlockSpec((tm, D), lambda i: (i, 0)),
            pl.BlockSpec((tm, D_ATTN), lambda i: (i, 0)),
            pl.BlockSpec((ncb, None, ng2, tm // FFT_N2, FFT_G, LANES), lambda i: (0, i // npos, 0, i % npos, 0, 0)),
            _resident((1, D_ATTN)), _resident((1, D_HYENA)),
            _resident(wa.shape), _resident(wh.shape),
        ],
        out_specs=pl.BlockSpec((tm, D), lambda i: (i, 0)),
        out_shape=jax.ShapeDtypeStruct((T, D), F32),
        compiler_params=_cparams(("parallel",), 48),
        name="oproj",
    )(x, a, h, ga, gh, wa, wh)


@functools.lru_cache(maxsize=None)
def _rope_tables_np(L):
    rows = L // GRID_W
    row = np.repeat(np.arange(rows, dtype=np.float32), GRID_W)
    col = np.tile(np.arange(GRID_W, dtype=np.float32), rows)
    inv = (ROPE_THETA ** (-np.arange(0, ROPE_HALF, 2, dtype=np.float32) / ROPE_HALF)).astype(np.float32)
    ar = (row[:, None] * inv[None]).astype(np.float64)
    ac = (col[:, None] * inv[None]).astype(np.float64)
    cr, sr, cc, sc = np.cos(ar), np.sin(ar), np.cos(ac), np.sin(ac)
    cos_t = np.concatenate([cr, cr, cc, cc], axis=-1).astype(np.float32)
    sin_t = np.concatenate([-sr, sr, -sc, sc], axis=-1).astype(np.float32)
    return cos_t, sin_t


def _rope_tables(L):
    return tuple(jnp.asarray(t) for t in _rope_tables_np(L))


@functools.lru_cache(maxsize=None)
def _filter_feats_np(L):
    t01 = np.linspace(0.0, 1.0, L, dtype=np.float32)[:, None]
    bands = (FILTER_EMB - 1) // 2
    fr = np.linspace(1e-4, bands - 1, bands, dtype=np.float32)[None]
    w = (np.float32(2.0 * math.pi) * np.arange(L, dtype=np.float32)[:, None] / np.float32(L)).astype(np.float32)
    arg = (fr * w).astype(np.float32).astype(np.float64)
    feats = np.concatenate([t01, np.cos(arg), -np.sin(arg)], axis=-1).astype(np.float32)
    feats = np.pad(feats, ((0, 0), (0, FILT_PAD - FILTER_EMB)))
    mirrored = np.concatenate([feats[:1], feats[1:][::-1]], axis=0)
    return np.stack([feats, mirrored])


def _filter_feats(L):
    return jnp.asarray(_filter_feats_np(L))


@functools.lru_cache(maxsize=None)
def _dft_tables_np(L):
    n = 2 * L
    n2 = FFT_N2
    n1 = n // n2
    n1h = n1 // 2
    k1 = np.arange(n1, dtype=np.int64)[None, :, None]
    t1 = np.arange(n1, dtype=np.int64)[None, None, :]
    t2 = np.arange(n2, dtype=np.int64)[:, None, None]
    ang = (-2.0 * math.pi / n) * ((k1 * (n2 * t1 + t2)) % n)
    cr_full, ci_full = np.cos(ang), np.sin(ang)
    fa_re = np.concatenate([cr_full, ci_full], -2)
    cr, ci = cr_full[..., :n1h], ci_full[..., :n1h]
    fa = np.concatenate([np.concatenate([cr, -ci], -1), np.concatenate([ci, cr], -1)], -2)
    crt, cit = np.swapaxes(cr, 1, 2), np.swapaxes(ci, 1, 2)
    fc = np.concatenate([np.concatenate([crt, cit], -1), np.concatenate([-cit, crt], -1)], -2) / n
    a = np.arange(n2, dtype=np.int64)
    ang2 = (-2.0 * math.pi / n2) * ((a[:, None] * a[None, :]) % n2)
    fr, fi = np.cos(ang2), np.sin(ang2)
    fb = np.concatenate([np.concatenate([fr, -fi], -1), np.concatenate([fi, fr], -1)], -2)
    fbi = np.concatenate([np.concatenate([fr, fi], -1), np.concatenate([-fi, fr], -1)], -2)
    return tuple(t.astype(np.float32) for t in (fa, fa_re, fb, fbi, fc))


def _dft_tables(L):
    return tuple(jnp.asarray(t, dtype=BF16) for t in _dft_tables_np(L))


def _trunk(x, p):
    B, L, D = x.shape
    T = B * L
    P = B // 2
    n2 = FFT_N2
    n1h = L // n2
    C = D_HYENA
    ncb = C // LANES

    x0 = x.reshape(T, D)
    x1 = _ffn(x0, p["ffn1_norm"], p["ffn1_w13"], p["ffn1_w2"], p["final_norm"], final_norm=False)

    cos_t, sin_t = _rope_tables(L)
    q, k, v, hyc = _mix_in(x1, p["mix_norm"], p["wq"], p["wkv"], p["wh"], p["q_norm"], p["k_norm"],
                           cos_t, sin_t, p["conv_w"], p["conv_b"], L=L)
    attn = _attention(q.reshape(B, L, D_ATTN), k.reshape(B, L, D_KV), v.reshape(B, L, D_KV))
    ng2 = n2 // FFT_G
    npb = ncb // 2
    hyv = hyc.reshape(3 * npb, P, 2, ng2, n1h, FFT_G, LANES)

    fa, fa_re, fb, fbi, fc = _dft_tables(L)
    taps, asum = _filters(_filter_feats(L), p["filt_w1"], p["filt_b1"], p["filt_w2"], p["filt_b2"],
                          p["filt_w3"], p["filt_freq"], p["decay"])
    kf = _fft_b_filt(_fft_a_real(taps, fa_re), fb, asum)

    s1 = _fft_a(hyv, fa, ncb=ncb, pb0=0)
    s2 = _fft_b(s1, kf, fb, fbi, kcb0=0)
    z, s1 = _fft_c(s2, fc, hyv, hyv, p["bias0"], upb0=0, gpb0=npb, fa=fa)
    s2 = _fft_b(s1, kf, fb, fbi, kcb0=ncb)
    ho, = _fft_c(s2, fc, z, hyv, p["bias1"], upb0=0, gpb0=2 * npb)

    x2 = _out_proj(x1, attn.reshape(T, D_ATTN), ho.reshape(ncb, B, ng2, n1h, FFT_G, LANES),
                   p["gon_a"], p["gon_h"], p["wo_a"], p["wo_h"])
    x3 = _ffn(x2, p["ffn2_norm"], p["ffn2_w13"], p["ffn2_w2"], p["final_norm"], final_norm=True)
    return x3.reshape(B, L, D)


def _split_bf16(w):
    hi = w.astype(BF16)
    lo = (w - hi.astype(F32)).astype(BF16)
    return jnp.stack([hi, lo], axis=1)


def kernel(x_prompt, x_sample, ffn1_norm, ffn1_w13, ffn1_w2, mix_norm, w_in, q_norm, k_norm, conv_w, conv_b, filt_w1, filt_b1, filt_w2, filt_b2, filt_w3, filt_freq, hyena_decay, hyena_bias, group_out_norm, w_out, ffn2_norm, ffn2_w13, ffn2_w2, final_norm):
    hp = FILT_PAD - FILTER_HIDDEN
    w_in0 = w_in[0]
    p = {
        "ffn1_norm": ffn1_norm[0][None], "ffn2_norm": ffn2_norm[0][None],
        "ffn1_w13": ffn1_w13[0].astype(BF16), "ffn1_w2": ffn1_w2[0].astype(BF16),
        "ffn2_w13": ffn2_w13[0].astype(BF16), "ffn2_w2": ffn2_w2[0].astype(BF16),
        "final_norm": final_norm[None],
        "mix_norm": mix_norm[0][None],
        "wq": w_in0[:, :D_ATTN].astype(BF16),
        "wkv": w_in0[:, D_ATTN:D_ATTN + 2 * D_KV].astype(BF16),
        "wh": w_in0[:, D_ATTN + 2 * D_KV:].astype(BF16),
        "q_norm": q_norm[0][None], "k_norm": k_norm[0][None],
        "conv_w": conv_w[0], "conv_b": conv_b[0][None],
        "filt_w1": jnp.pad(filt_w1[0], ((0, FILT_PAD - FILTER_EMB), (0, hp))),
        "filt_b1": jnp.pad(filt_b1[0], (0, hp))[None],
        "filt_w2": jnp.pad(filt_w2[0], ((0, hp), (0, hp))),
        "filt_b2": jnp.pad(filt_b2[0], (0, hp))[None],
        "filt_w3": _split_bf16(jnp.pad(filt_w3[0], ((0, hp), (0, 0)))
                               .reshape(FILT_PAD, 2, 2 * D_HYENA).transpose(1, 0, 2)),
        "filt_freq": jnp.pad(filt_freq[0], (0, hp))[None],
        "decay": hyena_decay[0].reshape(2, 1, 2 * D_HYENA),
        "bias0": hyena_bias[0, 0][None], "bias1": hyena_bias[0, 1][None],
        "gon_a": group_out_norm[0, :D_ATTN][None], "gon_h": group_out_norm[0, D_ATTN:][None],
        "wo_a": w_out[0, :D_ATTN].astype(BF16), "wo_h": w_out[0, D_ATTN:].astype(BF16),
    }
    return (_trunk(x_prompt, p), _trunk(x_sample, p))
```
